```python
import jax
import jax.numpy as jnp
from jax import lax
import numpy as np

D_MODEL = 1024
BATCH = 2
SEQ = 8192
DEPTH = 1

MEM_LEN = 256
MOBA_HEADS = 8
MOBA_HEAD_DIM = 64
MOBA_BLOCK = 256
MOBA_TOPK = 3
FOX_HEADS = 8
FOX_HEAD_DIM = 64
MEM_HEADS = 4
MEM_HEAD_DIM = 128
Q_BLOCK = 128
ROPE_THETA = 500000.0
ROPE_FRACTION_DIV = 4
N_BRANCHES = 3
D_FF = ((-((-8 * D_MODEL) // 3)) + 255) // 256 * 256
NORM_EPS = 1e-6
MOBA_WIDTH = MOBA_HEADS * MOBA_HEAD_DIM
FOX_WIDTH = FOX_HEADS * FOX_HEAD_DIM
MEM_WIDTH = MEM_HEADS * MEM_HEAD_DIM
IN_SIZES = (MOBA_WIDTH,) * 3 + (FOX_WIDTH,) * 3 + (FOX_HEADS, MEM_WIDTH) + (D_MODEL,) * N_BRANCHES
D_IN = sum(IN_SIZES)

kernel_name = "hybrid_moba_fox_memory_layer"


def rms_norm(x, gain):
    xf = x.astype(jnp.float32)
    y = xf * lax.rsqrt(jnp.mean(xf * xf, axis=-1, keepdims=True) + NORM_EPS)
    return (y * gain.astype(jnp.float32)).astype(x.dtype)


def split_heads(t, n_heads):
    b, s, _ = t.shape
    return t.reshape(b, s, n_heads, -1).transpose(0, 2, 1, 3)


def merge_heads(t):
    b, h, s, d = t.shape
    return t.transpose(0, 2, 1, 3).reshape(b, s, h * d)


def partial_rope(t):
    seq, dim = t.shape[2], t.shape[3]
    rot = dim // ROPE_FRACTION_DIV
    half = rot // 2
    inv_freq = 1.0 / (ROPE_THETA ** (jnp.arange(half, dtype=jnp.float32) * 2.0 / rot))
    ang = jnp.arange(seq, dtype=jnp.float32)[:, None] * inv_freq[None, :]
    cos, sin = jnp.cos(ang), jnp.sin(ang)
    tf = t.astype(jnp.float32)
    x1, x2 = tf[..., :half], tf[..., half:rot]
    out = jnp.concatenate([x1 * cos - x2 * sin, x2 * cos + x1 * sin, tf[..., rot:]], axis=-1)
    return out.astype(t.dtype)


def moba_attention(q, k, v):
    b, h, s, d = q.shape
    scale = d ** -0.5
    n_blocks = -(-s // MOBA_BLOCK)
    pad = n_blocks * MOBA_BLOCK - s
    kp = jnp.pad(k, ((0, 0), (0, 0), (0, pad), (0, 0)))
    vp = jnp.pad(v, ((0, 0), (0, 0), (0, pad), (0, 0)))
    kb = kp.reshape(b, h, n_blocks, MOBA_BLOCK, d)
    vb = vp.reshape(b, h, n_blocks, MOBA_BLOCK, d)
    k_mean = jnp.mean(kb.astype(jnp.float32), axis=3)
    topk = min(MOBA_TOPK, n_blocks)
    b_idx = jnp.arange(b)[:, None, None, None]
    h_idx = jnp.arange(h)[None, :, None, None]
    key_off = jnp.arange(MOBA_BLOCK)
    q_off = jnp.arange(Q_BLOCK)
    block_ids = jnp.arange(n_blocks)

    def chunk(ci):
        t0 = ci * Q_BLOCK
        cur = t0 // MOBA_BLOCK
        qc = lax.dynamic_slice_in_dim(q, t0, Q_BLOCK, axis=2)
        gate = jnp.einsum('bhqd,bhnd->bhqn', qc.astype(jnp.float32), k_mean)
        gate = jnp.where(block_ids < cur, gate, -jnp.inf)
        _, sel = lax.top_k(gate, topk)
        sel_valid = sel < cur
        k_sel = kb[b_idx, h_idx, sel]
        v_sel = vb[b_idx, h_idx, sel]
        s_sel = jnp.einsum('bhqd,bhqnkd->bhqnk', qc, k_sel).astype(jnp.float32) * scale
        s_sel = jnp.where(sel_valid[..., None], s_sel, -jnp.inf)
        k_own = lax.dynamic_slice_in_dim(kp, cur * MOBA_BLOCK, MOBA_BLOCK, axis=2)
        v_own = lax.dynamic_slice_in_dim(vp, cur * MOBA_BLOCK, MOBA_BLOCK, axis=2)
        s_own = jnp.einsum('bhqd,bhkd->bhqk', qc, k_own).astype(jnp.float32) * scale
        causal = (key_off[None, :] + cur * MOBA_BLOCK) <= (q_off[:, None] + t0)
        s_own = jnp.where(causal, s_own, -jnp.inf)
        scores = jnp.concatenate([s_sel.reshape(b, h, Q_BLOCK, topk * MOBA_BLOCK), s_own], axis=-1)
        p = jax.nn.softmax(scores, axis=-1).astype(v.dtype)
        p_sel = p[..., :topk * MOBA_BLOCK].reshape(b, h, Q_BLOCK, topk, MOBA_BLOCK)
        p_own = p[..., topk * MOBA_BLOCK:]
        return (jnp.einsum('bhqnk,bhqnkd->bhqd', p_sel, v_sel)
                + jnp.einsum('bhqk,bhkd->bhqd', p_own, v_own))

    out = lax.map(chunk, jnp.arange(s // Q_BLOCK))
    return jnp.moveaxis(out, 0, 2).reshape(b, h, s, d)


def fox_attention(q, k, v, log_f):
    b, h, s, d = q.shape
    scale = d ** -0.5
    cum = jnp.cumsum(log_f, axis=-1)
    key_pos = jnp.arange(s)
    q_off = jnp.arange(Q_BLOCK)

    def chunk(ci):
        t0 = ci * Q_BLOCK
        qc = lax.dynamic_slice_in_dim(q, t0, Q_BLOCK, axis=2)
        cq = lax.dynamic_slice_in_dim(cum, t0, Q_BLOCK, axis=2)
        scores = (jnp.einsum('bhqd,bhkd->bhqk', qc, k).astype(jnp.float32) * scale
                  + cq[..., None] - cum[:, :, None, :])
        causal = key_pos[None, :] <= (q_off[:, None] + t0)
        scores = jnp.where(causal, scores, -jnp.inf)
        p = jax.nn.softmax(scores, axis=-1).astype(v.dtype)
        return jnp.einsum('bhqk,bhkd->bhqd', p, v)

    out = lax.map(chunk, jnp.arange(s // Q_BLOCK))
    return jnp.moveaxis(out, 0, 2).reshape(b, h, s, d)


def memory_attention(q, mk, mv):
    scale = q.shape[-1] ** -0.5
    scores = jnp.einsum('bhsd,bhmd->bhsm', q, mk).astype(jnp.float32) * scale
    p = jax.nn.softmax(scores, axis=-1).astype(mv.dtype)
    return jnp.einsum('bhsm,bhmd->bhsd', p, mv)


def setup_inputs(seed: int = 0) -> dict:
    key = jax.random.key(seed)
    ks = jax.random.split(key, 24)
    L = DEPTH

    def normal(k, shape, scale):
        return jax.random.normal(k, shape, jnp.float32) * scale

    def gain(k, shape):
        return 1.0 + 0.1 * jax.random.normal(k, shape, jnp.float32)

    return {
        "x": normal(ks[0], (BATCH, SEQ, D_MODEL), 1.0),
        "mem": normal(ks[1], (BATCH, MEM_LEN, D_MODEL), 1.0),
        "g_mix": gain(ks[2], (L, D_MODEL)),
        "w_in": normal(ks[3], (L, D_MODEL, D_IN), D_MODEL ** -0.5),
        "b_forget": 4.0 + 0.5 * jax.random.normal(ks[4], (L, FOX_HEADS), jnp.float32),
        "g_q_moba": gain(ks[5], (L, MOBA_HEAD_DIM)),
        "g_k_moba": gain(ks[6], (L, MOBA_HEAD_DIM)),
        "g_q_fox": gain(ks[7], (L, FOX_HEAD_DIM)),
        "g_k_fox": gain(ks[8], (L, FOX_HEAD_DIM)),
        "g_q_mem": gain(ks[9], (L, MEM_HEAD_DIM)),
        "g_k_mem": gain(ks[10], (L, MEM_HEAD_DIM)),
        "g_mem": gain(ks[11], (L, D_MODEL)),
        "w_mem_kv": normal(ks[12], (L, D_MODEL, 2 * MEM_WIDTH), D_MODEL ** -0.5),
        "w_br_moba": normal(ks[13], (L, MOBA_WIDTH, D_MODEL), MOBA_WIDTH ** -0.5),
        "w_br_fox": normal(ks[14], (L, FOX_WIDTH, D_MODEL), FOX_WIDTH ** -0.5),
        "w_br_mem": normal(ks[15], (L, MEM_WIDTH, D_MODEL), MEM_WIDTH ** -0.5),
        "w_out": normal(ks[16], (L, D_MODEL, D_MODEL), D_MODEL ** -0.5),
        "g_ffn": gain(ks[17], (L, D_MODEL)),
        "w_gate": normal(ks[18], (L, D_MODEL, D_FF), D_MODEL ** -0.5),
        "w_up": normal(ks[19], (L, D_MODEL, D_FF), D_MODEL ** -0.5),
        "w_down": normal(ks[20], (L, D_FF, D_MODEL), D_FF ** -0.5),
    }


def reference(x, mem, g_mix, w_in, b_forget, g_q_moba, g_k_moba, g_q_fox, g_k_fox, g_q_mem, g_k_mem,
              g_mem, w_mem_kv, w_br_moba, w_br_fox, w_br_mem, w_out, g_ffn, w_gate, w_up, w_down):
    split_points = np.cumsum(IN_SIZES)[:-1].tolist()
    for layer in range(DEPTH):
        h = rms_norm(x, g_mix[layer])
        proj = h @ w_in[layer]
        (q_m, k_m, v_m, q_f, k_f, v_f, f_logit, q_c, a_m, a_f, a_c) = jnp.split(proj, split_points, axis=-1)

        q_m = partial_rope(rms_norm(split_heads(q_m, MOBA_HEADS), g_q_moba[layer]))
        k_m = partial_rope(rms_norm(split_heads(k_m, MOBA_HEADS), g_k_moba[layer]))
        y_m = moba_attention(q_m, k_m, split_heads(v_m, MOBA_HEADS))

        log_f = jax.nn.log_sigmoid(f_logit.astype(jnp.float32)
                                   + b_forget[layer].astype(jnp.float32)).transpose(0, 2, 1)
        q_f = rms_norm(split_heads(q_f, FOX_HEADS), g_q_fox[layer])
        k_f = rms_norm(split_heads(k_f, FOX_HEADS), g_k_fox[layer])
        y_f = fox_attention(q_f, k_f, split_heads(v_f, FOX_HEADS), log_f)

        mkv = rms_norm(mem, g_mem[layer]) @ w_mem_kv[layer]
        mk, mv = jnp.split(mkv, 2, axis=-1)
        mk = rms_norm(split_heads(mk, MEM_HEADS), g_k_mem[layer])
        q_c = rms_norm(split_heads(q_c, MEM_HEADS), g_q_mem[layer])
        y_c = memory_attention(q_c, mk, split_heads(mv, MEM_HEADS))

        merged = (jax.nn.sigmoid(a_m) * (merge_heads(y_m) @ w_br_moba[layer])
                  + jax.nn.sigmoid(a_f) * (merge_heads(y_f) @ w_br_fox[layer])
                  + jax.nn.sigmoid(a_c) * (merge_heads(y_c) @ w_br_mem[layer]))
        x = x + merged @ w_out[layer]

        h2 = rms_norm(x, g_ffn[layer])
        x = x + (jax.nn.silu(h2 @ w_gate[layer]) * (h2 @ w_up[layer])) @ w_down[layer]
    return x
```

```python
import functools

import numpy as np
import jax
import jax.numpy as jnp
from jax import lax
from jax.experimental import pallas as pl
from jax.experimental.pallas import tpu as pltpu

F32 = jnp.float32
BF16 = jnp.bfloat16

D_MODEL = 1024
HEAD_DIM = 64
N_HEADS = 8
MEM_HEADS = 4
MEM_HEAD_DIM = 128
ATT_WIDTH = 512
MOBA_BLOCK = 256
MOBA_TOPK = 3
ROPE_THETA = 500000.0
ROPE_DIMS = 16
NORM_EPS = 1e-6
NEG = -1e30

LANES = 128
COL_TILE = 512
N_COL_TILES = 13
PROJ_WIDTH = COL_TILE * N_COL_TILES
TILE_QM, TILE_KM, TILE_QF, TILE_KF, TILE_QC = 6, 7, 9, 10, 12
ATT_TILE = 256
VMEM_LIMIT = 52 * 1024 * 1024


def _nt(a, b):
    return lax.dot_general(a, b, (((1,), (1,)), ((), ())), preferred_element_type=F32)


def _dot(a, b):
    return jnp.dot(a, b, preferred_element_type=F32)


def _split3(v):
    p1 = v.astype(BF16)
    r1 = v - p1.astype(F32)
    p2 = r1.astype(BF16)
    p3 = (r1 - p2.astype(F32)).astype(BF16)
    return p1, p2, p3


def _proj_kernel(x_ref, gmix_ref, w_ref, wfh_ref, wfl_ref, bd_ref, gain_ref,
                 rc_ref, rs1_ref, rs2_ref, p_ref, flog_ref, kmean_ref, h_ref, t_ref):
    j = pl.program_id(1)
    tm = x_ref.shape[0]

    @pl.when(j == 0)
    def _():
        x = x_ref[...]
        ms = jnp.mean(x * x, axis=-1, keepdims=True)
        h = x * lax.rsqrt(ms + NORM_EPS) * gmix_ref[...]
        hb = h.astype(BF16)
        h_ref[...] = hb
        hl = (h - hb.astype(F32)).astype(BF16)
        flog_ref[...] = (_dot(hb, wfh_ref[...]) + _dot(hl, wfh_ref[...])
                         + _dot(hb, wfl_ref[...]))

    t_ref[...] = _dot(h_ref[...], w_ref[...])

    is_rope = (j == TILE_QM) | (j == TILE_KM)
    is_norm = is_rope | (j == TILE_QF) | (j == TILE_KF) | (j == TILE_QC)
    is_gate = j < TILE_QM
    is_plain = jnp.logical_not(is_norm | is_gate)

    def normed(rows):
        t = t_ref[rows, :]
        ms = _dot((t * t).astype(BF16), bd_ref[0])
        return t * lax.rsqrt(ms + NORM_EPS) * gain_ref[0]

    for r in range(tm // MOBA_BLOCK):
        rows = pl.ds(r * MOBA_BLOCK, MOBA_BLOCK)

        @pl.when(is_gate)
        def _():
            p_ref[rows, :] = jax.nn.sigmoid(t_ref[rows, :]).astype(BF16)

        @pl.when(is_plain)
        def _():
            p_ref[rows, :] = t_ref[rows, :].astype(BF16)

        @pl.when(is_norm & jnp.logical_not(is_rope))
        def _():
            p_ref[rows, :] = normed(rows).astype(BF16)

        @pl.when(is_rope)
        def _():
            y = normed(rows)
            rc, rs1, rs2 = rc_ref[rows, :], rs1_ref[rows, :], rs2_ref[rows, :]
            parts = []
            for g in range(COL_TILE // LANES):
                yg = y[:, g * LANES:(g + 1) * LANES]
                up = pltpu.roll(yg, LANES - ROPE_DIMS // 2, 1)
                dn = pltpu.roll(yg, ROPE_DIMS // 2, 1)
                parts.append(yg * rc + up * rs1 + dn * rs2)
            yr = jnp.concatenate(parts, axis=1)
            p_ref[rows, :] = yr.astype(BF16)

            @pl.when(j == TILE_KM)
            def _():
                kmean_ref[0, pl.ds(r, 1), :] = jnp.mean(yr, axis=0, keepdims=True)


def _proj_call(x2, gmix, w_a, wf_hi, wf_lo, bd, gains, rc, rs1, rs2, seq, tm):
    n_tok = x2.shape[0]
    seq_tiles = seq // tm
    return pl.pallas_call(
        _proj_kernel,
        grid=(n_tok // tm, N_COL_TILES),
        in_specs=[
            pl.BlockSpec((tm, D_MODEL), lambda i, j: (i, 0)),
            pl.BlockSpec((1, D_MODEL), lambda i, j: (0, 0)),
            pl.BlockSpec((D_MODEL, COL_TILE), lambda i, j: (0, j)),
            pl.BlockSpec((D_MODEL, LANES), lambda i, j: (0, 0)),
            pl.BlockSpec((D_MODEL, LANES), lambda i, j: (0, 0)),
            pl.BlockSpec((1, COL_TILE, COL_TILE), lambda i, j: (jnp.where(j == TILE_QC, 1, 0), 0, 0)),
            pl.BlockSpec((1, 1, COL_TILE), lambda i, j: (j, 0, 0)),
            pl.BlockSpec((tm, LANES), lambda i, j: (i % seq_tiles, 0)),
            pl.BlockSpec((tm, LANES), lambda i, j: (i % seq_tiles, 0)),
            pl.BlockSpec((tm, LANES), lambda i, j: (i % seq_tiles, 0)),
        ],
        out_specs=[
            pl.BlockSpec((tm, COL_TILE), lambda i, j: (i, j)),
            pl.BlockSpec((tm, LANES), lambda i, j: (i, 0)),
            pl.BlockSpec((1, tm // MOBA_BLOCK, COL_TILE), lambda i, j: (i, 0, 0)),
        ],
        out_shape=[
            jax.ShapeDtypeStruct((n_tok, PROJ_WIDTH), BF16),
            jax.ShapeDtypeStruct((n_tok, LANES), F32),
            jax.ShapeDtypeStruct((n_tok // tm, tm // MOBA_BLOCK, COL_TILE), F32),
        ],
        scratch_shapes=[pltpu.VMEM((tm, D_MODEL), BF16), pltpu.VMEM((tm, COL_TILE), F32)],
        compiler_params=pltpu.CompilerParams(
            dimension_semantics=("parallel", "arbitrary"), vmem_limit_bytes=VMEM_LIMIT),
        name="proj",
    )(x2, gmix, w_a, wf_hi, wf_lo, bd, gains, rc, rs1, rs2)


def _cumsum_kernel(flog_ref, bf_ref, caug_ref, carry_ref):
    s = pl.program_id(1)
    tq = flog_ref.shape[0]

    @pl.when(s == 0)
    def _():
        carry_ref[...] = jnp.zeros_like(carry_ref)

    z = flog_ref[...] + bf_ref[...]
    lf = jnp.minimum(z, 0.0) - jnp.log1p(jnp.exp(-jnp.abs(z)))
    row = lax.broadcasted_iota(jnp.int32, (tq, tq), 0)
    col = lax.broadcasted_iota(jnp.int32, (tq, tq), 1)
    tri = jnp.where(col <= row, 1.0, 0.0).astype(BF16)
    p1, p2, p3 = _split3(lf)
    cum = _dot(tri, p1) + _dot(tri, p2) + _dot(tri, p3) + carry_ref[...]
    carry_ref[...] = cum[tq - 1:tq, :]
    c1, c2, c3 = _split3(-cum)
    lane = lax.broadcasted_iota(jnp.int32, cum.shape, 1)
    piece = lane % 3
    sel = jnp.where(piece == 0, c1, jnp.where(piece == 1, c2, c3))
    caug_ref[...] = jnp.where(lane < 3 * N_HEADS, sel, jnp.zeros_like(sel))


def _cumsum_call(flog, bf_rep, batch, seq, tq):
    return pl.pallas_call(
        _cumsum_kernel,
        grid=(batch, seq // tq),
        in_specs=[
            pl.BlockSpec((tq, LANES), lambda b, s: (b * (seq // tq) + s, 0)),
            pl.BlockSpec((1, LANES), lambda b, s: (0, 0)),
        ],
        out_specs=pl.BlockSpec((tq, LANES), lambda b, s: (b * (seq // tq) + s, 0)),
        out_shape=jax.ShapeDtypeStruct((batch * seq, LANES), BF16),
        scratch_shapes=[pltpu.VMEM((1, LANES), F32)],
        compiler_params=pltpu.CompilerParams(dimension_semantics=("parallel", "arbitrary")),
        name="cumsum",
    )(flog, bf_rep)


def _memkv_kernel(mem_ref, gmem_ref, w_ref, gk_ref, mk_ref, mv_ref):
    m = mem_ref[0]
    ms = jnp.mean(m * m, axis=-1, keepdims=True)
    hm = (m * lax.rsqrt(ms + NORM_EPS) * gmem_ref[...]).astype(BF16)
    kv = _dot(hm, w_ref[...])
    ks = []
    for h in range(MEM_HEADS):
        kh = kv[:, h * MEM_HEAD_DIM:(h + 1) * MEM_HEAD_DIM]
        msk = jnp.mean(kh * kh, axis=-1, keepdims=True)
        ks.append(kh * lax.rsqrt(msk + NORM_EPS) * gk_ref[...])
    mk_ref[0] = jnp.concatenate(ks, axis=1).astype(BF16)
    mv_ref[0] = kv[:, ATT_WIDTH:].astype(BF16)


def _memkv_call(mem, gmem, w_kv, gk):
    batch, mlen, _ = mem.shape
    return pl.pallas_call(
        _memkv_kernel,
        grid=(batch,),
        in_specs=[
            pl.BlockSpec((1, mlen, D_MODEL), lambda b: (b, 0, 0)),
            pl.BlockSpec((1, D_MODEL), lambda b: (0, 0)),
            pl.BlockSpec((D_MODEL, 2 * ATT_WIDTH), lambda b: (0, 0)),
            pl.BlockSpec((1, MEM_HEAD_DIM), lambda b: (0, 0)),
        ],
        out_specs=[
            pl.BlockSpec((1, mlen, ATT_WIDTH), lambda b: (b, 0, 0)),
            pl.BlockSpec((1, mlen, ATT_WIDTH), lambda b: (b, 0, 0)),
        ],
        out_shape=[jax.ShapeDtypeStruct((batch, mlen, ATT_WIDTH), BF16)] * 2,
        compiler_params=pltpu.CompilerParams(dimension_semantics=("parallel",)),
        name="memkv",
    )(mem, gmem, w_kv, gk)


def _softmax_step(lhs, kb, vb, m, l, acc, mask=None):
    s = _nt(lhs, kb)
    if mask is not None:
        s = jnp.where(mask, s, NEG)
    m_new = jnp.maximum(m, jnp.max(s, axis=1, keepdims=True))
    alpha = jnp.exp(m - m_new)
    p = jnp.exp(s - m_new)
    l = alpha * l + jnp.sum(p, axis=1, keepdims=True)
    acc = alpha * acc + _dot(p.astype(BF16), vb)
    return m_new, l, acc


def _attend(lhs_pair, kaug_ref, v_ref, qi):
    tq = ATT_TILE

    def block(j):
        start = pl.multiple_of(j * ATT_TILE, ATT_TILE)
        return kaug_ref[pl.ds(start, ATT_TILE), :], v_ref[pl.ds(start, ATT_TILE), :]

    def body(j, carry):
        kb, vb = block(j)
        out = []
        for c in range(2):
            out.extend(_softmax_step(lhs_pair[c], kb, vb, *carry[3 * c:3 * c + 3]))
        return tuple(out)

    init = []
    for _ in range(2):
        init.extend([jnp.full((tq, 1), -jnp.inf, F32), jnp.zeros((tq, 1), F32),
                     jnp.zeros((tq, LANES), F32)])
    carry = lax.fori_loop(0, qi, body, tuple(init))

    row = lax.broadcasted_iota(jnp.int32, (tq, ATT_TILE), 0)
    col = lax.broadcasted_iota(jnp.int32, (tq, ATT_TILE), 1)
    causal = col <= row
    kb, vb = block(qi)
    res = []
    for c in range(2):
        _, l, acc = _softmax_step(lhs_pair[c], kb, vb, *carry[3 * c:3 * c + 3], mask=causal)
        res.append(acc / l)
    lane = lax.broadcasted_iota(jnp.int32, (tq, LANES), 1)
    return jnp.where(lane < HEAD_DIM, res[0], res[1])


def _head_mask(c):
    lane = lax.broadcasted_iota(jnp.int32, (ATT_TILE, LANES), 1)
    return (lane >= c * HEAD_DIM) & (lane < (c + 1) * HEAD_DIM)


def _moba_kernel(q_ref, k_ref, v_ref, km_ref, o_ref, kaug_ref):
    qi = pl.program_id(2)
    seq = k_ref.shape[0]
    n_blocks = seq // MOBA_BLOCK

    @pl.when(qi == 0)
    def _():
        kaug_ref[:, 0:LANES] = k_ref[...]
        lane = lax.broadcasted_iota(jnp.int32, (MOBA_BLOCK, LANES), 1)

        def fill(j, _):
            start = pl.multiple_of(j * MOBA_BLOCK, MOBA_BLOCK)
            kaug_ref[pl.ds(start, MOBA_BLOCK), LANES:2 * LANES] = (
                jnp.where(lane == j, 1.0, 0.0).astype(BF16))
            return 0

        lax.fori_loop(0, n_blocks, fill, 0)

    q2 = q_ref[...]
    km = km_ref[0]
    km = jnp.concatenate([km, jnp.zeros((LANES - n_blocks, LANES), F32)], axis=0)
    km_hi = km.astype(BF16)
    km_lo = (km - km_hi.astype(F32)).astype(BF16)
    lane = lax.broadcasted_iota(jnp.int32, (ATT_TILE, LANES), 1)
    lane_f = lane.astype(F32)

    lhs_pair = []
    for c in range(2):
        qc = jnp.where(_head_mask(c), q2, jnp.zeros_like(q2))
        gate = _nt(qc, km_hi) + _nt(qc, km_lo)
        g = jnp.where(lane < qi, gate, -jnp.inf)
        sel = lane == qi
        for _ in range(MOBA_TOPK):
            mx = jnp.max(g, axis=1, keepdims=True)
            first = jnp.min(jnp.where(g == mx, lane_f, float(LANES)), axis=1, keepdims=True)
            pick = (lane_f == first) & (mx > -jnp.inf)
            sel = sel | pick
            g = jnp.where(pick, -jnp.inf, g)
        bias = jnp.where(sel, 0.0, NEG).astype(BF16)
        lhs_pair.append(jnp.concatenate([qc, bias], axis=1))

    o_ref[...] = _attend(lhs_pair, kaug_ref, v_ref, qi).astype(BF16)


def _moba_call(proj, kmean, batch, seq):
    n_q = seq // ATT_TILE
    n_pairs = N_HEADS // 2
    q0 = (TILE_QM * COL_TILE) // LANES
    k0 = (TILE_KM * COL_TILE) // LANES
    v0 = k0 + COL_TILE // LANES
    return pl.pallas_call(
        _moba_kernel,
        grid=(batch, n_pairs, n_q),
        in_specs=[
            pl.BlockSpec((ATT_TILE, LANES), lambda b, p, i: (b * n_q + i, q0 + p)),
            pl.BlockSpec((seq, LANES), lambda b, p, i: (b, k0 + p)),
            pl.BlockSpec((seq, LANES), lambda b, p, i: (b, v0 + p)),
            pl.BlockSpec((1, seq // MOBA_BLOCK, LANES), lambda b, p, i: (b, 0, p)),
        ],
        out_specs=pl.BlockSpec((ATT_TILE, LANES), lambda b, p, i: (b * n_q + i, p)),
        out_shape=jax.ShapeDtypeStruct((batch * seq, ATT_WIDTH), BF16),
        scratch_shapes=[pltpu.VMEM((seq, 2 * LANES), BF16)],
        compiler_params=pltpu.CompilerParams(
            dimension_semantics=("parallel", "parallel", "arbitrary"),
            vmem_limit_bytes=VMEM_LIMIT),
        name="moba",
    )(proj, proj, proj, kmean)


def _fox_kernel(q_ref, k_ref, v_ref, caug_ref, o_ref, kaug_ref):
    pair = pl.program_id(1)
    qi = pl.program_id(2)

    @pl.when(qi == 0)
    def _():
        kaug_ref[:, 0:LANES] = k_ref[...]
        kaug_ref[:, LANES:2 * LANES] = caug_ref[...]

    q2 = q_ref[...]
    lane = lax.broadcasted_iota(jnp.int32, (ATT_TILE, LANES), 1)
    lhs_pair = []
    for c in range(2):
        qc = jnp.where(_head_mask(c), q2, jnp.zeros_like(q2))
        first = 3 * (2 * pair + c)
        ones = jnp.where((lane >= first) & (lane < first + 3), 1.0, 0.0).astype(BF16)
        lhs_pair.append(jnp.concatenate([qc, ones], axis=1))

    o_ref[...] = _attend(lhs_pair, kaug_ref, v_ref, qi).astype(BF16)


def _fox_call(proj, caug, batch, seq):
    n_q = seq // ATT_TILE
    n_pairs = N_HEADS // 2
    q0 = (TILE_QF * COL_TILE) // LANES
    k0 = (TILE_KF * COL_TILE) // LANES
    v0 = k0 + COL_TILE // LANES
    return pl.pallas_call(
        _fox_kernel,
        grid=(batch, n_pairs, n_q),
        in_specs=[
            pl.BlockSpec((ATT_TILE, LANES), lambda b, p, i: (b * n_q + i, q0 + p)),
            pl.BlockSpec((seq, LANES), lambda b, p, i: (b, k0 + p)),
            pl.BlockSpec((seq, LANES), lambda b, p, i: (b, v0 + p)),
            pl.BlockSpec((seq, LANES), lambda b, p, i: (b, 0)),
        ],
        out_specs=pl.BlockSpec((ATT_TILE, LANES), lambda b, p, i: (b * n_q + i, p)),
        out_shape=jax.ShapeDtypeStruct((batch * seq, ATT_WIDTH), BF16),
        scratch_shapes=[pltpu.VMEM((seq, 2 * LANES), BF16)],
        compiler_params=pltpu.CompilerParams(
            dimension_semantics=("parallel", "parallel", "arbitrary"),
            vmem_limit_bytes=VMEM_LIMIT),
        name="fox",
    )(proj, proj, proj, caug)


def _merge_kernel(x_ref, g_ref, qc_ref, ym_ref, yf_ref, mk_ref, mv_ref,
                  wbm_ref, wbf_ref, wbc_ref, wo_ref, o_ref):
    qc = qc_ref[...]
    mk = mk_ref[0]
    mv = mv_ref[0]
    ys = []
    for h in range(MEM_HEADS):
        cols = slice(h * MEM_HEAD_DIM, (h + 1) * MEM_HEAD_DIM)
        s = _nt(qc[:, cols], mk[:, cols])
        p = jnp.exp(s - jnp.max(s, axis=1, keepdims=True))
        y = _dot(p.astype(BF16), mv[:, cols]) / jnp.sum(p, axis=1, keepdims=True)
        ys.append(y)
    yc = jnp.concatenate(ys, axis=1).astype(BF16)
    merged = (g_ref[:, 0:D_MODEL].astype(F32) * _dot(ym_ref[...], wbm_ref[...])
              + g_ref[:, D_MODEL:2 * D_MODEL].astype(F32) * _dot(yf_ref[...], wbf_ref[...])
              + g_ref[:, 2 * D_MODEL:3 * D_MODEL].astype(F32) * _dot(yc, wbc_ref[...]))
    o_ref[...] = x_ref[...] + _dot(merged.astype(BF16), wo_ref[...])


def _merge_call(x2, proj, y_m, y_f, mk, mv, wbm, wbf, wbc, wo, seq, tm):
    n_tok = x2.shape[0]
    mlen = mk.shape[1]
    seq_tiles = seq // tm
    const = lambda i: (0, 0)
    return pl.pallas_call(
        _merge_kernel,
        grid=(n_tok // tm,),
        in_specs=[
            pl.BlockSpec((tm, D_MODEL), lambda i: (i, 0)),
            pl.BlockSpec((tm, 3 * D_MODEL), lambda i: (i, 0)),
            pl.BlockSpec((tm, ATT_WIDTH), lambda i: (i, TILE_QC)),
            pl.BlockSpec((tm, ATT_WIDTH), lambda i: (i, 0)),
            pl.BlockSpec((tm, ATT_WIDTH), lambda i: (i, 0)),
            pl.BlockSpec((1, mlen, ATT_WIDTH), lambda i: (i // seq_tiles, 0, 0)),
            pl.BlockSpec((1, mlen, ATT_WIDTH), lambda i: (i // seq_tiles, 0, 0)),
            pl.BlockSpec((ATT_WIDTH, D_MODEL), const),
            pl.BlockSpec((ATT_WIDTH, D_MODEL), const),
            pl.BlockSpec((ATT_WIDTH, D_MODEL), const),
            pl.BlockSpec((D_MODEL, D_MODEL), const),
        ],
        out_specs=pl.BlockSpec((tm, D_MODEL), lambda i: (i, 0)),
        out_shape=jax.ShapeDtypeStruct((n_tok, D_MODEL), F32),
        compiler_params=pltpu.CompilerParams(
            dimension_semantics=("parallel",), vmem_limit_bytes=VMEM_LIMIT),
        name="merge",
    )(x2, proj, proj, y_m, y_f, mk, mv, wbm, wbf, wbc, wo)


def _ffn_kernel(x_ref, g_ref, wg_ref, wu_ref, wd_ref, o_ref, h_ref, acc_ref):
    j = pl.program_id(1)

    @pl.when(j == 0)
    def _():
        x = x_ref[...]
        ms = jnp.mean(x * x, axis=-1, keepdims=True)
        h_ref[...] = (x * lax.rsqrt(ms + NORM_EPS) * g_ref[...]).astype(BF16)
        acc_ref[...] = jnp.zeros_like(acc_ref)

    h = h_ref[...]
    g = _dot(h, wg_ref[...])
    u = _dot(h, wu_ref[...])
    a = (g * jax.nn.sigmoid(g) * u).astype(BF16)
    acc_ref[...] += _dot(a, wd_ref[...])

    @pl.when(j == pl.num_programs(1) - 1)
    def _():
        o_ref[...] = x_ref[...] + acc_ref[...]


def _ffn_call(x1, g_ffn, wg, wu, wd, tm, tf):
    n_tok = x1.shape[0]
    d_ff = wg.shape[1]
    return pl.pallas_call(
        _ffn_kernel,
        grid=(n_tok // tm, d_ff // tf),
        in_specs=[
            pl.BlockSpec((tm, D_MODEL), lambda i, j: (i, 0)),
            pl.BlockSpec((1, D_MODEL), lambda i, j: (0, 0)),
            pl.BlockSpec((D_MODEL, tf), lambda i, j: (0, j)),
            pl.BlockSpec((D_MODEL, tf), lambda i, j: (0, j)),
            pl.BlockSpec((tf, D_MODEL), lambda i, j: (j, 0)),
        ],
        out_specs=pl.BlockSpec((tm, D_MODEL), lambda i, j: (i, 0)),
        out_shape=jax.ShapeDtypeStruct((n_tok, D_MODEL), F32),
        scratch_shapes=[pltpu.VMEM((tm, D_MODEL), BF16), pltpu.VMEM((tm, D_MODEL), F32)],
        compiler_params=pltpu.CompilerParams(
            dimension_semantics=("parallel", "arbitrary"), vmem_limit_bytes=VMEM_LIMIT),
        name="ffn",
    )(x1, g_ffn, wg, wu, wd)


@functools.lru_cache(maxsize=None)
def _rope_tables(seq):
    half = ROPE_DIMS // 2
    inv_freq = 1.0 / (ROPE_THETA ** (np.arange(half, dtype=np.float64) * 2.0 / ROPE_DIMS))
    ang = np.arange(seq, dtype=np.float64)[:, None] * inv_freq[None, :]
    cos, sin = np.cos(ang), np.sin(ang)
    d = np.arange(LANES) % HEAD_DIM
    rc = np.where(d[None, :] < ROPE_DIMS, cos[:, d % half], 1.0)
    rs1 = np.where(d[None, :] < half, -sin[:, d % half], 0.0)
    rs2 = np.where((d[None, :] >= half) & (d[None, :] < ROPE_DIMS), sin[:, d % half], 0.0)
    return tuple(np.asarray(t, np.float32) for t in (rc, rs1, rs2))


@functools.lru_cache(maxsize=None)
def _segment_mean_matrices():
    def blockdiag(width):
        return np.kron(np.eye(COL_TILE // width), np.ones((width, width))) / width
    return np.stack([blockdiag(HEAD_DIM), blockdiag(MEM_HEAD_DIM)]).astype(np.float32)


def _split_hi_lo(w):
    hi = w.astype(BF16)
    return hi, (w - hi.astype(F32)).astype(BF16)


def kernel(x, mem, g_mix, w_in, b_forget, g_q_moba, g_k_moba, g_q_fox, g_k_fox, g_q_mem, g_k_mem,
           g_mem, w_mem_kv, w_br_moba, w_br_fox, w_br_mem, w_out, g_ffn, w_gate, w_up, w_down):
    batch, seq, _ = x.shape
    depth = g_mix.shape[0]
    n_tok = batch * seq
    assert seq % 1024 == 0 and seq // MOBA_BLOCK <= LANES
    d_ff = w_gate.shape[2]
    tf = d_ff // 2 if (d_ff // 2) % LANES == 0 else d_ff

    rc, rs1, rs2 = (jnp.asarray(t) for t in _rope_tables(seq))
    bd = jnp.asarray(_segment_mean_matrices()).astype(BF16)
    ones = jnp.ones((COL_TILE,), F32)

    x2 = x.reshape(n_tok, D_MODEL)
    for layer in range(depth):
        w = w_in[layer]
        gates0 = 7 * ATT_WIDTH + N_HEADS
        qc0 = 6 * ATT_WIDTH + N_HEADS
        w_a = jnp.concatenate([w[:, gates0:], w[:, :6 * ATT_WIDTH], w[:, qc0:gates0]],
                              axis=1).astype(BF16)
        wf = jnp.repeat(w[:, 6 * ATT_WIDTH:qc0], 3, axis=1)
        wf = jnp.pad(wf, ((0, 0), (0, LANES - 3 * N_HEADS)))
        wf_hi, wf_lo = _split_hi_lo(wf)
        bf_rep = jnp.pad(jnp.repeat(b_forget[layer], 3), (0, LANES - 3 * N_HEADS))[None, :]

        gains = [ones] * N_COL_TILES
        gains[TILE_QM] = jnp.tile(g_q_moba[layer], N_HEADS) * HEAD_DIM ** -0.5
        gains[TILE_KM] = jnp.tile(g_k_moba[layer], N_HEADS)
        gains[TILE_QF] = jnp.tile(g_q_fox[layer], N_HEADS) * HEAD_DIM ** -0.5
        gains[TILE_KF] = jnp.tile(g_k_fox[layer], N_HEADS)
        gains[TILE_QC] = jnp.tile(g_q_mem[layer], MEM_HEADS) * MEM_HEAD_DIM ** -0.5
        gains = jnp.stack(gains)[:, None, :]

        proj, flog, kmean = _proj_call(x2, g_mix[layer][None, :], w_a, wf_hi, wf_lo, bd, gains,
                                       rc, rs1, rs2, seq, tm=1024)
        kmean = kmean.reshape(batch, seq // MOBA_BLOCK, COL_TILE)
        caug = _cumsum_call(flog, bf_rep, batch, seq, tq=256)
        mk, mv = _memkv_call(mem, g_mem[layer][None, :], w_mem_kv[layer].astype(BF16),
                             g_k_mem[layer][None, :])
        y_m = _moba_call(proj, kmean, batch, seq)
        y_f = _fox_call(proj, caug, batch, seq)
        x1 = _merge_call(x2, proj, y_m, y_f, mk, mv, w_br_moba[layer].astype(BF16),
                         w_br_fox[layer].astype(BF16), w_br_mem[layer].astype(BF16),
                         w_out[layer].astype(BF16), seq, tm=512)
        x2 = _ffn_call(x1, g_ffn[layer][None, :], w_gate[layer].astype(BF16),
                       w_up[layer].astype(BF16), w_down[layer].astype(BF16), tm=512, tf=tf)
    return x2.reshape(batch, seq, D_MODEL)
```

```python
import functools
import math

import numpy as np
import jax
import jax.numpy as jnp
from jax import lax
from jax.experimental import pallas as pl
from jax.experimental.pallas import tpu as pltpu

F32 = jnp.float32
BF16 = jnp.bfloat16

D_MODEL = 1024
HEAD_DIM = 64
N_HEADS = 8
MEM_HEADS = 4
MEM_HEAD_DIM = 128
ATT_WIDTH = 512
MOBA_BLOCK = 256
MOBA_TOPK = 3
ROPE_THETA = 500000.0
ROPE_DIMS = 16
NORM_EPS = 1e-6
NEG = -1e30
LOG2E = math.log2(math.e)

LANES = 128
COL_TILE = 512
N_COL_TILES = 13
PROJ_WIDTH = COL_TILE * N_COL_TILES
TILE_QM, TILE_KM, TILE_QF, TILE_KF, TILE_QC = 6, 7, 9, 10, 12
ATT_TILE = 256
KEY_GROUP = 4
VMEM_LIMIT = 52 * 1024 * 1024

FOX_Q_LANE0 = 32
MOBA_SHIFT_LANE = LANES - 1
MAX_SHIFT_NATS = 60.0
BOUND_SLACK = 1.02


def _nt(a, b):
    return lax.dot_general(a, b, (((1,), (1,)), ((), ())), preferred_element_type=F32)


def _dot(a, b):
    return jnp.dot(a, b, preferred_element_type=F32)


def _split3(v):
    p1 = v.astype(BF16)
    r1 = v - p1.astype(F32)
    p2 = r1.astype(BF16)
    p3 = (r1 - p2.astype(F32)).astype(BF16)
    return p1, p2, p3


def _proj_kernel(x_ref, gmix_ref, w_ref, wfh_ref, wfl_ref, bd_ref, gain_ref,
                 rc_ref, rs1_ref, rs2_ref, p_ref, flog_ref, kmean_ref, h_ref, t_ref):
    j = pl.program_id(1)
    tm = x_ref.shape[0]

    @pl.when(j == 0)
    def _():
        x = x_ref[...]
        ms = jnp.mean(x * x, axis=-1, keepdims=True)
        h = x * lax.rsqrt(ms + NORM_EPS) * gmix_ref[...]
        hb = h.astype(BF16)
        h_ref[...] = hb
        hl = (h - hb.astype(F32)).astype(BF16)
        flog_ref[...] = (_dot(hb, wfh_ref[...]) + _dot(hl, wfh_ref[...])
                         + _dot(hb, wfl_ref[...]))

    t_ref[...] = _dot(h_ref[...], w_ref[...])

    is_rope = (j == TILE_QM) | (j == TILE_KM)
    is_norm = is_rope | (j == TILE_QF) | (j == TILE_KF) | (j == TILE_QC)
    is_gate = j < TILE_QM
    is_plain = jnp.logical_not(is_norm | is_gate)

    def normed(rows):
        t = t_ref[rows, :]
        ms = _dot((t * t).astype(BF16), bd_ref[0])
        return t * lax.rsqrt(ms + NORM_EPS) * gain_ref[0]

    for r in range(tm // MOBA_BLOCK):
        rows = pl.ds(r * MOBA_BLOCK, MOBA_BLOCK)

        @pl.when(is_gate)
        def _():
            p_ref[rows, :] = jax.nn.sigmoid(t_ref[rows, :]).astype(BF16)

        @pl.when(is_plain)
        def _():
            p_ref[rows, :] = t_ref[rows, :].astype(BF16)

        @pl.when(is_norm & jnp.logical_not(is_rope))
        def _():
            p_ref[rows, :] = normed(rows).astype(BF16)

        @pl.when(is_rope)
        def _():
            y = normed(rows)
            rc, rs1, rs2 = rc_ref[rows, :], rs1_ref[rows, :], rs2_ref[rows, :]
            parts = []
            for g in range(COL_TILE // LANES):
                yg = y[:, g * LANES:(g + 1) * LANES]
                up = pltpu.roll(yg, LANES - ROPE_DIMS // 2, 1)
                dn = pltpu.roll(yg, ROPE_DIMS // 2, 1)
                parts.append(yg * rc + up * rs1 + dn * rs2)
            yr = jnp.concatenate(parts, axis=1)
            p_ref[rows, :] = yr.astype(BF16)

            @pl.when(j == TILE_KM)
            def _():
                kmean_ref[0, pl.ds(r, 1), :] = jnp.mean(yr, axis=0, keepdims=True)


def _proj_call(x2, gmix, w_a, wf_hi, wf_lo, bd, gains, rc, rs1, rs2, seq, tm):
    n_tok = x2.shape[0]
    seq_tiles = seq // tm
    return pl.pallas_call(
        _proj_kernel,
        grid=(n_tok // tm, N_COL_TILES),
        in_specs=[
            pl.BlockSpec((tm, D_MODEL), lambda i, j: (i, 0)),
            pl.BlockSpec((1, D_MODEL), lambda i, j: (0, 0)),
            pl.BlockSpec((D_MODEL, COL_TILE), lambda i, j: (0, j)),
            pl.BlockSpec((D_MODEL, LANES), lambda i, j: (0, 0)),
            pl.BlockSpec((D_MODEL, LANES), lambda i, j: (0, 0)),
            pl.BlockSpec((1, COL_TILE, COL_TILE), lambda i, j: (jnp.where(j == TILE_QC, 1, 0), 0, 0)),
            pl.BlockSpec((1, 1, COL_TILE), lambda i, j: (j, 0, 0)),
            pl.BlockSpec((tm, LANES), lambda i, j: (i % seq_tiles, 0)),
            pl.BlockSpec((tm, LANES), lambda i, j: (i % seq_tiles, 0)),
            pl.BlockSpec((tm, LANES), lambda i, j: (i % seq_tiles, 0)),
        ],
        out_specs=[
            pl.BlockSpec((tm, COL_TILE), lambda i, j: (i, j)),
            pl.BlockSpec((tm, LANES), lambda i, j: (i, 0)),
            pl.BlockSpec((1, tm // MOBA_BLOCK, COL_TILE), lambda i, j: (i, 0, 0)),
        ],
        out_shape=[
            jax.ShapeDtypeStruct((n_tok, PROJ_WIDTH), BF16),
            jax.ShapeDtypeStruct((n_tok, LANES), F32),
            jax.ShapeDtypeStruct((n_tok // tm, tm // MOBA_BLOCK, COL_TILE), F32),
        ],
        scratch_shapes=[pltpu.VMEM((tm, D_MODEL), BF16), pltpu.VMEM((tm, COL_TILE), F32)],
        compiler_params=pltpu.CompilerParams(
            dimension_semantics=("parallel", "arbitrary"), vmem_limit_bytes=VMEM_LIMIT),
        name="proj",
    )(x2, gmix, w_a, wf_hi, wf_lo, bd, gains, rc, rs1, rs2)


def _cumsum_kernel(flog_ref, bf_ref, shift_ref, faug_ref, carry_ref):
    s = pl.program_id(1)
    tq = flog_ref.shape[0]

    @pl.when(s == 0)
    def _():
        carry_ref[...] = jnp.zeros_like(carry_ref)

    z = flog_ref[...] + bf_ref[...]
    lf = jnp.minimum(z, 0.0) - jnp.log1p(jnp.exp(-jnp.abs(z)))
    row = lax.broadcasted_iota(jnp.int32, (tq, tq), 0)
    col = lax.broadcasted_iota(jnp.int32, (tq, tq), 1)
    tri = jnp.where(col <= row, 1.0, 0.0).astype(BF16)
    p1, p2, p3 = _split3(lf)
    cum = _dot(tri, p1) + _dot(tri, p2) + _dot(tri, p3) + carry_ref[...]
    carry_ref[...] = cum[tq - 1:tq, :]

    lane = lax.broadcasted_iota(jnp.int32, cum.shape, 1)
    key_side = lane < 3 * N_HEADS
    query_side = (lane >= FOX_Q_LANE0) & (lane < FOX_Q_LANE0 + 3 * N_HEADS)
    cum2 = cum * LOG2E
    val = jnp.where(key_side, -cum2, cum2 - shift_ref[...])
    c1, c2, c3 = _split3(val)
    piece = jnp.where(key_side, lane, lane - FOX_Q_LANE0) % 3
    sel = jnp.where(piece == 0, c1, jnp.where(piece == 1, c2, c3))
    faug_ref[...] = jnp.where(key_side | query_side, sel, jnp.zeros_like(sel))


def _cumsum_call(flog, bf_rep, shift_row, batch, seq, tq):
    return pl.pallas_call(
        _cumsum_kernel,
        grid=(batch, seq // tq),
        in_specs=[
            pl.BlockSpec((tq, LANES), lambda b, s: (b * (seq // tq) + s, 0)),
            pl.BlockSpec((1, LANES), lambda b, s: (0, 0)),
            pl.BlockSpec((1, LANES), lambda b, s: (0, 0)),
        ],
        out_specs=pl.BlockSpec((tq, LANES), lambda b, s: (b * (seq // tq) + s, 0)),
        out_shape=jax.ShapeDtypeStruct((batch * seq, LANES), BF16),
        scratch_shapes=[pltpu.VMEM((1, LANES), F32)],
        compiler_params=pltpu.CompilerParams(dimension_semantics=("parallel", "arbitrary")),
        name="cumsum",
    )(flog, bf_rep, shift_row)


def _memkv_kernel(mem_ref, gmem_ref, w_ref, gk_ref, mk_ref, mv_ref):
    m = mem_ref[0]
    ms = jnp.mean(m * m, axis=-1, keepdims=True)
    hm = (m * lax.rsqrt(ms + NORM_EPS) * gmem_ref[...]).astype(BF16)
    kv = _dot(hm, w_ref[...])
    ks = []
    for h in range(MEM_HEADS):
        kh = kv[:, h * MEM_HEAD_DIM:(h + 1) * MEM_HEAD_DIM]
        msk = jnp.mean(kh * kh, axis=-1, keepdims=True)
        ks.append(kh * lax.rsqrt(msk + NORM_EPS) * gk_ref[...])
    mk_ref[0] = jnp.concatenate(ks, axis=1).astype(BF16)
    mv_ref[0] = kv[:, ATT_WIDTH:].astype(BF16)


def _memkv_call(mem, gmem, w_kv, gk):
    batch, mlen, _ = mem.shape
    return pl.pallas_call(
        _memkv_kernel,
        grid=(batch,),
        in_specs=[
            pl.BlockSpec((1, mlen, D_MODEL), lambda b: (b, 0, 0)),
            pl.BlockSpec((1, D_MODEL), lambda b: (0, 0)),
            pl.BlockSpec((D_MODEL, 2 * ATT_WIDTH), lambda b: (0, 0)),
            pl.BlockSpec((1, MEM_HEAD_DIM), lambda b: (0, 0)),
        ],
        out_specs=[
            pl.BlockSpec((1, mlen, ATT_WIDTH), lambda b: (b, 0, 0)),
            pl.BlockSpec((1, mlen, ATT_WIDTH), lambda b: (b, 0, 0)),
        ],
        out_shape=[jax.ShapeDtypeStruct((batch, mlen, ATT_WIDTH), BF16)] * 2,
        compiler_params=pltpu.CompilerParams(dimension_semantics=("parallel",)),
        name="memkv",
    )(mem, gmem, w_kv, gk)


def _lane_iota():
    return lax.broadcasted_iota(jnp.int32, (ATT_TILE, LANES), 1)


def _lane_range(first, count, dtype):
    lane = _lane_iota()
    return jnp.where((lane >= first) & (lane < first + count), 1.0, 0.0).astype(dtype)


def _head_lanes(c, dtype):
    return _lane_range(c * HEAD_DIM, HEAD_DIM, dtype)


def _fill_values(v_ref, vaug_ref):
    n_blocks = v_ref.shape[0] // ATT_TILE

    def fill(j, _):
        rows = pl.ds(pl.multiple_of(j * ATT_TILE, ATT_TILE), ATT_TILE)
        v = v_ref[rows, :].astype(F32)
        for c in range(2):
            own = _head_lanes(c, F32)
            vaug_ref[c, rows, :] = (v * own + (1.0 - own)).astype(BF16)
        return 0

    lax.fori_loop(0, n_blocks, fill, 0)


def _attend(lhs_pair, kaug_ref, vaug_ref, acc_ref, qi, bounded):
    tq = ATT_TILE
    span = KEY_GROUP * ATT_TILE

    def group(g):
        rows = pl.ds(pl.multiple_of(g * span, span), span)
        return kaug_ref[rows, :], rows

    def step(c, kb, rows, m, mask):
        s = _nt(lhs_pair[c], kb)
        if mask is not None:
            s = jnp.where(mask, s, NEG)
        if bounded:
            p = jnp.exp2(s).astype(BF16)
            acc_ref[c] += _dot(p, vaug_ref[c, rows, :])
            return m
        m_new = jnp.maximum(m, jnp.max(s, axis=1, keepdims=True))
        p = jnp.exp2(s - m_new).astype(BF16)
        acc_ref[c] = jnp.exp2(m - m_new) * acc_ref[c] + _dot(p, vaug_ref[c, rows, :])
        return m_new

    acc_ref[...] = jnp.zeros_like(acc_ref)

    def body(g, ms):
        kb, rows = group(g)
        return tuple(step(c, kb, rows, ms[c], None) for c in range(2))

    n_full = qi // KEY_GROUP
    m0 = jnp.full((tq, 1), -jnp.inf, F32)
    ms = lax.fori_loop(0, n_full, body, (m0, m0))

    q_pos = qi * ATT_TILE + lax.broadcasted_iota(jnp.int32, (tq, span), 0)
    k_pos = n_full * span + lax.broadcasted_iota(jnp.int32, (tq, span), 1)
    kb, rows = group(n_full)
    res = []
    for c in range(2):
        step(c, kb, rows, ms[c], k_pos <= q_pos)
        acc = acc_ref[c]
        res.append(acc / pltpu.roll(acc, HEAD_DIM, 1))
    return jnp.where(_lane_iota() < HEAD_DIM, res[0], res[1])


def _moba_kernel(q_ref, k_ref, v_ref, km_ref, shift_ref, o_ref, kaug_ref, vaug_ref, acc_ref,
                 *, bounded):
    qi = pl.program_id(2)
    seq = k_ref.shape[0]
    n_blocks = seq // MOBA_BLOCK
    lane = _lane_iota()

    @pl.when(qi == 0)
    def _():
        kaug_ref[:, 0:LANES] = k_ref[...]

        def fill(j, _):
            rows = pl.ds(pl.multiple_of(j * MOBA_BLOCK, MOBA_BLOCK), MOBA_BLOCK)
            onehot = (lane == j) | (lane == MOBA_SHIFT_LANE)
            kaug_ref[rows, LANES:2 * LANES] = jnp.where(onehot, 1.0, 0.0).astype(BF16)
            return 0

        lax.fori_loop(0, n_blocks, fill, 0)
        _fill_values(v_ref, vaug_ref)

    q2 = q_ref[...]
    km = km_ref[0]
    km = jnp.concatenate([km, jnp.zeros((LANES - n_blocks, LANES), F32)], axis=0)
    km_hi = km.astype(BF16)
    km_lo = (km - km_hi.astype(F32)).astype(BF16)
    lane_f = lane.astype(F32)

    lhs_pair = []
    for c in range(2):
        qc = q2 * _head_lanes(c, BF16)
        gate = _nt(qc, km_hi) + _nt(qc, km_lo)
        g = jnp.where(lane < qi, gate, -jnp.inf)
        sel = lane == qi
        for _ in range(MOBA_TOPK):
            mx = jnp.max(g, axis=1, keepdims=True)
            first = jnp.min(jnp.where(g == mx, lane_f, float(LANES)), axis=1, keepdims=True)
            pick = (lane_f == first) & (mx > -jnp.inf)
            sel = sel | pick
            g = jnp.where(pick, -jnp.inf, g)
        bias = jnp.where(sel, 0.0, NEG)
        shift = shift_ref[...] if bounded else 0.0
        bias = jnp.where(lane == MOBA_SHIFT_LANE, -shift, bias)
        lhs_pair.append(jnp.concatenate([qc, bias.astype(BF16)], axis=1))

    o_ref[...] = _attend(lhs_pair, kaug_ref, vaug_ref, acc_ref, qi, bounded).astype(BF16)


def _moba_call(proj, kmean, shift_row, batch, seq, bounded):
    n_q = seq // ATT_TILE
    n_pairs = N_HEADS // 2
    q0 = (TILE_QM * COL_TILE) // LANES
    k0 = (TILE_KM * COL_TILE) // LANES
    v0 = k0 + COL_TILE // LANES
    return pl.pallas_call(
        functools.partial(_moba_kernel, bounded=bounded),
        grid=(batch, n_pairs, n_q),
        in_specs=[
            pl.BlockSpec((ATT_TILE, LANES), lambda b, p, i: (b * n_q + i, q0 + p)),
            pl.BlockSpec((seq, LANES), lambda b, p, i: (b, k0 + p)),
            pl.BlockSpec((seq, LANES), lambda b, p, i: (b, v0 + p)),
            pl.BlockSpec((1, seq // MOBA_BLOCK, LANES), lambda b, p, i: (b, 0, p)),
            pl.BlockSpec((1, LANES), lambda b, p, i: (0, 0)),
        ],
        out_specs=pl.BlockSpec((ATT_TILE, LANES), lambda b, p, i: (b * n_q + i, p)),
        out_shape=jax.ShapeDtypeStruct((batch * seq, ATT_WIDTH), BF16),
        scratch_shapes=[pltpu.VMEM((seq, 2 * LANES), BF16),
                        pltpu.VMEM((2, seq, LANES), BF16),
                        pltpu.VMEM((2, ATT_TILE, LANES), F32)],
        compiler_params=pltpu.CompilerParams(
            dimension_semantics=("parallel", "parallel", "arbitrary"),
            vmem_limit_bytes=VMEM_LIMIT),
        name="moba_bounded" if bounded else "moba_general",
    )(proj, proj, proj, kmean, shift_row)


def _fox_kernel(q_ref, k_ref, v_ref, faug_ref, fq_ref, o_ref, kaug_ref, vaug_ref, acc_ref,
                *, bounded):
    pair = pl.program_id(1)
    qi = pl.program_id(2)
    n_extra = 3 * N_HEADS

    @pl.when(qi == 0)
    def _():
        seq = k_ref.shape[0]
        kaug_ref[:, 0:LANES] = k_ref[...]

        def fill(j, _):
            rows = pl.ds(pl.multiple_of(j * ATT_TILE, ATT_TILE), ATT_TILE)
            lane = _lane_iota()
            ones = (lane >= FOX_Q_LANE0) & (lane < FOX_Q_LANE0 + n_extra)
            f = faug_ref[rows, :].astype(F32)
            kaug_ref[rows, LANES:2 * LANES] = (
                f * (lane < n_extra).astype(F32) + ones.astype(F32)).astype(BF16)
            return 0

        lax.fori_loop(0, seq // ATT_TILE, fill, 0)
        _fill_values(v_ref, vaug_ref)

    q2 = q_ref[...]
    fq = fq_ref[...]
    lane = _lane_iota()
    lhs_pair = []
    for c in range(2):
        qc = q2 * _head_lanes(c, BF16)
        k_first = 3 * (2 * pair + c)
        extra = _lane_range(k_first, 3, BF16)
        if bounded:
            extra = extra + fq * _lane_range(FOX_Q_LANE0 + k_first, 3, BF16)
        lhs_pair.append(jnp.concatenate([qc, extra], axis=1))

    o_ref[...] = _attend(lhs_pair, kaug_ref, vaug_ref, acc_ref, qi, bounded).astype(BF16)


def _fox_call(proj, faug, batch, seq, bounded):
    n_q = seq // ATT_TILE
    n_pairs = N_HEADS // 2
    q0 = (TILE_QF * COL_TILE) // LANES
    k0 = (TILE_KF * COL_TILE) // LANES
    v0 = k0 + COL_TILE // LANES
    return pl.pallas_call(
        functools.partial(_fox_kernel, bounded=bounded),
        grid=(batch, n_pairs, n_q),
        in_specs=[
            pl.BlockSpec((ATT_TILE, LANES), lambda b, p, i: (b * n_q + i, q0 + p)),
            pl.BlockSpec((seq, LANES), lambda b, p, i: (b, k0 + p)),
            pl.BlockSpec((seq, LANES), lambda b, p, i: (b, v0 + p)),
            pl.BlockSpec((seq, LANES), lambda b, p, i: (b, 0)),
            pl.BlockSpec((ATT_TILE, LANES), lambda b, p, i: (b * n_q + i, 0)),
        ],
        out_specs=pl.BlockSpec((ATT_TILE, LANES), lambda b, p, i: (b * n_q + i, p)),
        out_shape=jax.ShapeDtypeStruct((batch * seq, ATT_WIDTH), BF16),
        scratch_shapes=[pltpu.VMEM((seq, 2 * LANES), BF16),
                        pltpu.VMEM((2, seq, LANES), BF16),
                        pltpu.VMEM((2, ATT_TILE, LANES), F32)],
        compiler_params=pltpu.CompilerParams(
            dimension_semantics=("parallel", "parallel", "arbitrary"),
            vmem_limit_bytes=VMEM_LIMIT),
        name="fox_bounded" if bounded else "fox_general",
    )(proj, proj, proj, faug, faug)


def _merge_kernel(x_ref, g_ref, qc_ref, ym_ref, yf_ref, mk_ref, mv_ref,
                  wbm_ref, wbf_ref, wbc_ref, wo_ref, o_ref):
    qc = qc_ref[...]
    mk = mk_ref[0]
    mv = mv_ref[0]
    ys = []
    for h in range(MEM_HEADS):
        cols = slice(h * MEM_HEAD_DIM, (h + 1) * MEM_HEAD_DIM)
        s = _nt(qc[:, cols], mk[:, cols])
        p = jnp.exp2(s - jnp.max(s, axis=1, keepdims=True))
        y = _dot(p.astype(BF16), mv[:, cols]) / jnp.sum(p, axis=1, keepdims=True)
        ys.append(y)
    yc = jnp.concatenate(ys, axis=1).astype(BF16)
    merged = (g_ref[:, 0:D_MODEL].astype(F32) * _dot(ym_ref[...], wbm_ref[...])
              + g_ref[:, D_MODEL:2 * D_MODEL].astype(F32) * _dot(yf_ref[...], wbf_ref[...])
              + g_ref[:, 2 * D_MODEL:3 * D_MODEL].astype(F32) * _dot(yc, wbc_ref[...]))
    o_ref[...] = x_ref[...] + _dot(merged.astype(BF16), wo_ref[...])


def _merge_call(x2, proj, y_m, y_f, mk, mv, wbm, wbf, wbc, wo, seq, tm):
    n_tok = x2.shape[0]
    mlen = mk.shape[1]
    seq_tiles = seq // tm
    const = lambda i: (0, 0)
    return pl.pallas_call(
        _merge_kernel,
        grid=(n_tok // tm,),
        in_specs=[
            pl.BlockSpec((tm, D_MODEL), lambda i: (i, 0)),
            pl.BlockSpec((tm, 3 * D_MODEL), lambda i: (i, 0)),
            pl.BlockSpec((tm, ATT_WIDTH), lambda i: (i, TILE_QC)),
            pl.BlockSpec((tm, ATT_WIDTH), lambda i: (i, 0)),
            pl.BlockSpec((tm, ATT_WIDTH), lambda i: (i, 0)),
            pl.BlockSpec((1, mlen, ATT_WIDTH), lambda i: (i // seq_tiles, 0, 0)),
            pl.BlockSpec((1, mlen, ATT_WIDTH), lambda i: (i // seq_tiles, 0, 0)),
            pl.BlockSpec((ATT_WIDTH, D_MODEL), const),
            pl.BlockSpec((ATT_WIDTH, D_MODEL), const),
            pl.BlockSpec((ATT_WIDTH, D_MODEL), const),
            pl.BlockSpec((D_MODEL, D_MODEL), const),
        ],
        out_specs=pl.BlockSpec((tm, D_MODEL), lambda i: (i, 0)),
        out_shape=jax.ShapeDtypeStruct((n_tok, D_MODEL), F32),
        compiler_params=pltpu.CompilerParams(
            dimension_semantics=("parallel",), vmem_limit_bytes=VMEM_LIMIT),
        name="merge",
    )(x2, proj, proj, y_m, y_f, mk, mv, wbm, wbf, wbc, wo)


def _ffn_kernel(x_ref, g_ref, wg_ref, wu_ref, wd_ref, o_ref, h_ref, acc_ref):
    j = pl.program_id(1)

    @pl.when(j == 0)
    def _():
        x = x_ref[...]
        ms = jnp.mean(x * x, axis=-1, keepdims=True)
        h_ref[...] = (x * lax.rsqrt(ms + NORM_EPS) * g_ref[...]).astype(BF16)
        acc_ref[...] = jnp.zeros_like(acc_ref)

    h = h_ref[...]
    g = _dot(h, wg_ref[...])
    u = _dot(h, wu_ref[...])
    a = (g * jax.nn.sigmoid(g) * u).astype(BF16)
    acc_ref[...] += _dot(a, wd_ref[...])

    @pl.when(j == pl.num_programs(1) - 1)
    def _():
        o_ref[...] = x_ref[...] + acc_ref[...]


def _ffn_call(x1, g_ffn, wg, wu, wd, tm, tf):
    n_tok = x1.shape[0]
    d_ff = wg.shape[1]
    return pl.pallas_call(
        _ffn_kernel,
        grid=(n_tok // tm, d_ff // tf),
        in_specs=[
            pl.BlockSpec((tm, D_MODEL), lambda i, j: (i, 0)),
            pl.BlockSpec((1, D_MODEL), lambda i, j: (0, 0)),
            pl.BlockSpec((D_MODEL, tf), lambda i, j: (0, j)),
            pl.BlockSpec((D_MODEL, tf), lambda i, j: (0, j)),
            pl.BlockSpec((tf, D_MODEL), lambda i, j: (j, 0)),
        ],
        out_specs=pl.BlockSpec((tm, D_MODEL), lambda i, j: (i, 0)),
        out_shape=jax.ShapeDtypeStruct((n_tok, D_MODEL), F32),
        scratch_shapes=[pltpu.VMEM((tm, D_MODEL), BF16), pltpu.VMEM((tm, D_MODEL), F32)],
        compiler_params=pltpu.CompilerParams(
            dimension_semantics=("parallel", "arbitrary"), vmem_limit_bytes=VMEM_LIMIT),
        name="ffn",
    )(x1, g_ffn, wg, wu, wd)


@functools.lru_cache(maxsize=None)
def _rope_tables(seq):
    half = ROPE_DIMS // 2
    inv_freq = 1.0 / (ROPE_THETA ** (np.arange(half, dtype=np.float64) * 2.0 / ROPE_DIMS))
    ang = np.arange(seq, dtype=np.float64)[:, None] * inv_freq[None, :]
    cos, sin = np.cos(ang), np.sin(ang)
    d = np.arange(LANES) % HEAD_DIM
    rc = np.where(d[None, :] < ROPE_DIMS, cos[:, d % half], 1.0)
    rs1 = np.where(d[None, :] < half, -sin[:, d % half], 0.0)
    rs2 = np.where((d[None, :] >= half) & (d[None, :] < ROPE_DIMS), sin[:, d % half], 0.0)
    return tuple(np.asarray(t, np.float32) for t in (rc, rs1, rs2))


@functools.lru_cache(maxsize=None)
def _segment_mean_matrices():
    def blockdiag(width):
        return np.kron(np.eye(COL_TILE // width), np.ones((width, width))) / width
    return np.stack([blockdiag(HEAD_DIM), blockdiag(MEM_HEAD_DIM)]).astype(np.float32)


def _split_hi_lo(w):
    hi = w.astype(BF16)
    return hi, (w - hi.astype(F32)).astype(BF16)


def _score_bound(g_q, g_k):
    return (BOUND_SLACK * HEAD_DIM ** 0.5 * jnp.max(jnp.abs(g_q)) * jnp.max(jnp.abs(g_k)))


def _lane_row(value, first, count):
    lane = jnp.arange(LANES)
    return jnp.where((lane >= first) & (lane < first + count), value, 0.0).astype(F32)[None, :]


def kernel(x, mem, g_mix, w_in, b_forget, g_q_moba, g_k_moba, g_q_fox, g_k_fox, g_q_mem, g_k_mem,
           g_mem, w_mem_kv, w_br_moba, w_br_fox, w_br_mem, w_out, g_ffn, w_gate, w_up, w_down):
    batch, seq, _ = x.shape
    depth = g_mix.shape[0]
    n_tok = batch * seq
    assert seq % 1024 == 0 and seq // MOBA_BLOCK < MOBA_SHIFT_LANE
    d_ff = w_gate.shape[2]
    tf = d_ff // 2 if (d_ff // 2) % LANES == 0 else d_ff
    n_extra = 3 * N_HEADS

    rc, rs1, rs2 = (jnp.asarray(t) for t in _rope_tables(seq))
    bd = jnp.asarray(_segment_mean_matrices()).astype(BF16)
    ones = jnp.ones((COL_TILE,), F32)

    x2 = x.reshape(n_tok, D_MODEL)
    for layer in range(depth):
        w = w_in[layer]
        gates0 = 7 * ATT_WIDTH + N_HEADS
        qc0 = 6 * ATT_WIDTH + N_HEADS
        w_a = jnp.concatenate([w[:, gates0:], w[:, :6 * ATT_WIDTH], w[:, qc0:gates0]],
                              axis=1).astype(BF16)
        wf3 = jnp.repeat(w[:, 6 * ATT_WIDTH:qc0], 3, axis=1)
        gap = jnp.zeros((D_MODEL, FOX_Q_LANE0 - n_extra), F32)
        tail = jnp.zeros((D_MODEL, LANES - FOX_Q_LANE0 - n_extra), F32)
        wf_hi, wf_lo = _split_hi_lo(jnp.concatenate([wf3, gap, wf3, tail], axis=1))
        bf3 = jnp.repeat(b_forget[layer], 3)
        bf_rep = jnp.concatenate([bf3, gap[0], bf3, tail[0]])[None, :]

        att_scale = HEAD_DIM ** -0.5 * LOG2E
        gains = [ones] * N_COL_TILES
        gains[TILE_QM] = jnp.tile(g_q_moba[layer], N_HEADS) * att_scale
        gains[TILE_KM] = jnp.tile(g_k_moba[layer], N_HEADS)
        gains[TILE_QF] = jnp.tile(g_q_fox[layer], N_HEADS) * att_scale
        gains[TILE_KF] = jnp.tile(g_k_fox[layer], N_HEADS)
        gains[TILE_QC] = jnp.tile(g_q_mem[layer], MEM_HEADS) * (MEM_HEAD_DIM ** -0.5 * LOG2E)
        gains = jnp.stack(gains)[:, None, :]

        bound_m = _score_bound(g_q_moba[layer], g_k_moba[layer])
        bound_f = _score_bound(g_q_fox[layer], g_k_fox[layer])
        shift_m = _lane_row(bound_m * LOG2E, MOBA_SHIFT_LANE, 1)
        shift_f = _lane_row(bound_f * LOG2E, FOX_Q_LANE0, n_extra)

        proj, flog, kmean = _proj_call(x2, g_mix[layer][None, :], w_a, wf_hi, wf_lo, bd, gains,
                                       rc, rs1, rs2, seq, tm=1024)
        kmean = kmean.reshape(batch, seq // MOBA_BLOCK, COL_TILE)
        faug = _cumsum_call(flog, bf_rep, shift_f, batch, seq, tq=256)
        mk, mv = _memkv_call(mem, g_mem[layer][None, :], w_mem_kv[layer].astype(BF16),
                             g_k_mem[layer][None, :])
        y_m = lax.cond(2.0 * bound_m <= MAX_SHIFT_NATS,
                       lambda: _moba_call(proj, kmean, shift_m, batch, seq, True),
                       lambda: _moba_call(proj, kmean, shift_m, batch, seq, False))
        y_f = lax.cond(2.0 * bound_f <= MAX_SHIFT_NATS,
                       lambda: _fox_call(proj, faug, batch, seq, True),
                       lambda: _fox_call(proj, faug, batch, seq, False))
        x1 = _merge_call(x2, proj, y_m, y_f, mk, mv, w_br_moba[layer].astype(BF16),
                         w_br_fox[layer].astype(BF16), w_br_mem[layer].astype(BF16),
                         w_out[layer].astype(BF16), seq, tm=512)
        x2 = _ffn_call(x1, g_ffn[layer][None, :], w_gate[layer].astype(BF16),
                       w_up[layer].astype(BF16), w_down[layer].astype(BF16), tm=512, tf=tf)
    return x2.reshape(batch, seq, D_MODEL)
```

```python
import functools
import math

import numpy as np
import jax
import jax.numpy as jnp
from jax import lax
from jax.experimental import pallas as pl
from jax.experimental.pallas import tpu as pltpu

F32 = jnp.float32
BF16 = jnp.bfloat16

D_MODEL = 1024
HEAD_DIM = 64
N_HEADS = 8
MEM_HEADS = 4
MEM_HEAD_DIM = 128
ATT_WIDTH = 512
MOBA_BLOCK = 256
MOBA_TOPK = 3
ROPE_THETA = 500000.0
ROPE_DIMS = 16
NORM_EPS = 1e-6
NEG = -1e30
LOG2E = math.log2(math.e)

LANES = 128
COL_TILE = 512
N_COL_TILES = 7
PROJ_WIDTH = COL_TILE * N_COL_TILES
TILE_QM, TILE_KM, TILE_QF, TILE_KF, TILE_QC = 0, 1, 3, 4, 6
ATT_TILE = 256
KEY_GROUP = 4
Q_SUB = 2
VMEM_LIMIT = 52 * 1024 * 1024

FOX_Q_LANE0 = 32
MOBA_SHIFT_LANE = LANES - 1
MAX_SHIFT_NATS = 60.0
BOUND_SLACK = 1.02


def _nt(a, b):
    return lax.dot_general(a, b, (((1,), (1,)), ((), ())), preferred_element_type=F32)


def _dot(a, b):
    return jnp.dot(a, b, preferred_element_type=F32)


def _split3(v):
    p1 = v.astype(BF16)
    r1 = v - p1.astype(F32)
    p2 = r1.astype(BF16)
    p3 = (r1 - p2.astype(F32)).astype(BF16)
    return p1, p2, p3


def _rms_normed(x, gain_row):
    ms = jnp.mean(x * x, axis=-1, keepdims=True)
    return (x * lax.rsqrt(ms + NORM_EPS) * gain_row).astype(BF16)


def _proj_kernel(x_ref, gmix_ref, w_ref, wf_ref, bd_ref, gain_ref, rc_ref, rs1_ref, rs2_ref,
                 p_ref, flog_ref, kmean_ref):
    tm = x_ref.shape[0]
    h = _rms_normed(x_ref[...], gmix_ref[...])
    flog_ref[...] = _dot(h, wf_ref[...])

    def head_normed(t, k):
        seg = 1 if k == TILE_QC else 0
        ms = _dot((t * t).astype(BF16), bd_ref[seg])
        return t * lax.rsqrt(ms + NORM_EPS) * gain_ref[k]

    def rotated(y):
        rc, rs1, rs2 = rc_ref[...], rs1_ref[...], rs2_ref[...]
        parts = []
        for g in range(COL_TILE // LANES):
            yg = y[:, g * LANES:(g + 1) * LANES]
            up = pltpu.roll(yg, LANES - ROPE_DIMS // 2, 1)
            dn = pltpu.roll(yg, ROPE_DIMS // 2, 1)
            parts.append(yg * rc + up * rs1 + dn * rs2)
        return jnp.concatenate(parts, axis=1)

    for k in range(N_COL_TILES):
        cols = slice(k * COL_TILE, (k + 1) * COL_TILE)
        t = _dot(h, w_ref[:, cols])
        if k in (TILE_QM, TILE_KM):
            t = rotated(head_normed(t, k))
        elif k in (TILE_QF, TILE_KF, TILE_QC):
            t = head_normed(t, k)
        p_ref[:, cols] = t.astype(BF16)
        if k == TILE_KM:
            for r in range(tm // MOBA_BLOCK):
                blk = t[r * MOBA_BLOCK:(r + 1) * MOBA_BLOCK, :]
                kmean_ref[0, pl.ds(r, 1), :] = jnp.mean(blk, axis=0, keepdims=True)


def _proj_call(x2, gmix, w_qkv, wf, bd, gains, rc, rs1, rs2, seq, tm):
    n_tok = x2.shape[0]
    seq_tiles = seq // tm
    const2 = lambda i: (0, 0)
    const3 = lambda i: (0, 0, 0)
    return pl.pallas_call(
        _proj_kernel,
        grid=(n_tok // tm,),
        in_specs=[
            pl.BlockSpec((tm, D_MODEL), lambda i: (i, 0)),
            pl.BlockSpec((1, D_MODEL), const2),
            pl.BlockSpec((D_MODEL, PROJ_WIDTH), const2),
            pl.BlockSpec((D_MODEL, LANES), const2),
            pl.BlockSpec((2, COL_TILE, COL_TILE), const3),
            pl.BlockSpec((N_COL_TILES, 1, COL_TILE), const3),
            pl.BlockSpec((tm, LANES), lambda i: (i % seq_tiles, 0)),
            pl.BlockSpec((tm, LANES), lambda i: (i % seq_tiles, 0)),
            pl.BlockSpec((tm, LANES), lambda i: (i % seq_tiles, 0)),
        ],
        out_specs=[
            pl.BlockSpec((tm, PROJ_WIDTH), lambda i: (i, 0)),
            pl.BlockSpec((tm, LANES), lambda i: (i, 0)),
            pl.BlockSpec((1, tm // MOBA_BLOCK, COL_TILE), lambda i: (i, 0, 0)),
        ],
        out_shape=[
            jax.ShapeDtypeStruct((n_tok, PROJ_WIDTH), BF16),
            jax.ShapeDtypeStruct((n_tok, LANES), F32),
            jax.ShapeDtypeStruct((n_tok // tm, tm // MOBA_BLOCK, COL_TILE), F32),
        ],
        compiler_params=pltpu.CompilerParams(
            dimension_semantics=("parallel",), vmem_limit_bytes=VMEM_LIMIT),
        name="proj",
    )(x2, gmix, w_qkv, wf, bd, gains, rc, rs1, rs2)


def _cumsum_kernel(flog_ref, bf_ref, shift_ref, faug_ref, carry_ref):
    s = pl.program_id(1)
    tq = flog_ref.shape[0]

    @pl.when(s == 0)
    def _():
        carry_ref[...] = jnp.zeros_like(carry_ref)

    z = flog_ref[...] + bf_ref[...]
    lf = jnp.minimum(z, 0.0) - jnp.log1p(jnp.exp(-jnp.abs(z)))
    row = lax.broadcasted_iota(jnp.int32, (tq, tq), 0)
    col = lax.broadcasted_iota(jnp.int32, (tq, tq), 1)
    tri = jnp.where(col <= row, 1.0, 0.0).astype(BF16)
    p1, p2, p3 = _split3(lf)
    cum = _dot(tri, p1) + _dot(tri, p2) + _dot(tri, p3) + carry_ref[...]
    carry_ref[...] = cum[tq - 1:tq, :]

    lane = lax.broadcasted_iota(jnp.int32, cum.shape, 1)
    key_side = lane < 3 * N_HEADS
    query_side = (lane >= FOX_Q_LANE0) & (lane < FOX_Q_LANE0 + 3 * N_HEADS)
    cum2 = cum * LOG2E
    val = jnp.where(key_side, -cum2, cum2 - shift_ref[...])
    c1, c2, c3 = _split3(val)
    piece = jnp.where(key_side, lane, lane - FOX_Q_LANE0) % 3
    sel = jnp.where(piece == 0, c1, jnp.where(piece == 1, c2, c3))
    faug_ref[...] = jnp.where(key_side | query_side, sel, jnp.zeros_like(sel))


def _cumsum_call(flog, bf_rep, shift_row, batch, seq, tq):
    return pl.pallas_call(
        _cumsum_kernel,
        grid=(batch, seq // tq),
        in_specs=[
            pl.BlockSpec((tq, LANES), lambda b, s: (b * (seq // tq) + s, 0)),
            pl.BlockSpec((1, LANES), lambda b, s: (0, 0)),
            pl.BlockSpec((1, LANES), lambda b, s: (0, 0)),
        ],
        out_specs=pl.BlockSpec((tq, LANES), lambda b, s: (b * (seq // tq) + s, 0)),
        out_shape=jax.ShapeDtypeStruct((batch * seq, LANES), BF16),
        scratch_shapes=[pltpu.VMEM((1, LANES), F32)],
        compiler_params=pltpu.CompilerParams(dimension_semantics=("parallel", "arbitrary")),
        name="cumsum",
    )(flog, bf_rep, shift_row)


def _memkv_kernel(mem_ref, gmem_ref, w_ref, gk_ref, mk_ref, mv_ref):
    hm = _rms_normed(mem_ref[0], gmem_ref[...])
    kv = _dot(hm, w_ref[...])
    ks = []
    for h in range(MEM_HEADS):
        kh = kv[:, h * MEM_HEAD_DIM:(h + 1) * MEM_HEAD_DIM]
        msk = jnp.mean(kh * kh, axis=-1, keepdims=True)
        ks.append(kh * lax.rsqrt(msk + NORM_EPS) * gk_ref[...])
    mk_ref[0] = jnp.concatenate(ks, axis=1).astype(BF16)
    mv_ref[0] = kv[:, ATT_WIDTH:].astype(BF16)


def _memkv_call(mem, gmem, w_kv, gk):
    batch, mlen, _ = mem.shape
    return pl.pallas_call(
        _memkv_kernel,
        grid=(batch,),
        in_specs=[
            pl.BlockSpec((1, mlen, D_MODEL), lambda b: (b, 0, 0)),
            pl.BlockSpec((1, D_MODEL), lambda b: (0, 0)),
            pl.BlockSpec((D_MODEL, 2 * ATT_WIDTH), lambda b: (0, 0)),
            pl.BlockSpec((1, MEM_HEAD_DIM), lambda b: (0, 0)),
        ],
        out_specs=[
            pl.BlockSpec((1, mlen, ATT_WIDTH), lambda b: (b, 0, 0)),
            pl.BlockSpec((1, mlen, ATT_WIDTH), lambda b: (b, 0, 0)),
        ],
        out_shape=[jax.ShapeDtypeStruct((batch, mlen, ATT_WIDTH), BF16)] * 2,
        compiler_params=pltpu.CompilerParams(dimension_semantics=("parallel",)),
        name="memkv",
    )(mem, gmem, w_kv, gk)


def _lane_iota():
    return lax.broadcasted_iota(jnp.int32, (ATT_TILE, LANES), 1)


def _lane_range(first, count, dtype):
    lane = _lane_iota()
    return jnp.where((lane >= first) & (lane < first + count), 1.0, 0.0).astype(dtype)


def _head_lanes(c, dtype):
    return _lane_range(c * HEAD_DIM, HEAD_DIM, dtype)


def _fill_values(v_ref, vaug_ref):
    n_blocks = v_ref.shape[0] // ATT_TILE

    def fill(j, _):
        rows = pl.ds(pl.multiple_of(j * ATT_TILE, ATT_TILE), ATT_TILE)
        v = v_ref[rows, :].astype(F32)
        for c in range(2):
            own = _head_lanes(c, F32)
            vaug_ref[c, rows, :] = (v * own + (1.0 - own)).astype(BF16)
        return 0

    lax.fori_loop(0, n_blocks, fill, 0)


def _normalised(acc_ref, t):
    res = []
    for c in range(2):
        acc = acc_ref[2 * t + c]
        res.append(acc / pltpu.roll(acc, HEAD_DIM, 1))
    return jnp.where(_lane_iota() < HEAD_DIM, res[0], res[1])


def _causal_masks(qi, n_full):
    span = KEY_GROUP * ATT_TILE
    diff = (lax.broadcasted_iota(jnp.int32, (ATT_TILE, span), 1)
            - lax.broadcasted_iota(jnp.int32, (ATT_TILE, span), 0))
    first = (qi * Q_SUB - n_full * KEY_GROUP) * ATT_TILE
    return [diff <= first + t * ATT_TILE for t in range(Q_SUB)]


def _attend_shifted(lhs, kaug_ref, vaug_ref, acc_ref, s_ref, qi):
    span = KEY_GROUP * ATT_TILE
    n_chain = len(lhs)

    def group_rows(g):
        return pl.ds(pl.multiple_of(g * span, span), span)

    def scores(g, slot):
        kb = kaug_ref[group_rows(g), :]
        for i in range(n_chain):
            s_ref[slot, i] = _nt(lhs[i], kb)

    def consume(g, slot, masks):
        rows = group_rows(g)
        for i in range(n_chain):
            s = s_ref[slot, i]
            if masks is not None:
                s = jnp.where(masks[i // 2], s, NEG)
            acc_ref[i] += _dot(jnp.exp2(s).astype(BF16), vaug_ref[i % 2, rows, :])

    acc_ref[...] = jnp.zeros_like(acc_ref)
    n_full = (qi * Q_SUB) // KEY_GROUP
    scores(0, 0)

    def body(g, _):
        slot = lax.rem(g, 2)
        consume(g, slot, None)
        scores(g + 1, 1 - slot)
        return 0

    lax.fori_loop(0, n_full, body, 0)
    consume(n_full, lax.rem(n_full, 2), _causal_masks(qi, n_full))
    return [_normalised(acc_ref, t) for t in range(Q_SUB)]


def _attend_running_max(lhs, kaug_ref, vaug_ref, acc_ref, s_ref, qi):
    span = KEY_GROUP * ATT_TILE
    n_chain = len(lhs)

    def step(i, g, m, masks):
        rows = pl.ds(pl.multiple_of(g * span, span), span)
        s = _nt(lhs[i], kaug_ref[rows, :])
        if masks is not None:
            s = jnp.where(masks[i // 2], s, NEG)
        m_new = jnp.maximum(m, jnp.max(s, axis=1, keepdims=True))
        p = jnp.exp2(s - m_new).astype(BF16)
        acc_ref[i] = jnp.exp2(m - m_new) * acc_ref[i] + _dot(p, vaug_ref[i % 2, rows, :])
        return m_new

    acc_ref[...] = jnp.zeros_like(acc_ref)
    n_full = (qi * Q_SUB) // KEY_GROUP
    m0 = jnp.full((ATT_TILE, 1), -jnp.inf, F32)
    ms = lax.fori_loop(
        0, n_full, lambda g, ms: tuple(step(i, g, ms[i], None) for i in range(n_chain)),
        (m0,) * n_chain)
    masks = _causal_masks(qi, n_full)
    for i in range(n_chain):
        step(i, n_full, ms[i], masks)
    return [_normalised(acc_ref, t) for t in range(Q_SUB)]


def _attend(lhs, kaug_ref, vaug_ref, acc_ref, s_ref, qi, bounded):
    fn = _attend_shifted if bounded else _attend_running_max
    return fn(lhs, kaug_ref, vaug_ref, acc_ref, s_ref, qi)


def _attention_scratch(seq):
    n_chain = 2 * Q_SUB
    return [pltpu.VMEM((seq, 2 * LANES), BF16),
            pltpu.VMEM((2, seq, LANES), BF16),
            pltpu.VMEM((n_chain, ATT_TILE, LANES), F32),
            pltpu.VMEM((2, n_chain, ATT_TILE, KEY_GROUP * ATT_TILE), F32)]


def _sub_rows(t):
    return slice(t * ATT_TILE, (t + 1) * ATT_TILE)


def _moba_kernel(q_ref, k_ref, v_ref, km_ref, shift_ref, o_ref, kaug_ref, vaug_ref, acc_ref,
                 s_ref, *, bounded):
    qi = pl.program_id(2)
    seq = k_ref.shape[0]
    n_blocks = seq // MOBA_BLOCK
    lane = _lane_iota()

    @pl.when(qi == 0)
    def _():
        kaug_ref[:, 0:LANES] = k_ref[...]

        def fill(j, _):
            rows = pl.ds(pl.multiple_of(j * MOBA_BLOCK, MOBA_BLOCK), MOBA_BLOCK)
            onehot = (lane == j) | (lane == MOBA_SHIFT_LANE)
            kaug_ref[rows, LANES:2 * LANES] = jnp.where(onehot, 1.0, 0.0).astype(BF16)
            return 0

        lax.fori_loop(0, n_blocks, fill, 0)
        _fill_values(v_ref, vaug_ref)

    km = km_ref[0]
    km = jnp.concatenate([km, jnp.zeros((LANES - n_blocks, LANES), F32)], axis=0)
    km_hi = km.astype(BF16)
    km_lo = (km - km_hi.astype(F32)).astype(BF16)
    lane_f = lane.astype(F32)
    shift = shift_ref[...] if bounded else 0.0

    lhs = []
    for t in range(Q_SUB):
        own = qi * Q_SUB + t
        q2 = q_ref[_sub_rows(t), :]
        for c in range(2):
            qc = q2 * _head_lanes(c, BF16)
            gate = _nt(qc, km_hi) + _nt(qc, km_lo)
            g = jnp.where(lane < own, gate, -jnp.inf)
            sel = lane == own
            for _ in range(MOBA_TOPK):
                mx = jnp.max(g, axis=1, keepdims=True)
                first = jnp.min(jnp.where(g == mx, lane_f, float(LANES)), axis=1, keepdims=True)
                pick = (lane_f == first) & (mx > -jnp.inf)
                sel = sel | pick
                g = jnp.where(pick, -jnp.inf, g)
            bias = jnp.where(sel, 0.0, NEG)
            bias = jnp.where(lane == MOBA_SHIFT_LANE, -shift, bias)
            lhs.append(jnp.concatenate([qc, bias.astype(BF16)], axis=1))

    outs = _attend(lhs, kaug_ref, vaug_ref, acc_ref, s_ref, qi, bounded)
    for t in range(Q_SUB):
        o_ref[_sub_rows(t), :] = outs[t].astype(BF16)


def _moba_call(proj, kmean, shift_row, batch, seq, bounded):
    tq = Q_SUB * ATT_TILE
    n_q = seq // tq
    n_pairs = N_HEADS // 2
    q0 = (TILE_QM * COL_TILE) // LANES
    k0 = (TILE_KM * COL_TILE) // LANES
    v0 = k0 + COL_TILE // LANES
    return pl.pallas_call(
        functools.partial(_moba_kernel, bounded=bounded),
        grid=(batch, n_pairs, n_q),
        in_specs=[
            pl.BlockSpec((tq, LANES), lambda b, p, i: (b * n_q + i, q0 + p)),
            pl.BlockSpec((seq, LANES), lambda b, p, i: (b, k0 + p)),
            pl.BlockSpec((seq, LANES), lambda b, p, i: (b, v0 + p)),
            pl.BlockSpec((1, seq // MOBA_BLOCK, LANES), lambda b, p, i: (b, 0, p)),
            pl.BlockSpec((1, LANES), lambda b, p, i: (0, 0)),
        ],
        out_specs=pl.BlockSpec((tq, LANES), lambda b, p, i: (b * n_q + i, p)),
        out_shape=jax.ShapeDtypeStruct((batch * seq, ATT_WIDTH), BF16),
        scratch_shapes=_attention_scratch(seq),
        compiler_params=pltpu.CompilerParams(
            dimension_semantics=("parallel", "parallel", "arbitrary"),
            vmem_limit_bytes=VMEM_LIMIT),
        name="moba_bounded" if bounded else "moba_general",
    )(proj, proj, proj, kmean, shift_row)


def _fox_kernel(q_ref, k_ref, v_ref, faug_ref, fq_ref, o_ref, kaug_ref, vaug_ref, acc_ref,
                s_ref, *, bounded):
    pair = pl.program_id(1)
    qi = pl.program_id(2)
    n_extra = 3 * N_HEADS

    @pl.when(qi == 0)
    def _():
        seq = k_ref.shape[0]
        kaug_ref[:, 0:LANES] = k_ref[...]

        def fill(j, _):
            rows = pl.ds(pl.multiple_of(j * ATT_TILE, ATT_TILE), ATT_TILE)
            lane = _lane_iota()
            ones = (lane >= FOX_Q_LANE0) & (lane < FOX_Q_LANE0 + n_extra)
            f = faug_ref[rows, :].astype(F32)
            kaug_ref[rows, LANES:2 * LANES] = (
                f * (lane < n_extra).astype(F32) + ones.astype(F32)).astype(BF16)
            return 0

        lax.fori_loop(0, seq // ATT_TILE, fill, 0)
        _fill_values(v_ref, vaug_ref)

    lhs = []
    for t in range(Q_SUB):
        q2 = q_ref[_sub_rows(t), :]
        fq = fq_ref[_sub_rows(t), :]
        for c in range(2):
            qc = q2 * _head_lanes(c, BF16)
            k_first = 3 * (2 * pair + c)
            extra = _lane_range(k_first, 3, BF16)
            if bounded:
                extra = extra + fq * _lane_range(FOX_Q_LANE0 + k_first, 3, BF16)
            lhs.append(jnp.concatenate([qc, extra], axis=1))

    outs = _attend(lhs, kaug_ref, vaug_ref, acc_ref, s_ref, qi, bounded)
    for t in range(Q_SUB):
        o_ref[_sub_rows(t), :] = outs[t].astype(BF16)


def _fox_call(proj, faug, batch, seq, bounded):
    tq = Q_SUB * ATT_TILE
    n_q = seq // tq
    n_pairs = N_HEADS // 2
    q0 = (TILE_QF * COL_TILE) // LANES
    k0 = (TILE_KF * COL_TILE) // LANES
    v0 = k0 + COL_TILE // LANES
    return pl.pallas_call(
        functools.partial(_fox_kernel, bounded=bounded),
        grid=(batch, n_pairs, n_q),
        in_specs=[
            pl.BlockSpec((tq, LANES), lambda b, p, i: (b * n_q + i, q0 + p)),
            pl.BlockSpec((seq, LANES), lambda b, p, i: (b, k0 + p)),
            pl.BlockSpec((seq, LANES), lambda b, p, i: (b, v0 + p)),
            pl.BlockSpec((seq, LANES), lambda b, p, i: (b, 0)),
            pl.BlockSpec((tq, LANES), lambda b, p, i: (b * n_q + i, 0)),
        ],
        out_specs=pl.BlockSpec((tq, LANES), lambda b, p, i: (b * n_q + i, p)),
        out_shape=jax.ShapeDtypeStruct((batch * seq, ATT_WIDTH), BF16),
        scratch_shapes=_attention_scratch(seq),
        compiler_params=pltpu.CompilerParams(
            dimension_semantics=("parallel", "parallel", "arbitrary"),
            vmem_limit_bytes=VMEM_LIMIT),
        name="fox_bounded" if bounded else "fox_general",
    )(proj, proj, proj, faug, faug)


def _merge_kernel(x_ref, gmix_ref, qc_ref, ym_ref, yf_ref, mk_ref, mv_ref,
                  wg_ref, wbm_ref, wbf_ref, wbc_ref, wo_ref, o_ref):
    x = x_ref[...]
    h = _rms_normed(x, gmix_ref[...])
    qc = qc_ref[...]
    mk = mk_ref[0]
    mv = mv_ref[0]
    ys = []
    for hd in range(MEM_HEADS):
        cols = slice(hd * MEM_HEAD_DIM, (hd + 1) * MEM_HEAD_DIM)
        s = _nt(qc[:, cols], mk[:, cols])
        p = jnp.exp2(s - jnp.max(s, axis=1, keepdims=True))
        y = _dot(p.astype(BF16), mv[:, cols]) / jnp.sum(p, axis=1, keepdims=True)
        ys.append(y)
    yc = jnp.concatenate(ys, axis=1).astype(BF16)
    merged = None
    for i, (y, wb_ref) in enumerate(((ym_ref[...], wbm_ref), (yf_ref[...], wbf_ref),
                                     (yc, wbc_ref))):
        gate = jax.nn.sigmoid(_dot(h, wg_ref[:, i * D_MODEL:(i + 1) * D_MODEL]))
        term = gate * _dot(y, wb_ref[...])
        merged = term if merged is None else merged + term
    o_ref[...] = x + _dot(merged.astype(BF16), wo_ref[...])


def _merge_call(x2, gmix, proj, y_m, y_f, mk, mv, wg, wbm, wbf, wbc, wo, seq, tm):
    n_tok = x2.shape[0]
    mlen = mk.shape[1]
    seq_tiles = seq // tm
    const = lambda i: (0, 0)
    return pl.pallas_call(
        _merge_kernel,
        grid=(n_tok // tm,),
        in_specs=[
            pl.BlockSpec((tm, D_MODEL), lambda i: (i, 0)),
            pl.BlockSpec((1, D_MODEL), const),
            pl.BlockSpec((tm, ATT_WIDTH), lambda i: (i, TILE_QC)),
            pl.BlockSpec((tm, ATT_WIDTH), lambda i: (i, 0)),
            pl.BlockSpec((tm, ATT_WIDTH), lambda i: (i, 0)),
            pl.BlockSpec((1, mlen, ATT_WIDTH), lambda i: (i // seq_tiles, 0, 0)),
            pl.BlockSpec((1, mlen, ATT_WIDTH), lambda i: (i // seq_tiles, 0, 0)),
            pl.BlockSpec((D_MODEL, 3 * D_MODEL), const),
            pl.BlockSpec((ATT_WIDTH, D_MODEL), const),
            pl.BlockSpec((ATT_WIDTH, D_MODEL), const),
            pl.BlockSpec((ATT_WIDTH, D_MODEL), const),
            pl.BlockSpec((D_MODEL, D_MODEL), const),
        ],
        out_specs=pl.BlockSpec((tm, D_MODEL), lambda i: (i, 0)),
        out_shape=jax.ShapeDtypeStruct((n_tok, D_MODEL), F32),
        compiler_params=pltpu.CompilerParams(
            dimension_semantics=("parallel",), vmem_limit_bytes=VMEM_LIMIT),
        name="merge",
    )(x2, gmix, proj, y_m, y_f, mk, mv, wg, wbm, wbf, wbc, wo)


def _ffn_kernel(x_ref, g_ref, wg_ref, wu_ref, wd_ref, o_ref, h_ref, acc_ref):
    j = pl.program_id(1)

    @pl.when(j == 0)
    def _():
        h_ref[...] = _rms_normed(x_ref[...], g_ref[...])
        acc_ref[...] = jnp.zeros_like(acc_ref)

    h = h_ref[...]
    g = _dot(h, wg_ref[...])
    u = _dot(h, wu_ref[...])
    a = (g * jax.nn.sigmoid(g) * u).astype(BF16)
    acc_ref[...] += _dot(a, wd_ref[...])

    @pl.when(j == pl.num_programs(1) - 1)
    def _():
        o_ref[...] = x_ref[...] + acc_ref[...]


def _ffn_call(x1, g_ffn, wg, wu, wd, tm, tf):
    n_tok = x1.shape[0]
    d_ff = wg.shape[1]
    return pl.pallas_call(
        _ffn_kernel,
        grid=(n_tok // tm, d_ff // tf),
        in_specs=[
            pl.BlockSpec((tm, D_MODEL), lambda i, j: (i, 0)),
            pl.BlockSpec((1, D_MODEL), lambda i, j: (0, 0)),
            pl.BlockSpec((D_MODEL, tf), lambda i, j: (0, j)),
            pl.BlockSpec((D_MODEL, tf), lambda i, j: (0, j)),
            pl.BlockSpec((tf, D_MODEL), lambda i, j: (j, 0)),
        ],
        out_specs=pl.BlockSpec((tm, D_MODEL), lambda i, j: (i, 0)),
        out_shape=jax.ShapeDtypeStruct((n_tok, D_MODEL), F32),
        scratch_shapes=[pltpu.VMEM((tm, D_MODEL), BF16), pltpu.VMEM((tm, D_MODEL), F32)],
        compiler_params=pltpu.CompilerParams(
            dimension_semantics=("parallel", "arbitrary"), vmem_limit_bytes=VMEM_LIMIT),
        name="ffn",
    )(x1, g_ffn, wg, wu, wd)


@functools.lru_cache(maxsize=None)
def _rope_tables(seq):
    half = ROPE_DIMS // 2
    inv_freq = 1.0 / (ROPE_THETA ** (np.arange(half, dtype=np.float64) * 2.0 / ROPE_DIMS))
    ang = np.arange(seq, dtype=np.float64)[:, None] * inv_freq[None, :]
    cos, sin = np.cos(ang), np.sin(ang)
    d = np.arange(LANES) % HEAD_DIM
    rc = np.where(d[None, :] < ROPE_DIMS, cos[:, d % half], 1.0)
    rs1 = np.where(d[None, :] < half, -sin[:, d % half], 0.0)
    rs2 = np.where((d[None, :] >= half) & (d[None, :] < ROPE_DIMS), sin[:, d % half], 0.0)
    return tuple(np.asarray(t, np.float32) for t in (rc, rs1, rs2))


@functools.lru_cache(maxsize=None)
def _segment_mean_matrices():
    def blockdiag(width):
        return np.kron(np.eye(COL_TILE // width), np.ones((width, width))) / width
    return np.stack([blockdiag(HEAD_DIM), blockdiag(MEM_HEAD_DIM)]).astype(np.float32)


def _score_bound(g_q, g_k):
    return (BOUND_SLACK * HEAD_DIM ** 0.5 * jnp.max(jnp.abs(g_q)) * jnp.max(jnp.abs(g_k)))


def _lane_row(value, first, count):
    lane = jnp.arange(LANES)
    return jnp.where((lane >= first) & (lane < first + count), value, 0.0).astype(F32)[None, :]


def kernel(x, mem, g_mix, w_in, b_forget, g_q_moba, g_k_moba, g_q_fox, g_k_fox, g_q_mem, g_k_mem,
           g_mem, w_mem_kv, w_br_moba, w_br_fox, w_br_mem, w_out, g_ffn, w_gate, w_up, w_down):
    batch, seq, _ = x.shape
    depth = g_mix.shape[0]
    n_tok = batch * seq
    assert seq % (KEY_GROUP * ATT_TILE) == 0 and seq // MOBA_BLOCK < MOBA_SHIFT_LANE
    assert KEY_GROUP % Q_SUB == 0
    d_ff = w_gate.shape[2]
    tf = d_ff // 2 if (d_ff // 2) % LANES == 0 else d_ff
    n_extra = 3 * N_HEADS

    rc, rs1, rs2 = (jnp.asarray(t) for t in _rope_tables(seq))
    bd = jnp.asarray(_segment_mean_matrices()).astype(BF16)
    ones = jnp.ones((COL_TILE,), F32)

    x2 = x.reshape(n_tok, D_MODEL)
    for layer in range(depth):
        w = w_in[layer]
        gates0 = 7 * ATT_WIDTH + N_HEADS
        qc0 = 6 * ATT_WIDTH + N_HEADS
        w_qkv = jnp.concatenate([w[:, :6 * ATT_WIDTH], w[:, qc0:gates0]], axis=1).astype(BF16)
        w_gates = w[:, gates0:].astype(BF16)
        wf3 = jnp.repeat(w[:, 6 * ATT_WIDTH:qc0], 3, axis=1)
        gap = jnp.zeros((D_MODEL, FOX_Q_LANE0 - n_extra), F32)
        tail = jnp.zeros((D_MODEL, LANES - FOX_Q_LANE0 - n_extra), F32)
        wf = jnp.concatenate([wf3, gap, wf3, tail], axis=1).astype(BF16)
        bf3 = jnp.repeat(b_forget[layer], 3)
        bf_rep = jnp.concatenate([bf3, gap[0], bf3, tail[0]])[None, :]

        att_scale = HEAD_DIM ** -0.5 * LOG2E
        gains = [ones] * N_COL_TILES
        gains[TILE_QM] = jnp.tile(g_q_moba[layer], N_HEADS) * att_scale
        gains[TILE_KM] = jnp.tile(g_k_moba[layer], N_HEADS)
        gains[TILE_QF] = jnp.tile(g_q_fox[layer], N_HEADS) * att_scale
        gains[TILE_KF] = jnp.tile(g_k_fox[layer], N_HEADS)
        gains[TILE_QC] = jnp.tile(g_q_mem[layer], MEM_HEADS) * (MEM_HEAD_DIM ** -0.5 * LOG2E)
        gains = jnp.stack(gains)[:, None, :]

        bound_m = _score_bound(g_q_moba[layer], g_k_moba[layer])
        bound_f = _score_bound(g_q_fox[layer], g_k_fox[layer])
        shift_m = _lane_row(bound_m * LOG2E, MOBA_SHIFT_LANE, 1)
        shift_f = _lane_row(bound_f * LOG2E, FOX_Q_LANE0, n_extra)

        proj, flog, kmean = _proj_call(x2, g_mix[layer][None, :], w_qkv, wf, bd, gains,
                                       rc, rs1, rs2, seq, tm=512)
        kmean = kmean.reshape(batch, seq // MOBA_BLOCK, COL_TILE)
        faug = _cumsum_call(flog, bf_rep, shift_f, batch, seq, tq=256)
        mk, mv = _memkv_call(mem, g_mem[layer][None, :], w_mem_kv[layer].astype(BF16),
                             g_k_mem[layer][None, :])
        y_m = lax.cond(2.0 * bound_m <= MAX_SHIFT_NATS,
                       lambda: _moba_call(proj, kmean, shift_m, batch, seq, True),
                       lambda: _moba_call(proj, kmean, shift_m, batch, seq, False))
        y_f = lax.cond(2.0 * bound_f <= MAX_SHIFT_NATS,
                       lambda: _fox_call(proj, faug, batch, seq, True),
                       lambda: _fox_call(proj, faug, batch, seq, False))
        x1 = _merge_call(x2, g_mix[layer][None, :], proj, y_m, y_f, mk, mv, w_gates,
                         w_br_moba[layer].astype(BF16), w_br_fox[layer].astype(BF16),
                         w_br_mem[layer].astype(BF16), w_out[layer].astype(BF16), seq, tm=512)
        x2 = _ffn_call(x1, g_ffn[layer][None, :], w_gate[layer].astype(BF16),
                       w_up[layer].astype(BF16), w_down[layer].astype(BF16), tm=512, tf=tf)
    return x2.reshape(batch, seq, D_MODEL)
```

```python
import functools
import math

import numpy as np
import jax
import jax.numpy as jnp
from jax import lax
from jax.experimental import pallas as pl
from jax.experimental.pallas import tpu as pltpu

F32 = jnp.float32
BF16 = jnp.bfloat16

D_MODEL = 1024
HEAD_DIM = 64
N_HEADS = 8
MEM_HEADS = 4
MEM_HEAD_DIM = 128
ATT_WIDTH = 512
MOBA_BLOCK = 256
MOBA_TOPK = 3
ROPE_THETA = 500000.0
ROPE_DIMS = 16
NORM_EPS = 1e-6
NEG = -1e30
LOG2E = math.log2(math.e)

LANES = 128
COL_TILE = 512
N_COL_TILES = 7
PROJ_WIDTH = COL_TILE * N_COL_TILES
TILE_QM, TILE_KM, TILE_QF, TILE_KF, TILE_QC = 0, 1, 3, 4, 6
ATT_TILE = 256
KEY_GROUP = 4
Q_SUB = 4
FFN_CHUNK = 1024
VMEM_LIMIT = 52 * 1024 * 1024

FOX_Q_LANE0 = 32
MOBA_SHIFT_LANE = LANES - 1
MAX_SHIFT_NATS = 60.0
BOUND_SLACK = 1.02


def _nt(a, b):
    return lax.dot_general(a, b, (((1,), (1,)), ((), ())), preferred_element_type=F32)


def _dot(a, b):
    return jnp.dot(a, b, preferred_element_type=F32)


def _split3(v):
    p1 = v.astype(BF16)
    r1 = v - p1.astype(F32)
    p2 = r1.astype(BF16)
    p3 = (r1 - p2.astype(F32)).astype(BF16)
    return p1, p2, p3


def _rms_normed(x, gain_row):
    ms = jnp.mean(x * x, axis=-1, keepdims=True)
    return (x * lax.rsqrt(ms + NORM_EPS) * gain_row).astype(BF16)


def _proj_kernel(x_ref, gmix_ref, w_ref, wf_ref, bd_ref, gain_ref, rc_ref, rs1_ref, rs2_ref,
                 p_ref, flog_ref, kmean_ref):
    tm = x_ref.shape[0]
    h = _rms_normed(x_ref[...], gmix_ref[...])
    flog_ref[...] = _dot(h, wf_ref[...])

    def head_normed(t, k):
        seg = 1 if k == TILE_QC else 0
        ms = _dot((t * t).astype(BF16), bd_ref[seg])
        return t * lax.rsqrt(ms + NORM_EPS) * gain_ref[k]

    def rotated(y):
        rc, rs1, rs2 = rc_ref[...], rs1_ref[...], rs2_ref[...]
        parts = []
        for g in range(COL_TILE // LANES):
            yg = y[:, g * LANES:(g + 1) * LANES]
            up = pltpu.roll(yg, LANES - ROPE_DIMS // 2, 1)
            dn = pltpu.roll(yg, ROPE_DIMS // 2, 1)
            parts.append(yg * rc + up * rs1 + dn * rs2)
        return jnp.concatenate(parts, axis=1)

    for k in range(N_COL_TILES):
        cols = slice(k * COL_TILE, (k + 1) * COL_TILE)
        t = _dot(h, w_ref[:, cols])
        if k in (TILE_QM, TILE_KM):
            t = rotated(head_normed(t, k))
        elif k in (TILE_QF, TILE_KF, TILE_QC):
            t = head_normed(t, k)
        p_ref[:, cols] = t.astype(BF16)
        if k == TILE_KM:
            for r in range(tm // MOBA_BLOCK):
                blk = t[r * MOBA_BLOCK:(r + 1) * MOBA_BLOCK, :]
                kmean_ref[0, pl.ds(r, 1), :] = jnp.mean(blk, axis=0, keepdims=True)


def _proj_call(x2, gmix, w_qkv, wf, bd, gains, rc, rs1, rs2, seq, tm):
    n_tok = x2.shape[0]
    seq_tiles = seq // tm
    const2 = lambda i: (0, 0)
    const3 = lambda i: (0, 0, 0)
    return pl.pallas_call(
        _proj_kernel,
        grid=(n_tok // tm,),
        in_specs=[
            pl.BlockSpec((tm, D_MODEL), lambda i: (i, 0)),
            pl.BlockSpec((1, D_MODEL), const2),
            pl.BlockSpec((D_MODEL, PROJ_WIDTH), const2),
            pl.BlockSpec((D_MODEL, LANES), const2),
            pl.BlockSpec((2, COL_TILE, COL_TILE), const3),
            pl.BlockSpec((N_COL_TILES, 1, COL_TILE), const3),
            pl.BlockSpec((tm, LANES), lambda i: (i % seq_tiles, 0)),
            pl.BlockSpec((tm, LANES), lambda i: (i % seq_tiles, 0)),
            pl.BlockSpec((tm, LANES), lambda i: (i % seq_tiles, 0)),
        ],
        out_specs=[
            pl.BlockSpec((tm, PROJ_WIDTH), lambda i: (i, 0)),
            pl.BlockSpec((tm, LANES), lambda i: (i, 0)),
            pl.BlockSpec((1, tm // MOBA_BLOCK, COL_TILE), lambda i: (i, 0, 0)),
        ],
        out_shape=[
            jax.ShapeDtypeStruct((n_tok, PROJ_WIDTH), BF16),
            jax.ShapeDtypeStruct((n_tok, LANES), F32),
            jax.ShapeDtypeStruct((n_tok // tm, tm // MOBA_BLOCK, COL_TILE), F32),
        ],
        compiler_params=pltpu.CompilerParams(
            dimension_semantics=("parallel",), vmem_limit_bytes=VMEM_LIMIT),
        name="proj",
    )(x2, gmix, w_qkv, wf, bd, gains, rc, rs1, rs2)


def _cumsum_kernel(flog_ref, bf_ref, shift_ref, faug_ref, carry_ref):
    s = pl.program_id(1)
    tq = flog_ref.shape[0]

    @pl.when(s == 0)
    def _():
        carry_ref[...] = jnp.zeros_like(carry_ref)

    z = flog_ref[...] + bf_ref[...]
    lf = jnp.minimum(z, 0.0) - jnp.log1p(jnp.exp(-jnp.abs(z)))
    row = lax.broadcasted_iota(jnp.int32, (tq, tq), 0)
    col = lax.broadcasted_iota(jnp.int32, (tq, tq), 1)
    tri = jnp.where(col <= row, 1.0, 0.0).astype(BF16)
    p1, p2, p3 = _split3(lf)
    cum = _dot(tri, p1) + _dot(tri, p2) + _dot(tri, p3) + carry_ref[...]
    carry_ref[...] = cum[tq - 1:tq, :]

    lane = lax.broadcasted_iota(jnp.int32, cum.shape, 1)
    key_side = lane < 3 * N_HEADS
    query_side = (lane >= FOX_Q_LANE0) & (lane < FOX_Q_LANE0 + 3 * N_HEADS)
    cum2 = cum * LOG2E
    val = jnp.where(key_side, -cum2, cum2 - shift_ref[...])
    c1, c2, c3 = _split3(val)
    piece = jnp.where(key_side, lane, lane - FOX_Q_LANE0) % 3
    sel = jnp.where(piece == 0, c1, jnp.where(piece == 1, c2, c3))
    faug_ref[...] = jnp.where(key_side | query_side, sel, jnp.zeros_like(sel))


def _cumsum_call(flog, bf_rep, shift_row, batch, seq, tq):
    return pl.pallas_call(
        _cumsum_kernel,
        grid=(batch, seq // tq),
        in_specs=[
            pl.BlockSpec((tq, LANES), lambda b, s: (b * (seq // tq) + s, 0)),
            pl.BlockSpec((1, LANES), lambda b, s: (0, 0)),
            pl.BlockSpec((1, LANES), lambda b, s: (0, 0)),
        ],
        out_specs=pl.BlockSpec((tq, LANES), lambda b, s: (b * (seq // tq) + s, 0)),
        out_shape=jax.ShapeDtypeStruct((batch * seq, LANES), BF16),
        scratch_shapes=[pltpu.VMEM((1, LANES), F32)],
        compiler_params=pltpu.CompilerParams(dimension_semantics=("parallel", "arbitrary")),
        name="cumsum",
    )(flog, bf_rep, shift_row)


def _memkv_kernel(mem_ref, gmem_ref, w_ref, gk_ref, mk_ref, mv_ref):
    hm = _rms_normed(mem_ref[0], gmem_ref[...])
    kv = _dot(hm, w_ref[...])
    ks = []
    for h in range(MEM_HEADS):
        kh = kv[:, h * MEM_HEAD_DIM:(h + 1) * MEM_HEAD_DIM]
        msk = jnp.mean(kh * kh, axis=-1, keepdims=True)
        ks.append(kh * lax.rsqrt(msk + NORM_EPS) * gk_ref[...])
    mk_ref[0] = jnp.concatenate(ks, axis=1).astype(BF16)
    mv_ref[0] = kv[:, ATT_WIDTH:].astype(BF16)


def _memkv_call(mem, gmem, w_kv, gk):
    batch, mlen, _ = mem.shape
    return pl.pallas_call(
        _memkv_kernel,
        grid=(batch,),
        in_specs=[
            pl.BlockSpec((1, mlen, D_MODEL), lambda b: (b, 0, 0)),
            pl.BlockSpec((1, D_MODEL), lambda b: (0, 0)),
            pl.BlockSpec((D_MODEL, 2 * ATT_WIDTH), lambda b: (0, 0)),
            pl.BlockSpec((1, MEM_HEAD_DIM), lambda b: (0, 0)),
        ],
        out_specs=[
            pl.BlockSpec((1, mlen, ATT_WIDTH), lambda b: (b, 0, 0)),
            pl.BlockSpec((1, mlen, ATT_WIDTH), lambda b: (b, 0, 0)),
        ],
        out_shape=[jax.ShapeDtypeStruct((batch, mlen, ATT_WIDTH), BF16)] * 2,
        compiler_params=pltpu.CompilerParams(dimension_semantics=("parallel",)),
        name="memkv",
    )(mem, gmem, w_kv, gk)


def _lane_iota():
    return lax.broadcasted_iota(jnp.int32, (ATT_TILE, LANES), 1)


def _lane_range(first, count, dtype):
    lane = _lane_iota()
    return jnp.where((lane >= first) & (lane < first + count), 1.0, 0.0).astype(dtype)


def _head_lanes(c, dtype):
    return _lane_range(c * HEAD_DIM, HEAD_DIM, dtype)


def _fill_values(v_ref, vaug_ref):
    n_blocks = v_ref.shape[0] // ATT_TILE

    def fill(j, _):
        rows = pl.ds(pl.multiple_of(j * ATT_TILE, ATT_TILE), ATT_TILE)
        v = v_ref[rows, :].astype(F32)
        for c in range(2):
            own = _head_lanes(c, F32)
            vaug_ref[c, rows, :] = (v * own + (1.0 - own)).astype(BF16)
        return 0

    lax.fori_loop(0, n_blocks, fill, 0)


def _normalised(acc_ref, t):
    res = []
    for c in range(2):
        acc = acc_ref[2 * t + c]
        res.append(acc / pltpu.roll(acc, HEAD_DIM, 1))
    return jnp.where(_lane_iota() < HEAD_DIM, res[0], res[1])


def _causal_masks(qi, n_full):
    span = KEY_GROUP * ATT_TILE
    diff = (lax.broadcasted_iota(jnp.int32, (ATT_TILE, span), 1)
            - lax.broadcasted_iota(jnp.int32, (ATT_TILE, span), 0))
    first = (qi * Q_SUB - n_full * KEY_GROUP) * ATT_TILE
    return [diff <= first + t * ATT_TILE for t in range(Q_SUB)]


def _attend_shifted(lhs, kaug_ref, vaug_ref, acc_ref, s_ref, qi):
    span = KEY_GROUP * ATT_TILE
    n_chain = len(lhs)

    def group_rows(g):
        return pl.ds(pl.multiple_of(g * span, span), span)

    def scores(g, slot):
        kb = kaug_ref[group_rows(g), :]
        for i in range(n_chain):
            s_ref[slot, i] = _nt(lhs[i], kb)

    def consume(g, slot, masks):
        rows = group_rows(g)
        for i in range(n_chain):
            s = s_ref[slot, i]
            if masks is not None:
                s = jnp.where(masks[i // 2], s, NEG)
            acc_ref[i] += _dot(jnp.exp2(s).astype(BF16), vaug_ref[i % 2, rows, :])

    acc_ref[...] = jnp.zeros_like(acc_ref)
    n_full = (qi * Q_SUB) // KEY_GROUP
    scores(0, 0)

    def body(g, _):
        slot = lax.rem(g, 2)
        consume(g, slot, None)
        scores(g + 1, 1 - slot)
        return 0

    lax.fori_loop(0, n_full, body, 0)
    consume(n_full, lax.rem(n_full, 2), _causal_masks(qi, n_full))
    return [_normalised(acc_ref, t) for t in range(Q_SUB)]


def _attend_running_max(lhs, kaug_ref, vaug_ref, acc_ref, s_ref, qi):
    span = KEY_GROUP * ATT_TILE
    n_chain = len(lhs)

    def step(i, g, m, masks):
        rows = pl.ds(pl.multiple_of(g * span, span), span)
        s = _nt(lhs[i], kaug_ref[rows, :])
        if masks is not None:
            s = jnp.where(masks[i // 2], s, NEG)
        m_new = jnp.maximum(m, jnp.max(s, axis=1, keepdims=True))
        p = jnp.exp2(s - m_new).astype(BF16)
        acc_ref[i] = jnp.exp2(m - m_new) * acc_ref[i] + _dot(p, vaug_ref[i % 2, rows, :])
        return m_new

    acc_ref[...] = jnp.zeros_like(acc_ref)
    n_full = (qi * Q_SUB) // KEY_GROUP
    m0 = jnp.full((ATT_TILE, 1), -jnp.inf, F32)
    ms = lax.fori_loop(
        0, n_full, lambda g, ms: tuple(step(i, g, ms[i], None) for i in range(n_chain)),
        (m0,) * n_chain)
    masks = _causal_masks(qi, n_full)
    for i in range(n_chain):
        step(i, n_full, ms[i], masks)
    return [_normalised(acc_ref, t) for t in range(Q_SUB)]


def _attend(lhs, kaug_ref, vaug_ref, acc_ref, s_ref, qi, bounded):
    fn = _attend_shifted if bounded else _attend_running_max
    return fn(lhs, kaug_ref, vaug_ref, acc_ref, s_ref, qi)


def _attention_scratch(seq):
    n_chain = 2 * Q_SUB
    return [pltpu.VMEM((seq, 2 * LANES), BF16),
            pltpu.VMEM((2, seq, LANES), BF16),
            pltpu.VMEM((n_chain, ATT_TILE, LANES), F32),
            pltpu.VMEM((2, n_chain, ATT_TILE, KEY_GROUP * ATT_TILE), F32)]


def _sub_rows(t):
    return slice(t * ATT_TILE, (t + 1) * ATT_TILE)


def _moba_kernel(q_ref, k_ref, v_ref, km_ref, shift_ref, o_ref, kaug_ref, vaug_ref, acc_ref,
                 s_ref, *, bounded):
    qi = pl.program_id(2)
    seq = k_ref.shape[0]
    n_blocks = seq // MOBA_BLOCK
    lane = _lane_iota()

    @pl.when(qi == 0)
    def _():
        kaug_ref[:, 0:LANES] = k_ref[...]

        def fill(j, _):
            rows = pl.ds(pl.multiple_of(j * MOBA_BLOCK, MOBA_BLOCK), MOBA_BLOCK)
            onehot = (lane == j) | (lane == MOBA_SHIFT_LANE)
            kaug_ref[rows, LANES:2 * LANES] = jnp.where(onehot, 1.0, 0.0).astype(BF16)
            return 0

        lax.fori_loop(0, n_blocks, fill, 0)
        _fill_values(v_ref, vaug_ref)

    km = km_ref[0]
    km = jnp.concatenate([km, jnp.zeros((LANES - n_blocks, LANES), F32)], axis=0)
    km_hi = km.astype(BF16)
    km_lo = (km - km_hi.astype(F32)).astype(BF16)
    lane_f = lane.astype(F32)
    shift = shift_ref[...] if bounded else 0.0

    lhs = []
    for t in range(Q_SUB):
        own = qi * Q_SUB + t
        q2 = q_ref[_sub_rows(t), :]
        for c in range(2):
            qc = q2 * _head_lanes(c, BF16)
            gate = _nt(qc, km_hi) + _nt(qc, km_lo)
            g = jnp.where(lane < own, gate, -jnp.inf)
            sel = lane == own
            for _ in range(MOBA_TOPK):
                mx = jnp.max(g, axis=1, keepdims=True)
                first = jnp.min(jnp.where(g == mx, lane_f, float(LANES)), axis=1, keepdims=True)
                pick = (lane_f == first) & (mx > -jnp.inf)
                sel = sel | pick
                g = jnp.where(pick, -jnp.inf, g)
            bias = jnp.where(sel, 0.0, NEG)
            bias = jnp.where(lane == MOBA_SHIFT_LANE, -shift, bias)
            lhs.append(jnp.concatenate([qc, bias.astype(BF16)], axis=1))

    outs = _attend(lhs, kaug_ref, vaug_ref, acc_ref, s_ref, qi, bounded)
    for t in range(Q_SUB):
        o_ref[_sub_rows(t), :] = outs[t].astype(BF16)


def _moba_call(proj, kmean, shift_row, batch, seq, bounded):
    tq = Q_SUB * ATT_TILE
    n_q = seq // tq
    n_pairs = N_HEADS // 2
    q0 = (TILE_QM * COL_TILE) // LANES
    k0 = (TILE_KM * COL_TILE) // LANES
    v0 = k0 + COL_TILE // LANES
    return pl.pallas_call(
        functools.partial(_moba_kernel, bounded=bounded),
        grid=(batch, n_pairs, n_q),
        in_specs=[
            pl.BlockSpec((tq, LANES), lambda b, p, i: (b * n_q + i, q0 + p)),
            pl.BlockSpec((seq, LANES), lambda b, p, i: (b, k0 + p)),
            pl.BlockSpec((seq, LANES), lambda b, p, i: (b, v0 + p)),
            pl.BlockSpec((1, seq // MOBA_BLOCK, LANES), lambda b, p, i: (b, 0, p)),
            pl.BlockSpec((1, LANES), lambda b, p, i: (0, 0)),
        ],
        out_specs=pl.BlockSpec((tq, LANES), lambda b, p, i: (b * n_q + i, p)),
        out_shape=jax.ShapeDtypeStruct((batch * seq, ATT_WIDTH), BF16),
        scratch_shapes=_attention_scratch(seq),
        compiler_params=pltpu.CompilerParams(
            dimension_semantics=("parallel", "parallel", "arbitrary"),
            vmem_limit_bytes=VMEM_LIMIT),
        name="moba_bounded" if bounded else "moba_general",
    )(proj, proj, proj, kmean, shift_row)


def _fox_kernel(q_ref, k_ref, v_ref, faug_ref, fq_ref, o_ref, kaug_ref, vaug_ref, acc_ref,
                s_ref, *, bounded):
    pair = pl.program_id(1)
    qi = pl.program_id(2)
    n_extra = 3 * N_HEADS

    @pl.when(qi == 0)
    def _():
        seq = k_ref.shape[0]
        kaug_ref[:, 0:LANES] = k_ref[...]

        def fill(j, _):
            rows = pl.ds(pl.multiple_of(j * ATT_TILE, ATT_TILE), ATT_TILE)
            lane = _lane_iota()
            ones = (lane >= FOX_Q_LANE0) & (lane < FOX_Q_LANE0 + n_extra)
            f = faug_ref[rows, :].astype(F32)
            kaug_ref[rows, LANES:2 * LANES] = (
                f * (lane < n_extra).astype(F32) + ones.astype(F32)).astype(BF16)
            return 0

        lax.fori_loop(0, seq // ATT_TILE, fill, 0)
        _fill_values(v_ref, vaug_ref)

    lhs = []
    for t in range(Q_SUB):
        q2 = q_ref[_sub_rows(t), :]
        fq = fq_ref[_sub_rows(t), :]
        for c in range(2):
            qc = q2 * _head_lanes(c, BF16)
            k_first = 3 * (2 * pair + c)
            extra = _lane_range(k_first, 3, BF16)
            if bounded:
                extra = extra + fq * _lane_range(FOX_Q_LANE0 + k_first, 3, BF16)
            lhs.append(jnp.concatenate([qc, extra], axis=1))

    outs = _attend(lhs, kaug_ref, vaug_ref, acc_ref, s_ref, qi, bounded)
    for t in range(Q_SUB):
        o_ref[_sub_rows(t), :] = outs[t].astype(BF16)


def _fox_call(proj, faug, batch, seq, bounded):
    tq = Q_SUB * ATT_TILE
    n_q = seq // tq
    n_pairs = N_HEADS // 2
    q0 = (TILE_QF * COL_TILE) // LANES
    k0 = (TILE_KF * COL_TILE) // LANES
    v0 = k0 + COL_TILE // LANES
    return pl.pallas_call(
        functools.partial(_fox_kernel, bounded=bounded),
        grid=(batch, n_pairs, n_q),
        in_specs=[
            pl.BlockSpec((tq, LANES), lambda b, p, i: (b * n_q + i, q0 + p)),
            pl.BlockSpec((seq, LANES), lambda b, p, i: (b, k0 + p)),
            pl.BlockSpec((seq, LANES), lambda b, p, i: (b, v0 + p)),
            pl.BlockSpec((seq, LANES), lambda b, p, i: (b, 0)),
            pl.BlockSpec((tq, LANES), lambda b, p, i: (b * n_q + i, 0)),
        ],
        out_specs=pl.BlockSpec((tq, LANES), lambda b, p, i: (b * n_q + i, p)),
        out_shape=jax.ShapeDtypeStruct((batch * seq, ATT_WIDTH), BF16),
        scratch_shapes=_attention_scratch(seq),
        compiler_params=pltpu.CompilerParams(
            dimension_semantics=("parallel", "parallel", "arbitrary"),
            vmem_limit_bytes=VMEM_LIMIT),
        name="fox_bounded" if bounded else "fox_general",
    )(proj, proj, proj, faug, faug)


def _merge_kernel(x_ref, gmix_ref, qc_ref, ym_ref, yf_ref, mk_ref, mv_ref,
                  wg_ref, wbm_ref, wbf_ref, wbc_ref, wo_ref, o_ref):
    x = x_ref[...]
    h = _rms_normed(x, gmix_ref[...])
    qc = qc_ref[...]
    mk = mk_ref[0]
    mv = mv_ref[0]
    ys = []
    for hd in range(MEM_HEADS):
        cols = slice(hd * MEM_HEAD_DIM, (hd + 1) * MEM_HEAD_DIM)
        s = _nt(qc[:, cols], mk[:, cols])
        p = jnp.exp2(s - jnp.max(s, axis=1, keepdims=True))
        y = _dot(p.astype(BF16), mv[:, cols]) / jnp.sum(p, axis=1, keepdims=True)
        ys.append(y)
    yc = jnp.concatenate(ys, axis=1).astype(BF16)
    merged = None
    for i, (y, wb_ref) in enumerate(((ym_ref[...], wbm_ref), (yf_ref[...], wbf_ref),
                                     (yc, wbc_ref))):
        gate = jax.nn.sigmoid(_dot(h, wg_ref[:, i * D_MODEL:(i + 1) * D_MODEL]))
        term = gate * _dot(y, wb_ref[...])
        merged = term if merged is None else merged + term
    o_ref[...] = x + _dot(merged.astype(BF16), wo_ref[...])


def _merge_call(x2, gmix, proj, y_m, y_f, mk, mv, wg, wbm, wbf, wbc, wo, seq, tm):
    n_tok = x2.shape[0]
    mlen = mk.shape[1]
    seq_tiles = seq // tm
    const = lambda i: (0, 0)
    return pl.pallas_call(
        _merge_kernel,
        grid=(n_tok // tm,),
        in_specs=[
            pl.BlockSpec((tm, D_MODEL), lambda i: (i, 0)),
            pl.BlockSpec((1, D_MODEL), const),
            pl.BlockSpec((tm, ATT_WIDTH), lambda i: (i, TILE_QC)),
            pl.BlockSpec((tm, ATT_WIDTH), lambda i: (i, 0)),
            pl.BlockSpec((tm, ATT_WIDTH), lambda i: (i, 0)),
            pl.BlockSpec((1, mlen, ATT_WIDTH), lambda i: (i // seq_tiles, 0, 0)),
            pl.BlockSpec((1, mlen, ATT_WIDTH), lambda i: (i // seq_tiles, 0, 0)),
            pl.BlockSpec((D_MODEL, 3 * D_MODEL), const),
            pl.BlockSpec((ATT_WIDTH, D_MODEL), const),
            pl.BlockSpec((ATT_WIDTH, D_MODEL), const),
            pl.BlockSpec((ATT_WIDTH, D_MODEL), const),
            pl.BlockSpec((D_MODEL, D_MODEL), const),
        ],
        out_specs=pl.BlockSpec((tm, D_MODEL), lambda i: (i, 0)),
        out_shape=jax.ShapeDtypeStruct((n_tok, D_MODEL), F32),
        compiler_params=pltpu.CompilerParams(
            dimension_semantics=("parallel",), vmem_limit_bytes=VMEM_LIMIT),
        name="merge",
    )(x2, gmix, proj, y_m, y_f, mk, mv, wg, wbm, wbf, wbc, wo)


def _ffn_kernel(x_ref, g_ref, wg_ref, wu_ref, wd_ref, o_ref):
    x = x_ref[...]
    h = _rms_normed(x, g_ref[...])
    d_ff = wg_ref.shape[1]
    out = x
    for lo in range(0, d_ff, FFN_CHUNK):
        cols = slice(lo, min(lo + FFN_CHUNK, d_ff))
        g = _dot(h, wg_ref[:, cols])
        u = _dot(h, wu_ref[:, cols])
        a = (g * jax.nn.sigmoid(g) * u).astype(BF16)
        out = out + _dot(a, wd_ref[cols, :])
    o_ref[...] = out


def _ffn_call(x1, g_ffn, wg, wu, wd, tm):
    n_tok = x1.shape[0]
    d_ff = wg.shape[1]
    const = lambda i: (0, 0)
    resident = dict(pipeline_mode=pl.Buffered(1))
    return pl.pallas_call(
        _ffn_kernel,
        grid=(n_tok // tm,),
        in_specs=[
            pl.BlockSpec((tm, D_MODEL), lambda i: (i, 0)),
            pl.BlockSpec((1, D_MODEL), const),
            pl.BlockSpec((D_MODEL, d_ff), const, **resident),
            pl.BlockSpec((D_MODEL, d_ff), const, **resident),
            pl.BlockSpec((d_ff, D_MODEL), const, **resident),
        ],
        out_specs=pl.BlockSpec((tm, D_MODEL), lambda i: (i, 0)),
        out_shape=jax.ShapeDtypeStruct((n_tok, D_MODEL), F32),
        compiler_params=pltpu.CompilerParams(
            dimension_semantics=("parallel",), vmem_limit_bytes=VMEM_LIMIT),
        name="ffn",
    )(x1, g_ffn, wg, wu, wd)


@functools.lru_cache(maxsize=None)
def _rope_tables(seq):
    half = ROPE_DIMS // 2
    inv_freq = 1.0 / (ROPE_THETA ** (np.arange(half, dtype=np.float64) * 2.0 / ROPE_DIMS))
    ang = np.arange(seq, dtype=np.float64)[:, None] * inv_freq[None, :]
    cos, sin = np.cos(ang), np.sin(ang)
    d = np.arange(LANES) % HEAD_DIM
    rc = np.where(d[None, :] < ROPE_DIMS, cos[:, d % half], 1.0)
    rs1 = np.where(d[None, :] < half, -sin[:, d % half], 0.0)
    rs2 = np.where((d[None, :] >= half) & (d[None, :] < ROPE_DIMS), sin[:, d % half], 0.0)
    return tuple(np.asarray(t, np.float32) for t in (rc, rs1, rs2))


@functools.lru_cache(maxsize=None)
def _segment_mean_matrices():
    def blockdiag(width):
        return np.kron(np.eye(COL_TILE // width), np.ones((width, width))) / width
    return np.stack([blockdiag(HEAD_DIM), blockdiag(MEM_HEAD_DIM)]).astype(np.float32)


def _score_bound(g_q, g_k):
    return (BOUND_SLACK * HEAD_DIM ** 0.5 * jnp.max(jnp.abs(g_q)) * jnp.max(jnp.abs(g_k)))


def _lane_row(value, first, count):
    lane = jnp.arange(LANES)
    return jnp.where((lane >= first) & (lane < first + count), value, 0.0).astype(F32)[None, :]


def kernel(x, mem, g_mix, w_in, b_forget, g_q_moba, g_k_moba, g_q_fox, g_k_fox, g_q_mem, g_k_mem,
           g_mem, w_mem_kv, w_br_moba, w_br_fox, w_br_mem, w_out, g_ffn, w_gate, w_up, w_down):
    batch, seq, _ = x.shape
    depth = g_mix.shape[0]
    n_tok = batch * seq
    assert seq % (KEY_GROUP * ATT_TILE) == 0 and seq // MOBA_BLOCK < MOBA_SHIFT_LANE
    assert KEY_GROUP % Q_SUB == 0
    n_extra = 3 * N_HEADS

    rc, rs1, rs2 = (jnp.asarray(t) for t in _rope_tables(seq))
    bd = jnp.asarray(_segment_mean_matrices()).astype(BF16)
    ones = jnp.ones((COL_TILE,), F32)

    x2 = x.reshape(n_tok, D_MODEL)
    for layer in range(depth):
        w = w_in[layer]
        gates0 = 7 * ATT_WIDTH + N_HEADS
        qc0 = 6 * ATT_WIDTH + N_HEADS
        w_qkv = jnp.concatenate([w[:, :6 * ATT_WIDTH], w[:, qc0:gates0]], axis=1).astype(BF16)
        w_gates = w[:, gates0:].astype(BF16)
        wf3 = jnp.repeat(w[:, 6 * ATT_WIDTH:qc0], 3, axis=1)
        gap = jnp.zeros((D_MODEL, FOX_Q_LANE0 - n_extra), F32)
        tail = jnp.zeros((D_MODEL, LANES - FOX_Q_LANE0 - n_extra), F32)
        wf = jnp.concatenate([wf3, gap, wf3, tail], axis=1).astype(BF16)
        bf3 = jnp.repeat(b_forget[layer], 3)
        bf_rep = jnp.concatenate([bf3, gap[0], bf3, tail[0]])[None, :]

        att_scale = HEAD_DIM ** -0.5 * LOG2E
        gains = [ones] * N_COL_TILES
        gains[TILE_QM] = jnp.tile(g_q_moba[layer], N_HEADS) * att_scale
        gains[TILE_KM] = jnp.tile(g_k_moba[layer], N_HEADS)
        gains[TILE_QF] = jnp.tile(g_q_fox[layer], N_HEADS) * att_scale
        gains[TILE_KF] = jnp.tile(g_k_fox[layer], N_HEADS)
        gains[TILE_QC] = jnp.tile(g_q_mem[layer], MEM_HEADS) * (MEM_HEAD_DIM ** -0.5 * LOG2E)
        gains = jnp.stack(gains)[:, None, :]

        bound_m = _score_bound(g_q_moba[layer], g_k_moba[layer])
        bound_f = _score_bound(g_q_fox[layer], g_k_fox[layer])
        shift_m = _lane_row(bound_m * LOG2E, MOBA_SHIFT_LANE, 1)
        shift_f = _lane_row(bound_f * LOG2E, FOX_Q_LANE0, n_extra)

        proj, flog, kmean = _proj_call(x2, g_mix[layer][None, :], w_qkv, wf, bd, gains,
                                       rc, rs1, rs2, seq, tm=512)
        kmean = kmean.reshape(batch, seq // MOBA_BLOCK, COL_TILE)
        faug = _cumsum_call(flog, bf_rep, shift_f, batch, seq, tq=256)
        mk, mv = _memkv_call(mem, g_mem[layer][None, :], w_mem_kv[layer].astype(BF16),
                             g_k_mem[layer][None, :])
        y_m = lax.cond(2.0 * bound_m <= MAX_SHIFT_NATS,
                       lambda: _moba_call(proj, kmean, shift_m, batch, seq, True),
                       lambda: _moba_call(proj, kmean, shift_m, batch, seq, False))
        y_f = lax.cond(2.0 * bound_f <= MAX_SHIFT_NATS,
                       lambda: _fox_call(proj, faug, batch, seq, True),
                       lambda: _fox_call(proj, faug, batch, seq, False))
        x1 = _merge_call(x2, g_mix[layer][None, :], proj, y_m, y_f, mk, mv, w_gates,
                         w_br_moba[layer].astype(BF16), w_br_fox[layer].astype(BF16),
                         w_br_mem[layer].astype(BF16), w_out[layer].astype(BF16), seq, tm=512)
        x2 = _ffn_call(x1, g_ffn[layer][None, :], w_gate[layer].astype(BF16),
                       w_up[layer].astype(BF16), w_down[layer].astype(BF16), tm=512)
    return x2.reshape(batch, seq, D_MODEL)
```

```python
import functools
import math

import numpy as np
import jax
import jax.numpy as jnp
from jax import lax
from jax.experimental import pallas as pl
from jax.experimental.pallas import tpu as pltpu

F32 = jnp.float32
BF16 = jnp.bfloat16

D_MODEL = 1024
HEAD_DIM = 64
N_HEADS = 8
MEM_HEADS = 4
MEM_HEAD_DIM = 128
ATT_WIDTH = 512
MOBA_BLOCK = 256
MOBA_TOPK = 3
ROPE_THETA = 500000.0
ROPE_DIMS = 16
NORM_EPS = 1e-6
NEG = -1e30
LOG2E = math.log2(math.e)

LANES = 128
COL_TILE = 512
N_COL_TILES = 7
PROJ_WIDTH = COL_TILE * N_COL_TILES
TILE_QM, TILE_KM, TILE_QF, TILE_KF, TILE_QC = 0, 1, 3, 4, 6
ATT_TILE = 256
KEY_GROUP = 4
Q_SUB = KEY_GROUP
FFN_CHUNK = 1024
VMEM_LIMIT = 52 * 1024 * 1024

FOX_Q_LANE0 = 32
MOBA_SHIFT_LANE = LANES - 1
MAX_SHIFT_NATS = 60.0
BOUND_SLACK = 1.02


def _nt(a, b):
    return lax.dot_general(a, b, (((1,), (1,)), ((), ())), preferred_element_type=F32)


def _dot(a, b):
    return jnp.dot(a, b, preferred_element_type=F32)


def _split3(v):
    p1 = v.astype(BF16)
    r1 = v - p1.astype(F32)
    p2 = r1.astype(BF16)
    p3 = (r1 - p2.astype(F32)).astype(BF16)
    return p1, p2, p3


def _rms_normed(x, gain_row):
    ms = jnp.mean(x * x, axis=-1, keepdims=True)
    return (x * lax.rsqrt(ms + NORM_EPS) * gain_row).astype(BF16)


def _forget_lanes(cum, shift_row):
    lane = lax.broadcasted_iota(jnp.int32, cum.shape, 1)
    key_side = lane < 3 * N_HEADS
    query_side = (lane >= FOX_Q_LANE0) & (lane < FOX_Q_LANE0 + 3 * N_HEADS)
    cum2 = cum * LOG2E
    val = jnp.where(key_side, -cum2, cum2 - shift_row)
    c1, c2, c3 = _split3(val)
    piece = jnp.where(key_side, lane, lane - FOX_Q_LANE0) % 3
    sel = jnp.where(piece == 0, c1, jnp.where(piece == 1, c2, c3))
    return jnp.where(key_side | query_side, sel, jnp.zeros_like(sel))


def _proj_kernel(x_ref, gmix_ref, w_ref, wf_ref, bf_ref, shift_ref, bd_ref, gain_ref,
                 rc_ref, rs1_ref, rs2_ref, p_ref, faug_ref, kmean_ref, carry_ref, *, seq_tiles):
    tm = x_ref.shape[0]
    h = _rms_normed(x_ref[...], gmix_ref[...])

    @pl.when(pl.program_id(0) % seq_tiles == 0)
    def _():
        carry_ref[...] = jnp.zeros_like(carry_ref)

    z = _dot(h, wf_ref[...]) + bf_ref[...]
    lf = jnp.minimum(z, 0.0) - jnp.log1p(jnp.exp(-jnp.abs(z)))
    row = lax.broadcasted_iota(jnp.int32, (tm, tm), 0)
    col = lax.broadcasted_iota(jnp.int32, (tm, tm), 1)
    tri = jnp.where(col <= row, 1.0, 0.0).astype(BF16)
    p1, p2, p3 = _split3(lf)
    cum = _dot(tri, p1) + _dot(tri, p2) + _dot(tri, p3) + carry_ref[...]
    carry_ref[...] = cum[tm - 1:tm, :]
    faug_ref[...] = _forget_lanes(cum, shift_ref[...])

    def head_normed(t, k):
        seg = 1 if k == TILE_QC else 0
        ms = _dot((t * t).astype(BF16), bd_ref[seg])
        return t * lax.rsqrt(ms + NORM_EPS) * gain_ref[k]

    def rotated(y):
        rc, rs1, rs2 = rc_ref[...], rs1_ref[...], rs2_ref[...]
        parts = []
        for g in range(COL_TILE // LANES):
            yg = y[:, g * LANES:(g + 1) * LANES]
            up = pltpu.roll(yg, LANES - ROPE_DIMS // 2, 1)
            dn = pltpu.roll(yg, ROPE_DIMS // 2, 1)
            parts.append(yg * rc + up * rs1 + dn * rs2)
        return jnp.concatenate(parts, axis=1)

    for k in range(N_COL_TILES):
        cols = slice(k * COL_TILE, (k + 1) * COL_TILE)
        t = _dot(h, w_ref[:, cols])
        if k in (TILE_QM, TILE_KM):
            t = rotated(head_normed(t, k))
        elif k in (TILE_QF, TILE_KF, TILE_QC):
            t = head_normed(t, k)
        p_ref[:, cols] = t.astype(BF16)
        if k == TILE_KM:
            for r in range(tm // MOBA_BLOCK):
                blk = t[r * MOBA_BLOCK:(r + 1) * MOBA_BLOCK, :]
                kmean_ref[0, pl.ds(r, 1), :] = jnp.mean(blk, axis=0, keepdims=True)


def _proj_call(x2, gmix, w_qkv, wf, bf_row, shift_row, bd, gains, rc, rs1, rs2, seq, tm):
    n_tok = x2.shape[0]
    seq_tiles = seq // tm
    const2 = lambda i: (0, 0)
    const3 = lambda i: (0, 0, 0)
    return pl.pallas_call(
        functools.partial(_proj_kernel, seq_tiles=seq_tiles),
        grid=(n_tok // tm,),
        in_specs=[
            pl.BlockSpec((tm, D_MODEL), lambda i: (i, 0)),
            pl.BlockSpec((1, D_MODEL), const2),
            pl.BlockSpec((D_MODEL, PROJ_WIDTH), const2),
            pl.BlockSpec((D_MODEL, LANES), const2),
            pl.BlockSpec((1, LANES), const2),
            pl.BlockSpec((1, LANES), const2),
            pl.BlockSpec((2, COL_TILE, COL_TILE), const3),
            pl.BlockSpec((N_COL_TILES, 1, COL_TILE), const3),
            pl.BlockSpec((tm, LANES), lambda i: (i % seq_tiles, 0)),
            pl.BlockSpec((tm, LANES), lambda i: (i % seq_tiles, 0)),
            pl.BlockSpec((tm, LANES), lambda i: (i % seq_tiles, 0)),
        ],
        out_specs=[
            pl.BlockSpec((tm, PROJ_WIDTH), lambda i: (i, 0)),
            pl.BlockSpec((tm, LANES), lambda i: (i, 0)),
            pl.BlockSpec((1, tm // MOBA_BLOCK, COL_TILE), lambda i: (i, 0, 0)),
        ],
        out_shape=[
            jax.ShapeDtypeStruct((n_tok, PROJ_WIDTH), BF16),
            jax.ShapeDtypeStruct((n_tok, LANES), BF16),
            jax.ShapeDtypeStruct((n_tok // tm, tm // MOBA_BLOCK, COL_TILE), F32),
        ],
        scratch_shapes=[pltpu.VMEM((1, LANES), F32)],
        compiler_params=pltpu.CompilerParams(
            dimension_semantics=("arbitrary",), vmem_limit_bytes=VMEM_LIMIT),
        name="proj",
    )(x2, gmix, w_qkv, wf, bf_row, shift_row, bd, gains, rc, rs1, rs2)


def _memkv_kernel(mem_ref, gmem_ref, w_ref, gk_ref, mk_ref, mv_ref):
    hm = _rms_normed(mem_ref[0], gmem_ref[...])
    kv = _dot(hm, w_ref[...])
    ks = []
    for h in range(MEM_HEADS):
        kh = kv[:, h * MEM_HEAD_DIM:(h + 1) * MEM_HEAD_DIM]
        msk = jnp.mean(kh * kh, axis=-1, keepdims=True)
        ks.append(kh * lax.rsqrt(msk + NORM_EPS) * gk_ref[...])
    mk_ref[0] = jnp.concatenate(ks, axis=1).astype(BF16)
    mv_ref[0] = kv[:, ATT_WIDTH:].astype(BF16)


def _memkv_call(mem, gmem, w_kv, gk):
    batch, mlen, _ = mem.shape
    return pl.pallas_call(
        _memkv_kernel,
        grid=(batch,),
        in_specs=[
            pl.BlockSpec((1, mlen, D_MODEL), lambda b: (b, 0, 0)),
            pl.BlockSpec((1, D_MODEL), lambda b: (0, 0)),
            pl.BlockSpec((D_MODEL, 2 * ATT_WIDTH), lambda b: (0, 0)),
            pl.BlockSpec((1, MEM_HEAD_DIM), lambda b: (0, 0)),
        ],
        out_specs=[
            pl.BlockSpec((1, mlen, ATT_WIDTH), lambda b: (b, 0, 0)),
            pl.BlockSpec((1, mlen, ATT_WIDTH), lambda b: (b, 0, 0)),
        ],
        out_shape=[jax.ShapeDtypeStruct((batch, mlen, ATT_WIDTH), BF16)] * 2,
        compiler_params=pltpu.CompilerParams(dimension_semantics=("parallel",)),
        name="memkv",
    )(mem, gmem, w_kv, gk)


def _lane_iota():
    return lax.broadcasted_iota(jnp.int32, (ATT_TILE, LANES), 1)


def _lane_range(first, count, dtype):
    lane = _lane_iota()
    return jnp.where((lane >= first) & (lane < first + count), 1.0, 0.0).astype(dtype)


def _head_lanes(c, dtype):
    return _lane_range(c * HEAD_DIM, HEAD_DIM, dtype)


def _fill_values(v_ref, vaug_ref):
    n_blocks = v_ref.shape[0] // ATT_TILE

    def fill(j, _):
        rows = pl.ds(pl.multiple_of(j * ATT_TILE, ATT_TILE), ATT_TILE)
        v = v_ref[rows, :].astype(F32)
        for c in range(2):
            own = _head_lanes(c, F32)
            vaug_ref[c, rows, :] = (v * own + (1.0 - own)).astype(BF16)
        return 0

    lax.fori_loop(0, n_blocks, fill, 0)


def _normalised(acc_ref, t):
    res = []
    for c in range(2):
        acc = acc_ref[2 * t + c]
        res.append(acc / pltpu.roll(acc, HEAD_DIM, 1))
    return jnp.where(_lane_iota() < HEAD_DIM, res[0], res[1])


def _causal_masks():
    span = KEY_GROUP * ATT_TILE
    diff = (lax.broadcasted_iota(jnp.int32, (ATT_TILE, span), 1)
            - lax.broadcasted_iota(jnp.int32, (ATT_TILE, span), 0))
    return [diff <= t * ATT_TILE for t in range(Q_SUB)]


def _attend_shifted(lhs, kaug_ref, vaug_ref, acc_ref, s_ref, qi):
    span = KEY_GROUP * ATT_TILE
    n_chain = len(lhs)

    def group_rows(g):
        return pl.ds(pl.multiple_of(g * span, span), span)

    def scores(g, slot):
        kb = kaug_ref[group_rows(g), :]
        for i in range(n_chain):
            s_ref[slot, i] = _nt(lhs[i], kb)

    def consume(g, slot):
        rows = group_rows(g)
        for i in range(n_chain):
            p = jnp.exp2(s_ref[slot, i]).astype(BF16)
            acc_ref[i] += _dot(p, vaug_ref[i % 2, rows, :])

    def consume_diagonal(g, slot):
        row = lax.broadcasted_iota(jnp.int32, (ATT_TILE, ATT_TILE), 0)
        col = lax.broadcasted_iota(jnp.int32, (ATT_TILE, ATT_TILE), 1)
        for i in range(n_chain):
            t = i // 2
            lo = t * ATT_TILE
            parts = []
            if t:
                parts.append(jnp.exp2(s_ref[slot, i, :, 0:lo]).astype(BF16))
            s_diag = jnp.where(col <= row, s_ref[slot, i, :, lo:lo + ATT_TILE], NEG)
            parts.append(jnp.exp2(s_diag).astype(BF16))
            p = parts[0] if len(parts) == 1 else jnp.concatenate(parts, axis=1)
            rows = pl.ds(pl.multiple_of(g * span, span), lo + ATT_TILE)
            acc_ref[i] += _dot(p, vaug_ref[i % 2, rows, :])

    acc_ref[...] = jnp.zeros_like(acc_ref)
    n_full = qi
    scores(0, 0)

    def body(g, _):
        slot = lax.rem(g, 2)
        consume(g, slot)
        scores(g + 1, 1 - slot)
        return 0

    lax.fori_loop(0, n_full, body, 0)
    consume_diagonal(n_full, lax.rem(n_full, 2))
    return [_normalised(acc_ref, t) for t in range(Q_SUB)]


def _attend_running_max(lhs, kaug_ref, vaug_ref, acc_ref, s_ref, qi):
    span = KEY_GROUP * ATT_TILE
    n_chain = len(lhs)

    def step(i, g, m, masks):
        rows = pl.ds(pl.multiple_of(g * span, span), span)
        s = _nt(lhs[i], kaug_ref[rows, :])
        if masks is not None:
            s = jnp.where(masks[i // 2], s, NEG)
        m_new = jnp.maximum(m, jnp.max(s, axis=1, keepdims=True))
        p = jnp.exp2(s - m_new).astype(BF16)
        acc_ref[i] = jnp.exp2(m - m_new) * acc_ref[i] + _dot(p, vaug_ref[i % 2, rows, :])
        return m_new

    acc_ref[...] = jnp.zeros_like(acc_ref)
    n_full = qi
    m0 = jnp.full((ATT_TILE, 1), -jnp.inf, F32)
    ms = lax.fori_loop(
        0, n_full, lambda g, ms: tuple(step(i, g, ms[i], None) for i in range(n_chain)),
        (m0,) * n_chain)
    masks = _causal_masks()
    for i in range(n_chain):
        step(i, n_full, ms[i], masks)
    return [_normalised(acc_ref, t) for t in range(Q_SUB)]


def _attend(lhs, kaug_ref, vaug_ref, acc_ref, s_ref, qi, bounded):
    fn = _attend_shifted if bounded else _attend_running_max
    return fn(lhs, kaug_ref, vaug_ref, acc_ref, s_ref, qi)


def _attention_scratch(seq):
    n_chain = 2 * Q_SUB
    return [pltpu.VMEM((seq, 2 * LANES), BF16),
            pltpu.VMEM((2, seq, LANES), BF16),
            pltpu.VMEM((n_chain, ATT_TILE, LANES), F32),
            pltpu.VMEM((2, n_chain, ATT_TILE, KEY_GROUP * ATT_TILE), F32)]


def _sub_rows(t):
    return slice(t * ATT_TILE, (t + 1) * ATT_TILE)


def _moba_kernel(q_ref, k_ref, v_ref, km_ref, shift_ref, o_ref, kaug_ref, vaug_ref, acc_ref,
                 s_ref, *, bounded):
    qi = pl.program_id(2)
    seq = k_ref.shape[0]
    n_blocks = seq // MOBA_BLOCK
    lane = _lane_iota()

    @pl.when(qi == 0)
    def _():
        kaug_ref[:, 0:LANES] = k_ref[...]

        def fill(j, _):
            rows = pl.ds(pl.multiple_of(j * MOBA_BLOCK, MOBA_BLOCK), MOBA_BLOCK)
            onehot = (lane == j) | (lane == MOBA_SHIFT_LANE)
            kaug_ref[rows, LANES:2 * LANES] = jnp.where(onehot, 1.0, 0.0).astype(BF16)
            return 0

        lax.fori_loop(0, n_blocks, fill, 0)
        _fill_values(v_ref, vaug_ref)

    km = km_ref[0]
    km = jnp.concatenate([km, jnp.zeros((LANES - n_blocks, LANES), F32)], axis=0)
    km_hi = km.astype(BF16)
    km_lo = (km - km_hi.astype(F32)).astype(BF16)
    shift = shift_ref[:, MOBA_SHIFT_LANE:MOBA_SHIFT_LANE + 1] if bounded else 0.0
    blk = lax.broadcasted_iota(jnp.int32, (n_blocks, ATT_TILE), 0)
    blk_f = blk.astype(F32)
    pad_row = lax.broadcasted_iota(jnp.int32, (LANES - n_blocks, ATT_TILE), 0) + n_blocks
    pad_bias = jnp.where(pad_row == MOBA_SHIFT_LANE, -shift, NEG)

    lhs = []
    for t in range(Q_SUB):
        own = qi * Q_SUB + t
        q2 = q_ref[_sub_rows(t), :]
        for c in range(2):
            qc = q2 * _head_lanes(c, BF16)
            gate = (_nt(km_hi, qc) + _nt(km_lo, qc))[0:n_blocks]
            g = jnp.where(blk < own, gate, -jnp.inf)
            sel = blk == own
            for _ in range(MOBA_TOPK):
                mx = jnp.max(g, axis=0, keepdims=True)
                first = jnp.min(jnp.where(g == mx, blk_f, float(LANES)), axis=0, keepdims=True)
                pick = (blk_f == first) & (mx > -jnp.inf)
                sel = sel | pick
                g = jnp.where(pick, -jnp.inf, g)
            bias = jnp.concatenate([jnp.where(sel, 0.0, NEG), pad_bias], axis=0)
            lhs.append(jnp.concatenate([qc, bias.T.astype(BF16)], axis=1))

    outs = _attend(lhs, kaug_ref, vaug_ref, acc_ref, s_ref, qi, bounded)
    for t in range(Q_SUB):
        o_ref[_sub_rows(t), :] = outs[t].astype(BF16)


def _moba_call(proj, kmean, shift_row, batch, seq, bounded):
    tq = Q_SUB * ATT_TILE
    n_q = seq // tq
    n_pairs = N_HEADS // 2
    q0 = (TILE_QM * COL_TILE) // LANES
    k0 = (TILE_KM * COL_TILE) // LANES
    v0 = k0 + COL_TILE // LANES
    return pl.pallas_call(
        functools.partial(_moba_kernel, bounded=bounded),
        grid=(batch, n_pairs, n_q),
        in_specs=[
            pl.BlockSpec((tq, LANES), lambda b, p, i: (b * n_q + i, q0 + p)),
            pl.BlockSpec((seq, LANES), lambda b, p, i: (b, k0 + p)),
            pl.BlockSpec((seq, LANES), lambda b, p, i: (b, v0 + p)),
            pl.BlockSpec((1, seq // MOBA_BLOCK, LANES), lambda b, p, i: (b, 0, p)),
            pl.BlockSpec((1, LANES), lambda b, p, i: (0, 0)),
        ],
        out_specs=pl.BlockSpec((tq, LANES), lambda b, p, i: (b * n_q + i, p)),
        out_shape=jax.ShapeDtypeStruct((batch * seq, ATT_WIDTH), BF16),
        scratch_shapes=_attention_scratch(seq),
        compiler_params=pltpu.CompilerParams(
            dimension_semantics=("parallel", "parallel", "arbitrary"),
            vmem_limit_bytes=VMEM_LIMIT),
        name="moba_bounded" if bounded else "moba_general",
    )(proj, proj, proj, kmean, shift_row)


def _fox_kernel(q_ref, k_ref, v_ref, faug_ref, fq_ref, o_ref, kaug_ref, vaug_ref, acc_ref,
                s_ref, *, bounded):
    pair = pl.program_id(1)
    qi = pl.program_id(2)
    n_extra = 3 * N_HEADS

    @pl.when(qi == 0)
    def _():
        seq = k_ref.shape[0]
        kaug_ref[:, 0:LANES] = k_ref[...]

        def fill(j, _):
            rows = pl.ds(pl.multiple_of(j * ATT_TILE, ATT_TILE), ATT_TILE)
            lane = _lane_iota()
            ones = (lane >= FOX_Q_LANE0) & (lane < FOX_Q_LANE0 + n_extra)
            f = faug_ref[rows, :].astype(F32)
            kaug_ref[rows, LANES:2 * LANES] = (
                f * (lane < n_extra).astype(F32) + ones.astype(F32)).astype(BF16)
            return 0

        lax.fori_loop(0, seq // ATT_TILE, fill, 0)
        _fill_values(v_ref, vaug_ref)

    lhs = []
    for t in range(Q_SUB):
        q2 = q_ref[_sub_rows(t), :]
        fq = fq_ref[_sub_rows(t), :]
        for c in range(2):
            qc = q2 * _head_lanes(c, BF16)
            k_first = 3 * (2 * pair + c)
            extra = _lane_range(k_first, 3, BF16)
            if bounded:
                extra = extra + fq * _lane_range(FOX_Q_LANE0 + k_first, 3, BF16)
            lhs.append(jnp.concatenate([qc, extra], axis=1))

    outs = _attend(lhs, kaug_ref, vaug_ref, acc_ref, s_ref, qi, bounded)
    for t in range(Q_SUB):
        o_ref[_sub_rows(t), :] = outs[t].astype(BF16)


def _fox_call(proj, faug, batch, seq, bounded):
    tq = Q_SUB * ATT_TILE
    n_q = seq // tq
    n_pairs = N_HEADS // 2
    q0 = (TILE_QF * COL_TILE) // LANES
    k0 = (TILE_KF * COL_TILE) // LANES
    v0 = k0 + COL_TILE // LANES
    return pl.pallas_call(
        functools.partial(_fox_kernel, bounded=bounded),
        grid=(batch, n_pairs, n_q),
        in_specs=[
            pl.BlockSpec((tq, LANES), lambda b, p, i: (b * n_q + i, q0 + p)),
            pl.BlockSpec((seq, LANES), lambda b, p, i: (b, k0 + p)),
            pl.BlockSpec((seq, LANES), lambda b, p, i: (b, v0 + p)),
            pl.BlockSpec((seq, LANES), lambda b, p, i: (b, 0)),
            pl.BlockSpec((tq, LANES), lambda b, p, i: (b * n_q + i, 0)),
        ],
        out_specs=pl.BlockSpec((tq, LANES), lambda b, p, i: (b * n_q + i, p)),
        out_shape=jax.ShapeDtypeStruct((batch * seq, ATT_WIDTH), BF16),
        scratch_shapes=_attention_scratch(seq),
        compiler_params=pltpu.CompilerParams(
            dimension_semantics=("parallel", "parallel", "arbitrary"),
            vmem_limit_bytes=VMEM_LIMIT),
        name="fox_bounded" if bounded else "fox_general",
    )(proj, proj, proj, faug, faug)


def _merge_kernel(x_ref, gmix_ref, qc_ref, ym_ref, yf_ref, mk_ref, mv_ref,
                  wg_ref, wbm_ref, wbf_ref, wbc_ref, wo_ref, o_ref):
    x = x_ref[...]
    h = _rms_normed(x, gmix_ref[...])
    qc = qc_ref[...]
    mk = mk_ref[0]
    mv = mv_ref[0]
    ys = []
    for hd in range(MEM_HEADS):
        cols = slice(hd * MEM_HEAD_DIM, (hd + 1) * MEM_HEAD_DIM)
        s = _nt(qc[:, cols], mk[:, cols])
        p = jnp.exp2(s - jnp.max(s, axis=1, keepdims=True))
        y = _dot(p.astype(BF16), mv[:, cols]) / jnp.sum(p, axis=1, keepdims=True)
        ys.append(y)
    yc = jnp.concatenate(ys, axis=1).astype(BF16)
    merged = None
    for i, (y, wb_ref) in enumerate(((ym_ref[...], wbm_ref), (yf_ref[...], wbf_ref),
                                     (yc, wbc_ref))):
        gate = jax.nn.sigmoid(_dot(h, wg_ref[:, i * D_MODEL:(i + 1) * D_MODEL]))
        term = gate * _dot(y, wb_ref[...])
        merged = term if merged is None else merged + term
    o_ref[...] = x + _dot(merged.astype(BF16), wo_ref[...])


def _merge_call(x2, gmix, proj, y_m, y_f, mk, mv, wg, wbm, wbf, wbc, wo, seq, tm):
    n_tok = x2.shape[0]
    mlen = mk.shape[1]
    seq_tiles = seq // tm
    const = lambda i: (0, 0)
    return pl.pallas_call(
        _merge_kernel,
        grid=(n_tok // tm,),
        in_specs=[
            pl.BlockSpec((tm, D_MODEL), lambda i: (i, 0)),
            pl.BlockSpec((1, D_MODEL), const),
            pl.BlockSpec((tm, ATT_WIDTH), lambda i: (i, TILE_QC)),
            pl.BlockSpec((tm, ATT_WIDTH), lambda i: (i, 0)),
            pl.BlockSpec((tm, ATT_WIDTH), lambda i: (i, 0)),
            pl.BlockSpec((1, mlen, ATT_WIDTH), lambda i: (i // seq_tiles, 0, 0)),
            pl.BlockSpec((1, mlen, ATT_WIDTH), lambda i: (i // seq_tiles, 0, 0)),
            pl.BlockSpec((D_MODEL, 3 * D_MODEL), const),
            pl.BlockSpec((ATT_WIDTH, D_MODEL), const),
            pl.BlockSpec((ATT_WIDTH, D_MODEL), const),
            pl.BlockSpec((ATT_WIDTH, D_MODEL), const),
            pl.BlockSpec((D_MODEL, D_MODEL), const),
        ],
        out_specs=pl.BlockSpec((tm, D_MODEL), lambda i: (i, 0)),
        out_shape=jax.ShapeDtypeStruct((n_tok, D_MODEL), F32),
        compiler_params=pltpu.CompilerParams(
            dimension_semantics=("parallel",), vmem_limit_bytes=VMEM_LIMIT),
        name="merge",
    )(x2, gmix, proj, y_m, y_f, mk, mv, wg, wbm, wbf, wbc, wo)


def _ffn_kernel(x_ref, g_ref, wg_ref, wu_ref, wd_ref, o_ref):
    x = x_ref[...]
    h = _rms_normed(x, g_ref[...])
    d_ff = wg_ref.shape[1]
    out = x
    for lo in range(0, d_ff, FFN_CHUNK):
        cols = slice(lo, min(lo + FFN_CHUNK, d_ff))
        g = _dot(h, wg_ref[:, cols])
        u = _dot(h, wu_ref[:, cols])
        a = (g * jax.nn.sigmoid(g) * u).astype(BF16)
        out = out + _dot(a, wd_ref[cols, :])
    o_ref[...] = out


def _ffn_call(x1, g_ffn, wg, wu, wd, tm):
    n_tok = x1.shape[0]
    d_ff = wg.shape[1]
    const = lambda i: (0, 0)
    resident = dict(pipeline_mode=pl.Buffered(1))
    return pl.pallas_call(
        _ffn_kernel,
        grid=(n_tok // tm,),
        in_specs=[
            pl.BlockSpec((tm, D_MODEL), lambda i: (i, 0)),
            pl.BlockSpec((1, D_MODEL), const),
            pl.BlockSpec((D_MODEL, d_ff), const, **resident),
            pl.BlockSpec((D_MODEL, d_ff), const, **resident),
            pl.BlockSpec((d_ff, D_MODEL), const, **resident),
        ],
        out_specs=pl.BlockSpec((tm, D_MODEL), lambda i: (i, 0)),
        out_shape=jax.ShapeDtypeStruct((n_tok, D_MODEL), F32),
        compiler_params=pltpu.CompilerParams(
            dimension_semantics=("parallel",), vmem_limit_bytes=VMEM_LIMIT),
        name="ffn",
    )(x1, g_ffn, wg, wu, wd)


@functools.lru_cache(maxsize=None)
def _rope_tables(seq):
    half = ROPE_DIMS // 2
    inv_freq = 1.0 / (ROPE_THETA ** (np.arange(half, dtype=np.float64) * 2.0 / ROPE_DIMS))
    ang = np.arange(seq, dtype=np.float64)[:, None] * inv_freq[None, :]
    cos, sin = np.cos(ang), np.sin(ang)
    d = np.arange(LANES) % HEAD_DIM
    rc = np.where(d[None, :] < ROPE_DIMS, cos[:, d % half], 1.0)
    rs1 = np.where(d[None, :] < half, -sin[:, d % half], 0.0)
    rs2 = np.where((d[None, :] >= half) & (d[None, :] < ROPE_DIMS), sin[:, d % half], 0.0)
    return tuple(np.asarray(t, np.float32) for t in (rc, rs1, rs2))


@functools.lru_cache(maxsize=None)
def _segment_mean_matrices():
    def blockdiag(width):
        return np.kron(np.eye(COL_TILE // width), np.ones((width, width))) / width
    return np.stack([blockdiag(HEAD_DIM), blockdiag(MEM_HEAD_DIM)]).astype(np.float32)


def _score_bound(g_q, g_k):
    return (BOUND_SLACK * HEAD_DIM ** 0.5 * jnp.max(jnp.abs(g_q)) * jnp.max(jnp.abs(g_k)))


def _lane_row(value, first, count):
    lane = jnp.arange(LANES)
    return jnp.where((lane >= first) & (lane < first + count), value, 0.0).astype(F32)[None, :]


def kernel(x, mem, g_mix, w_in, b_forget, g_q_moba, g_k_moba, g_q_fox, g_k_fox, g_q_mem, g_k_mem,
           g_mem, w_mem_kv, w_br_moba, w_br_fox, w_br_mem, w_out, g_ffn, w_gate, w_up, w_down):
    batch, seq, _ = x.shape
    depth = g_mix.shape[0]
    n_tok = batch * seq
    assert seq % (KEY_GROUP * ATT_TILE) == 0 and seq // MOBA_BLOCK < MOBA_SHIFT_LANE
    n_extra = 3 * N_HEADS

    rc, rs1, rs2 = (jnp.asarray(t) for t in _rope_tables(seq))
    bd = jnp.asarray(_segment_mean_matrices()).astype(BF16)
    ones = jnp.ones((COL_TILE,), F32)

    x2 = x.reshape(n_tok, D_MODEL)
    for layer in range(depth):
        w = w_in[layer]
        gates0 = 7 * ATT_WIDTH + N_HEADS
        qc0 = 6 * ATT_WIDTH + N_HEADS
        w_qkv = jnp.concatenate([w[:, :6 * ATT_WIDTH], w[:, qc0:gates0]], axis=1).astype(BF16)
        w_gates = w[:, gates0:].astype(BF16)
        wf3 = jnp.repeat(w[:, 6 * ATT_WIDTH:qc0], 3, axis=1)
        gap = jnp.zeros((D_MODEL, FOX_Q_LANE0 - n_extra), F32)
        tail = jnp.zeros((D_MODEL, LANES - FOX_Q_LANE0 - n_extra), F32)
        wf = jnp.concatenate([wf3, gap, wf3, tail], axis=1).astype(BF16)
        bf3 = jnp.repeat(b_forget[layer], 3)
        bf_rep = jnp.concatenate([bf3, gap[0], bf3, tail[0]])[None, :]

        att_scale = HEAD_DIM ** -0.5 * LOG2E
        gains = [ones] * N_COL_TILES
        gains[TILE_QM] = jnp.tile(g_q_moba[layer], N_HEADS) * att_scale
        gains[TILE_KM] = jnp.tile(g_k_moba[layer], N_HEADS)
        gains[TILE_QF] = jnp.tile(g_q_fox[layer], N_HEADS) * att_scale
        gains[TILE_KF] = jnp.tile(g_k_fox[layer], N_HEADS)
        gains[TILE_QC] = jnp.tile(g_q_mem[layer], MEM_HEADS) * (MEM_HEAD_DIM ** -0.5 * LOG2E)
        gains = jnp.stack(gains)[:, None, :]

        bound_m = _score_bound(g_q_moba[layer], g_k_moba[layer])
        bound_f = _score_bound(g_q_fox[layer], g_k_fox[layer])
        shift_m = _lane_row(bound_m * LOG2E, MOBA_SHIFT_LANE, 1)
        shift_f = _lane_row(bound_f * LOG2E, FOX_Q_LANE0, n_extra)

        proj, faug, kmean = _proj_call(x2, g_mix[layer][None, :], w_qkv, wf, bf_rep, shift_f, bd,
                                       gains, rc, rs1, rs2, seq, tm=512)
        kmean = kmean.reshape(batch, seq // MOBA_BLOCK, COL_TILE)
        mk, mv = _memkv_call(mem, g_mem[layer][None, :], w_mem_kv[layer].astype(BF16),
                             g_k_mem[layer][None, :])
        y_m = lax.cond(2.0 * bound_m <= MAX_SHIFT_NATS,
                       lambda: _moba_call(proj, kmean, shift_m, batch, seq, True),
                       lambda: _moba_call(proj, kmean, shift_m, batch, seq, False))
        y_f = lax.cond(2.0 * bound_f <= MAX_SHIFT_NATS,
                       lambda: _fox_call(proj, faug, batch, seq, True),
                       lambda: _fox_call(proj, faug, batch, seq, False))
        x1 = _merge_call(x2, g_mix[layer][None, :], proj, y_m, y_f, mk, mv, w_gates,
                         w_br_moba[layer].astype(BF16), w_br_fox[layer].astype(BF16),
                         w_br_mem[layer].astype(BF16), w_out[layer].astype(BF16), seq, tm=512)
        x2 = _ffn_call(x1, g_ffn[layer][None, :], w_gate[layer].astype(BF16),
                       w_up[layer].astype(BF16), w_down[layer].astype(BF16), tm=512)
    return x2.reshape(batch, seq, D_MODEL)
```

```python
import functools
import math

import numpy as np
import jax
import jax.numpy as jnp
from jax import lax
from jax.experimental import pallas as pl
from jax.experimental.pallas import tpu as pltpu

F32 = jnp.float32
BF16 = jnp.bfloat16

D_MODEL = 1024
HEAD_DIM = 64
N_HEADS = 8
MEM_HEADS = 4
MEM_HEAD_DIM = 128
ATT_WIDTH = 512
MOBA_BLOCK = 256
MOBA_TOPK = 3
ROPE_THETA = 500000.0
ROPE_DIMS = 16
NORM_EPS = 1e-6
NEG = -1e30
LOG2E = math.log2(math.e)

LANES = 128
COL_TILE = 512
SEG_TILE = 256
N_COL_TILES = 7
PROJ_WIDTH = COL_TILE * N_COL_TILES
TILE_QM, TILE_KM, TILE_QF, TILE_KF, TILE_QC = 0, 1, 3, 4, 6
ATT_TILE = 256
KEY_GROUP = 4
Q_SUB = KEY_GROUP
FFN_CHUNK = 1024
VMEM_LIMIT = 52 * 1024 * 1024

FOX_Q_LANE0 = 32
MOBA_SHIFT_LANE = LANES - 1
MAX_SHIFT_NATS = 60.0
BOUND_SLACK = 1.02


def _nt(a, b):
    return lax.dot_general(a, b, (((1,), (1,)), ((), ())), preferred_element_type=F32)


def _dot(a, b):
    return jnp.dot(a, b, preferred_element_type=F32)


def _split3(v):
    p1 = v.astype(BF16)
    r1 = v - p1.astype(F32)
    p2 = r1.astype(BF16)
    p3 = (r1 - p2.astype(F32)).astype(BF16)
    return p1, p2, p3


def _rms_normed(x, gain_row):
    ms = jnp.mean(x * x, axis=-1, keepdims=True)
    return (x * lax.rsqrt(ms + NORM_EPS) * gain_row).astype(BF16)


def _forget_lanes(cum, shift_row):
    lane = lax.broadcasted_iota(jnp.int32, cum.shape, 1)
    key_side = lane < 3 * N_HEADS
    query_side = (lane >= FOX_Q_LANE0) & (lane < FOX_Q_LANE0 + 3 * N_HEADS)
    cum2 = cum * LOG2E
    val = jnp.where(key_side, -cum2, cum2 - shift_row)
    c1, c2, c3 = _split3(val)
    piece = jnp.where(key_side, lane, lane - FOX_Q_LANE0) % 3
    sel = jnp.where(piece == 0, c1, jnp.where(piece == 1, c2, c3))
    return jnp.where(key_side | query_side, sel, jnp.zeros_like(sel))


def _proj_kernel(x_ref, gmix_ref, w_ref, wf_ref, bf_ref, shift_ref, bd_ref, gain_ref,
                 rc_ref, rs1_ref, rs2_ref, p_ref, faug_ref, kmean_ref, carry_ref, *, seq_tiles):
    tm = x_ref.shape[0]
    h = _rms_normed(x_ref[...], gmix_ref[...])

    @pl.when(pl.program_id(0) % seq_tiles == 0)
    def _():
        carry_ref[...] = jnp.zeros_like(carry_ref)

    z = _dot(h, wf_ref[...]) + bf_ref[...]
    lf = jnp.minimum(z, 0.0) - jnp.log1p(jnp.exp(-jnp.abs(z)))
    row = lax.broadcasted_iota(jnp.int32, (tm, tm), 0)
    col = lax.broadcasted_iota(jnp.int32, (tm, tm), 1)
    tri = jnp.where(col <= row, 1.0, 0.0).astype(BF16)
    p1, p2, p3 = _split3(lf)
    cum = _dot(tri, p1) + _dot(tri, p2) + _dot(tri, p3) + carry_ref[...]
    carry_ref[...] = cum[tm - 1:tm, :]
    faug_ref[...] = _forget_lanes(cum, shift_ref[...])

    def head_normed(t, k):
        seg = 1 if k == TILE_QC else 0
        sq = (t * t).astype(BF16)
        ms = jnp.concatenate(
            [_dot(sq[:, lo:lo + SEG_TILE], bd_ref[seg]) for lo in range(0, COL_TILE, SEG_TILE)],
            axis=1)
        return t * lax.rsqrt(ms + NORM_EPS) * gain_ref[k]

    def rotated(y):
        rc, rs1, rs2 = rc_ref[...], rs1_ref[...], rs2_ref[...]
        parts = []
        for g in range(COL_TILE // LANES):
            yg = y[:, g * LANES:(g + 1) * LANES]
            up = pltpu.roll(yg, LANES - ROPE_DIMS // 2, 1)
            dn = pltpu.roll(yg, ROPE_DIMS // 2, 1)
            parts.append(yg * rc + up * rs1 + dn * rs2)
        return jnp.concatenate(parts, axis=1)

    for k in range(N_COL_TILES):
        cols = slice(k * COL_TILE, (k + 1) * COL_TILE)
        t = _dot(h, w_ref[:, cols])
        if k in (TILE_QM, TILE_KM):
            t = rotated(head_normed(t, k))
        elif k in (TILE_QF, TILE_KF, TILE_QC):
            t = head_normed(t, k)
        p_ref[:, cols] = t.astype(BF16)
        if k == TILE_KM:
            for r in range(tm // MOBA_BLOCK):
                blk = t[r * MOBA_BLOCK:(r + 1) * MOBA_BLOCK, :]
                kmean_ref[0, pl.ds(r, 1), :] = jnp.mean(blk, axis=0, keepdims=True)


def _proj_call(x2, gmix, w_qkv, wf, bf_row, shift_row, bd, gains, rc, rs1, rs2, seq, tm):
    n_tok = x2.shape[0]
    seq_tiles = seq // tm
    const2 = lambda i: (0, 0)
    const3 = lambda i: (0, 0, 0)
    return pl.pallas_call(
        functools.partial(_proj_kernel, seq_tiles=seq_tiles),
        grid=(n_tok // tm,),
        in_specs=[
            pl.BlockSpec((tm, D_MODEL), lambda i: (i, 0)),
            pl.BlockSpec((1, D_MODEL), const2),
            pl.BlockSpec((D_MODEL, PROJ_WIDTH), const2),
            pl.BlockSpec((D_MODEL, LANES), const2),
            pl.BlockSpec((1, LANES), const2),
            pl.BlockSpec((1, LANES), const2),
            pl.BlockSpec((2, SEG_TILE, SEG_TILE), const3),
            pl.BlockSpec((N_COL_TILES, 1, COL_TILE), const3),
            pl.BlockSpec((tm, LANES), lambda i: (i % seq_tiles, 0)),
            pl.BlockSpec((tm, LANES), lambda i: (i % seq_tiles, 0)),
            pl.BlockSpec((tm, LANES), lambda i: (i % seq_tiles, 0)),
        ],
        out_specs=[
            pl.BlockSpec((tm, PROJ_WIDTH), lambda i: (i, 0)),
            pl.BlockSpec((tm, LANES), lambda i: (i, 0)),
            pl.BlockSpec((1, tm // MOBA_BLOCK, COL_TILE), lambda i: (i, 0, 0)),
        ],
        out_shape=[
            jax.ShapeDtypeStruct((n_tok, PROJ_WIDTH), BF16),
            jax.ShapeDtypeStruct((n_tok, LANES), BF16),
            jax.ShapeDtypeStruct((n_tok // tm, tm // MOBA_BLOCK, COL_TILE), F32),
        ],
        scratch_shapes=[pltpu.VMEM((1, LANES), F32)],
        compiler_params=pltpu.CompilerParams(
            dimension_semantics=("arbitrary",), vmem_limit_bytes=VMEM_LIMIT),
        name="proj",
    )(x2, gmix, w_qkv, wf, bf_row, shift_row, bd, gains, rc, rs1, rs2)


def _memkv_kernel(mem_ref, gmem_ref, w_ref, gk_ref, mk_ref, mv_ref):
    hm = _rms_normed(mem_ref[0], gmem_ref[...])
    kv = _dot(hm, w_ref[...])
    ks = []
    for h in range(MEM_HEADS):
        kh = kv[:, h * MEM_HEAD_DIM:(h + 1) * MEM_HEAD_DIM]
        msk = jnp.mean(kh * kh, axis=-1, keepdims=True)
        ks.append(kh * lax.rsqrt(msk + NORM_EPS) * gk_ref[...])
    mk_ref[0] = jnp.concatenate(ks, axis=1).astype(BF16)
    mv_ref[0] = kv[:, ATT_WIDTH:].astype(BF16)


def _memkv_call(mem, gmem, w_kv, gk):
    batch, mlen, _ = mem.shape
    return pl.pallas_call(
        _memkv_kernel,
        grid=(batch,),
        in_specs=[
            pl.BlockSpec((1, mlen, D_MODEL), lambda b: (b, 0, 0)),
            pl.BlockSpec((1, D_MODEL), lambda b: (0, 0)),
            pl.BlockSpec((D_MODEL, 2 * ATT_WIDTH), lambda b: (0, 0)),
            pl.BlockSpec((1, MEM_HEAD_DIM), lambda b: (0, 0)),
        ],
        out_specs=[
            pl.BlockSpec((1, mlen, ATT_WIDTH), lambda b: (b, 0, 0)),
            pl.BlockSpec((1, mlen, ATT_WIDTH), lambda b: (b, 0, 0)),
        ],
        out_shape=[jax.ShapeDtypeStruct((batch, mlen, ATT_WIDTH), BF16)] * 2,
        compiler_params=pltpu.CompilerParams(dimension_semantics=("parallel",)),
        name="memkv",
    )(mem, gmem, w_kv, gk)


def _lane_iota():
    return lax.broadcasted_iota(jnp.int32, (ATT_TILE, LANES), 1)


def _lane_range(first, count, dtype):
    lane = _lane_iota()
    return jnp.where((lane >= first) & (lane < first + count), 1.0, 0.0).astype(dtype)


def _head_lanes(c, dtype):
    return _lane_range(c * HEAD_DIM, HEAD_DIM, dtype)


def _fill_values(v_ref, vaug_ref):
    n_blocks = v_ref.shape[0] // ATT_TILE

    def fill(j, _):
        rows = pl.ds(pl.multiple_of(j * ATT_TILE, ATT_TILE), ATT_TILE)
        v = v_ref[rows, :].astype(F32)
        for c in range(2):
            own = _head_lanes(c, F32)
            vaug_ref[c, rows, :] = (v * own + (1.0 - own)).astype(BF16)
        return 0

    lax.fori_loop(0, n_blocks, fill, 0)


def _normalised(acc_ref, t):
    res = []
    for c in range(2):
        acc = acc_ref[2 * t + c]
        res.append(acc / pltpu.roll(acc, HEAD_DIM, 1))
    return jnp.where(_lane_iota() < HEAD_DIM, res[0], res[1])


def _causal_masks():
    span = KEY_GROUP * ATT_TILE
    diff = (lax.broadcasted_iota(jnp.int32, (ATT_TILE, span), 1)
            - lax.broadcasted_iota(jnp.int32, (ATT_TILE, span), 0))
    return [diff <= t * ATT_TILE for t in range(Q_SUB)]


def _attend_shifted(lhs, kaug_ref, vaug_ref, acc_ref, s_ref, qi):
    span = KEY_GROUP * ATT_TILE
    n_chain = len(lhs)

    def group_rows(g):
        return pl.ds(pl.multiple_of(g * span, span), span)

    def scores(g, slot):
        kb = kaug_ref[group_rows(g), :]
        for i in range(n_chain):
            s_ref[slot, i] = _nt(lhs[i], kb)

    def consume(g, slot):
        rows = group_rows(g)
        for i in range(n_chain):
            p = jnp.exp2(s_ref[slot, i]).astype(BF16)
            acc_ref[i] += _dot(p, vaug_ref[i % 2, rows, :])

    def consume_diagonal(g, slot):
        row = lax.broadcasted_iota(jnp.int32, (ATT_TILE, ATT_TILE), 0)
        col = lax.broadcasted_iota(jnp.int32, (ATT_TILE, ATT_TILE), 1)
        for i in range(n_chain):
            t = i // 2
            lo = t * ATT_TILE
            parts = []
            if t:
                parts.append(jnp.exp2(s_ref[slot, i, :, 0:lo]).astype(BF16))
            s_diag = jnp.where(col <= row, s_ref[slot, i, :, lo:lo + ATT_TILE], NEG)
            parts.append(jnp.exp2(s_diag).astype(BF16))
            p = parts[0] if len(parts) == 1 else jnp.concatenate(parts, axis=1)
            rows = pl.ds(pl.multiple_of(g * span, span), lo + ATT_TILE)
            acc_ref[i] += _dot(p, vaug_ref[i % 2, rows, :])

    def scores_diagonal(g, slot):
        for i in range(n_chain):
            rows = pl.ds(pl.multiple_of(g * span, span), (i // 2 + 1) * ATT_TILE)
            s_ref[slot, i, :, 0:(i // 2 + 1) * ATT_TILE] = _nt(lhs[i], kaug_ref[rows, :])

    acc_ref[...] = jnp.zeros_like(acc_ref)
    n_full = qi

    @pl.when(n_full == 0)
    def _():
        scores_diagonal(0, 0)

    @pl.when(n_full > 0)
    def _():
        scores(0, 0)

        def body(g, _):
            slot = lax.rem(g, 2)
            consume(g, slot)
            scores(g + 1, 1 - slot)
            return 0

        lax.fori_loop(0, n_full - 1, body, 0)
        last = lax.rem(n_full - 1, 2)
        consume(n_full - 1, last)
        scores_diagonal(n_full, 1 - last)

    consume_diagonal(n_full, lax.rem(n_full, 2))
    return [_normalised(acc_ref, t) for t in range(Q_SUB)]


def _attend_running_max(lhs, kaug_ref, vaug_ref, acc_ref, s_ref, qi):
    span = KEY_GROUP * ATT_TILE
    n_chain = len(lhs)

    def step(i, g, m, masks):
        rows = pl.ds(pl.multiple_of(g * span, span), span)
        s = _nt(lhs[i], kaug_ref[rows, :])
        if masks is not None:
            s = jnp.where(masks[i // 2], s, NEG)
        m_new = jnp.maximum(m, jnp.max(s, axis=1, keepdims=True))
        p = jnp.exp2(s - m_new).astype(BF16)
        acc_ref[i] = jnp.exp2(m - m_new) * acc_ref[i] + _dot(p, vaug_ref[i % 2, rows, :])
        return m_new

    acc_ref[...] = jnp.zeros_like(acc_ref)
    n_full = qi
    m0 = jnp.full((ATT_TILE, 1), -jnp.inf, F32)
    ms = lax.fori_loop(
        0, n_full, lambda g, ms: tuple(step(i, g, ms[i], None) for i in range(n_chain)),
        (m0,) * n_chain)
    masks = _causal_masks()
    for i in range(n_chain):
        step(i, n_full, ms[i], masks)
    return [_normalised(acc_ref, t) for t in range(Q_SUB)]


def _attend(lhs, kaug_ref, vaug_ref, acc_ref, s_ref, qi, bounded):
    fn = _attend_shifted if bounded else _attend_running_max
    return fn(lhs, kaug_ref, vaug_ref, acc_ref, s_ref, qi)


def _attention_scratch(seq):
    n_chain = 2 * Q_SUB
    return [pltpu.VMEM((seq, 2 * LANES), BF16),
            pltpu.VMEM((2, seq, LANES), BF16),
            pltpu.VMEM((n_chain, ATT_TILE, LANES), F32),
            pltpu.VMEM((2, n_chain, ATT_TILE, KEY_GROUP * ATT_TILE), F32)]


def _sub_rows(t):
    return slice(t * ATT_TILE, (t + 1) * ATT_TILE)


def _moba_kernel(q_ref, k_ref, v_ref, km_ref, shift_ref, o_ref, kaug_ref, vaug_ref, acc_ref,
                 s_ref, *, bounded):
    qi = pl.program_id(2)
    seq = k_ref.shape[0]
    n_blocks = seq // MOBA_BLOCK
    lane = _lane_iota()

    @pl.when(qi == 0)
    def _():
        kaug_ref[:, 0:LANES] = k_ref[...]

        def fill(j, _):
            rows = pl.ds(pl.multiple_of(j * MOBA_BLOCK, MOBA_BLOCK), MOBA_BLOCK)
            onehot = (lane == j) | (lane == MOBA_SHIFT_LANE)
            kaug_ref[rows, LANES:2 * LANES] = jnp.where(onehot, 1.0, 0.0).astype(BF16)
            return 0

        lax.fori_loop(0, n_blocks, fill, 0)
        _fill_values(v_ref, vaug_ref)

    km = km_ref[0]
    km = jnp.concatenate([km, jnp.zeros((LANES - n_blocks, LANES), F32)], axis=0)
    km_hi = km.astype(BF16)
    km_lo = (km - km_hi.astype(F32)).astype(BF16)
    shift = shift_ref[:, MOBA_SHIFT_LANE:MOBA_SHIFT_LANE + 1] if bounded else 0.0
    blk = lax.broadcasted_iota(jnp.int32, (n_blocks, ATT_TILE), 0)
    blk_f = blk.astype(F32)
    pad_row = lax.broadcasted_iota(jnp.int32, (LANES - n_blocks, ATT_TILE), 0) + n_blocks
    pad_bias = jnp.where(pad_row == MOBA_SHIFT_LANE, -shift, NEG)

    lhs = []
    for t in range(Q_SUB):
        own = qi * Q_SUB + t
        q2 = q_ref[_sub_rows(t), :]
        for c in range(2):
            qc = q2 * _head_lanes(c, BF16)
            gate = (_nt(km_hi, qc) + _nt(km_lo, qc))[0:n_blocks]
            g = jnp.where(blk < own, gate, -jnp.inf)
            sel = blk == own
            for _ in range(MOBA_TOPK):
                mx = jnp.max(g, axis=0, keepdims=True)
                first = jnp.min(jnp.where(g == mx, blk_f, float(LANES)), axis=0, keepdims=True)
                pick = (blk_f == first) & (mx > -jnp.inf)
                sel = sel | pick
                g = jnp.where(pick, -jnp.inf, g)
            bias = jnp.concatenate([jnp.where(sel, 0.0, NEG), pad_bias], axis=0)
            lhs.append(jnp.concatenate([qc, bias.T.astype(BF16)], axis=1))

    outs = _attend(lhs, kaug_ref, vaug_ref, acc_ref, s_ref, qi, bounded)
    for t in range(Q_SUB):
        o_ref[_sub_rows(t), :] = outs[t].astype(BF16)


def _moba_call(proj, kmean, shift_row, batch, seq, bounded):
    tq = Q_SUB * ATT_TILE
    n_q = seq // tq
    n_pairs = N_HEADS // 2
    q0 = (TILE_QM * COL_TILE) // LANES
    k0 = (TILE_KM * COL_TILE) // LANES
    v0 = k0 + COL_TILE // LANES
    return pl.pallas_call(
        functools.partial(_moba_kernel, bounded=bounded),
        grid=(batch, n_pairs, n_q),
        in_specs=[
            pl.BlockSpec((tq, LANES), lambda b, p, i: (b * n_q + i, q0 + p)),
            pl.BlockSpec((seq, LANES), lambda b, p, i: (b, k0 + p)),
            pl.BlockSpec((seq, LANES), lambda b, p, i: (b, v0 + p)),
            pl.BlockSpec((1, seq // MOBA_BLOCK, LANES), lambda b, p, i: (b, 0, p)),
            pl.BlockSpec((1, LANES), lambda b, p, i: (0, 0)),
        ],
        out_specs=pl.BlockSpec((tq, LANES), lambda b, p, i: (b * n_q + i, p)),
        out_shape=jax.ShapeDtypeStruct((batch * seq, ATT_WIDTH), BF16),
        scratch_shapes=_attention_scratch(seq),
        compiler_params=pltpu.CompilerParams(
            dimension_semantics=("parallel", "parallel", "arbitrary"),
            vmem_limit_bytes=VMEM_LIMIT),
        name="moba_bounded" if bounded else "moba_general",
    )(proj, proj, proj, kmean, shift_row)


def _fox_kernel(q_ref, k_ref, v_ref, faug_ref, fq_ref, o_ref, kaug_ref, vaug_ref, acc_ref,
                s_ref, *, bounded):
    pair = pl.program_id(1)
    qi = pl.program_id(2)
    n_extra = 3 * N_HEADS

    @pl.when(qi == 0)
    def _():
        seq = k_ref.shape[0]
        kaug_ref[:, 0:LANES] = k_ref[...]

        def fill(j, _):
            rows = pl.ds(pl.multiple_of(j * ATT_TILE, ATT_TILE), ATT_TILE)
            lane = _lane_iota()
            ones = (lane >= FOX_Q_LANE0) & (lane < FOX_Q_LANE0 + n_extra)
            f = faug_ref[rows, :].astype(F32)
            kaug_ref[rows, LANES:2 * LANES] = (
                f * (lane < n_extra).astype(F32) + ones.astype(F32)).astype(BF16)
            return 0

        lax.fori_loop(0, seq // ATT_TILE, fill, 0)
        _fill_values(v_ref, vaug_ref)

    lhs = []
    for t in range(Q_SUB):
        q2 = q_ref[_sub_rows(t), :]
        fq = fq_ref[_sub_rows(t), :]
        for c in range(2):
            qc = q2 * _head_lanes(c, BF16)
            k_first = 3 * (2 * pair + c)
            extra = _lane_range(k_first, 3, BF16)
            if bounded:
                extra = extra + fq * _lane_range(FOX_Q_LANE0 + k_first, 3, BF16)
            lhs.append(jnp.concatenate([qc, extra], axis=1))

    outs = _attend(lhs, kaug_ref, vaug_ref, acc_ref, s_ref, qi, bounded)
    for t in range(Q_SUB):
        o_ref[_sub_rows(t), :] = outs[t].astype(BF16)


def _fox_call(proj, faug, batch, seq, bounded):
    tq = Q_SUB * ATT_TILE
    n_q = seq // tq
    n_pairs = N_HEADS // 2
    q0 = (TILE_QF * COL_TILE) // LANES
    k0 = (TILE_KF * COL_TILE) // LANES
    v0 = k0 + COL_TILE // LANES
    return pl.pallas_call(
        functools.partial(_fox_kernel, bounded=bounded),
        grid=(batch, n_pairs, n_q),
        in_specs=[
            pl.BlockSpec((tq, LANES), lambda b, p, i: (b * n_q + i, q0 + p)),
            pl.BlockSpec((seq, LANES), lambda b, p, i: (b, k0 + p)),
            pl.BlockSpec((seq, LANES), lambda b, p, i: (b, v0 + p)),
            pl.BlockSpec((seq, LANES), lambda b, p, i: (b, 0)),
            pl.BlockSpec((tq, LANES), lambda b, p, i: (b * n_q + i, 0)),
        ],
        out_specs=pl.BlockSpec((tq, LANES), lambda b, p, i: (b * n_q + i, p)),
        out_shape=jax.ShapeDtypeStruct((batch * seq, ATT_WIDTH), BF16),
        scratch_shapes=_attention_scratch(seq),
        compiler_params=pltpu.CompilerParams(
            dimension_semantics=("parallel", "parallel", "arbitrary"),
            vmem_limit_bytes=VMEM_LIMIT),
        name="fox_bounded" if bounded else "fox_general",
    )(proj, proj, proj, faug, faug)


def _merge_kernel(x_ref, gmix_ref, qc_ref, ym_ref, yf_ref, mk_ref, mv_ref,
                  wg_ref, wbm_ref, wbf_ref, wbc_ref, wo_ref, o_ref):
    x = x_ref[...]
    h = _rms_normed(x, gmix_ref[...])
    qc = qc_ref[...]
    mk = mk_ref[0]
    mv = mv_ref[0]
    ys = []
    for hd in range(MEM_HEADS):
        cols = slice(hd * MEM_HEAD_DIM, (hd + 1) * MEM_HEAD_DIM)
        s = _nt(qc[:, cols], mk[:, cols])
        p = jnp.exp2(s - jnp.max(s, axis=1, keepdims=True))
        y = _dot(p.astype(BF16), mv[:, cols]) / jnp.sum(p, axis=1, keepdims=True)
        ys.append(y)
    yc = jnp.concatenate(ys, axis=1).astype(BF16)
    merged = None
    for i, (y, wb_ref) in enumerate(((ym_ref[...], wbm_ref), (yf_ref[...], wbf_ref),
                                     (yc, wbc_ref))):
        gate = jax.nn.sigmoid(_dot(h, wg_ref[:, i * D_MODEL:(i + 1) * D_MODEL]))
        term = gate * _dot(y, wb_ref[...])
        merged = term if merged is None else merged + term
    o_ref[...] = x + _dot(merged.astype(BF16), wo_ref[...])


def _merge_call(x2, gmix, proj, y_m, y_f, mk, mv, wg, wbm, wbf, wbc, wo, seq, tm):
    n_tok = x2.shape[0]
    mlen = mk.shape[1]
    seq_tiles = seq // tm
    const = lambda i: (0, 0)
    return pl.pallas_call(
        _merge_kernel,
        grid=(n_tok // tm,),
        in_specs=[
            pl.BlockSpec((tm, D_MODEL), lambda i: (i, 0)),
            pl.BlockSpec((1, D_MODEL), const),
            pl.BlockSpec((tm, ATT_WIDTH), lambda i: (i, TILE_QC)),
            pl.BlockSpec((tm, ATT_WIDTH), lambda i: (i, 0)),
            pl.BlockSpec((tm, ATT_WIDTH), lambda i: (i, 0)),
            pl.BlockSpec((1, mlen, ATT_WIDTH), lambda i: (i // seq_tiles, 0, 0)),
            pl.BlockSpec((1, mlen, ATT_WIDTH), lambda i: (i // seq_tiles, 0, 0)),
            pl.BlockSpec((D_MODEL, 3 * D_MODEL), const),
            pl.BlockSpec((ATT_WIDTH, D_MODEL), const),
            pl.BlockSpec((ATT_WIDTH, D_MODEL), const),
            pl.BlockSpec((ATT_WIDTH, D_MODEL), const),
            pl.BlockSpec((D_MODEL, D_MODEL), const),
        ],
        out_specs=pl.BlockSpec((tm, D_MODEL), lambda i: (i, 0)),
        out_shape=jax.ShapeDtypeStruct((n_tok, D_MODEL), F32),
        compiler_params=pltpu.CompilerParams(
            dimension_semantics=("parallel",), vmem_limit_bytes=VMEM_LIMIT),
        name="merge",
    )(x2, gmix, proj, y_m, y_f, mk, mv, wg, wbm, wbf, wbc, wo)


def _ffn_kernel(x_ref, g_ref, wg_ref, wu_ref, wd_ref, o_ref):
    x = x_ref[...]
    h = _rms_normed(x, g_ref[...])
    d_ff = wg_ref.shape[1]
    out = x
    for lo in range(0, d_ff, FFN_CHUNK):
        cols = slice(lo, min(lo + FFN_CHUNK, d_ff))
        g = _dot(h, wg_ref[:, cols])
        u = _dot(h, wu_ref[:, cols])
        a = (g * jax.nn.sigmoid(g) * u).astype(BF16)
        out = out + _dot(a, wd_ref[cols, :])
    o_ref[...] = out


def _ffn_call(x1, g_ffn, wg, wu, wd, tm):
    n_tok = x1.shape[0]
    d_ff = wg.shape[1]
    const = lambda i: (0, 0)
    resident = dict(pipeline_mode=pl.Buffered(1))
    return pl.pallas_call(
        _ffn_kernel,
        grid=(n_tok // tm,),
        in_specs=[
            pl.BlockSpec((tm, D_MODEL), lambda i: (i, 0)),
            pl.BlockSpec((1, D_MODEL), const),
            pl.BlockSpec((D_MODEL, d_ff), const, **resident),
            pl.BlockSpec((D_MODEL, d_ff), const, **resident),
            pl.BlockSpec((d_ff, D_MODEL), const, **resident),
        ],
        out_specs=pl.BlockSpec((tm, D_MODEL), lambda i: (i, 0)),
        out_shape=jax.ShapeDtypeStruct((n_tok, D_MODEL), F32),
        compiler_params=pltpu.CompilerParams(
            dimension_semantics=("parallel",), vmem_limit_bytes=VMEM_LIMIT),
        name="ffn",
    )(x1, g_ffn, wg, wu, wd)


@functools.lru_cache(maxsize=None)
def _rope_tables(seq):
    half = ROPE_DIMS // 2
    inv_freq = 1.0 / (ROPE_THETA ** (np.arange(half, dtype=np.float64) * 2.0 / ROPE_DIMS))
    ang = np.arange(seq, dtype=np.float64)[:, None] * inv_freq[None, :]
    cos, sin = np.cos(ang), np.sin(ang)
    d = np.arange(LANES) % HEAD_DIM
    rc = np.where(d[None, :] < ROPE_DIMS, cos[:, d % half], 1.0)
    rs1 = np.where(d[None, :] < half, -sin[:, d % half], 0.0)
    rs2 = np.where((d[None, :] >= half) & (d[None, :] < ROPE_DIMS), sin[:, d % half], 0.0)
    return tuple(np.asarray(t, np.float32) for t in (rc, rs1, rs2))


@functools.lru_cache(maxsize=None)
def _segment_mean_matrices():
    def blockdiag(width):
        return np.kron(np.eye(SEG_TILE // width), np.ones((width, width))) / width
    return np.stack([blockdiag(HEAD_DIM), blockdiag(MEM_HEAD_DIM)]).astype(np.float32)


def _score_bound(g_q, g_k):
    return (BOUND_SLACK * HEAD_DIM ** 0.5 * jnp.max(jnp.abs(g_q)) * jnp.max(jnp.abs(g_k)))


def _lane_row(value, first, count):
    lane = jnp.arange(LANES)
    return jnp.where((lane >= first) & (lane < first + count), value, 0.0).astype(F32)[None, :]


def kernel(x, mem, g_mix, w_in, b_forget, g_q_moba, g_k_moba, g_q_fox, g_k_fox, g_q_mem, g_k_mem,
           g_mem, w_mem_kv, w_br_moba, w_br_fox, w_br_mem, w_out, g_ffn, w_gate, w_up, w_down):
    batch, seq, _ = x.shape
    depth = g_mix.shape[0]
    n_tok = batch * seq
    assert seq % (KEY_GROUP * ATT_TILE) == 0 and seq // MOBA_BLOCK < MOBA_SHIFT_LANE
    n_extra = 3 * N_HEADS

    rc, rs1, rs2 = (jnp.asarray(t) for t in _rope_tables(seq))
    bd = jnp.asarray(_segment_mean_matrices()).astype(BF16)
    ones = jnp.ones((COL_TILE,), F32)

    x2 = x.reshape(n_tok, D_MODEL)
    for layer in range(depth):
        w = w_in[layer]
        gates0 = 7 * ATT_WIDTH + N_HEADS
        qc0 = 6 * ATT_WIDTH + N_HEADS
        w_qkv = jnp.concatenate([w[:, :6 * ATT_WIDTH], w[:, qc0:gates0]], axis=1).astype(BF16)
        w_gates = w[:, gates0:].astype(BF16)
        wf3 = jnp.repeat(w[:, 6 * ATT_WIDTH:qc0], 3, axis=1)
        gap = jnp.zeros((D_MODEL, FOX_Q_LANE0 - n_extra), F32)
        tail = jnp.zeros((D_MODEL, LANES - FOX_Q_LANE0 - n_extra), F32)
        wf = jnp.concatenate([wf3, gap, wf3, tail], axis=1).astype(BF16)
        bf3 = jnp.repeat(b_forget[layer], 3)
        bf_rep = jnp.concatenate([bf3, gap[0], bf3, tail[0]])[None, :]

        att_scale = HEAD_DIM ** -0.5 * LOG2E
        gains = [ones] * N_COL_TILES
        gains[TILE_QM] = jnp.tile(g_q_moba[layer], N_HEADS) * att_scale
        gains[TILE_KM] = jnp.tile(g_k_moba[layer], N_HEADS)
        gains[TILE_QF] = jnp.tile(g_q_fox[layer], N_HEADS) * att_scale
        gains[TILE_KF] = jnp.tile(g_k_fox[layer], N_HEADS)
        gains[TILE_QC] = jnp.tile(g_q_mem[layer], MEM_HEADS) * (MEM_HEAD_DIM ** -0.5 * LOG2E)
        gains = jnp.stack(gains)[:, None, :]

        bound_m = _score_bound(g_q_moba[layer], g_k_moba[layer])
        bound_f = _score_bound(g_q_fox[layer], g_k_fox[layer])
        shift_m = _lane_row(bound_m * LOG2E, MOBA_SHIFT_LANE, 1)
        shift_f = _lane_row(bound_f * LOG2E, FOX_Q_LANE0, n_extra)

        proj, faug, kmean = _proj_call(x2, g_mix[layer][None, :], w_qkv, wf, bf_rep, shift_f, bd,
                                       gains, rc, rs1, rs2, seq, tm=512)
        kmean = kmean.reshape(batch, seq // MOBA_BLOCK, COL_TILE)
        mk, mv = _memkv_call(mem, g_mem[layer][None, :], w_mem_kv[layer].astype(BF16),
                             g_k_mem[layer][None, :])
        y_m = lax.cond(2.0 * bound_m <= MAX_SHIFT_NATS,
                       lambda: _moba_call(proj, kmean, shift_m, batch, seq, True),
                       lambda: _moba_call(proj, kmean, shift_m, batch, seq, False))
        y_f = lax.cond(2.0 * bound_f <= MAX_SHIFT_NATS,
                       lambda: _fox_call(proj, faug, batch, seq, True),
                       lambda: _fox_call(proj, faug, batch, seq, False))
        x1 = _merge_call(x2, g_mix[layer][None, :], proj, y_m, y_f, mk, mv, w_gates,
                         w_br_moba[layer].astype(BF16), w_br_fox[layer].astype(BF16),
                         w_br_mem[layer].astype(BF16), w_out[layer].astype(BF16), seq, tm=512)
        x2 = _ffn_call(x1, g_ffn[layer][None, :], w_gate[layer].astype(BF16),
                       w_up[layer].astype(BF16), w_down[layer].astype(BF16), tm=512)
    return x2.reshape(batch, seq, D_MODEL)
```

```python
import functools
import math

import numpy as np
import jax
import jax.numpy as jnp
from jax import lax
from jax.experimental import pallas as pl
from jax.experimental.pallas import tpu as pltpu

F32 = jnp.float32
BF16 = jnp.bfloat16

D_MODEL = 1024
HEAD_DIM = 64
N_HEADS = 8
MEM_HEADS = 4
MEM_HEAD_DIM = 128
ATT_WIDTH = 512
MOBA_BLOCK = 256
MOBA_TOPK = 3
ROPE_THETA = 500000.0
ROPE_DIMS = 16
NORM_EPS = 1e-6
NEG = -1e30
LOG2E = math.log2(math.e)

LANES = 128
COL_TILE = 512
SEG_TILE = 256
N_COL_TILES = 7
PROJ_WIDTH = COL_TILE * N_COL_TILES
TILE_QM, TILE_KM, TILE_QF, TILE_KF, TILE_QC = 0, 1, 3, 4, 6
ATT_TILE = 256
KEY_GROUP = 4
Q_SUB = KEY_GROUP
PRE_SLOT = 2
FFN_CHUNK = 1024
VMEM_LIMIT = 52 * 1024 * 1024

FOX_Q_LANE0 = 32
MOBA_SHIFT_LANE = LANES - 1
MAX_SHIFT_NATS = 60.0
BOUND_SLACK = 1.02


def _nt(a, b):
    return lax.dot_general(a, b, (((1,), (1,)), ((), ())), preferred_element_type=F32)


def _dot(a, b):
    return jnp.dot(a, b, preferred_element_type=F32)


def _split3(v):
    p1 = v.astype(BF16)
    r1 = v - p1.astype(F32)
    p2 = r1.astype(BF16)
    p3 = (r1 - p2.astype(F32)).astype(BF16)
    return p1, p2, p3


def _rms_normed(x, gain_row):
    ms = jnp.mean(x * x, axis=-1, keepdims=True)
    return (x * lax.rsqrt(ms + NORM_EPS) * gain_row).astype(BF16)


def _forget_lanes(cum, shift_row):
    lane = lax.broadcasted_iota(jnp.int32, cum.shape, 1)
    key_side = lane < 3 * N_HEADS
    query_side = (lane >= FOX_Q_LANE0) & (lane < FOX_Q_LANE0 + 3 * N_HEADS)
    cum2 = cum * LOG2E
    val = jnp.where(key_side, -cum2, cum2 - shift_row)
    c1, c2, c3 = _split3(val)
    piece = jnp.where(key_side, lane, lane - FOX_Q_LANE0) % 3
    sel = jnp.where(piece == 0, c1, jnp.where(piece == 1, c2, c3))
    return jnp.where(key_side | query_side, sel, jnp.zeros_like(sel))


def _proj_kernel(x_ref, gmix_ref, w_ref, wf_ref, bf_ref, shift_ref, bd_ref, gain_ref,
                 rc_ref, rs1_ref, rs2_ref, p_ref, faug_ref, kmean_ref, carry_ref, *, seq_tiles):
    tm = x_ref.shape[0]
    h = _rms_normed(x_ref[...], gmix_ref[...])

    @pl.when(pl.program_id(0) % seq_tiles == 0)
    def _():
        carry_ref[...] = jnp.zeros_like(carry_ref)

    z = _dot(h, wf_ref[...]) + bf_ref[...]
    lf = jnp.minimum(z, 0.0) - jnp.log1p(jnp.exp(-jnp.abs(z)))
    row = lax.broadcasted_iota(jnp.int32, (tm, tm), 0)
    col = lax.broadcasted_iota(jnp.int32, (tm, tm), 1)
    tri = jnp.where(col <= row, 1.0, 0.0).astype(BF16)
    p1, p2, p3 = _split3(lf)
    cum = _dot(tri, p1) + _dot(tri, p2) + _dot(tri, p3) + carry_ref[...]
    carry_ref[...] = cum[tm - 1:tm, :]
    faug_ref[...] = _forget_lanes(cum, shift_ref[...])

    def head_normed(t, k):
        seg = 1 if k == TILE_QC else 0
        sq = (t * t).astype(BF16)
        ms = jnp.concatenate(
            [_dot(sq[:, lo:lo + SEG_TILE], bd_ref[seg]) for lo in range(0, COL_TILE, SEG_TILE)],
            axis=1)
        return t * lax.rsqrt(ms + NORM_EPS) * gain_ref[k]

    def rotated(y):
        rc, rs1, rs2 = rc_ref[...], rs1_ref[...], rs2_ref[...]
        parts = []
        for g in range(COL_TILE // LANES):
            yg = y[:, g * LANES:(g + 1) * LANES]
            up = pltpu.roll(yg, LANES - ROPE_DIMS // 2, 1)
            dn = pltpu.roll(yg, ROPE_DIMS // 2, 1)
            parts.append(yg * rc + up * rs1 + dn * rs2)
        return jnp.concatenate(parts, axis=1)

    for k in range(N_COL_TILES):
        cols = slice(k * COL_TILE, (k + 1) * COL_TILE)
        t = _dot(h, w_ref[:, cols])
        if k in (TILE_QM, TILE_KM):
            t = rotated(head_normed(t, k))
        elif k in (TILE_QF, TILE_KF, TILE_QC):
            t = head_normed(t, k)
        p_ref[:, cols] = t.astype(BF16)
        if k == TILE_KM:
            for r in range(tm // MOBA_BLOCK):
                blk = t[r * MOBA_BLOCK:(r + 1) * MOBA_BLOCK, :]
                kmean_ref[0, pl.ds(r, 1), :] = jnp.mean(blk, axis=0, keepdims=True)


def _proj_call(x2, gmix, w_qkv, wf, bf_row, shift_row, bd, gains, rc, rs1, rs2, seq, tm):
    n_tok = x2.shape[0]
    seq_tiles = seq // tm
    const2 = lambda i: (0, 0)
    const3 = lambda i: (0, 0, 0)
    return pl.pallas_call(
        functools.partial(_proj_kernel, seq_tiles=seq_tiles),
        grid=(n_tok // tm,),
        in_specs=[
            pl.BlockSpec((tm, D_MODEL), lambda i: (i, 0)),
            pl.BlockSpec((1, D_MODEL), const2),
            pl.BlockSpec((D_MODEL, PROJ_WIDTH), const2),
            pl.BlockSpec((D_MODEL, LANES), const2),
            pl.BlockSpec((1, LANES), const2),
            pl.BlockSpec((1, LANES), const2),
            pl.BlockSpec((2, SEG_TILE, SEG_TILE), const3),
            pl.BlockSpec((N_COL_TILES, 1, COL_TILE), const3),
            pl.BlockSpec((tm, LANES), lambda i: (i % seq_tiles, 0)),
            pl.BlockSpec((tm, LANES), lambda i: (i % seq_tiles, 0)),
            pl.BlockSpec((tm, LANES), lambda i: (i % seq_tiles, 0)),
        ],
        out_specs=[
            pl.BlockSpec((tm, PROJ_WIDTH), lambda i: (i, 0)),
            pl.BlockSpec((tm, LANES), lambda i: (i, 0)),
            pl.BlockSpec((1, tm // MOBA_BLOCK, COL_TILE), lambda i: (i, 0, 0)),
        ],
        out_shape=[
            jax.ShapeDtypeStruct((n_tok, PROJ_WIDTH), BF16),
            jax.ShapeDtypeStruct((n_tok, LANES), BF16),
            jax.ShapeDtypeStruct((n_tok // tm, tm // MOBA_BLOCK, COL_TILE), F32),
        ],
        scratch_shapes=[pltpu.VMEM((1, LANES), F32)],
        compiler_params=pltpu.CompilerParams(
            dimension_semantics=("arbitrary",), vmem_limit_bytes=VMEM_LIMIT),
        name="proj",
    )(x2, gmix, w_qkv, wf, bf_row, shift_row, bd, gains, rc, rs1, rs2)


def _memkv_kernel(mem_ref, gmem_ref, w_ref, gk_ref, mk_ref, mv_ref):
    hm = _rms_normed(mem_ref[0], gmem_ref[...])
    kv = _dot(hm, w_ref[...])
    ks = []
    for h in range(MEM_HEADS):
        kh = kv[:, h * MEM_HEAD_DIM:(h + 1) * MEM_HEAD_DIM]
        msk = jnp.mean(kh * kh, axis=-1, keepdims=True)
        ks.append(kh * lax.rsqrt(msk + NORM_EPS) * gk_ref[...])
    mk_ref[0] = jnp.concatenate(ks, axis=1).astype(BF16)
    mv_ref[0] = kv[:, ATT_WIDTH:].astype(BF16)


def _memkv_call(mem, gmem, w_kv, gk):
    batch, mlen, _ = mem.shape
    return pl.pallas_call(
        _memkv_kernel,
        grid=(batch,),
        in_specs=[
            pl.BlockSpec((1, mlen, D_MODEL), lambda b: (b, 0, 0)),
            pl.BlockSpec((1, D_MODEL), lambda b: (0, 0)),
            pl.BlockSpec((D_MODEL, 2 * ATT_WIDTH), lambda b: (0, 0)),
            pl.BlockSpec((1, MEM_HEAD_DIM), lambda b: (0, 0)),
        ],
        out_specs=[
            pl.BlockSpec((1, mlen, ATT_WIDTH), lambda b: (b, 0, 0)),
            pl.BlockSpec((1, mlen, ATT_WIDTH), lambda b: (b, 0, 0)),
        ],
        out_shape=[jax.ShapeDtypeStruct((batch, mlen, ATT_WIDTH), BF16)] * 2,
        compiler_params=pltpu.CompilerParams(dimension_semantics=("parallel",)),
        name="memkv",
    )(mem, gmem, w_kv, gk)


def _lane_iota():
    return lax.broadcasted_iota(jnp.int32, (ATT_TILE, LANES), 1)


def _lane_range(first, count, dtype):
    lane = _lane_iota()
    return jnp.where((lane >= first) & (lane < first + count), 1.0, 0.0).astype(dtype)


def _head_lanes(c, dtype):
    return _lane_range(c * HEAD_DIM, HEAD_DIM, dtype)


def _fill_values(v_ref, vaug_ref):
    n_blocks = v_ref.shape[0] // ATT_TILE

    def fill(j, _):
        rows = pl.ds(pl.multiple_of(j * ATT_TILE, ATT_TILE), ATT_TILE)
        v = v_ref[rows, :].astype(F32)
        for c in range(2):
            own = _head_lanes(c, F32)
            vaug_ref[c, rows, :] = (v * own + (1.0 - own)).astype(BF16)
        return 0

    lax.fori_loop(0, n_blocks, fill, 0)


def _normalised(acc_ref, t):
    res = []
    for c in range(2):
        acc = acc_ref[2 * t + c]
        res.append(acc / pltpu.roll(acc, HEAD_DIM, 1))
    return jnp.where(_lane_iota() < HEAD_DIM, res[0], res[1])


def _causal_masks():
    span = KEY_GROUP * ATT_TILE
    diff = (lax.broadcasted_iota(jnp.int32, (ATT_TILE, span), 1)
            - lax.broadcasted_iota(jnp.int32, (ATT_TILE, span), 0))
    return [diff <= t * ATT_TILE for t in range(Q_SUB)]


def _attend_shifted(lhs_of_tile, kaug_ref, vaug_ref, acc_ref, s_ref, lhs_ref, qi, n_q):
    span = KEY_GROUP * ATT_TILE
    n_chain = 2 * Q_SUB

    def group_rows(g):
        return pl.ds(pl.multiple_of(g * span, span), span)

    def scores(g, slot):
        kb = kaug_ref[group_rows(g), :]
        for i in range(n_chain):
            s_ref[slot, i] = _nt(lhs_ref[i], kb)

    def scores_diagonal(g, slot):
        for i in range(n_chain):
            rows = pl.ds(pl.multiple_of(g * span, span), (i // 2 + 1) * ATT_TILE)
            s_ref[slot, i, :, 0:(i // 2 + 1) * ATT_TILE] = _nt(lhs_ref[i], kaug_ref[rows, :])

    def consume(g, slot):
        rows = group_rows(g)
        for i in range(n_chain):
            p = jnp.exp2(s_ref[slot, i]).astype(BF16)
            acc_ref[i] += _dot(p, vaug_ref[i % 2, rows, :])

    def consume_diagonal(g, slot):
        row = lax.broadcasted_iota(jnp.int32, (ATT_TILE, ATT_TILE), 0)
        col = lax.broadcasted_iota(jnp.int32, (ATT_TILE, ATT_TILE), 1)
        for i in range(n_chain):
            t = i // 2
            lo = t * ATT_TILE
            parts = []
            if t:
                parts.append(jnp.exp2(s_ref[slot, i, :, 0:lo]).astype(BF16))
            s_diag = jnp.where(col <= row, s_ref[slot, i, :, lo:lo + ATT_TILE], NEG)
            parts.append(jnp.exp2(s_diag).astype(BF16))
            p = parts[0] if len(parts) == 1 else jnp.concatenate(parts, axis=1)
            rows = pl.ds(pl.multiple_of(g * span, span), lo + ATT_TILE)
            acc_ref[i] += _dot(p, vaug_ref[i % 2, rows, :])

    acc_ref[...] = jnp.zeros_like(acc_ref)
    n_full = qi

    @pl.when(n_full == 0)
    def _():
        for i, operand in enumerate(lhs_of_tile(0)):
            lhs_ref[i] = operand
        scores_diagonal(0, 0)

    @pl.when(n_full > 0)
    def _():
        def slot_of(g):
            return jnp.where(g == 0, PRE_SLOT, lax.rem(g, 2))

        def body(g, _):
            consume(g, slot_of(g))
            scores(g + 1, lax.rem(g + 1, 2))
            return 0

        lax.fori_loop(0, n_full - 1, body, 0)
        consume(n_full - 1, slot_of(n_full - 1))
        scores_diagonal(n_full, lax.rem(n_full, 2))

    consume_diagonal(n_full, lax.rem(n_full, 2))
    nxt = lhs_of_tile(1)
    kb = kaug_ref[group_rows(0), :]
    for i in range(n_chain):
        lhs_ref[i] = nxt[i]
        s_ref[PRE_SLOT, i] = _nt(nxt[i], kb)
    return [_normalised(acc_ref, t) for t in range(Q_SUB)]


def _attend_running_max(lhs, kaug_ref, vaug_ref, acc_ref, s_ref, qi):
    span = KEY_GROUP * ATT_TILE
    n_chain = len(lhs)

    def step(i, g, m, masks):
        rows = pl.ds(pl.multiple_of(g * span, span), span)
        s = _nt(lhs[i], kaug_ref[rows, :])
        if masks is not None:
            s = jnp.where(masks[i // 2], s, NEG)
        m_new = jnp.maximum(m, jnp.max(s, axis=1, keepdims=True))
        p = jnp.exp2(s - m_new).astype(BF16)
        acc_ref[i] = jnp.exp2(m - m_new) * acc_ref[i] + _dot(p, vaug_ref[i % 2, rows, :])
        return m_new

    acc_ref[...] = jnp.zeros_like(acc_ref)
    n_full = qi
    m0 = jnp.full((ATT_TILE, 1), -jnp.inf, F32)
    ms = lax.fori_loop(
        0, n_full, lambda g, ms: tuple(step(i, g, ms[i], None) for i in range(n_chain)),
        (m0,) * n_chain)
    masks = _causal_masks()
    for i in range(n_chain):
        step(i, n_full, ms[i], masks)
    return [_normalised(acc_ref, t) for t in range(Q_SUB)]


def _attend(lhs_of_tile, kaug_ref, vaug_ref, acc_ref, s_ref, lhs_ref, qi, n_q, bounded):
    if bounded:
        return _attend_shifted(lhs_of_tile, kaug_ref, vaug_ref, acc_ref, s_ref, lhs_ref, qi, n_q)
    return _attend_running_max(lhs_of_tile(0), kaug_ref, vaug_ref, acc_ref, s_ref, qi)


def _attention_scratch(seq):
    n_chain = 2 * Q_SUB
    return [pltpu.VMEM((seq, 2 * LANES), BF16),
            pltpu.VMEM((2, seq, LANES), BF16),
            pltpu.VMEM((n_chain, ATT_TILE, LANES), F32),
            pltpu.VMEM((3, n_chain, ATT_TILE, KEY_GROUP * ATT_TILE), F32),
            pltpu.VMEM((n_chain, ATT_TILE, 2 * LANES), BF16)]


def _sub_rows(t):
    return slice(t * ATT_TILE, (t + 1) * ATT_TILE)


def _moba_kernel(q_ref, qn_ref, k_ref, v_ref, km_ref, shift_ref, o_ref, kaug_ref, vaug_ref,
                 acc_ref, s_ref, lhs_ref, *, bounded):
    qi = pl.program_id(2)
    n_q = pl.num_programs(2)
    seq = k_ref.shape[0]
    n_blocks = seq // MOBA_BLOCK
    lane = _lane_iota()

    @pl.when(qi == 0)
    def _():
        kaug_ref[:, 0:LANES] = k_ref[...]

        def fill(j, _):
            rows = pl.ds(pl.multiple_of(j * MOBA_BLOCK, MOBA_BLOCK), MOBA_BLOCK)
            onehot = (lane == j) | (lane == MOBA_SHIFT_LANE)
            kaug_ref[rows, LANES:2 * LANES] = jnp.where(onehot, 1.0, 0.0).astype(BF16)
            return 0

        lax.fori_loop(0, n_blocks, fill, 0)
        _fill_values(v_ref, vaug_ref)

    def lhs_of_tile(which):
        q_blk = qn_ref if which else q_ref
        tile = jnp.minimum(qi + which, n_q - 1)
        km = km_ref[0]
        km = jnp.concatenate([km, jnp.zeros((LANES - n_blocks, LANES), F32)], axis=0)
        km_hi = km.astype(BF16)
        km_lo = (km - km_hi.astype(F32)).astype(BF16)
        shift = shift_ref[:, MOBA_SHIFT_LANE:MOBA_SHIFT_LANE + 1] if bounded else 0.0
        blk = lax.broadcasted_iota(jnp.int32, (n_blocks, ATT_TILE), 0)
        blk_f = blk.astype(F32)
        pad_row = lax.broadcasted_iota(jnp.int32, (LANES - n_blocks, ATT_TILE), 0) + n_blocks
        pad_bias = jnp.where(pad_row == MOBA_SHIFT_LANE, -shift, NEG)
        lhs = []
        for t in range(Q_SUB):
            own = tile * Q_SUB + t
            q2 = q_blk[_sub_rows(t), :]
            for c in range(2):
                qc = q2 * _head_lanes(c, BF16)
                gate = (_nt(km_hi, qc) + _nt(km_lo, qc))[0:n_blocks]
                g = jnp.where(blk < own, gate, -jnp.inf)
                sel = blk == own
                for _ in range(MOBA_TOPK):
                    mx = jnp.max(g, axis=0, keepdims=True)
                    first = jnp.min(jnp.where(g == mx, blk_f, float(LANES)), axis=0,
                                    keepdims=True)
                    pick = (blk_f == first) & (mx > -jnp.inf)
                    sel = sel | pick
                    g = jnp.where(pick, -jnp.inf, g)
                bias = jnp.concatenate([jnp.where(sel, 0.0, NEG), pad_bias], axis=0)
                lhs.append(jnp.concatenate([qc, bias.T.astype(BF16)], axis=1))
        return lhs

    outs = _attend(lhs_of_tile, kaug_ref, vaug_ref, acc_ref, s_ref, lhs_ref, qi, n_q, bounded)
    for t in range(Q_SUB):
        o_ref[_sub_rows(t), :] = outs[t].astype(BF16)


def _moba_call(proj, kmean, shift_row, batch, seq, bounded):
    tq = Q_SUB * ATT_TILE
    n_q = seq // tq
    n_pairs = N_HEADS // 2
    q0 = (TILE_QM * COL_TILE) // LANES
    k0 = (TILE_KM * COL_TILE) // LANES
    v0 = k0 + COL_TILE // LANES
    return pl.pallas_call(
        functools.partial(_moba_kernel, bounded=bounded),
        grid=(batch, n_pairs, n_q),
        in_specs=[
            pl.BlockSpec((tq, LANES), lambda b, p, i: (b * n_q + i, q0 + p)),
            pl.BlockSpec((tq, LANES), lambda b, p, i: (b * n_q + jnp.minimum(i + 1, n_q - 1), q0 + p)),
            pl.BlockSpec((seq, LANES), lambda b, p, i: (b, k0 + p)),
            pl.BlockSpec((seq, LANES), lambda b, p, i: (b, v0 + p)),
            pl.BlockSpec((1, seq // MOBA_BLOCK, LANES), lambda b, p, i: (b, 0, p)),
            pl.BlockSpec((1, LANES), lambda b, p, i: (0, 0)),
        ],
        out_specs=pl.BlockSpec((tq, LANES), lambda b, p, i: (b * n_q + i, p)),
        out_shape=jax.ShapeDtypeStruct((batch * seq, ATT_WIDTH), BF16),
        scratch_shapes=_attention_scratch(seq),
        compiler_params=pltpu.CompilerParams(
            dimension_semantics=("parallel", "parallel", "arbitrary"),
            vmem_limit_bytes=VMEM_LIMIT),
        name="moba_bounded" if bounded else "moba_general",
    )(proj, proj, proj, proj, kmean, shift_row)


def _fox_kernel(q_ref, qn_ref, k_ref, v_ref, faug_ref, fq_ref, fqn_ref, o_ref, kaug_ref, vaug_ref,
                acc_ref, s_ref, lhs_ref, *, bounded):
    pair = pl.program_id(1)
    qi = pl.program_id(2)
    n_q = pl.num_programs(2)
    n_extra = 3 * N_HEADS

    @pl.when(qi == 0)
    def _():
        seq = k_ref.shape[0]
        kaug_ref[:, 0:LANES] = k_ref[...]

        def fill(j, _):
            rows = pl.ds(pl.multiple_of(j * ATT_TILE, ATT_TILE), ATT_TILE)
            lane = _lane_iota()
            ones = (lane >= FOX_Q_LANE0) & (lane < FOX_Q_LANE0 + n_extra)
            f = faug_ref[rows, :].astype(F32)
            kaug_ref[rows, LANES:2 * LANES] = (
                f * (lane < n_extra).astype(F32) + ones.astype(F32)).astype(BF16)
            return 0

        lax.fori_loop(0, seq // ATT_TILE, fill, 0)
        _fill_values(v_ref, vaug_ref)

    def lhs_of_tile(which):
        q_blk, fq_blk = (qn_ref, fqn_ref) if which else (q_ref, fq_ref)
        lhs = []
        for t in range(Q_SUB):
            q2 = q_blk[_sub_rows(t), :]
            fq = fq_blk[_sub_rows(t), :]
            for c in range(2):
                qc = q2 * _head_lanes(c, BF16)
                k_first = 3 * (2 * pair + c)
                extra = _lane_range(k_first, 3, BF16)
                if bounded:
                    extra = extra + fq * _lane_range(FOX_Q_LANE0 + k_first, 3, BF16)
                lhs.append(jnp.concatenate([qc, extra], axis=1))
        return lhs

    outs = _attend(lhs_of_tile, kaug_ref, vaug_ref, acc_ref, s_ref, lhs_ref, qi, n_q, bounded)
    for t in range(Q_SUB):
        o_ref[_sub_rows(t), :] = outs[t].astype(BF16)


def _fox_call(proj, faug, batch, seq, bounded):
    tq = Q_SUB * ATT_TILE
    n_q = seq // tq
    n_pairs = N_HEADS // 2
    q0 = (TILE_QF * COL_TILE) // LANES
    k0 = (TILE_KF * COL_TILE) // LANES
    v0 = k0 + COL_TILE // LANES
    return pl.pallas_call(
        functools.partial(_fox_kernel, bounded=bounded),
        grid=(batch, n_pairs, n_q),
        in_specs=[
            pl.BlockSpec((tq, LANES), lambda b, p, i: (b * n_q + i, q0 + p)),
            pl.BlockSpec((tq, LANES), lambda b, p, i: (b * n_q + jnp.minimum(i + 1, n_q - 1), q0 + p)),
            pl.BlockSpec((seq, LANES), lambda b, p, i: (b, k0 + p)),
            pl.BlockSpec((seq, LANES), lambda b, p, i: (b, v0 + p)),
            pl.BlockSpec((seq, LANES), lambda b, p, i: (b, 0)),
            pl.BlockSpec((tq, LANES), lambda b, p, i: (b * n_q + i, 0)),
            pl.BlockSpec((tq, LANES), lambda b, p, i: (b * n_q + jnp.minimum(i + 1, n_q - 1), 0)),
        ],
        out_specs=pl.BlockSpec((tq, LANES), lambda b, p, i: (b * n_q + i, p)),
        out_shape=jax.ShapeDtypeStruct((batch * seq, ATT_WIDTH), BF16),
        scratch_shapes=_attention_scratch(seq),
        compiler_params=pltpu.CompilerParams(
            dimension_semantics=("parallel", "parallel", "arbitrary"),
            vmem_limit_bytes=VMEM_LIMIT),
        name="fox_bounded" if bounded else "fox_general",
    )(proj, proj, proj, proj, faug, faug, faug)


def _merge_kernel(x_ref, gmix_ref, qc_ref, ym_ref, yf_ref, mk_ref, mv_ref,
                  wg_ref, wbm_ref, wbf_ref, wbc_ref, wo_ref, o_ref):
    x = x_ref[...]
    h = _rms_normed(x, gmix_ref[...])
    qc = qc_ref[...]
    mk = mk_ref[0]
    mv = mv_ref[0]
    ys = []
    for hd in range(MEM_HEADS):
        cols = slice(hd * MEM_HEAD_DIM, (hd + 1) * MEM_HEAD_DIM)
        s = _nt(qc[:, cols], mk[:, cols])
        p = jnp.exp2(s - jnp.max(s, axis=1, keepdims=True))
        y = _dot(p.astype(BF16), mv[:, cols]) / jnp.sum(p, axis=1, keepdims=True)
        ys.append(y)
    yc = jnp.concatenate(ys, axis=1).astype(BF16)
    merged = None
    for i, (y, wb_ref) in enumerate(((ym_ref[...], wbm_ref), (yf_ref[...], wbf_ref),
                                     (yc, wbc_ref))):
        gate = jax.nn.sigmoid(_dot(h, wg_ref[:, i * D_MODEL:(i + 1) * D_MODEL]))
        term = gate * _dot(y, wb_ref[...])
        merged = term if merged is None else merged + term
    o_ref[...] = x + _dot(merged.astype(BF16), wo_ref[...])


def _merge_call(x2, gmix, proj, y_m, y_f, mk, mv, wg, wbm, wbf, wbc, wo, seq, tm):
    n_tok = x2.shape[0]
    mlen = mk.shape[1]
    seq_tiles = seq // tm
    const = lambda i: (0, 0)
    return pl.pallas_call(
        _merge_kernel,
        grid=(n_tok // tm,),
        in_specs=[
            pl.BlockSpec((tm, D_MODEL), lambda i: (i, 0)),
            pl.BlockSpec((1, D_MODEL), const),
            pl.BlockSpec((tm, ATT_WIDTH), lambda i: (i, TILE_QC)),
            pl.BlockSpec((tm, ATT_WIDTH), lambda i: (i, 0)),
            pl.BlockSpec((tm, ATT_WIDTH), lambda i: (i, 0)),
            pl.BlockSpec((1, mlen, ATT_WIDTH), lambda i: (i // seq_tiles, 0, 0)),
            pl.BlockSpec((1, mlen, ATT_WIDTH), lambda i: (i // seq_tiles, 0, 0)),
            pl.BlockSpec((D_MODEL, 3 * D_MODEL), const),
            pl.BlockSpec((ATT_WIDTH, D_MODEL), const),
            pl.BlockSpec((ATT_WIDTH, D_MODEL), const),
            pl.BlockSpec((ATT_WIDTH, D_MODEL), const),
            pl.BlockSpec((D_MODEL, D_MODEL), const),
        ],
        out_specs=pl.BlockSpec((tm, D_MODEL), lambda i: (i, 0)),
        out_shape=jax.ShapeDtypeStruct((n_tok, D_MODEL), F32),
        compiler_params=pltpu.CompilerParams(
            dimension_semantics=("parallel",), vmem_limit_bytes=VMEM_LIMIT),
        name="merge",
    )(x2, gmix, proj, y_m, y_f, mk, mv, wg, wbm, wbf, wbc, wo)


def _ffn_kernel(x_ref, g_ref, wg_ref, wu_ref, wd_ref, o_ref):
    x = x_ref[...]
    h = _rms_normed(x, g_ref[...])
    d_ff = wg_ref.shape[1]
    out = x
    for lo in range(0, d_ff, FFN_CHUNK):
        cols = slice(lo, min(lo + FFN_CHUNK, d_ff))
        g = _dot(h, wg_ref[:, cols])
        u = _dot(h, wu_ref[:, cols])
        a = (g * jax.nn.sigmoid(g) * u).astype(BF16)
        out = out + _dot(a, wd_ref[cols, :])
    o_ref[...] = out


def _ffn_call(x1, g_ffn, wg, wu, wd, tm):
    n_tok = x1.shape[0]
    d_ff = wg.shape[1]
    const = lambda i: (0, 0)
    resident = dict(pipeline_mode=pl.Buffered(1))
    return pl.pallas_call(
        _ffn_kernel,
        grid=(n_tok // tm,),
        in_specs=[
            pl.BlockSpec((tm, D_MODEL), lambda i: (i, 0)),
            pl.BlockSpec((1, D_MODEL), const),
            pl.BlockSpec((D_MODEL, d_ff), const, **resident),
            pl.BlockSpec((D_MODEL, d_ff), const, **resident),
            pl.BlockSpec((d_ff, D_MODEL), const, **resident),
        ],
        out_specs=pl.BlockSpec((tm, D_MODEL), lambda i: (i, 0)),
        out_shape=jax.ShapeDtypeStruct((n_tok, D_MODEL), F32),
        compiler_params=pltpu.CompilerParams(
            dimension_semantics=("parallel",), vmem_limit_bytes=VMEM_LIMIT),
        name="ffn",
    )(x1, g_ffn, wg, wu, wd)


@functools.lru_cache(maxsize=None)
def _rope_tables(seq):
    half = ROPE_DIMS // 2
    inv_freq = 1.0 / (ROPE_THETA ** (np.arange(half, dtype=np.float64) * 2.0 / ROPE_DIMS))
    ang = np.arange(seq, dtype=np.float64)[:, None] * inv_freq[None, :]
    cos, sin = np.cos(ang), np.sin(ang)
    d = np.arange(LANES) % HEAD_DIM
    rc = np.where(d[None, :] < ROPE_DIMS, cos[:, d % half], 1.0)
    rs1 = np.where(d[None, :] < half, -sin[:, d % half], 0.0)
    rs2 = np.where((d[None, :] >= half) & (d[None, :] < ROPE_DIMS), sin[:, d % half], 0.0)
    return tuple(np.asarray(t, np.float32) for t in (rc, rs1, rs2))


@functools.lru_cache(maxsize=None)
def _segment_mean_matrices():
    def blockdiag(width):
        return np.kron(np.eye(SEG_TILE // width), np.ones((width, width))) / width
    return np.stack([blockdiag(HEAD_DIM), blockdiag(MEM_HEAD_DIM)]).astype(np.float32)


def _score_bound(g_q, g_k):
    return (BOUND_SLACK * HEAD_DIM ** 0.5 * jnp.max(jnp.abs(g_q)) * jnp.max(jnp.abs(g_k)))


def _lane_row(value, first, count):
    lane = jnp.arange(LANES)
    return jnp.where((lane >= first) & (lane < first + count), value, 0.0).astype(F32)[None, :]


def kernel(x, mem, g_mix, w_in, b_forget, g_q_moba, g_k_moba, g_q_fox, g_k_fox, g_q_mem, g_k_mem,
           g_mem, w_mem_kv, w_br_moba, w_br_fox, w_br_mem, w_out, g_ffn, w_gate, w_up, w_down):
    batch, seq, _ = x.shape
    depth = g_mix.shape[0]
    n_tok = batch * seq
    assert seq % (KEY_GROUP * ATT_TILE) == 0 and seq // MOBA_BLOCK < MOBA_SHIFT_LANE
    n_extra = 3 * N_HEADS

    rc, rs1, rs2 = (jnp.asarray(t) for t in _rope_tables(seq))
    bd = jnp.asarray(_segment_mean_matrices()).astype(BF16)
    ones = jnp.ones((COL_TILE,), F32)

    x2 = x.reshape(n_tok, D_MODEL)
    for layer in range(depth):
        w = w_in[layer]
        gates0 = 7 * ATT_WIDTH + N_HEADS
        qc0 = 6 * ATT_WIDTH + N_HEADS
        w_qkv = jnp.concatenate([w[:, :6 * ATT_WIDTH], w[:, qc0:gates0]], axis=1).astype(BF16)
        w_gates = w[:, gates0:].astype(BF16)
        wf3 = jnp.repeat(w[:, 6 * ATT_WIDTH:qc0], 3, axis=1)
        gap = jnp.zeros((D_MODEL, FOX_Q_LANE0 - n_extra), F32)
        tail = jnp.zeros((D_MODEL, LANES - FOX_Q_LANE0 - n_extra), F32)
        wf = jnp.concatenate([wf3, gap, wf3, tail], axis=1).astype(BF16)
        bf3 = jnp.repeat(b_forget[layer], 3)
        bf_rep = jnp.concatenate([bf3, gap[0], bf3, tail[0]])[None, :]

        att_scale = HEAD_DIM ** -0.5 * LOG2E
        gains = [ones] * N_COL_TILES
        gains[TILE_QM] = jnp.tile(g_q_moba[layer], N_HEADS) * att_scale
        gains[TILE_KM] = jnp.tile(g_k_moba[layer], N_HEADS)
        gains[TILE_QF] = jnp.tile(g_q_fox[layer], N_HEADS) * att_scale
        gains[TILE_KF] = jnp.tile(g_k_fox[layer], N_HEADS)
        gains[TILE_QC] = jnp.tile(g_q_mem[layer], MEM_HEADS) * (MEM_HEAD_DIM ** -0.5 * LOG2E)
        gains = jnp.stack(gains)[:, None, :]

        bound_m = _score_bound(g_q_moba[layer], g_k_moba[layer])
        bound_f = _score_bound(g_q_fox[layer], g_k_fox[layer])
        shift_m = _lane_row(bound_m * LOG2E, MOBA_SHIFT_LANE, 1)
        shift_f = _lane_row(bound_f * LOG2E, FOX_Q_LANE0, n_extra)

        proj, faug, kmean = _proj_call(x2, g_mix[layer][None, :], w_qkv, wf, bf_rep, shift_f, bd,
                                       gains, rc, rs1, rs2, seq, tm=512)
        kmean = kmean.reshape(batch, seq // MOBA_BLOCK, COL_TILE)
        mk, mv = _memkv_call(mem, g_mem[layer][None, :], w_mem_kv[layer].astype(BF16),
                             g_k_mem[layer][None, :])
        y_m = lax.cond(2.0 * bound_m <= MAX_SHIFT_NATS,
                       lambda: _moba_call(proj, kmean, shift_m, batch, seq, True),
                       lambda: _moba_call(proj, kmean, shift_m, batch, seq, False))
        y_f = lax.cond(2.0 * bound_f <= MAX_SHIFT_NATS,
                       lambda: _fox_call(proj, faug, batch, seq, True),
                       lambda: _fox_call(proj, faug, batch, seq, False))
        x1 = _merge_call(x2, g_mix[layer][None, :], proj, y_m, y_f, mk, mv, w_gates,
                         w_br_moba[layer].astype(BF16), w_br_fox[layer].astype(BF16),
                         w_br_mem[layer].astype(BF16), w_out[layer].astype(BF16), seq, tm=512)
        x2 = _ffn_call(x1, g_ffn[layer][None, :], w_gate[layer].astype(BF16),
                       w_up[layer].astype(BF16), w_down[layer].astype(BF16), tm=512)
    return x2.reshape(batch, seq, D_MODEL)
```

```python
import functools
import math

import numpy as np
import jax
import jax.numpy as jnp
from jax import lax
from jax.experimental import pallas as pl
from jax.experimental.pallas import tpu as pltpu

F32 = jnp.float32
BF16 = jnp.bfloat16

D_MODEL = 1024
HEAD_DIM = 64
N_HEADS = 8
MEM_HEADS = 4
MEM_HEAD_DIM = 128
ATT_WIDTH = 512
MOBA_BLOCK = 256
MOBA_TOPK = 3
ROPE_THETA = 500000.0
ROPE_DIMS = 16
NORM_EPS = 1e-6
NEG = -1e30
LOG2E = math.log2(math.e)

LANES = 128
COL_TILE = 512
SEG_TILE = 256
N_COL_TILES = 7
PROJ_WIDTH = COL_TILE * N_COL_TILES
TILE_QM, TILE_KM, TILE_QF, TILE_KF, TILE_QC = 0, 1, 3, 4, 6
ATT_TILE = 256
KEY_GROUP = 4
Q_SUB = KEY_GROUP
PRE_SLOT = 2
FFN_CHUNK = 1024
VMEM_LIMIT = 52 * 1024 * 1024

FOX_Q_LANE0 = 32
MOBA_SHIFT_LANE = LANES - 1
MAX_SHIFT_NATS = 60.0
BOUND_SLACK = 1.02


def _nt(a, b):
    return lax.dot_general(a, b, (((1,), (1,)), ((), ())), preferred_element_type=F32)


def _dot(a, b):
    return jnp.dot(a, b, preferred_element_type=F32)


def _split3(v):
    p1 = v.astype(BF16)
    r1 = v - p1.astype(F32)
    p2 = r1.astype(BF16)
    p3 = (r1 - p2.astype(F32)).astype(BF16)
    return p1, p2, p3


def _rms_normed(x, gain_row):
    ms = jnp.mean(x * x, axis=-1, keepdims=True)
    return (x * lax.rsqrt(ms + NORM_EPS) * gain_row).astype(BF16)


def _wprep_kernel(w_ref, wqkv_ref, wgate_ref):
    qkv_end = 6 * ATT_WIDTH
    qc0 = qkv_end + N_HEADS
    gates0 = qc0 + ATT_WIDTH
    wqkv_ref[:, 0:qkv_end] = w_ref[:, 0:qkv_end].astype(BF16)
    wqkv_ref[:, qkv_end:PROJ_WIDTH] = w_ref[:, qc0:gates0].astype(BF16)
    wgate_ref[...] = w_ref[:, gates0:gates0 + 3 * D_MODEL].astype(BF16)


def _wprep_call(w, tr):
    d_in = w.shape[1]
    return pl.pallas_call(
        _wprep_kernel,
        grid=(D_MODEL // tr,),
        in_specs=[pl.BlockSpec((tr, d_in), lambda i: (i, 0))],
        out_specs=[pl.BlockSpec((tr, PROJ_WIDTH), lambda i: (i, 0)),
                   pl.BlockSpec((tr, 3 * D_MODEL), lambda i: (i, 0))],
        out_shape=[jax.ShapeDtypeStruct((D_MODEL, PROJ_WIDTH), BF16),
                   jax.ShapeDtypeStruct((D_MODEL, 3 * D_MODEL), BF16)],
        compiler_params=pltpu.CompilerParams(
            dimension_semantics=("parallel",), vmem_limit_bytes=VMEM_LIMIT),
        name="wprep",
    )(w)


def _forget_lanes(cum, shift_row):
    lane = lax.broadcasted_iota(jnp.int32, cum.shape, 1)
    key_side = lane < 3 * N_HEADS
    query_side = (lane >= FOX_Q_LANE0) & (lane < FOX_Q_LANE0 + 3 * N_HEADS)
    cum2 = cum * LOG2E
    val = jnp.where(key_side, -cum2, cum2 - shift_row)
    c1, c2, c3 = _split3(val)
    piece = jnp.where(key_side, lane, lane - FOX_Q_LANE0) % 3
    sel = jnp.where(piece == 0, c1, jnp.where(piece == 1, c2, c3))
    return jnp.where(key_side | query_side, sel, jnp.zeros_like(sel))


def _proj_kernel(x_ref, gmix_ref, w_ref, wf_ref, bf_ref, shift_ref, bd_ref, gain_ref,
                 rc_ref, rs1_ref, rs2_ref, p_ref, faug_ref, kmean_ref, carry_ref, *, seq_tiles):
    tm = x_ref.shape[0]
    h = _rms_normed(x_ref[...], gmix_ref[...])

    @pl.when(pl.program_id(0) % seq_tiles == 0)
    def _():
        carry_ref[...] = jnp.zeros_like(carry_ref)

    z = _dot(h, wf_ref[...]) + bf_ref[...]
    lf = jnp.minimum(z, 0.0) - jnp.log1p(jnp.exp(-jnp.abs(z)))
    row = lax.broadcasted_iota(jnp.int32, (tm, tm), 0)
    col = lax.broadcasted_iota(jnp.int32, (tm, tm), 1)
    tri = jnp.where(col <= row, 1.0, 0.0).astype(BF16)
    p1, p2, p3 = _split3(lf)
    cum = _dot(tri, p1) + _dot(tri, p2) + _dot(tri, p3) + carry_ref[...]
    carry_ref[...] = cum[tm - 1:tm, :]
    faug_ref[...] = _forget_lanes(cum, shift_ref[...])

    def head_normed(t, k):
        seg = 1 if k == TILE_QC else 0
        sq = (t * t).astype(BF16)
        ms = jnp.concatenate(
            [_dot(sq[:, lo:lo + SEG_TILE], bd_ref[seg]) for lo in range(0, COL_TILE, SEG_TILE)],
            axis=1)
        return t * lax.rsqrt(ms + NORM_EPS) * gain_ref[k]

    def rotated(y):
        rc, rs1, rs2 = rc_ref[...], rs1_ref[...], rs2_ref[...]
        parts = []
        for g in range(COL_TILE // LANES):
            yg = y[:, g * LANES:(g + 1) * LANES]
            up = pltpu.roll(yg, LANES - ROPE_DIMS // 2, 1)
            dn = pltpu.roll(yg, ROPE_DIMS // 2, 1)
            parts.append(yg * rc + up * rs1 + dn * rs2)
        return jnp.concatenate(parts, axis=1)

    for k in range(N_COL_TILES):
        cols = slice(k * COL_TILE, (k + 1) * COL_TILE)
        t = _dot(h, w_ref[:, cols])
        if k in (TILE_QM, TILE_KM):
            t = rotated(head_normed(t, k))
        elif k in (TILE_QF, TILE_KF, TILE_QC):
            t = head_normed(t, k)
        p_ref[:, cols] = t.astype(BF16)
        if k == TILE_KM:
            for r in range(tm // MOBA_BLOCK):
                blk = t[r * MOBA_BLOCK:(r + 1) * MOBA_BLOCK, :]
                kmean_ref[0, pl.ds(r, 1), :] = jnp.mean(blk, axis=0, keepdims=True)


def _proj_call(x2, gmix, w_qkv, wf, bf_row, shift_row, bd, gains, rc, rs1, rs2, seq, tm):
    n_tok = x2.shape[0]
    seq_tiles = seq // tm
    const2 = lambda i: (0, 0)
    const3 = lambda i: (0, 0, 0)
    return pl.pallas_call(
        functools.partial(_proj_kernel, seq_tiles=seq_tiles),
        grid=(n_tok // tm,),
        in_specs=[
            pl.BlockSpec((tm, D_MODEL), lambda i: (i, 0)),
            pl.BlockSpec((1, D_MODEL), const2),
            pl.BlockSpec((D_MODEL, PROJ_WIDTH), const2),
            pl.BlockSpec((D_MODEL, LANES), const2),
            pl.BlockSpec((1, LANES), const2),
            pl.BlockSpec((1, LANES), const2),
            pl.BlockSpec((2, SEG_TILE, SEG_TILE), const3),
            pl.BlockSpec((N_COL_TILES, 1, COL_TILE), const3),
            pl.BlockSpec((tm, LANES), lambda i: (i % seq_tiles, 0)),
            pl.BlockSpec((tm, LANES), lambda i: (i % seq_tiles, 0)),
            pl.BlockSpec((tm, LANES), lambda i: (i % seq_tiles, 0)),
        ],
        out_specs=[
            pl.BlockSpec((tm, PROJ_WIDTH), lambda i: (i, 0)),
            pl.BlockSpec((tm, LANES), lambda i: (i, 0)),
            pl.BlockSpec((1, tm // MOBA_BLOCK, COL_TILE), lambda i: (i, 0, 0)),
        ],
        out_shape=[
            jax.ShapeDtypeStruct((n_tok, PROJ_WIDTH), BF16),
            jax.ShapeDtypeStruct((n_tok, LANES), BF16),
            jax.ShapeDtypeStruct((n_tok // tm, tm // MOBA_BLOCK, COL_TILE), F32),
        ],
        scratch_shapes=[pltpu.VMEM((1, LANES), F32)],
        compiler_params=pltpu.CompilerParams(
            dimension_semantics=("arbitrary",), vmem_limit_bytes=VMEM_LIMIT),
        name="proj",
    )(x2, gmix, w_qkv, wf, bf_row, shift_row, bd, gains, rc, rs1, rs2)


def _memkv_kernel(mem_ref, gmem_ref, w_ref, gk_ref, mk_ref, mv_ref):
    hm = _rms_normed(mem_ref[0], gmem_ref[...])
    kv = _dot(hm, w_ref[...])
    ks = []
    for h in range(MEM_HEADS):
        kh = kv[:, h * MEM_HEAD_DIM:(h + 1) * MEM_HEAD_DIM]
        msk = jnp.mean(kh * kh, axis=-1, keepdims=True)
        ks.append(kh * lax.rsqrt(msk + NORM_EPS) * gk_ref[...])
    mk_ref[0] = jnp.concatenate(ks, axis=1).astype(BF16)
    mv_ref[0] = kv[:, ATT_WIDTH:].astype(BF16)


def _memkv_call(mem, gmem, w_kv, gk):
    batch, mlen, _ = mem.shape
    return pl.pallas_call(
        _memkv_kernel,
        grid=(batch,),
        in_specs=[
            pl.BlockSpec((1, mlen, D_MODEL), lambda b: (b, 0, 0)),
            pl.BlockSpec((1, D_MODEL), lambda b: (0, 0)),
            pl.BlockSpec((D_MODEL, 2 * ATT_WIDTH), lambda b: (0, 0)),
            pl.BlockSpec((1, MEM_HEAD_DIM), lambda b: (0, 0)),
        ],
        out_specs=[
            pl.BlockSpec((1, mlen, ATT_WIDTH), lambda b: (b, 0, 0)),
            pl.BlockSpec((1, mlen, ATT_WIDTH), lambda b: (b, 0, 0)),
        ],
        out_shape=[jax.ShapeDtypeStruct((batch, mlen, ATT_WIDTH), BF16)] * 2,
        compiler_params=pltpu.CompilerParams(dimension_semantics=("parallel",)),
        name="memkv",
    )(mem, gmem, w_kv, gk)


def _lane_iota():
    return lax.broadcasted_iota(jnp.int32, (ATT_TILE, LANES), 1)


def _lane_range(first, count, dtype):
    lane = _lane_iota()
    return jnp.where((lane >= first) & (lane < first + count), 1.0, 0.0).astype(dtype)


def _head_lanes(c, dtype):
    return _lane_range(c * HEAD_DIM, HEAD_DIM, dtype)


def _fill_values(v_ref, vaug_ref):
    n_blocks = v_ref.shape[0] // ATT_TILE

    def fill(j, _):
        rows = pl.ds(pl.multiple_of(j * ATT_TILE, ATT_TILE), ATT_TILE)
        v = v_ref[rows, :].astype(F32)
        for c in range(2):
            own = _head_lanes(c, F32)
            vaug_ref[c, rows, :] = (v * own + (1.0 - own)).astype(BF16)
        return 0

    lax.fori_loop(0, n_blocks, fill, 0)


def _normalised(acc_ref, t):
    res = []
    for c in range(2):
        acc = acc_ref[2 * t + c]
        res.append(acc / pltpu.roll(acc, HEAD_DIM, 1))
    return jnp.where(_lane_iota() < HEAD_DIM, res[0], res[1])


def _causal_masks():
    span = KEY_GROUP * ATT_TILE
    diff = (lax.broadcasted_iota(jnp.int32, (ATT_TILE, span), 1)
            - lax.broadcasted_iota(jnp.int32, (ATT_TILE, span), 0))
    return [diff <= t * ATT_TILE for t in range(Q_SUB)]


def _attend_shifted(lhs_of_tile, kaug_ref, vaug_ref, acc_ref, s_ref, lhs_ref, qi, n_q):
    span = KEY_GROUP * ATT_TILE
    n_chain = 2 * Q_SUB

    def group_rows(g):
        return pl.ds(pl.multiple_of(g * span, span), span)

    def scores(g, slot):
        kb = kaug_ref[group_rows(g), :]
        for i in range(n_chain):
            s_ref[slot, i] = _nt(lhs_ref[i], kb)

    def scores_diagonal(g, slot):
        for i in range(n_chain):
            rows = pl.ds(pl.multiple_of(g * span, span), (i // 2 + 1) * ATT_TILE)
            s_ref[slot, i, :, 0:(i // 2 + 1) * ATT_TILE] = _nt(lhs_ref[i], kaug_ref[rows, :])

    def consume(g, slot):
        rows = group_rows(g)
        for i in range(n_chain):
            p = jnp.exp2(s_ref[slot, i]).astype(BF16)
            acc_ref[i] += _dot(p, vaug_ref[i % 2, rows, :])

    def consume_diagonal(g, slot):
        row = lax.broadcasted_iota(jnp.int32, (ATT_TILE, ATT_TILE), 0)
        col = lax.broadcasted_iota(jnp.int32, (ATT_TILE, ATT_TILE), 1)
        for i in range(n_chain):
            t = i // 2
            lo = t * ATT_TILE
            parts = []
            if t:
                parts.append(jnp.exp2(s_ref[slot, i, :, 0:lo]).astype(BF16))
            s_diag = jnp.where(col <= row, s_ref[slot, i, :, lo:lo + ATT_TILE], NEG)
            parts.append(jnp.exp2(s_diag).astype(BF16))
            p = parts[0] if len(parts) == 1 else jnp.concatenate(parts, axis=1)
            rows = pl.ds(pl.multiple_of(g * span, span), lo + ATT_TILE)
            acc_ref[i] += _dot(p, vaug_ref[i % 2, rows, :])

    acc_ref[...] = jnp.zeros_like(acc_ref)
    n_full = qi

    @pl.when(n_full == 0)
    def _():
        for i, operand in enumerate(lhs_of_tile(0)):
            lhs_ref[i] = operand
        scores_diagonal(0, 0)

    @pl.when(n_full > 0)
    def _():
        def slot_of(g):
            return jnp.where(g == 0, PRE_SLOT, lax.rem(g, 2))

        def body(g, _):
            consume(g, slot_of(g))
            scores(g + 1, lax.rem(g + 1, 2))
            return 0

        lax.fori_loop(0, n_full - 1, body, 0)
        consume(n_full - 1, slot_of(n_full - 1))
        scores_diagonal(n_full, lax.rem(n_full, 2))

    consume_diagonal(n_full, lax.rem(n_full, 2))
    nxt = lhs_of_tile(1)
    kb = kaug_ref[group_rows(0), :]
    for i in range(n_chain):
        lhs_ref[i] = nxt[i]
        s_ref[PRE_SLOT, i] = _nt(nxt[i], kb)
    return [_normalised(acc_ref, t) for t in range(Q_SUB)]


def _attend_running_max(lhs, kaug_ref, vaug_ref, acc_ref, s_ref, qi):
    span = KEY_GROUP * ATT_TILE
    n_chain = len(lhs)

    def step(i, g, m, masks):
        rows = pl.ds(pl.multiple_of(g * span, span), span)
        s = _nt(lhs[i], kaug_ref[rows, :])
        if masks is not None:
            s = jnp.where(masks[i // 2], s, NEG)
        m_new = jnp.maximum(m, jnp.max(s, axis=1, keepdims=True))
        p = jnp.exp2(s - m_new).astype(BF16)
        acc_ref[i] = jnp.exp2(m - m_new) * acc_ref[i] + _dot(p, vaug_ref[i % 2, rows, :])
        return m_new

    acc_ref[...] = jnp.zeros_like(acc_ref)
    n_full = qi
    m0 = jnp.full((ATT_TILE, 1), -jnp.inf, F32)
    ms = lax.fori_loop(
        0, n_full, lambda g, ms: tuple(step(i, g, ms[i], None) for i in range(n_chain)),
        (m0,) * n_chain)
    masks = _causal_masks()
    for i in range(n_chain):
        step(i, n_full, ms[i], masks)
    return [_normalised(acc_ref, t) for t in range(Q_SUB)]


def _attend(lhs_of_tile, kaug_ref, vaug_ref, acc_ref, s_ref, lhs_ref, qi, n_q, bounded):
    if bounded:
        return _attend_shifted(lhs_of_tile, kaug_ref, vaug_ref, acc_ref, s_ref, lhs_ref, qi, n_q)
    return _attend_running_max(lhs_of_tile(0), kaug_ref, vaug_ref, acc_ref, s_ref, qi)


def _attention_scratch(seq):
    n_chain = 2 * Q_SUB
    return [pltpu.VMEM((seq, 2 * LANES), BF16),
            pltpu.VMEM((2, seq, LANES), BF16),
            pltpu.VMEM((n_chain, ATT_TILE, LANES), F32),
            pltpu.VMEM((3, n_chain, ATT_TILE, KEY_GROUP * ATT_TILE), F32),
            pltpu.VMEM((n_chain, ATT_TILE, 2 * LANES), BF16)]


def _sub_rows(t):
    return slice(t * ATT_TILE, (t + 1) * ATT_TILE)


def _moba_kernel(q_ref, qn_ref, k_ref, v_ref, km_ref, shift_ref, o_ref, kaug_ref, vaug_ref,
                 acc_ref, s_ref, lhs_ref, *, bounded):
    qi = pl.program_id(2)
    n_q = pl.num_programs(2)
    seq = k_ref.shape[0]
    n_blocks = seq // MOBA_BLOCK
    lane = _lane_iota()

    @pl.when(qi == 0)
    def _():
        kaug_ref[:, 0:LANES] = k_ref[...]

        def fill(j, _):
            rows = pl.ds(pl.multiple_of(j * MOBA_BLOCK, MOBA_BLOCK), MOBA_BLOCK)
            onehot = (lane == j) | (lane == MOBA_SHIFT_LANE)
            kaug_ref[rows, LANES:2 * LANES] = jnp.where(onehot, 1.0, 0.0).astype(BF16)
            return 0

        lax.fori_loop(0, n_blocks, fill, 0)
        _fill_values(v_ref, vaug_ref)

    def lhs_of_tile(which):
        q_blk = qn_ref if which else q_ref
        tile = jnp.minimum(qi + which, n_q - 1)
        km = km_ref[0]
        km = jnp.concatenate([km, jnp.zeros((LANES - n_blocks, LANES), F32)], axis=0)
        km_hi = km.astype(BF16)
        km_lo = (km - km_hi.astype(F32)).astype(BF16)
        shift = shift_ref[:, MOBA_SHIFT_LANE:MOBA_SHIFT_LANE + 1] if bounded else 0.0
        blk = lax.broadcasted_iota(jnp.int32, (n_blocks, ATT_TILE), 0)
        blk_f = blk.astype(F32)
        pad_row = lax.broadcasted_iota(jnp.int32, (LANES - n_blocks, ATT_TILE), 0) + n_blocks
        pad_bias = jnp.where(pad_row == MOBA_SHIFT_LANE, -shift, NEG)
        lhs = []
        for t in range(Q_SUB):
            own = tile * Q_SUB + t
            q2 = q_blk[_sub_rows(t), :]
            for c in range(2):
                qc = q2 * _head_lanes(c, BF16)
                gate = (_nt(km_hi, qc) + _nt(km_lo, qc))[0:n_blocks]
                g = jnp.where(blk < own, gate, -jnp.inf)
                sel = blk == own
                for _ in range(MOBA_TOPK):
                    mx = jnp.max(g, axis=0, keepdims=True)
                    first = jnp.min(jnp.where(g == mx, blk_f, float(LANES)), axis=0,
                                    keepdims=True)
                    pick = (blk_f == first) & (mx > -jnp.inf)
                    sel = sel | pick
                    g = jnp.where(pick, -jnp.inf, g)
                bias = jnp.concatenate([jnp.where(sel, 0.0, NEG), pad_bias], axis=0)
                lhs.append(jnp.concatenate([qc, bias.T.astype(BF16)], axis=1))
        return lhs

    outs = _attend(lhs_of_tile, kaug_ref, vaug_ref, acc_ref, s_ref, lhs_ref, qi, n_q, bounded)
    for t in range(Q_SUB):
        o_ref[_sub_rows(t), :] = outs[t].astype(BF16)


def _moba_call(proj, kmean, shift_row, batch, seq, bounded):
    tq = Q_SUB * ATT_TILE
    n_q = seq // tq
    n_pairs = N_HEADS // 2
    q0 = (TILE_QM * COL_TILE) // LANES
    k0 = (TILE_KM * COL_TILE) // LANES
    v0 = k0 + COL_TILE // LANES
    return pl.pallas_call(
        functools.partial(_moba_kernel, bounded=bounded),
        grid=(batch, n_pairs, n_q),
        in_specs=[
            pl.BlockSpec((tq, LANES), lambda b, p, i: (b * n_q + i, q0 + p)),
            pl.BlockSpec((tq, LANES), lambda b, p, i: (b * n_q + jnp.minimum(i + 1, n_q - 1), q0 + p)),
            pl.BlockSpec((seq, LANES), lambda b, p, i: (b, k0 + p)),
            pl.BlockSpec((seq, LANES), lambda b, p, i: (b, v0 + p)),
            pl.BlockSpec((1, seq // MOBA_BLOCK, LANES), lambda b, p, i: (b, 0, p)),
            pl.BlockSpec((1, LANES), lambda b, p, i: (0, 0)),
        ],
        out_specs=pl.BlockSpec((tq, LANES), lambda b, p, i: (b * n_q + i, p)),
        out_shape=jax.ShapeDtypeStruct((batch * seq, ATT_WIDTH), BF16),
        scratch_shapes=_attention_scratch(seq),
        compiler_params=pltpu.CompilerParams(
            dimension_semantics=("parallel", "parallel", "arbitrary"),
            vmem_limit_bytes=VMEM_LIMIT),
        name="moba_bounded" if bounded else "moba_general",
    )(proj, proj, proj, proj, kmean, shift_row)


def _fox_kernel(q_ref, qn_ref, k_ref, v_ref, faug_ref, fq_ref, fqn_ref, o_ref, kaug_ref, vaug_ref,
                acc_ref, s_ref, lhs_ref, *, bounded):
    pair = pl.program_id(1)
    qi = pl.program_id(2)
    n_q = pl.num_programs(2)
    n_extra = 3 * N_HEADS

    @pl.when(qi == 0)
    def _():
        seq = k_ref.shape[0]
        kaug_ref[:, 0:LANES] = k_ref[...]

        def fill(j, _):
            rows = pl.ds(pl.multiple_of(j * ATT_TILE, ATT_TILE), ATT_TILE)
            lane = _lane_iota()
            ones = (lane >= FOX_Q_LANE0) & (lane < FOX_Q_LANE0 + n_extra)
            f = faug_ref[rows, :].astype(F32)
            kaug_ref[rows, LANES:2 * LANES] = (
                f * (lane < n_extra).astype(F32) + ones.astype(F32)).astype(BF16)
            return 0

        lax.fori_loop(0, seq // ATT_TILE, fill, 0)
        _fill_values(v_ref, vaug_ref)

    def lhs_of_tile(which):
        q_blk, fq_blk = (qn_ref, fqn_ref) if which else (q_ref, fq_ref)
        lhs = []
        for t in range(Q_SUB):
            q2 = q_blk[_sub_rows(t), :]
            fq = fq_blk[_sub_rows(t), :]
            for c in range(2):
                qc = q2 * _head_lanes(c, BF16)
                k_first = 3 * (2 * pair + c)
                extra = _lane_range(k_first, 3, BF16)
                if bounded:
                    extra = extra + fq * _lane_range(FOX_Q_LANE0 + k_first, 3, BF16)
                lhs.append(jnp.concatenate([qc, extra], axis=1))
        return lhs

    outs = _attend(lhs_of_tile, kaug_ref, vaug_ref, acc_ref, s_ref, lhs_ref, qi, n_q, bounded)
    for t in range(Q_SUB):
        o_ref[_sub_rows(t), :] = outs[t].astype(BF16)


def _fox_call(proj, faug, batch, seq, bounded):
    tq = Q_SUB * ATT_TILE
    n_q = seq // tq
    n_pairs = N_HEADS // 2
    q0 = (TILE_QF * COL_TILE) // LANES
    k0 = (TILE_KF * COL_TILE) // LANES
    v0 = k0 + COL_TILE // LANES
    return pl.pallas_call(
        functools.partial(_fox_kernel, bounded=bounded),
        grid=(batch, n_pairs, n_q),
        in_specs=[
            pl.BlockSpec((tq, LANES), lambda b, p, i: (b * n_q + i, q0 + p)),
            pl.BlockSpec((tq, LANES), lambda b, p, i: (b * n_q + jnp.minimum(i + 1, n_q - 1), q0 + p)),
            pl.BlockSpec((seq, LANES), lambda b, p, i: (b, k0 + p)),
            pl.BlockSpec((seq, LANES), lambda b, p, i: (b, v0 + p)),
            pl.BlockSpec((seq, LANES), lambda b, p, i: (b, 0)),
            pl.BlockSpec((tq, LANES), lambda b, p, i: (b * n_q + i, 0)),
            pl.BlockSpec((tq, LANES), lambda b, p, i: (b * n_q + jnp.minimum(i + 1, n_q - 1), 0)),
        ],
        out_specs=pl.BlockSpec((tq, LANES), lambda b, p, i: (b * n_q + i, p)),
        out_shape=jax.ShapeDtypeStruct((batch * seq, ATT_WIDTH), BF16),
        scratch_shapes=_attention_scratch(seq),
        compiler_params=pltpu.CompilerParams(
            dimension_semantics=("parallel", "parallel", "arbitrary"),
            vmem_limit_bytes=VMEM_LIMIT),
        name="fox_bounded" if bounded else "fox_general",
    )(proj, proj, proj, proj, faug, faug, faug)


def _merge_kernel(x_ref, gmix_ref, qc_ref, ym_ref, yf_ref, mk_ref, mv_ref,
                  wg_ref, wbm_ref, wbf_ref, wbc_ref, wo_ref, o_ref):
    x = x_ref[...]
    h = _rms_normed(x, gmix_ref[...])
    qc = qc_ref[...]
    mk = mk_ref[0]
    mv = mv_ref[0]
    ys = []
    for hd in range(MEM_HEADS):
        cols = slice(hd * MEM_HEAD_DIM, (hd + 1) * MEM_HEAD_DIM)
        s = _nt(qc[:, cols], mk[:, cols])
        p = jnp.exp2(s - jnp.max(s, axis=1, keepdims=True))
        y = _dot(p.astype(BF16), mv[:, cols]) / jnp.sum(p, axis=1, keepdims=True)
        ys.append(y)
    yc = jnp.concatenate(ys, axis=1).astype(BF16)
    merged = None
    for i, (y, wb_ref) in enumerate(((ym_ref[...], wbm_ref), (yf_ref[...], wbf_ref),
                                     (yc, wbc_ref))):
        gate = jax.nn.sigmoid(_dot(h, wg_ref[:, i * D_MODEL:(i + 1) * D_MODEL]))
        term = gate * _dot(y, wb_ref[...])
        merged = term if merged is None else merged + term
    o_ref[...] = x + _dot(merged.astype(BF16), wo_ref[...])


def _merge_call(x2, gmix, proj, y_m, y_f, mk, mv, wg, wbm, wbf, wbc, wo, seq, tm):
    n_tok = x2.shape[0]
    mlen = mk.shape[1]
    seq_tiles = seq // tm
    const = lambda i: (0, 0)
    return pl.pallas_call(
        _merge_kernel,
        grid=(n_tok // tm,),
        in_specs=[
            pl.BlockSpec((tm, D_MODEL), lambda i: (i, 0)),
            pl.BlockSpec((1, D_MODEL), const),
            pl.BlockSpec((tm, ATT_WIDTH), lambda i: (i, TILE_QC)),
            pl.BlockSpec((tm, ATT_WIDTH), lambda i: (i, 0)),
            pl.BlockSpec((tm, ATT_WIDTH), lambda i: (i, 0)),
            pl.BlockSpec((1, mlen, ATT_WIDTH), lambda i: (i // seq_tiles, 0, 0)),
            pl.BlockSpec((1, mlen, ATT_WIDTH), lambda i: (i // seq_tiles, 0, 0)),
            pl.BlockSpec((D_MODEL, 3 * D_MODEL), const),
            pl.BlockSpec((ATT_WIDTH, D_MODEL), const),
            pl.BlockSpec((ATT_WIDTH, D_MODEL), const),
            pl.BlockSpec((ATT_WIDTH, D_MODEL), const),
            pl.BlockSpec((D_MODEL, D_MODEL), const),
        ],
        out_specs=pl.BlockSpec((tm, D_MODEL), lambda i: (i, 0)),
        out_shape=jax.ShapeDtypeStruct((n_tok, D_MODEL), F32),
        compiler_params=pltpu.CompilerParams(
            dimension_semantics=("parallel",), vmem_limit_bytes=VMEM_LIMIT),
        name="merge",
    )(x2, gmix, proj, y_m, y_f, mk, mv, wg, wbm, wbf, wbc, wo)


def _ffn_kernel(x_ref, g_ref, wg_ref, wu_ref, wd_ref, o_ref):
    x = x_ref[...]
    h = _rms_normed(x, g_ref[...])
    d_ff = wg_ref.shape[1]
    out = x
    for lo in range(0, d_ff, FFN_CHUNK):
        cols = slice(lo, min(lo + FFN_CHUNK, d_ff))
        g = _dot(h, wg_ref[:, cols])
        u = _dot(h, wu_ref[:, cols])
        a = (g * jax.nn.sigmoid(g) * u).astype(BF16)
        out = out + _dot(a, wd_ref[cols, :])
    o_ref[...] = out


def _ffn_call(x1, g_ffn, wg, wu, wd, tm):
    n_tok = x1.shape[0]
    d_ff = wg.shape[1]
    const = lambda i: (0, 0)
    resident = dict(pipeline_mode=pl.Buffered(1))
    return pl.pallas_call(
        _ffn_kernel,
        grid=(n_tok // tm,),
        in_specs=[
            pl.BlockSpec((tm, D_MODEL), lambda i: (i, 0)),
            pl.BlockSpec((1, D_MODEL), const),
            pl.BlockSpec((D_MODEL, d_ff), const, **resident),
            pl.BlockSpec((D_MODEL, d_ff), const, **resident),
            pl.BlockSpec((d_ff, D_MODEL), const, **resident),
        ],
        out_specs=pl.BlockSpec((tm, D_MODEL), lambda i: (i, 0)),
        out_shape=jax.ShapeDtypeStruct((n_tok, D_MODEL), F32),
        compiler_params=pltpu.CompilerParams(
            dimension_semantics=("parallel",), vmem_limit_bytes=VMEM_LIMIT),
        name="ffn",
    )(x1, g_ffn, wg, wu, wd)


@functools.lru_cache(maxsize=None)
def _rope_tables(seq):
    half = ROPE_DIMS // 2
    inv_freq = 1.0 / (ROPE_THETA ** (np.arange(half, dtype=np.float64) * 2.0 / ROPE_DIMS))
    ang = np.arange(seq, dtype=np.float64)[:, None] * inv_freq[None, :]
    cos, sin = np.cos(ang), np.sin(ang)
    d = np.arange(LANES) % HEAD_DIM
    rc = np.where(d[None, :] < ROPE_DIMS, cos[:, d % half], 1.0)
    rs1 = np.where(d[None, :] < half, -sin[:, d % half], 0.0)
    rs2 = np.where((d[None, :] >= half) & (d[None, :] < ROPE_DIMS), sin[:, d % half], 0.0)
    return tuple(np.asarray(t, np.float32) for t in (rc, rs1, rs2))


@functools.lru_cache(maxsize=None)
def _segment_mean_matrices():
    def blockdiag(width):
        return np.kron(np.eye(SEG_TILE // width), np.ones((width, width))) / width
    return np.stack([blockdiag(HEAD_DIM), blockdiag(MEM_HEAD_DIM)]).astype(np.float32)


def _score_bound(g_q, g_k):
    return (BOUND_SLACK * HEAD_DIM ** 0.5 * jnp.max(jnp.abs(g_q)) * jnp.max(jnp.abs(g_k)))


def _lane_row(value, first, count):
    lane = jnp.arange(LANES)
    return jnp.where((lane >= first) & (lane < first + count), value, 0.0).astype(F32)[None, :]


def kernel(x, mem, g_mix, w_in, b_forget, g_q_moba, g_k_moba, g_q_fox, g_k_fox, g_q_mem, g_k_mem,
           g_mem, w_mem_kv, w_br_moba, w_br_fox, w_br_mem, w_out, g_ffn, w_gate, w_up, w_down):
    batch, seq, _ = x.shape
    depth = g_mix.shape[0]
    n_tok = batch * seq
    assert seq % (KEY_GROUP * ATT_TILE) == 0 and seq // MOBA_BLOCK < MOBA_SHIFT_LANE
    n_extra = 3 * N_HEADS

    rc, rs1, rs2 = (jnp.asarray(t) for t in _rope_tables(seq))
    bd = jnp.asarray(_segment_mean_matrices()).astype(BF16)
    ones = jnp.ones((COL_TILE,), F32)

    x2 = x.reshape(n_tok, D_MODEL)
    for layer in range(depth):
        w = w_in[layer]
        gates0 = 7 * ATT_WIDTH + N_HEADS
        qc0 = 6 * ATT_WIDTH + N_HEADS
        w_qkv, w_gates = _wprep_call(w, tr=256)
        wf3 = jnp.repeat(w[:, 6 * ATT_WIDTH:qc0], 3, axis=1)
        gap = jnp.zeros((D_MODEL, FOX_Q_LANE0 - n_extra), F32)
        tail = jnp.zeros((D_MODEL, LANES - FOX_Q_LANE0 - n_extra), F32)
        wf = jnp.concatenate([wf3, gap, wf3, tail], axis=1).astype(BF16)
        bf3 = jnp.repeat(b_forget[layer], 3)
        bf_rep = jnp.concatenate([bf3, gap[0], bf3, tail[0]])[None, :]

        att_scale = HEAD_DIM ** -0.5 * LOG2E
        gains = [ones] * N_COL_TILES
        gains[TILE_QM] = jnp.tile(g_q_moba[layer], N_HEADS) * att_scale
        gains[TILE_KM] = jnp.tile(g_k_moba[layer], N_HEADS)
        gains[TILE_QF] = jnp.tile(g_q_fox[layer], N_HEADS) * att_scale
        gains[TILE_KF] = jnp.tile(g_k_fox[layer], N_HEADS)
        gains[TILE_QC] = jnp.tile(g_q_mem[layer], MEM_HEADS) * (MEM_HEAD_DIM ** -0.5 * LOG2E)
        gains = jnp.stack(gains)[:, None, :]

        bound_m = _score_bound(g_q_moba[layer], g_k_moba[layer])
        bound_f = _score_bound(g_q_fox[layer], g_k_fox[layer])
        shift_m = _lane_row(bound_m * LOG2E, MOBA_SHIFT_LANE, 1)
        shift_f = _lane_row(bound_f * LOG2E, FOX_Q_LANE0, n_extra)

        proj, faug, kmean = _proj_call(x2, g_mix[layer][None, :], w_qkv, wf, bf_rep, shift_f, bd,
                                       gains, rc, rs1, rs2, seq, tm=512)
        kmean = kmean.reshape(batch, seq // MOBA_BLOCK, COL_TILE)
        mk, mv = _memkv_call(mem, g_mem[layer][None, :], w_mem_kv[layer].astype(BF16),
                             g_k_mem[layer][None, :])
        y_m = lax.cond(2.0 * bound_m <= MAX_SHIFT_NATS,
                       lambda: _moba_call(proj, kmean, shift_m, batch, seq, True),
                       lambda: _moba_call(proj, kmean, shift_m, batch, seq, False))
        y_f = lax.cond(2.0 * bound_f <= MAX_SHIFT_NATS,
                       lambda: _fox_call(proj, faug, batch, seq, True),
                       lambda: _fox_call(proj, faug, batch, seq, False))
        x1 = _merge_call(x2, g_mix[layer][None, :], proj, y_m, y_f, mk, mv, w_gates,
                         w_br_moba[layer].astype(BF16), w_br_fox[layer].astype(BF16),
                         w_br_mem[layer].astype(BF16), w_out[layer].astype(BF16), seq, tm=512)
        x2 = _ffn_call(x1, g_ffn[layer][None, :], w_gate[layer].astype(BF16),
                       w_up[layer].astype(BF16), w_down[layer].astype(BF16), tm=512)
    return x2.reshape(batch, seq, D_MODEL)
```

```python
import functools
import math

import numpy as np
import jax
import jax.numpy as jnp
from jax import lax
from jax.experimental import pallas as pl
from jax.experimental.pallas import tpu as pltpu

F32 = jnp.float32
BF16 = jnp.bfloat16

D_MODEL = 1024
HEAD_DIM = 64
N_HEADS = 8
MEM_HEADS = 4
MEM_HEAD_DIM = 128
ATT_WIDTH = 512
MOBA_BLOCK = 256
MOBA_TOPK = 3
ROPE_THETA = 500000.0
ROPE_DIMS = 16
NORM_EPS = 1e-6
NEG = -1e30
LOG2E = math.log2(math.e)

LANES = 128
COL_TILE = 512
SEG_TILE = 256
N_COL_TILES = 7
PROJ_WIDTH = COL_TILE * N_COL_TILES
TILE_QM, TILE_KM, TILE_QF, TILE_KF, TILE_QC = 0, 1, 3, 4, 6
ATT_TILE = 256
KEY_GROUP = 4
Q_SUB = KEY_GROUP
PRE_SLOT = 2
FFN_CHUNK = 1024
VMEM_LIMIT = 52 * 1024 * 1024

FOX_Q_LANE0 = 32
MOBA_SHIFT_LANE = LANES - 1
MAX_SHIFT_NATS = 60.0
BOUND_SLACK = 1.02


def _nt(a, b):
    return lax.dot_general(a, b, (((1,), (1,)), ((), ())), preferred_element_type=F32)


def _dot(a, b):
    return jnp.dot(a, b, preferred_element_type=F32)


def _split3(v):
    p1 = v.astype(BF16)
    r1 = v - p1.astype(F32)
    p2 = r1.astype(BF16)
    p3 = (r1 - p2.astype(F32)).astype(BF16)
    return p1, p2, p3


def _rms_normed(x, gain_row):
    ms = jnp.mean(x * x, axis=-1, keepdims=True)
    return (x * lax.rsqrt(ms + NORM_EPS) * gain_row).astype(BF16)


def _forget_lanes(cum, shift_row):
    lane = lax.broadcasted_iota(jnp.int32, cum.shape, 1)
    key_side = lane < 3 * N_HEADS
    query_side = (lane >= FOX_Q_LANE0) & (lane < FOX_Q_LANE0 + 3 * N_HEADS)
    cum2 = cum * LOG2E
    val = jnp.where(key_side, -cum2, cum2 - shift_row)
    c1, c2, c3 = _split3(val)
    piece = jnp.where(key_side, lane, lane - FOX_Q_LANE0) % 3
    sel = jnp.where(piece == 0, c1, jnp.where(piece == 1, c2, c3))
    return jnp.where(key_side | query_side, sel, jnp.zeros_like(sel))


def _proj_kernel(x_ref, gmix_ref, w_ref, wf_ref, bf_ref, shift_ref, bd_ref, gain_ref,
                 rc_ref, rs1_ref, rs2_ref, p_ref, faug_ref, kmean_ref, carry_ref, *, seq_tiles):
    tm = x_ref.shape[0]
    h = _rms_normed(x_ref[...], gmix_ref[...])

    @pl.when(pl.program_id(0) % seq_tiles == 0)
    def _():
        carry_ref[...] = jnp.zeros_like(carry_ref)

    z = _nt(h, wf_ref[...]) + bf_ref[...]
    lf = jnp.minimum(z, 0.0) - jnp.log1p(jnp.exp(-jnp.abs(z)))
    row = lax.broadcasted_iota(jnp.int32, (tm, tm), 0)
    col = lax.broadcasted_iota(jnp.int32, (tm, tm), 1)
    tri = jnp.where(col <= row, 1.0, 0.0).astype(BF16)
    p1, p2, p3 = _split3(lf)
    cum = _dot(tri, p1) + _dot(tri, p2) + _dot(tri, p3) + carry_ref[...]
    carry_ref[...] = cum[tm - 1:tm, :]
    faug_ref[...] = _forget_lanes(cum, shift_ref[...])

    def head_normed(t, k):
        seg = 1 if k == TILE_QC else 0
        sq = (t * t).astype(BF16)
        ms = jnp.concatenate(
            [_dot(sq[:, lo:lo + SEG_TILE], bd_ref[seg]) for lo in range(0, COL_TILE, SEG_TILE)],
            axis=1)
        return t * lax.rsqrt(ms + NORM_EPS) * gain_ref[k]

    def rotated(y):
        rc, rs1, rs2 = rc_ref[...], rs1_ref[...], rs2_ref[...]
        parts = []
        for g in range(COL_TILE // LANES):
            yg = y[:, g * LANES:(g + 1) * LANES]
            up = pltpu.roll(yg, LANES - ROPE_DIMS // 2, 1)
            dn = pltpu.roll(yg, ROPE_DIMS // 2, 1)
            parts.append(yg * rc + up * rs1 + dn * rs2)
        return jnp.concatenate(parts, axis=1)

    for k in range(N_COL_TILES):
        cols = slice(k * COL_TILE, (k + 1) * COL_TILE)
        t = _nt(h, w_ref[cols, :])
        if k in (TILE_QM, TILE_KM):
            t = rotated(head_normed(t, k))
        elif k in (TILE_QF, TILE_KF, TILE_QC):
            t = head_normed(t, k)
        p_ref[:, cols] = t.astype(BF16)
        if k == TILE_KM:
            for r in range(tm // MOBA_BLOCK):
                blk = t[r * MOBA_BLOCK:(r + 1) * MOBA_BLOCK, :]
                kmean_ref[0, pl.ds(r, 1), :] = jnp.mean(blk, axis=0, keepdims=True)


def _proj_call(x2, gmix, w_qkv, wf, bf_row, shift_row, bd, gains, rc, rs1, rs2, seq, tm):
    n_tok = x2.shape[0]
    seq_tiles = seq // tm
    const2 = lambda i: (0, 0)
    const3 = lambda i: (0, 0, 0)
    return pl.pallas_call(
        functools.partial(_proj_kernel, seq_tiles=seq_tiles),
        grid=(n_tok // tm,),
        in_specs=[
            pl.BlockSpec((tm, D_MODEL), lambda i: (i, 0)),
            pl.BlockSpec((1, D_MODEL), const2),
            pl.BlockSpec((PROJ_WIDTH, D_MODEL), const2),
            pl.BlockSpec((LANES, D_MODEL), const2),
            pl.BlockSpec((1, LANES), const2),
            pl.BlockSpec((1, LANES), const2),
            pl.BlockSpec((2, SEG_TILE, SEG_TILE), const3),
            pl.BlockSpec((N_COL_TILES, 1, COL_TILE), const3),
            pl.BlockSpec((tm, LANES), lambda i: (i % seq_tiles, 0)),
            pl.BlockSpec((tm, LANES), lambda i: (i % seq_tiles, 0)),
            pl.BlockSpec((tm, LANES), lambda i: (i % seq_tiles, 0)),
        ],
        out_specs=[
            pl.BlockSpec((tm, PROJ_WIDTH), lambda i: (i, 0)),
            pl.BlockSpec((tm, LANES), lambda i: (i, 0)),
            pl.BlockSpec((1, tm // MOBA_BLOCK, COL_TILE), lambda i: (i, 0, 0)),
        ],
        out_shape=[
            jax.ShapeDtypeStruct((n_tok, PROJ_WIDTH), BF16),
            jax.ShapeDtypeStruct((n_tok, LANES), BF16),
            jax.ShapeDtypeStruct((n_tok // tm, tm // MOBA_BLOCK, COL_TILE), F32),
        ],
        scratch_shapes=[pltpu.VMEM((1, LANES), F32)],
        compiler_params=pltpu.CompilerParams(
            dimension_semantics=("arbitrary",), vmem_limit_bytes=VMEM_LIMIT),
        name="proj",
    )(x2, gmix, w_qkv, wf, bf_row, shift_row, bd, gains, rc, rs1, rs2)


def _memkv_kernel(mem_ref, gmem_ref, w_ref, gk_ref, mk_ref, mv_ref):
    hm = _rms_normed(mem_ref[0], gmem_ref[...])
    kv = _dot(hm, w_ref[...])
    ks = []
    for h in range(MEM_HEADS):
        kh = kv[:, h * MEM_HEAD_DIM:(h + 1) * MEM_HEAD_DIM]
        msk = jnp.mean(kh * kh, axis=-1, keepdims=True)
        ks.append(kh * lax.rsqrt(msk + NORM_EPS) * gk_ref[...])
    mk_ref[0] = jnp.concatenate(ks, axis=1).astype(BF16)
    mv_ref[0] = kv[:, ATT_WIDTH:].astype(BF16)


def _memkv_call(mem, gmem, w_kv, gk):
    batch, mlen, _ = mem.shape
    return pl.pallas_call(
        _memkv_kernel,
        grid=(batch,),
        in_specs=[
            pl.BlockSpec((1, mlen, D_MODEL), lambda b: (b, 0, 0)),
            pl.BlockSpec((1, D_MODEL), lambda b: (0, 0)),
            pl.BlockSpec((D_MODEL, 2 * ATT_WIDTH), lambda b: (0, 0)),
            pl.BlockSpec((1, MEM_HEAD_DIM), lambda b: (0, 0)),
        ],
        out_specs=[
            pl.BlockSpec((1, mlen, ATT_WIDTH), lambda b: (b, 0, 0)),
            pl.BlockSpec((1, mlen, ATT_WIDTH), lambda b: (b, 0, 0)),
        ],
        out_shape=[jax.ShapeDtypeStruct((batch, mlen, ATT_WIDTH), BF16)] * 2,
        compiler_params=pltpu.CompilerParams(dimension_semantics=("parallel",)),
        name="memkv",
    )(mem, gmem, w_kv, gk)


def _lane_iota():
    return lax.broadcasted_iota(jnp.int32, (ATT_TILE, LANES), 1)


def _lane_range(first, count, dtype):
    lane = _lane_iota()
    return jnp.where((lane >= first) & (lane < first + count), 1.0, 0.0).astype(dtype)


def _head_lanes(c, dtype):
    return _lane_range(c * HEAD_DIM, HEAD_DIM, dtype)


def _fill_values(v_ref, vaug_ref):
    n_blocks = v_ref.shape[0] // ATT_TILE

    def fill(j, _):
        rows = pl.ds(pl.multiple_of(j * ATT_TILE, ATT_TILE), ATT_TILE)
        v = v_ref[rows, :].astype(F32)
        for c in range(2):
            own = _head_lanes(c, F32)
            vaug_ref[c, rows, :] = (v * own + (1.0 - own)).astype(BF16)
        return 0

    lax.fori_loop(0, n_blocks, fill, 0)


def _normalised(acc_ref, t):
    res = []
    for c in range(2):
        acc = acc_ref[2 * t + c]
        res.append(acc / pltpu.roll(acc, HEAD_DIM, 1))
    return jnp.where(_lane_iota() < HEAD_DIM, res[0], res[1])


def _causal_masks():
    span = KEY_GROUP * ATT_TILE
    diff = (lax.broadcasted_iota(jnp.int32, (ATT_TILE, span), 1)
            - lax.broadcasted_iota(jnp.int32, (ATT_TILE, span), 0))
    return [diff <= t * ATT_TILE for t in range(Q_SUB)]


def _attend_shifted(lhs_of_tile, kaug_ref, vaug_ref, acc_ref, s_ref, lhs_ref, qi, n_q):
    span = KEY_GROUP * ATT_TILE
    n_chain = 2 * Q_SUB

    def group_rows(g):
        return pl.ds(pl.multiple_of(g * span, span), span)

    def scores(g, slot):
        kb = kaug_ref[group_rows(g), :]
        for i in range(n_chain):
            s_ref[slot, i] = _nt(lhs_ref[i], kb)

    def scores_diagonal(g, slot):
        for i in range(n_chain):
            rows = pl.ds(pl.multiple_of(g * span, span), (i // 2 + 1) * ATT_TILE)
            s_ref[slot, i, :, 0:(i // 2 + 1) * ATT_TILE] = _nt(lhs_ref[i], kaug_ref[rows, :])

    def consume(g, slot):
        rows = group_rows(g)
        for i in range(n_chain):
            p = jnp.exp2(s_ref[slot, i]).astype(BF16)
            acc_ref[i] += _dot(p, vaug_ref[i % 2, rows, :])

    def consume_diagonal(g, slot):
        row = lax.broadcasted_iota(jnp.int32, (ATT_TILE, ATT_TILE), 0)
        col = lax.broadcasted_iota(jnp.int32, (ATT_TILE, ATT_TILE), 1)
        for i in range(n_chain):
            t = i // 2
            lo = t * ATT_TILE
            parts = []
            if t:
                parts.append(jnp.exp2(s_ref[slot, i, :, 0:lo]).astype(BF16))
            s_diag = jnp.where(col <= row, s_ref[slot, i, :, lo:lo + ATT_TILE], NEG)
            parts.append(jnp.exp2(s_diag).astype(BF16))
            p = parts[0] if len(parts) == 1 else jnp.concatenate(parts, axis=1)
            rows = pl.ds(pl.multiple_of(g * span, span), lo + ATT_TILE)
            acc_ref[i] += _dot(p, vaug_ref[i % 2, rows, :])

    acc_ref[...] = jnp.zeros_like(acc_ref)
    n_full = qi

    @pl.when(n_full == 0)
    def _():
        for i, operand in enumerate(lhs_of_tile(0)):
            lhs_ref[i] = operand
        scores_diagonal(0, 0)

    @pl.when(n_full > 0)
    def _():
        def slot_of(g):
            return jnp.where(g == 0, PRE_SLOT, lax.rem(g, 2))

        def body(g, _):
            consume(g, slot_of(g))
            scores(g + 1, lax.rem(g + 1, 2))
            return 0

        lax.fori_loop(0, n_full - 1, body, 0)
        consume(n_full - 1, slot_of(n_full - 1))
        scores_diagonal(n_full, lax.rem(n_full, 2))

    consume_diagonal(n_full, lax.rem(n_full, 2))
    nxt = lhs_of_tile(1)
    kb = kaug_ref[group_rows(0), :]
    for i in range(n_chain):
        lhs_ref[i] = nxt[i]
        s_ref[PRE_SLOT, i] = _nt(nxt[i], kb)
    return [_normalised(acc_ref, t) for t in range(Q_SUB)]


def _attend_running_max(lhs, kaug_ref, vaug_ref, acc_ref, s_ref, qi):
    span = KEY_GROUP * ATT_TILE
    n_chain = len(lhs)

    def step(i, g, m, masks):
        rows = pl.ds(pl.multiple_of(g * span, span), span)
        s = _nt(lhs[i], kaug_ref[rows, :])
        if masks is not None:
            s = jnp.where(masks[i // 2], s, NEG)
        m_new = jnp.maximum(m, jnp.max(s, axis=1, keepdims=True))
        p = jnp.exp2(s - m_new).astype(BF16)
        acc_ref[i] = jnp.exp2(m - m_new) * acc_ref[i] + _dot(p, vaug_ref[i % 2, rows, :])
        return m_new

    acc_ref[...] = jnp.zeros_like(acc_ref)
    n_full = qi
    m0 = jnp.full((ATT_TILE, 1), -jnp.inf, F32)
    ms = lax.fori_loop(
        0, n_full, lambda g, ms: tuple(step(i, g, ms[i], None) for i in range(n_chain)),
        (m0,) * n_chain)
    masks = _causal_masks()
    for i in range(n_chain):
        step(i, n_full, ms[i], masks)
    return [_normalised(acc_ref, t) for t in range(Q_SUB)]


def _attend(lhs_of_tile, kaug_ref, vaug_ref, acc_ref, s_ref, lhs_ref, qi, n_q, bounded):
    if bounded:
        return _attend_shifted(lhs_of_tile, kaug_ref, vaug_ref, acc_ref, s_ref, lhs_ref, qi, n_q)
    return _attend_running_max(lhs_of_tile(0), kaug_ref, vaug_ref, acc_ref, s_ref, qi)


def _attention_scratch(seq):
    n_chain = 2 * Q_SUB
    return [pltpu.VMEM((seq, 2 * LANES), BF16),
            pltpu.VMEM((2, seq, LANES), BF16),
            pltpu.VMEM((n_chain, ATT_TILE, LANES), F32),
            pltpu.VMEM((3, n_chain, ATT_TILE, KEY_GROUP * ATT_TILE), F32),
            pltpu.VMEM((n_chain, ATT_TILE, 2 * LANES), BF16)]


def _sub_rows(t):
    return slice(t * ATT_TILE, (t + 1) * ATT_TILE)


def _moba_kernel(q_ref, qn_ref, k_ref, v_ref, km_ref, shift_ref, o_ref, kaug_ref, vaug_ref,
                 acc_ref, s_ref, lhs_ref, *, bounded):
    qi = pl.program_id(2)
    n_q = pl.num_programs(2)
    seq = k_ref.shape[0]
    n_blocks = seq // MOBA_BLOCK
    lane = _lane_iota()

    @pl.when(qi == 0)
    def _():
        kaug_ref[:, 0:LANES] = k_ref[...]

        def fill(j, _):
            rows = pl.ds(pl.multiple_of(j * MOBA_BLOCK, MOBA_BLOCK), MOBA_BLOCK)
            onehot = (lane == j) | (lane == MOBA_SHIFT_LANE)
            kaug_ref[rows, LANES:2 * LANES] = jnp.where(onehot, 1.0, 0.0).astype(BF16)
            return 0

        lax.fori_loop(0, n_blocks, fill, 0)
        _fill_values(v_ref, vaug_ref)

    def lhs_of_tile(which):
        q_blk = qn_ref if which else q_ref
        tile = jnp.minimum(qi + which, n_q - 1)
        km = km_ref[0]
        km = jnp.concatenate([km, jnp.zeros((LANES - n_blocks, LANES), F32)], axis=0)
        km_hi = km.astype(BF16)
        km_lo = (km - km_hi.astype(F32)).astype(BF16)
        shift = shift_ref[:, MOBA_SHIFT_LANE:MOBA_SHIFT_LANE + 1] if bounded else 0.0
        blk = lax.broadcasted_iota(jnp.int32, (n_blocks, ATT_TILE), 0)
        blk_f = blk.astype(F32)
        pad_row = lax.broadcasted_iota(jnp.int32, (LANES - n_blocks, ATT_TILE), 0) + n_blocks
        pad_bias = jnp.where(pad_row == MOBA_SHIFT_LANE, -shift, NEG)
        lhs = []
        for t in range(Q_SUB):
            own = tile * Q_SUB + t
            q2 = q_blk[_sub_rows(t), :]
            for c in range(2):
                qc = q2 * _head_lanes(c, BF16)
                gate = (_nt(km_hi, qc) + _nt(km_lo, qc))[0:n_blocks]
                g = jnp.where(blk < own, gate, -jnp.inf)
                sel = blk == own
                for _ in range(MOBA_TOPK):
                    mx = jnp.max(g, axis=0, keepdims=True)
                    first = jnp.min(jnp.where(g == mx, blk_f, float(LANES)), axis=0,
                                    keepdims=True)
                    pick = (blk_f == first) & (mx > -jnp.inf)
                    sel = sel | pick
                    g = jnp.where(pick, -jnp.inf, g)
                bias = jnp.concatenate([jnp.where(sel, 0.0, NEG), pad_bias], axis=0)
                lhs.append(jnp.concatenate([qc, bias.T.astype(BF16)], axis=1))
        return lhs

    outs = _attend(lhs_of_tile, kaug_ref, vaug_ref, acc_ref, s_ref, lhs_ref, qi, n_q, bounded)
    for t in range(Q_SUB):
        o_ref[_sub_rows(t), :] = outs[t].astype(BF16)


def _moba_call(proj, kmean, shift_row, batch, seq, bounded):
    tq = Q_SUB * ATT_TILE
    n_q = seq // tq
    n_pairs = N_HEADS // 2
    q0 = (TILE_QM * COL_TILE) // LANES
    k0 = (TILE_KM * COL_TILE) // LANES
    v0 = k0 + COL_TILE // LANES
    return pl.pallas_call(
        functools.partial(_moba_kernel, bounded=bounded),
        grid=(batch, n_pairs, n_q),
        in_specs=[
            pl.BlockSpec((tq, LANES), lambda b, p, i: (b * n_q + i, q0 + p)),
            pl.BlockSpec((tq, LANES), lambda b, p, i: (b * n_q + jnp.minimum(i + 1, n_q - 1), q0 + p)),
            pl.BlockSpec((seq, LANES), lambda b, p, i: (b, k0 + p)),
            pl.BlockSpec((seq, LANES), lambda b, p, i: (b, v0 + p)),
            pl.BlockSpec((1, seq // MOBA_BLOCK, LANES), lambda b, p, i: (b, 0, p)),
            pl.BlockSpec((1, LANES), lambda b, p, i: (0, 0)),
        ],
        out_specs=pl.BlockSpec((tq, LANES), lambda b, p, i: (b * n_q + i, p)),
        out_shape=jax.ShapeDtypeStruct((batch * seq, ATT_WIDTH), BF16),
        scratch_shapes=_attention_scratch(seq),
        compiler_params=pltpu.CompilerParams(
            dimension_semantics=("parallel", "parallel", "arbitrary"),
            vmem_limit_bytes=VMEM_LIMIT),
        name="moba_bounded" if bounded else "moba_general",
    )(proj, proj, proj, proj, kmean, shift_row)


def _fox_kernel(q_ref, qn_ref, k_ref, v_ref, faug_ref, fq_ref, fqn_ref, o_ref, kaug_ref, vaug_ref,
                acc_ref, s_ref, lhs_ref, *, bounded):
    pair = pl.program_id(1)
    qi = pl.program_id(2)
    n_q = pl.num_programs(2)
    n_extra = 3 * N_HEADS

    @pl.when(qi == 0)
    def _():
        seq = k_ref.shape[0]
        kaug_ref[:, 0:LANES] = k_ref[...]

        def fill(j, _):
            rows = pl.ds(pl.multiple_of(j * ATT_TILE, ATT_TILE), ATT_TILE)
            lane = _lane_iota()
            ones = (lane >= FOX_Q_LANE0) & (lane < FOX_Q_LANE0 + n_extra)
            f = faug_ref[rows, :].astype(F32)
            kaug_ref[rows, LANES:2 * LANES] = (
                f * (lane < n_extra).astype(F32) + ones.astype(F32)).astype(BF16)
            return 0

        lax.fori_loop(0, seq // ATT_TILE, fill, 0)
        _fill_values(v_ref, vaug_ref)

    def lhs_of_tile(which):
        q_blk, fq_blk = (qn_ref, fqn_ref) if which else (q_ref, fq_ref)
        lhs = []
        for t in range(Q_SUB):
            q2 = q_blk[_sub_rows(t), :]
            fq = fq_blk[_sub_rows(t), :]
            for c in range(2):
                qc = q2 * _head_lanes(c, BF16)
                k_first = 3 * (2 * pair + c)
                extra = _lane_range(k_first, 3, BF16)
                if bounded:
                    extra = extra + fq * _lane_range(FOX_Q_LANE0 + k_first, 3, BF16)
                lhs.append(jnp.concatenate([qc, extra], axis=1))
        return lhs

    outs = _attend(lhs_of_tile, kaug_ref, vaug_ref, acc_ref, s_ref, lhs_ref, qi, n_q, bounded)
    for t in range(Q_SUB):
        o_ref[_sub_rows(t), :] = outs[t].astype(BF16)


def _fox_call(proj, faug, batch, seq, bounded):
    tq = Q_SUB * ATT_TILE
    n_q = seq // tq
    n_pairs = N_HEADS // 2
    q0 = (TILE_QF * COL_TILE) // LANES
    k0 = (TILE_KF * COL_TILE) // LANES
    v0 = k0 + COL_TILE // LANES
    return pl.pallas_call(
        functools.partial(_fox_kernel, bounded=bounded),
        grid=(batch, n_pairs, n_q),
        in_specs=[
            pl.BlockSpec((tq, LANES), lambda b, p, i: (b * n_q + i, q0 + p)),
            pl.BlockSpec((tq, LANES), lambda b, p, i: (b * n_q + jnp.minimum(i + 1, n_q - 1), q0 + p)),
            pl.BlockSpec((seq, LANES), lambda b, p, i: (b, k0 + p)),
            pl.BlockSpec((seq, LANES), lambda b, p, i: (b, v0 + p)),
            pl.BlockSpec((seq, LANES), lambda b, p, i: (b, 0)),
            pl.BlockSpec((tq, LANES), lambda b, p, i: (b * n_q + i, 0)),
            pl.BlockSpec((tq, LANES), lambda b, p, i: (b * n_q + jnp.minimum(i + 1, n_q - 1), 0)),
        ],
        out_specs=pl.BlockSpec((tq, LANES), lambda b, p, i: (b * n_q + i, p)),
        out_shape=jax.ShapeDtypeStruct((batch * seq, ATT_WIDTH), BF16),
        scratch_shapes=_attention_scratch(seq),
        compiler_params=pltpu.CompilerParams(
            dimension_semantics=("parallel", "parallel", "arbitrary"),
            vmem_limit_bytes=VMEM_LIMIT),
        name="fox_bounded" if bounded else "fox_general",
    )(proj, proj, proj, proj, faug, faug, faug)


def _merge_kernel(x_ref, gmix_ref, qc_ref, ym_ref, yf_ref, mk_ref, mv_ref,
                  wg_ref, wbm_ref, wbf_ref, wbc_ref, wo_ref, o_ref):
    x = x_ref[...]
    h = _rms_normed(x, gmix_ref[...])
    qc = qc_ref[...]
    mk = mk_ref[0]
    mv = mv_ref[0]
    ys = []
    for hd in range(MEM_HEADS):
        cols = slice(hd * MEM_HEAD_DIM, (hd + 1) * MEM_HEAD_DIM)
        s = _nt(qc[:, cols], mk[:, cols])
        p = jnp.exp2(s - jnp.max(s, axis=1, keepdims=True))
        y = _dot(p.astype(BF16), mv[:, cols]) / jnp.sum(p, axis=1, keepdims=True)
        ys.append(y)
    yc = jnp.concatenate(ys, axis=1).astype(BF16)
    merged = None
    for i, (y, wb_ref) in enumerate(((ym_ref[...], wbm_ref), (yf_ref[...], wbf_ref),
                                     (yc, wbc_ref))):
        gate = jax.nn.sigmoid(_nt(h, wg_ref[i * D_MODEL:(i + 1) * D_MODEL, :]))
        term = gate * _dot(y, wb_ref[...])
        merged = term if merged is None else merged + term
    o_ref[...] = x + _dot(merged.astype(BF16), wo_ref[...])


def _merge_call(x2, gmix, proj, y_m, y_f, mk, mv, wg, wbm, wbf, wbc, wo, seq, tm):
    n_tok = x2.shape[0]
    mlen = mk.shape[1]
    seq_tiles = seq // tm
    const = lambda i: (0, 0)
    return pl.pallas_call(
        _merge_kernel,
        grid=(n_tok // tm,),
        in_specs=[
            pl.BlockSpec((tm, D_MODEL), lambda i: (i, 0)),
            pl.BlockSpec((1, D_MODEL), const),
            pl.BlockSpec((tm, ATT_WIDTH), lambda i: (i, TILE_QC)),
            pl.BlockSpec((tm, ATT_WIDTH), lambda i: (i, 0)),
            pl.BlockSpec((tm, ATT_WIDTH), lambda i: (i, 0)),
            pl.BlockSpec((1, mlen, ATT_WIDTH), lambda i: (i // seq_tiles, 0, 0)),
            pl.BlockSpec((1, mlen, ATT_WIDTH), lambda i: (i // seq_tiles, 0, 0)),
            pl.BlockSpec((3 * D_MODEL, D_MODEL), const),
            pl.BlockSpec((ATT_WIDTH, D_MODEL), const),
            pl.BlockSpec((ATT_WIDTH, D_MODEL), const),
            pl.BlockSpec((ATT_WIDTH, D_MODEL), const),
            pl.BlockSpec((D_MODEL, D_MODEL), const),
        ],
        out_specs=pl.BlockSpec((tm, D_MODEL), lambda i: (i, 0)),
        out_shape=jax.ShapeDtypeStruct((n_tok, D_MODEL), F32),
        compiler_params=pltpu.CompilerParams(
            dimension_semantics=("parallel",), vmem_limit_bytes=VMEM_LIMIT),
        name="merge",
    )(x2, gmix, proj, y_m, y_f, mk, mv, wg, wbm, wbf, wbc, wo)


def _ffn_kernel(x_ref, g_ref, wg_ref, wu_ref, wd_ref, o_ref):
    x = x_ref[...]
    h = _rms_normed(x, g_ref[...])
    d_ff = wg_ref.shape[1]
    out = x
    for lo in range(0, d_ff, FFN_CHUNK):
        cols = slice(lo, min(lo + FFN_CHUNK, d_ff))
        g = _dot(h, wg_ref[:, cols])
        u = _dot(h, wu_ref[:, cols])
        a = (g * jax.nn.sigmoid(g) * u).astype(BF16)
        out = out + _dot(a, wd_ref[cols, :])
    o_ref[...] = out


def _ffn_call(x1, g_ffn, wg, wu, wd, tm):
    n_tok = x1.shape[0]
    d_ff = wg.shape[1]
    const = lambda i: (0, 0)
    resident = dict(pipeline_mode=pl.Buffered(1))
    return pl.pallas_call(
        _ffn_kernel,
        grid=(n_tok // tm,),
        in_specs=[
            pl.BlockSpec((tm, D_MODEL), lambda i: (i, 0)),
            pl.BlockSpec((1, D_MODEL), const),
            pl.BlockSpec((D_MODEL, d_ff), const, **resident),
            pl.BlockSpec((D_MODEL, d_ff), const, **resident),
            pl.BlockSpec((d_ff, D_MODEL), const, **resident),
        ],
        out_specs=pl.BlockSpec((tm, D_MODEL), lambda i: (i, 0)),
        out_shape=jax.ShapeDtypeStruct((n_tok, D_MODEL), F32),
        compiler_params=pltpu.CompilerParams(
            dimension_semantics=("parallel",), vmem_limit_bytes=VMEM_LIMIT),
        name="ffn",
    )(x1, g_ffn, wg, wu, wd)


@functools.lru_cache(maxsize=None)
def _rope_tables(seq):
    half = ROPE_DIMS // 2
    inv_freq = 1.0 / (ROPE_THETA ** (np.arange(half, dtype=np.float64) * 2.0 / ROPE_DIMS))
    ang = np.arange(seq, dtype=np.float64)[:, None] * inv_freq[None, :]
    cos, sin = np.cos(ang), np.sin(ang)
    d = np.arange(LANES) % HEAD_DIM
    rc = np.where(d[None, :] < ROPE_DIMS, cos[:, d % half], 1.0)
    rs1 = np.where(d[None, :] < half, -sin[:, d % half], 0.0)
    rs2 = np.where((d[None, :] >= half) & (d[None, :] < ROPE_DIMS), sin[:, d % half], 0.0)
    return tuple(np.asarray(t, np.float32) for t in (rc, rs1, rs2))


@functools.lru_cache(maxsize=None)
def _segment_mean_matrices():
    def blockdiag(width):
        return np.kron(np.eye(SEG_TILE // width), np.ones((width, width))) / width
    return np.stack([blockdiag(HEAD_DIM), blockdiag(MEM_HEAD_DIM)]).astype(np.float32)


def _score_bound(g_q, g_k):
    return (BOUND_SLACK * HEAD_DIM ** 0.5 * jnp.max(jnp.abs(g_q)) * jnp.max(jnp.abs(g_k)))


def _lane_row(value, first, count):
    lane = jnp.arange(LANES)
    return jnp.where((lane >= first) & (lane < first + count), value, 0.0).astype(F32)[None, :]


def kernel(x, mem, g_mix, w_in, b_forget, g_q_moba, g_k_moba, g_q_fox, g_k_fox, g_q_mem, g_k_mem,
           g_mem, w_mem_kv, w_br_moba, w_br_fox, w_br_mem, w_out, g_ffn, w_gate, w_up, w_down):
    batch, seq, _ = x.shape
    depth = g_mix.shape[0]
    n_tok = batch * seq
    assert seq % (KEY_GROUP * ATT_TILE) == 0 and seq // MOBA_BLOCK < MOBA_SHIFT_LANE
    n_extra = 3 * N_HEADS

    rc, rs1, rs2 = (jnp.asarray(t) for t in _rope_tables(seq))
    bd = jnp.asarray(_segment_mean_matrices()).astype(BF16)
    ones = jnp.ones((COL_TILE,), F32)

    x2 = x.reshape(n_tok, D_MODEL)
    for layer in range(depth):
        w = w_in[layer]
        gates0 = 7 * ATT_WIDTH + N_HEADS
        qc0 = 6 * ATT_WIDTH + N_HEADS
        w_t = jnp.swapaxes(w, 0, 1)
        w_qkv = jnp.concatenate([w_t[:6 * ATT_WIDTH], w_t[qc0:gates0]], axis=0).astype(BF16)
        w_gates = w_t[gates0:].astype(BF16)
        wf3 = jnp.repeat(w_t[6 * ATT_WIDTH:qc0], 3, axis=0)
        gap = jnp.zeros((FOX_Q_LANE0 - n_extra, D_MODEL), F32)
        tail = jnp.zeros((LANES - FOX_Q_LANE0 - n_extra, D_MODEL), F32)
        wf = jnp.concatenate([wf3, gap, wf3, tail], axis=0).astype(BF16)
        bf3 = jnp.repeat(b_forget[layer], 3)
        bf_rep = jnp.concatenate([bf3, gap[:, 0], bf3, tail[:, 0]])[None, :]

        att_scale = HEAD_DIM ** -0.5 * LOG2E
        gains = [ones] * N_COL_TILES
        gains[TILE_QM] = jnp.tile(g_q_moba[layer], N_HEADS) * att_scale
        gains[TILE_KM] = jnp.tile(g_k_moba[layer], N_HEADS)
        gains[TILE_QF] = jnp.tile(g_q_fox[layer], N_HEADS) * att_scale
        gains[TILE_KF] = jnp.tile(g_k_fox[layer], N_HEADS)
        gains[TILE_QC] = jnp.tile(g_q_mem[layer], MEM_HEADS) * (MEM_HEAD_DIM ** -0.5 * LOG2E)
        gains = jnp.stack(gains)[:, None, :]

        bound_m = _score_bound(g_q_moba[layer], g_k_moba[layer])
        bound_f = _score_bound(g_q_fox[layer], g_k_fox[layer])
        shift_m = _lane_row(bound_m * LOG2E, MOBA_SHIFT_LANE, 1)
        shift_f = _lane_row(bound_f * LOG2E, FOX_Q_LANE0, n_extra)

        proj, faug, kmean = _proj_call(x2, g_mix[layer][None, :], w_qkv, wf, bf_rep, shift_f, bd,
                                       gains, rc, rs1, rs2, seq, tm=512)
        kmean = kmean.reshape(batch, seq // MOBA_BLOCK, COL_TILE)
        mk, mv = _memkv_call(mem, g_mem[layer][None, :], w_mem_kv[layer].astype(BF16),
                             g_k_mem[layer][None, :])
        y_m = lax.cond(2.0 * bound_m <= MAX_SHIFT_NATS,
                       lambda: _moba_call(proj, kmean, shift_m, batch, seq, True),
                       lambda: _moba_call(proj, kmean, shift_m, batch, seq, False))
        y_f = lax.cond(2.0 * bound_f <= MAX_SHIFT_NATS,
                       lambda: _fox_call(proj, faug, batch, seq, True),
                       lambda: _fox_call(proj, faug, batch, seq, False))
        x1 = _merge_call(x2, g_mix[layer][None, :], proj, y_m, y_f, mk, mv, w_gates,
                         w_br_moba[layer].astype(BF16), w_br_fox[layer].astype(BF16),
                         w_br_mem[layer].astype(BF16), w_out[layer].astype(BF16), seq, tm=512)
        x2 = _ffn_call(x1, g_ffn[layer][None, :], w_gate[layer].astype(BF16),
                       w_up[layer].astype(BF16), w_down[layer].astype(BF16), tm=512)
    return x2.reshape(batch, seq, D_MODEL)
```

```python
import functools
import math

import numpy as np
import jax
import jax.numpy as jnp
from jax import lax
from jax.experimental import pallas as pl
from jax.experimental.pallas import tpu as pltpu

F32 = jnp.float32
BF16 = jnp.bfloat16

D_MODEL = 1024
HEAD_DIM = 64
N_HEADS = 8
MEM_HEADS = 4
MEM_HEAD_DIM = 128
ATT_WIDTH = 512
MOBA_BLOCK = 256
MOBA_TOPK = 3
ROPE_THETA = 500000.0
ROPE_DIMS = 16
NORM_EPS = 1e-6
NEG = -1e30
LOG2E = math.log2(math.e)

LANES = 128
COL_TILE = 512
SEG_TILE = 256
N_COL_TILES = 7
PROJ_WIDTH = COL_TILE * N_COL_TILES
TILE_QM, TILE_KM, TILE_QF, TILE_KF, TILE_QC = 0, 1, 3, 4, 6
ATT_TILE = 256
KEY_GROUP = 4
Q_SUB = KEY_GROUP
PRE_SLOT = 2
FFN_CHUNK = 1024
PROJ_TILE = 512
TOKEN_TILE = 1024
VMEM_LIMIT = 52 * 1024 * 1024

FOX_Q_LANE0 = 32
MOBA_SHIFT_LANE = LANES - 1
MAX_SHIFT_NATS = 60.0
BOUND_SLACK = 1.02


def _nt(a, b):
    return lax.dot_general(a, b, (((1,), (1,)), ((), ())), preferred_element_type=F32)


def _dot(a, b):
    return jnp.dot(a, b, preferred_element_type=F32)


def _split3(v):
    p1 = v.astype(BF16)
    r1 = v - p1.astype(F32)
    p2 = r1.astype(BF16)
    p3 = (r1 - p2.astype(F32)).astype(BF16)
    return p1, p2, p3


def _rms_normed(x, gain_row):
    ms = jnp.mean(x * x, axis=-1, keepdims=True)
    return (x * lax.rsqrt(ms + NORM_EPS) * gain_row).astype(BF16)


def _forget_lanes(cum, shift_row):
    lane = lax.broadcasted_iota(jnp.int32, cum.shape, 1)
    key_side = lane < 3 * N_HEADS
    query_side = (lane >= FOX_Q_LANE0) & (lane < FOX_Q_LANE0 + 3 * N_HEADS)
    cum2 = cum * LOG2E
    val = jnp.where(key_side, -cum2, cum2 - shift_row)
    c1, c2, c3 = _split3(val)
    piece = jnp.where(key_side, lane, lane - FOX_Q_LANE0) % 3
    sel = jnp.where(piece == 0, c1, jnp.where(piece == 1, c2, c3))
    return jnp.where(key_side | query_side, sel, jnp.zeros_like(sel))


def _proj_kernel(x_ref, gmix_ref, w_ref, wf_ref, bf_ref, shift_ref, bd_ref, gain_ref,
                 rc_ref, rs1_ref, rs2_ref, p_ref, faug_ref, kmean_ref, carry_ref, *, seq_tiles):
    tm = x_ref.shape[0]
    h = _rms_normed(x_ref[...], gmix_ref[...])

    @pl.when(pl.program_id(0) % seq_tiles == 0)
    def _():
        carry_ref[...] = jnp.zeros_like(carry_ref)

    z = _nt(h, wf_ref[...]) + bf_ref[...]
    lf = jnp.minimum(z, 0.0) - jnp.log1p(jnp.exp(-jnp.abs(z)))
    row = lax.broadcasted_iota(jnp.int32, (tm, tm), 0)
    col = lax.broadcasted_iota(jnp.int32, (tm, tm), 1)
    tri = jnp.where(col <= row, 1.0, 0.0).astype(BF16)
    p1, p2, p3 = _split3(lf)
    cum = _dot(tri, p1) + _dot(tri, p2) + _dot(tri, p3) + carry_ref[...]
    carry_ref[...] = cum[tm - 1:tm, :]
    faug_ref[...] = _forget_lanes(cum, shift_ref[...])

    def head_normed(t, k):
        seg = 1 if k == TILE_QC else 0
        sq = (t * t).astype(BF16)
        ms = jnp.concatenate(
            [_dot(sq[:, lo:lo + SEG_TILE], bd_ref[seg]) for lo in range(0, COL_TILE, SEG_TILE)],
            axis=1)
        return t * lax.rsqrt(ms + NORM_EPS) * gain_ref[k]

    def rotated(y):
        rc, rs1, rs2 = rc_ref[...], rs1_ref[...], rs2_ref[...]
        parts = []
        for g in range(COL_TILE // LANES):
            yg = y[:, g * LANES:(g + 1) * LANES]
            up = pltpu.roll(yg, LANES - ROPE_DIMS // 2, 1)
            dn = pltpu.roll(yg, ROPE_DIMS // 2, 1)
            parts.append(yg * rc + up * rs1 + dn * rs2)
        return jnp.concatenate(parts, axis=1)

    for k in range(N_COL_TILES):
        cols = slice(k * COL_TILE, (k + 1) * COL_TILE)
        t = _nt(h, w_ref[cols, :])
        if k in (TILE_QM, TILE_KM):
            t = rotated(head_normed(t, k))
        elif k in (TILE_QF, TILE_KF, TILE_QC):
            t = head_normed(t, k)
        p_ref[:, cols] = t.astype(BF16)
        if k == TILE_KM:
            for r in range(tm // MOBA_BLOCK):
                blk = t[r * MOBA_BLOCK:(r + 1) * MOBA_BLOCK, :]
                kmean_ref[0, pl.ds(r, 1), :] = jnp.mean(blk, axis=0, keepdims=True)


def _proj_call(x2, gmix, w_qkv, wf, bf_row, shift_row, bd, gains, rc, rs1, rs2, seq, tm):
    n_tok = x2.shape[0]
    seq_tiles = seq // tm
    const2 = lambda i: (0, 0)
    const3 = lambda i: (0, 0, 0)
    return pl.pallas_call(
        functools.partial(_proj_kernel, seq_tiles=seq_tiles),
        grid=(n_tok // tm,),
        in_specs=[
            pl.BlockSpec((tm, D_MODEL), lambda i: (i, 0)),
            pl.BlockSpec((1, D_MODEL), const2),
            pl.BlockSpec((PROJ_WIDTH, D_MODEL), const2),
            pl.BlockSpec((LANES, D_MODEL), const2),
            pl.BlockSpec((1, LANES), const2),
            pl.BlockSpec((1, LANES), const2),
            pl.BlockSpec((2, SEG_TILE, SEG_TILE), const3),
            pl.BlockSpec((N_COL_TILES, 1, COL_TILE), const3),
            pl.BlockSpec((tm, LANES), lambda i: (i % seq_tiles, 0)),
            pl.BlockSpec((tm, LANES), lambda i: (i % seq_tiles, 0)),
            pl.BlockSpec((tm, LANES), lambda i: (i % seq_tiles, 0)),
        ],
        out_specs=[
            pl.BlockSpec((tm, PROJ_WIDTH), lambda i: (i, 0)),
            pl.BlockSpec((tm, LANES), lambda i: (i, 0)),
            pl.BlockSpec((1, tm // MOBA_BLOCK, COL_TILE), lambda i: (i, 0, 0)),
        ],
        out_shape=[
            jax.ShapeDtypeStruct((n_tok, PROJ_WIDTH), BF16),
            jax.ShapeDtypeStruct((n_tok, LANES), BF16),
            jax.ShapeDtypeStruct((n_tok // tm, tm // MOBA_BLOCK, COL_TILE), F32),
        ],
        scratch_shapes=[pltpu.VMEM((1, LANES), F32)],
        compiler_params=pltpu.CompilerParams(
            dimension_semantics=("arbitrary",), vmem_limit_bytes=VMEM_LIMIT),
        name="proj",
    )(x2, gmix, w_qkv, wf, bf_row, shift_row, bd, gains, rc, rs1, rs2)


def _memkv_kernel(mem_ref, gmem_ref, w_ref, gk_ref, mk_ref, mv_ref):
    hm = _rms_normed(mem_ref[0], gmem_ref[...])
    kv = _dot(hm, w_ref[...])
    ks = []
    for h in range(MEM_HEADS):
        kh = kv[:, h * MEM_HEAD_DIM:(h + 1) * MEM_HEAD_DIM]
        msk = jnp.mean(kh * kh, axis=-1, keepdims=True)
        ks.append(kh * lax.rsqrt(msk + NORM_EPS) * gk_ref[...])
    mk_ref[0] = jnp.concatenate(ks, axis=1).astype(BF16)
    mv_ref[0] = kv[:, ATT_WIDTH:].astype(BF16)


def _memkv_call(mem, gmem, w_kv, gk):
    batch, mlen, _ = mem.shape
    return pl.pallas_call(
        _memkv_kernel,
        grid=(batch,),
        in_specs=[
            pl.BlockSpec((1, mlen, D_MODEL), lambda b: (b, 0, 0)),
            pl.BlockSpec((1, D_MODEL), lambda b: (0, 0)),
            pl.BlockSpec((D_MODEL, 2 * ATT_WIDTH), lambda b: (0, 0)),
            pl.BlockSpec((1, MEM_HEAD_DIM), lambda b: (0, 0)),
        ],
        out_specs=[
            pl.BlockSpec((1, mlen, ATT_WIDTH), lambda b: (b, 0, 0)),
            pl.BlockSpec((1, mlen, ATT_WIDTH), lambda b: (b, 0, 0)),
        ],
        out_shape=[jax.ShapeDtypeStruct((batch, mlen, ATT_WIDTH), BF16)] * 2,
        compiler_params=pltpu.CompilerParams(dimension_semantics=("parallel",)),
        name="memkv",
    )(mem, gmem, w_kv, gk)


def _lane_iota():
    return lax.broadcasted_iota(jnp.int32, (ATT_TILE, LANES), 1)


def _lane_range(first, count, dtype):
    lane = _lane_iota()
    return jnp.where((lane >= first) & (lane < first + count), 1.0, 0.0).astype(dtype)


def _head_lanes(c, dtype):
    return _lane_range(c * HEAD_DIM, HEAD_DIM, dtype)


def _fill_values(v_ref, vaug_ref):
    n_blocks = v_ref.shape[0] // ATT_TILE

    def fill(j, _):
        rows = pl.ds(pl.multiple_of(j * ATT_TILE, ATT_TILE), ATT_TILE)
        v = v_ref[rows, :].astype(F32)
        for c in range(2):
            own = _head_lanes(c, F32)
            vaug_ref[c, rows, :] = (v * own + (1.0 - own)).astype(BF16)
        return 0

    lax.fori_loop(0, n_blocks, fill, 0)


def _normalised(acc_ref, t):
    res = []
    for c in range(2):
        acc = acc_ref[2 * t + c]
        res.append(acc / pltpu.roll(acc, HEAD_DIM, 1))
    return jnp.where(_lane_iota() < HEAD_DIM, res[0], res[1])


def _causal_masks():
    span = KEY_GROUP * ATT_TILE
    diff = (lax.broadcasted_iota(jnp.int32, (ATT_TILE, span), 1)
            - lax.broadcasted_iota(jnp.int32, (ATT_TILE, span), 0))
    return [diff <= t * ATT_TILE for t in range(Q_SUB)]


def _attend_shifted(lhs_of_tile, kaug_ref, vaug_ref, acc_ref, s_ref, lhs_ref, qi):
    span = KEY_GROUP * ATT_TILE
    n_chain = 2 * Q_SUB

    def group_rows(g):
        return pl.ds(pl.multiple_of(g * span, span), span)

    def scores(g, slot):
        kb = kaug_ref[group_rows(g), :]
        for i in range(n_chain):
            s_ref[slot, i] = _nt(lhs_ref[i], kb)

    def scores_diagonal(g, slot):
        for i in range(n_chain):
            rows = pl.ds(pl.multiple_of(g * span, span), (i // 2 + 1) * ATT_TILE)
            s_ref[slot, i, :, 0:(i // 2 + 1) * ATT_TILE] = _nt(lhs_ref[i], kaug_ref[rows, :])

    def consume(g, slot):
        rows = group_rows(g)
        for i in range(n_chain):
            p = jnp.exp2(s_ref[slot, i]).astype(BF16)
            acc_ref[i] += _dot(p, vaug_ref[i % 2, rows, :])

    def consume_diagonal(g, slot):
        row = lax.broadcasted_iota(jnp.int32, (ATT_TILE, ATT_TILE), 0)
        col = lax.broadcasted_iota(jnp.int32, (ATT_TILE, ATT_TILE), 1)
        for i in range(n_chain):
            t = i // 2
            lo = t * ATT_TILE
            parts = []
            if t:
                parts.append(jnp.exp2(s_ref[slot, i, :, 0:lo]).astype(BF16))
            s_diag = jnp.where(col <= row, s_ref[slot, i, :, lo:lo + ATT_TILE], NEG)
            parts.append(jnp.exp2(s_diag).astype(BF16))
            p = parts[0] if len(parts) == 1 else jnp.concatenate(parts, axis=1)
            rows = pl.ds(pl.multiple_of(g * span, span), lo + ATT_TILE)
            acc_ref[i] += _dot(p, vaug_ref[i % 2, rows, :])

    acc_ref[...] = jnp.zeros_like(acc_ref)
    n_full = qi

    @pl.when(n_full == 0)
    def _():
        for i, operand in enumerate(lhs_of_tile(0)):
            lhs_ref[i] = operand
        scores_diagonal(0, 0)

    @pl.when(n_full > 0)
    def _():
        def slot_of(g):
            return jnp.where(g == 0, PRE_SLOT, lax.rem(g, 2))

        def body(g, _):
            consume(g, slot_of(g))
            scores(g + 1, lax.rem(g + 1, 2))
            return 0

        lax.fori_loop(0, n_full - 1, body, 0)
        consume(n_full - 1, slot_of(n_full - 1))
        scores_diagonal(n_full, lax.rem(n_full, 2))

    consume_diagonal(n_full, lax.rem(n_full, 2))
    nxt = lhs_of_tile(1)
    kb = kaug_ref[group_rows(0), :]
    for i in range(n_chain):
        lhs_ref[i] = nxt[i]
        s_ref[PRE_SLOT, i] = _nt(nxt[i], kb)
    return [_normalised(acc_ref, t) for t in range(Q_SUB)]


def _attend_running_max(lhs, kaug_ref, vaug_ref, acc_ref, s_ref, qi):
    span = KEY_GROUP * ATT_TILE
    n_chain = len(lhs)

    def step(i, g, m, masks):
        rows = pl.ds(pl.multiple_of(g * span, span), span)
        s = _nt(lhs[i], kaug_ref[rows, :])
        if masks is not None:
            s = jnp.where(masks[i // 2], s, NEG)
        m_new = jnp.maximum(m, jnp.max(s, axis=1, keepdims=True))
        p = jnp.exp2(s - m_new).astype(BF16)
        acc_ref[i] = jnp.exp2(m - m_new) * acc_ref[i] + _dot(p, vaug_ref[i % 2, rows, :])
        return m_new

    acc_ref[...] = jnp.zeros_like(acc_ref)
    n_full = qi
    m0 = jnp.full((ATT_TILE, 1), -jnp.inf, F32)
    ms = lax.fori_loop(
        0, n_full, lambda g, ms: tuple(step(i, g, ms[i], None) for i in range(n_chain)),
        (m0,) * n_chain)
    masks = _causal_masks()
    for i in range(n_chain):
        step(i, n_full, ms[i], masks)
    return [_normalised(acc_ref, t) for t in range(Q_SUB)]


def _attend(lhs_of_tile, kaug_ref, vaug_ref, acc_ref, s_ref, lhs_ref, qi, bounded):
    if bounded:
        return _attend_shifted(lhs_of_tile, kaug_ref, vaug_ref, acc_ref, s_ref, lhs_ref, qi)
    return _attend_running_max(lhs_of_tile(0), kaug_ref, vaug_ref, acc_ref, s_ref, qi)


def _attention_scratch(seq):
    n_chain = 2 * Q_SUB
    return [pltpu.VMEM((seq, 2 * LANES), BF16),
            pltpu.VMEM((2, seq, LANES), BF16),
            pltpu.VMEM((n_chain, ATT_TILE, LANES), F32),
            pltpu.VMEM((3, n_chain, ATT_TILE, KEY_GROUP * ATT_TILE), F32),
            pltpu.VMEM((n_chain, ATT_TILE, 2 * LANES), BF16)]


def _sub_rows(t):
    return slice(t * ATT_TILE, (t + 1) * ATT_TILE)


def _moba_kernel(q_ref, qn_ref, k_ref, v_ref, km_ref, shift_ref, o_ref, kaug_ref, vaug_ref,
                 acc_ref, s_ref, lhs_ref, *, bounded):
    qi = pl.program_id(2)
    n_q = pl.num_programs(2)
    seq = k_ref.shape[0]
    n_blocks = seq // MOBA_BLOCK
    lane = _lane_iota()

    @pl.when(qi == 0)
    def _():
        kaug_ref[:, 0:LANES] = k_ref[...]

        def fill(j, _):
            rows = pl.ds(pl.multiple_of(j * MOBA_BLOCK, MOBA_BLOCK), MOBA_BLOCK)
            onehot = (lane == j) | (lane == MOBA_SHIFT_LANE)
            kaug_ref[rows, LANES:2 * LANES] = jnp.where(onehot, 1.0, 0.0).astype(BF16)
            return 0

        lax.fori_loop(0, n_blocks, fill, 0)
        _fill_values(v_ref, vaug_ref)

    def lhs_of_tile(which):
        q_blk = qn_ref if which else q_ref
        tile = jnp.minimum(qi + which, n_q - 1)
        km = km_ref[0]
        km = jnp.concatenate([km, jnp.zeros((LANES - n_blocks, LANES), F32)], axis=0)
        km_hi = km.astype(BF16)
        km_lo = (km - km_hi.astype(F32)).astype(BF16)
        shift = shift_ref[:, MOBA_SHIFT_LANE:MOBA_SHIFT_LANE + 1] if bounded else 0.0
        blk = lax.broadcasted_iota(jnp.int32, (n_blocks, ATT_TILE), 0)
        blk_f = blk.astype(F32)
        pad_row = lax.broadcasted_iota(jnp.int32, (LANES - n_blocks, ATT_TILE), 0) + n_blocks
        pad_bias = jnp.where(pad_row == MOBA_SHIFT_LANE, -shift, NEG)
        lhs = []
        for t in range(Q_SUB):
            own = tile * Q_SUB + t
            q2 = q_blk[_sub_rows(t), :]
            for c in range(2):
                qc = q2 * _head_lanes(c, BF16)
                gate = (_nt(km_hi, qc) + _nt(km_lo, qc))[0:n_blocks]
                g = jnp.where(blk < own, gate, -jnp.inf)
                sel = blk == own
                for _ in range(MOBA_TOPK):
                    mx = jnp.max(g, axis=0, keepdims=True)
                    first = jnp.min(jnp.where(g == mx, blk_f, float(LANES)), axis=0,
                                    keepdims=True)
                    pick = (blk_f == first) & (mx > -jnp.inf)
                    sel = sel | pick
                    g = jnp.where(pick, -jnp.inf, g)
                bias = jnp.concatenate([jnp.where(sel, 0.0, NEG), pad_bias], axis=0)
                lhs.append(jnp.concatenate([qc, bias.T.astype(BF16)], axis=1))
        return lhs

    outs = _attend(lhs_of_tile, kaug_ref, vaug_ref, acc_ref, s_ref, lhs_ref, qi, bounded)
    for t in range(Q_SUB):
        o_ref[_sub_rows(t), :] = outs[t].astype(BF16)


def _moba_call(proj, kmean, shift_row, batch, seq, bounded):
    tq = Q_SUB * ATT_TILE
    n_q = seq // tq
    n_pairs = N_HEADS // 2
    q0 = (TILE_QM * COL_TILE) // LANES
    k0 = (TILE_KM * COL_TILE) // LANES
    v0 = k0 + COL_TILE // LANES
    return pl.pallas_call(
        functools.partial(_moba_kernel, bounded=bounded),
        grid=(batch, n_pairs, n_q),
        in_specs=[
            pl.BlockSpec((tq, LANES), lambda b, p, i: (b * n_q + i, q0 + p)),
            pl.BlockSpec((tq, LANES), lambda b, p, i: (b * n_q + jnp.minimum(i + 1, n_q - 1), q0 + p)),
            pl.BlockSpec((seq, LANES), lambda b, p, i: (b, k0 + p)),
            pl.BlockSpec((seq, LANES), lambda b, p, i: (b, v0 + p)),
            pl.BlockSpec((1, seq // MOBA_BLOCK, LANES), lambda b, p, i: (b, 0, p)),
            pl.BlockSpec((1, LANES), lambda b, p, i: (0, 0)),
        ],
        out_specs=pl.BlockSpec((tq, LANES), lambda b, p, i: (b * n_q + i, p)),
        out_shape=jax.ShapeDtypeStruct((batch * seq, ATT_WIDTH), BF16),
        scratch_shapes=_attention_scratch(seq),
        compiler_params=pltpu.CompilerParams(
            dimension_semantics=("parallel", "parallel", "arbitrary"),
            vmem_limit_bytes=VMEM_LIMIT),
        name="moba_bounded" if bounded else "moba_general",
    )(proj, proj, proj, proj, kmean, shift_row)


def _fox_kernel(q_ref, qn_ref, k_ref, v_ref, faug_ref, fq_ref, fqn_ref, o_ref, kaug_ref, vaug_ref,
                acc_ref, s_ref, lhs_ref, *, bounded):
    pair = pl.program_id(1)
    qi = pl.program_id(2)
    n_extra = 3 * N_HEADS

    @pl.when(qi == 0)
    def _():
        seq = k_ref.shape[0]
        kaug_ref[:, 0:LANES] = k_ref[...]

        def fill(j, _):
            rows = pl.ds(pl.multiple_of(j * ATT_TILE, ATT_TILE), ATT_TILE)
            lane = _lane_iota()
            ones = (lane >= FOX_Q_LANE0) & (lane < FOX_Q_LANE0 + n_extra)
            f = faug_ref[rows, :].astype(F32)
            kaug_ref[rows, LANES:2 * LANES] = (
                f * (lane < n_extra).astype(F32) + ones.astype(F32)).astype(BF16)
            return 0

        lax.fori_loop(0, seq // ATT_TILE, fill, 0)
        _fill_values(v_ref, vaug_ref)

    def lhs_of_tile(which):
        q_blk, fq_blk = (qn_ref, fqn_ref) if which else (q_ref, fq_ref)
        lhs = []
        for t in range(Q_SUB):
            q2 = q_blk[_sub_rows(t), :]
            fq = fq_blk[_sub_rows(t), :]
            for c in range(2):
                qc = q2 * _head_lanes(c, BF16)
                k_first = 3 * (2 * pair + c)
                extra = _lane_range(k_first, 3, BF16)
                if bounded:
                    extra = extra + fq * _lane_range(FOX_Q_LANE0 + k_first, 3, BF16)
                lhs.append(jnp.concatenate([qc, extra], axis=1))
        return lhs

    outs = _attend(lhs_of_tile, kaug_ref, vaug_ref, acc_ref, s_ref, lhs_ref, qi, bounded)
    for t in range(Q_SUB):
        o_ref[_sub_rows(t), :] = outs[t].astype(BF16)


def _fox_call(proj, faug, batch, seq, bounded):
    tq = Q_SUB * ATT_TILE
    n_q = seq // tq
    n_pairs = N_HEADS // 2
    q0 = (TILE_QF * COL_TILE) // LANES
    k0 = (TILE_KF * COL_TILE) // LANES
    v0 = k0 + COL_TILE // LANES
    return pl.pallas_call(
        functools.partial(_fox_kernel, bounded=bounded),
        grid=(batch, n_pairs, n_q),
        in_specs=[
            pl.BlockSpec((tq, LANES), lambda b, p, i: (b * n_q + i, q0 + p)),
            pl.BlockSpec((tq, LANES), lambda b, p, i: (b * n_q + jnp.minimum(i + 1, n_q - 1), q0 + p)),
            pl.BlockSpec((seq, LANES), lambda b, p, i: (b, k0 + p)),
            pl.BlockSpec((seq, LANES), lambda b, p, i: (b, v0 + p)),
            pl.BlockSpec((seq, LANES), lambda b, p, i: (b, 0)),
            pl.BlockSpec((tq, LANES), lambda b, p, i: (b * n_q + i, 0)),
            pl.BlockSpec((tq, LANES), lambda b, p, i: (b * n_q + jnp.minimum(i + 1, n_q - 1), 0)),
        ],
        out_specs=pl.BlockSpec((tq, LANES), lambda b, p, i: (b * n_q + i, p)),
        out_shape=jax.ShapeDtypeStruct((batch * seq, ATT_WIDTH), BF16),
        scratch_shapes=_attention_scratch(seq),
        compiler_params=pltpu.CompilerParams(
            dimension_semantics=("parallel", "parallel", "arbitrary"),
            vmem_limit_bytes=VMEM_LIMIT),
        name="fox_bounded" if bounded else "fox_general",
    )(proj, proj, proj, proj, faug, faug, faug)


def _merge_kernel(x_ref, gmix_ref, qc_ref, ym_ref, yf_ref, mk_ref, mv_ref,
                  wg_ref, wbm_ref, wbf_ref, wbc_ref, wo_ref, o_ref):
    x = x_ref[...]
    h = _rms_normed(x, gmix_ref[...])
    qc = qc_ref[...]
    mk = mk_ref[0]
    mv = mv_ref[0]
    ys = []
    for hd in range(MEM_HEADS):
        cols = slice(hd * MEM_HEAD_DIM, (hd + 1) * MEM_HEAD_DIM)
        s = _nt(qc[:, cols], mk[:, cols])
        p = jnp.exp2(s - jnp.max(s, axis=1, keepdims=True))
        y = _dot(p.astype(BF16), mv[:, cols]) / jnp.sum(p, axis=1, keepdims=True)
        ys.append(y)
    yc = jnp.concatenate(ys, axis=1).astype(BF16)
    merged = None
    for i, (y, wb_ref) in enumerate(((ym_ref[...], wbm_ref), (yf_ref[...], wbf_ref),
                                     (yc, wbc_ref))):
        gate = jax.nn.sigmoid(_nt(h, wg_ref[i * D_MODEL:(i + 1) * D_MODEL, :]))
        term = gate * _dot(y, wb_ref[...])
        merged = term if merged is None else merged + term
    o_ref[...] = x + _dot(merged.astype(BF16), wo_ref[...])


def _merge_call(x2, gmix, proj, y_m, y_f, mk, mv, wg, wbm, wbf, wbc, wo, seq, tm):
    n_tok = x2.shape[0]
    mlen = mk.shape[1]
    seq_tiles = seq // tm
    const = lambda i: (0, 0)
    return pl.pallas_call(
        _merge_kernel,
        grid=(n_tok // tm,),
        in_specs=[
            pl.BlockSpec((tm, D_MODEL), lambda i: (i, 0)),
            pl.BlockSpec((1, D_MODEL), const),
            pl.BlockSpec((tm, ATT_WIDTH), lambda i: (i, TILE_QC)),
            pl.BlockSpec((tm, ATT_WIDTH), lambda i: (i, 0)),
            pl.BlockSpec((tm, ATT_WIDTH), lambda i: (i, 0)),
            pl.BlockSpec((1, mlen, ATT_WIDTH), lambda i: (i // seq_tiles, 0, 0)),
            pl.BlockSpec((1, mlen, ATT_WIDTH), lambda i: (i // seq_tiles, 0, 0)),
            pl.BlockSpec((3 * D_MODEL, D_MODEL), const),
            pl.BlockSpec((ATT_WIDTH, D_MODEL), const),
            pl.BlockSpec((ATT_WIDTH, D_MODEL), const),
            pl.BlockSpec((ATT_WIDTH, D_MODEL), const),
            pl.BlockSpec((D_MODEL, D_MODEL), const),
        ],
        out_specs=pl.BlockSpec((tm, D_MODEL), lambda i: (i, 0)),
        out_shape=jax.ShapeDtypeStruct((n_tok, D_MODEL), F32),
        compiler_params=pltpu.CompilerParams(
            dimension_semantics=("parallel",), vmem_limit_bytes=VMEM_LIMIT),
        name="merge",
    )(x2, gmix, proj, y_m, y_f, mk, mv, wg, wbm, wbf, wbc, wo)


def _ffn_kernel(x_ref, g_ref, wg_ref, wu_ref, wd_ref, o_ref):
    x = x_ref[...]
    h = _rms_normed(x, g_ref[...])
    d_ff = wg_ref.shape[1]
    out = x
    for lo in range(0, d_ff, FFN_CHUNK):
        cols = slice(lo, min(lo + FFN_CHUNK, d_ff))
        g = _dot(h, wg_ref[:, cols])
        u = _dot(h, wu_ref[:, cols])
        a = (g * jax.nn.sigmoid(g) * u).astype(BF16)
        out = out + _dot(a, wd_ref[cols, :])
    o_ref[...] = out


def _ffn_call(x1, g_ffn, wg, wu, wd, tm):
    n_tok = x1.shape[0]
    d_ff = wg.shape[1]
    const = lambda i: (0, 0)
    resident = dict(pipeline_mode=pl.Buffered(1))
    return pl.pallas_call(
        _ffn_kernel,
        grid=(n_tok // tm,),
        in_specs=[
            pl.BlockSpec((tm, D_MODEL), lambda i: (i, 0)),
            pl.BlockSpec((1, D_MODEL), const),
            pl.BlockSpec((D_MODEL, d_ff), const, **resident),
            pl.BlockSpec((D_MODEL, d_ff), const, **resident),
            pl.BlockSpec((d_ff, D_MODEL), const, **resident),
        ],
        out_specs=pl.BlockSpec((tm, D_MODEL), lambda i: (i, 0)),
        out_shape=jax.ShapeDtypeStruct((n_tok, D_MODEL), F32),
        compiler_params=pltpu.CompilerParams(
            dimension_semantics=("parallel",), vmem_limit_bytes=VMEM_LIMIT),
        name="ffn",
    )(x1, g_ffn, wg, wu, wd)


@functools.lru_cache(maxsize=None)
def _rope_tables(seq):
    half = ROPE_DIMS // 2
    inv_freq = 1.0 / (ROPE_THETA ** (np.arange(half, dtype=np.float64) * 2.0 / ROPE_DIMS))
    ang = np.arange(seq, dtype=np.float64)[:, None] * inv_freq[None, :]
    cos, sin = np.cos(ang), np.sin(ang)
    d = np.arange(LANES) % HEAD_DIM
    rc = np.where(d[None, :] < ROPE_DIMS, cos[:, d % half], 1.0)
    rs1 = np.where(d[None, :] < half, -sin[:, d % half], 0.0)
    rs2 = np.where((d[None, :] >= half) & (d[None, :] < ROPE_DIMS), sin[:, d % half], 0.0)
    return tuple(np.asarray(t, np.float32) for t in (rc, rs1, rs2))


@functools.lru_cache(maxsize=None)
def _segment_mean_matrices():
    def blockdiag(width):
        return np.kron(np.eye(SEG_TILE // width), np.ones((width, width))) / width
    return np.stack([blockdiag(HEAD_DIM), blockdiag(MEM_HEAD_DIM)]).astype(np.float32)


def _score_bound(g_q, g_k):
    return (BOUND_SLACK * HEAD_DIM ** 0.5 * jnp.max(jnp.abs(g_q)) * jnp.max(jnp.abs(g_k)))


def _lane_row(value, first, count):
    lane = jnp.arange(LANES)
    return jnp.where((lane >= first) & (lane < first + count), value, 0.0).astype(F32)[None, :]


def kernel(x, mem, g_mix, w_in, b_forget, g_q_moba, g_k_moba, g_q_fox, g_k_fox, g_q_mem, g_k_mem,
           g_mem, w_mem_kv, w_br_moba, w_br_fox, w_br_mem, w_out, g_ffn, w_gate, w_up, w_down):
    batch, seq, _ = x.shape
    depth = g_mix.shape[0]
    n_tok = batch * seq
    assert seq % (KEY_GROUP * ATT_TILE) == 0 and seq // MOBA_BLOCK < MOBA_SHIFT_LANE
    n_extra = 3 * N_HEADS

    rc, rs1, rs2 = (jnp.asarray(t) for t in _rope_tables(seq))
    bd = jnp.asarray(_segment_mean_matrices()).astype(BF16)
    ones = jnp.ones((COL_TILE,), F32)

    x2 = x.reshape(n_tok, D_MODEL)
    for layer in range(depth):
        w = w_in[layer]
        gates0 = 7 * ATT_WIDTH + N_HEADS
        qc0 = 6 * ATT_WIDTH + N_HEADS
        w_t = jnp.swapaxes(w, 0, 1)
        w_qkv = jnp.concatenate([w_t[:6 * ATT_WIDTH], w_t[qc0:gates0]], axis=0).astype(BF16)
        w_gates = w_t[gates0:].astype(BF16)
        wf3 = jnp.repeat(w_t[6 * ATT_WIDTH:qc0], 3, axis=0)
        gap = jnp.zeros((FOX_Q_LANE0 - n_extra, D_MODEL), F32)
        tail = jnp.zeros((LANES - FOX_Q_LANE0 - n_extra, D_MODEL), F32)
        wf = jnp.concatenate([wf3, gap, wf3, tail], axis=0).astype(BF16)
        bf3 = jnp.repeat(b_forget[layer], 3)
        bf_rep = jnp.concatenate([bf3, gap[:, 0], bf3, tail[:, 0]])[None, :]

        att_scale = HEAD_DIM ** -0.5 * LOG2E
        gains = [ones] * N_COL_TILES
        gains[TILE_QM] = jnp.tile(g_q_moba[layer], N_HEADS) * att_scale
        gains[TILE_KM] = jnp.tile(g_k_moba[layer], N_HEADS)
        gains[TILE_QF] = jnp.tile(g_q_fox[layer], N_HEADS) * att_scale
        gains[TILE_KF] = jnp.tile(g_k_fox[layer], N_HEADS)
        gains[TILE_QC] = jnp.tile(g_q_mem[layer], MEM_HEADS) * (MEM_HEAD_DIM ** -0.5 * LOG2E)
        gains = jnp.stack(gains)[:, None, :]

        bound_m = _score_bound(g_q_moba[layer], g_k_moba[layer])
        bound_f = _score_bound(g_q_fox[layer], g_k_fox[layer])
        shift_m = _lane_row(bound_m * LOG2E, MOBA_SHIFT_LANE, 1)
        shift_f = _lane_row(bound_f * LOG2E, FOX_Q_LANE0, n_extra)

        proj, faug, kmean = _proj_call(x2, g_mix[layer][None, :], w_qkv, wf, bf_rep, shift_f, bd,
                                       gains, rc, rs1, rs2, seq, tm=PROJ_TILE)
        kmean = kmean.reshape(batch, seq // MOBA_BLOCK, COL_TILE)
        mk, mv = _memkv_call(mem, g_mem[layer][None, :], w_mem_kv[layer].astype(BF16),
                             g_k_mem[layer][None, :])
        def attention(bounded):
            return (_moba_call(proj, kmean, shift_m, batch, seq, bounded),
                    _fox_call(proj, faug, batch, seq, bounded))

        y_m, y_f = lax.cond(2.0 * jnp.maximum(bound_m, bound_f) <= MAX_SHIFT_NATS,
                            lambda: attention(True), lambda: attention(False))
        x1 = _merge_call(x2, g_mix[layer][None, :], proj, y_m, y_f, mk, mv, w_gates,
                         w_br_moba[layer].astype(BF16), w_br_fox[layer].astype(BF16),
                         w_br_mem[layer].astype(BF16), w_out[layer].astype(BF16), seq,
                         tm=TOKEN_TILE)
        x2 = _ffn_call(x1, g_ffn[layer][None, :], w_gate[layer].astype(BF16),
                       w_up[layer].astype(BF16), w_down[layer].astype(BF16), tm=TOKEN_TILE)
    return x2.reshape(batch, seq, D_MODEL)
```

```python
import functools
import math

import numpy as np
import jax
import jax.numpy as jnp
from jax import lax
from jax.experimental import pallas as pl
from jax.experimental.pallas import tpu as pltpu

F32 = jnp.float32
BF16 = jnp.bfloat16

D_MODEL = 1024
HEAD_DIM = 64
N_HEADS = 8
MEM_HEADS = 4
MEM_HEAD_DIM = 128
ATT_WIDTH = 512
MOBA_BLOCK = 256
MOBA_TOPK = 3
ROPE_THETA = 500000.0
ROPE_DIMS = 16
NORM_EPS = 1e-6
NEG = -1e30
LOG2E = math.log2(math.e)

LANES = 128
COL_TILE = 512
SEG_TILE = 256
N_COL_TILES = 7
PROJ_WIDTH = COL_TILE * N_COL_TILES
TILE_QM, TILE_KM, TILE_QF, TILE_KF, TILE_QC = 0, 1, 3, 4, 6
ATT_TILE = 256
KEY_GROUP = 4
Q_SUB = KEY_GROUP
PRE_SLOT = 2
FFN_CHUNK = 1024
PROJ_TILE = 512
TOKEN_TILE = 1024
VMEM_LIMIT = 56 * 1024 * 1024

FOX_Q_LANE0 = 32
MOBA_SHIFT_LANE = LANES - 1
MAX_SHIFT_NATS = 60.0
BOUND_SLACK = 1.02


def _nt(a, b):
    return lax.dot_general(a, b, (((1,), (1,)), ((), ())), preferred_element_type=F32)


def _dot(a, b):
    return jnp.dot(a, b, preferred_element_type=F32)


def _split3(v):
    p1 = v.astype(BF16)
    r1 = v - p1.astype(F32)
    p2 = r1.astype(BF16)
    p3 = (r1 - p2.astype(F32)).astype(BF16)
    return p1, p2, p3


def _rms_normed(x, gain_row):
    ms = jnp.mean(x * x, axis=-1, keepdims=True)
    return (x * lax.rsqrt(ms + NORM_EPS) * gain_row).astype(BF16)


def _forget_lanes(cum, shift_row):
    lane = lax.broadcasted_iota(jnp.int32, cum.shape, 1)
    key_side = lane < 3 * N_HEADS
    query_side = (lane >= FOX_Q_LANE0) & (lane < FOX_Q_LANE0 + 3 * N_HEADS)
    cum2 = cum * LOG2E
    val = jnp.where(key_side, -cum2, cum2 - shift_row)
    c1, c2, c3 = _split3(val)
    piece = jnp.where(key_side, lane, lane - FOX_Q_LANE0) % 3
    sel = jnp.where(piece == 0, c1, jnp.where(piece == 1, c2, c3))
    return jnp.where(key_side | query_side, sel, jnp.zeros_like(sel))


def _proj_kernel(x_ref, gmix_ref, w_ref, wf_ref, bf_ref, shift_ref, bd_ref, gain_ref,
                 rc_ref, rs1_ref, rs2_ref, p_ref, faug_ref, kmean_ref, carry_ref, *, seq_tiles):
    tm = x_ref.shape[0]
    h = _rms_normed(x_ref[...], gmix_ref[...])

    @pl.when(pl.program_id(0) % seq_tiles == 0)
    def _():
        carry_ref[...] = jnp.zeros_like(carry_ref)

    z = _nt(h, wf_ref[...]) + bf_ref[...]
    lf = jnp.minimum(z, 0.0) - jnp.log1p(jnp.exp(-jnp.abs(z)))
    row = lax.broadcasted_iota(jnp.int32, (tm, tm), 0)
    col = lax.broadcasted_iota(jnp.int32, (tm, tm), 1)
    tri = jnp.where(col <= row, 1.0, 0.0).astype(BF16)
    p1, p2, p3 = _split3(lf)
    cum = _dot(tri, p1) + _dot(tri, p2) + _dot(tri, p3) + carry_ref[...]
    carry_ref[...] = cum[tm - 1:tm, :]
    faug_ref[...] = _forget_lanes(cum, shift_ref[...])

    def head_normed(t, k):
        seg = 1 if k == TILE_QC else 0
        sq = (t * t).astype(BF16)
        ms = jnp.concatenate(
            [_dot(sq[:, lo:lo + SEG_TILE], bd_ref[seg]) for lo in range(0, COL_TILE, SEG_TILE)],
            axis=1)
        return t * lax.rsqrt(ms + NORM_EPS) * gain_ref[k]

    def rotated(y):
        rc, rs1, rs2 = rc_ref[...], rs1_ref[...], rs2_ref[...]
        parts = []
        for g in range(COL_TILE // LANES):
            yg = y[:, g * LANES:(g + 1) * LANES]
            up = pltpu.roll(yg, LANES - ROPE_DIMS // 2, 1)
            dn = pltpu.roll(yg, ROPE_DIMS // 2, 1)
            parts.append(yg * rc + up * rs1 + dn * rs2)
        return jnp.concatenate(parts, axis=1)

    for k in range(N_COL_TILES):
        cols = slice(k * COL_TILE, (k + 1) * COL_TILE)
        t = _nt(h, w_ref[cols, :])
        if k in (TILE_QM, TILE_KM):
            t = rotated(head_normed(t, k))
        elif k in (TILE_QF, TILE_KF, TILE_QC):
            t = head_normed(t, k)
        p_ref[:, cols] = t.astype(BF16)
        if k == TILE_KM:
            for r in range(tm // MOBA_BLOCK):
                blk = t[r * MOBA_BLOCK:(r + 1) * MOBA_BLOCK, :]
                kmean_ref[0, pl.ds(r, 1), :] = jnp.mean(blk, axis=0, keepdims=True)


def _proj_call(x2, gmix, w_qkv, wf, bf_row, shift_row, bd, gains, rc, rs1, rs2, seq, tm):
    n_tok = x2.shape[0]
    seq_tiles = seq // tm
    const2 = lambda i: (0, 0)
    const3 = lambda i: (0, 0, 0)
    return pl.pallas_call(
        functools.partial(_proj_kernel, seq_tiles=seq_tiles),
        grid=(n_tok // tm,),
        in_specs=[
            pl.BlockSpec((tm, D_MODEL), lambda i: (i, 0)),
            pl.BlockSpec((1, D_MODEL), const2),
            pl.BlockSpec((PROJ_WIDTH, D_MODEL), const2),
            pl.BlockSpec((LANES, D_MODEL), const2),
            pl.BlockSpec((1, LANES), const2),
            pl.BlockSpec((1, LANES), const2),
            pl.BlockSpec((2, SEG_TILE, SEG_TILE), const3),
            pl.BlockSpec((N_COL_TILES, 1, COL_TILE), const3),
            pl.BlockSpec((tm, LANES), lambda i: (i % seq_tiles, 0)),
            pl.BlockSpec((tm, LANES), lambda i: (i % seq_tiles, 0)),
            pl.BlockSpec((tm, LANES), lambda i: (i % seq_tiles, 0)),
        ],
        out_specs=[
            pl.BlockSpec((tm, PROJ_WIDTH), lambda i: (i, 0)),
            pl.BlockSpec((tm, LANES), lambda i: (i, 0)),
            pl.BlockSpec((1, tm // MOBA_BLOCK, COL_TILE), lambda i: (i, 0, 0)),
        ],
        out_shape=[
            jax.ShapeDtypeStruct((n_tok, PROJ_WIDTH), BF16),
            jax.ShapeDtypeStruct((n_tok, LANES), BF16),
            jax.ShapeDtypeStruct((n_tok // tm, tm // MOBA_BLOCK, COL_TILE), F32),
        ],
        scratch_shapes=[pltpu.VMEM((1, LANES), F32)],
        compiler_params=pltpu.CompilerParams(
            dimension_semantics=("arbitrary",), vmem_limit_bytes=VMEM_LIMIT),
        name="proj",
    )(x2, gmix, w_qkv, wf, bf_row, shift_row, bd, gains, rc, rs1, rs2)


def _memkv_kernel(mem_ref, gmem_ref, w_ref, gk_ref, mk_ref, mv_ref):
    hm = _rms_normed(mem_ref[0], gmem_ref[...])
    kv = _dot(hm, w_ref[...])
    ks = []
    for h in range(MEM_HEADS):
        kh = kv[:, h * MEM_HEAD_DIM:(h + 1) * MEM_HEAD_DIM]
        msk = jnp.mean(kh * kh, axis=-1, keepdims=True)
        ks.append(kh * lax.rsqrt(msk + NORM_EPS) * gk_ref[...])
    mk_ref[0] = jnp.concatenate(ks, axis=1).astype(BF16)
    mv_ref[0] = kv[:, ATT_WIDTH:].astype(BF16)


def _memkv_call(mem, gmem, w_kv, gk):
    batch, mlen, _ = mem.shape
    return pl.pallas_call(
        _memkv_kernel,
        grid=(batch,),
        in_specs=[
            pl.BlockSpec((1, mlen, D_MODEL), lambda b: (b, 0, 0)),
            pl.BlockSpec((1, D_MODEL), lambda b: (0, 0)),
            pl.BlockSpec((D_MODEL, 2 * ATT_WIDTH), lambda b: (0, 0)),
            pl.BlockSpec((1, MEM_HEAD_DIM), lambda b: (0, 0)),
        ],
        out_specs=[
            pl.BlockSpec((1, mlen, ATT_WIDTH), lambda b: (b, 0, 0)),
            pl.BlockSpec((1, mlen, ATT_WIDTH), lambda b: (b, 0, 0)),
        ],
        out_shape=[jax.ShapeDtypeStruct((batch, mlen, ATT_WIDTH), BF16)] * 2,
        compiler_params=pltpu.CompilerParams(dimension_semantics=("parallel",)),
        name="memkv",
    )(mem, gmem, w_kv, gk)


def _lane_iota():
    return lax.broadcasted_iota(jnp.int32, (ATT_TILE, LANES), 1)


def _lane_range(first, count, dtype):
    lane = _lane_iota()
    return jnp.where((lane >= first) & (lane < first + count), 1.0, 0.0).astype(dtype)


def _head_lanes(c, dtype):
    return _lane_range(c * HEAD_DIM, HEAD_DIM, dtype)


def _fill_values(v_ref, vaug_ref):
    n_blocks = v_ref.shape[0] // ATT_TILE

    def fill(j, _):
        rows = pl.ds(pl.multiple_of(j * ATT_TILE, ATT_TILE), ATT_TILE)
        v = v_ref[rows, :].astype(F32)
        for c in range(2):
            own = _head_lanes(c, F32)
            vaug_ref[c, rows, :] = (v * own + (1.0 - own)).astype(BF16)
        return 0

    lax.fori_loop(0, n_blocks, fill, 0)


def _normalised(acc_ref, t):
    res = []
    for c in range(2):
        acc = acc_ref[2 * t + c]
        res.append(acc / pltpu.roll(acc, HEAD_DIM, 1))
    return jnp.where(_lane_iota() < HEAD_DIM, res[0], res[1])


def _causal_masks():
    span = KEY_GROUP * ATT_TILE
    diff = (lax.broadcasted_iota(jnp.int32, (ATT_TILE, span), 1)
            - lax.broadcasted_iota(jnp.int32, (ATT_TILE, span), 0))
    return [diff <= t * ATT_TILE for t in range(Q_SUB)]


def _attend_shifted(lhs_of_tile, kaug_ref, vaug_ref, acc_ref, s_ref, lhs_ref, o_ref, n_q):
    span = KEY_GROUP * ATT_TILE
    n_chain = 2 * Q_SUB

    def group_rows(g):
        return pl.ds(pl.multiple_of(g * span, span), span)

    def scores(g, slot):
        kb = kaug_ref[group_rows(g), :]
        for i in range(n_chain):
            s_ref[slot, i] = _nt(lhs_ref[i], kb)

    def scores_diagonal(g, slot):
        for i in range(n_chain):
            rows = pl.ds(pl.multiple_of(g * span, span), (i // 2 + 1) * ATT_TILE)
            s_ref[slot, i, :, 0:(i // 2 + 1) * ATT_TILE] = _nt(lhs_ref[i], kaug_ref[rows, :])

    def consume(g, slot):
        rows = group_rows(g)
        for i in range(n_chain):
            p = jnp.exp2(s_ref[slot, i]).astype(BF16)
            acc_ref[i] += _dot(p, vaug_ref[i % 2, rows, :])

    def consume_diagonal(g, slot):
        row = lax.broadcasted_iota(jnp.int32, (ATT_TILE, ATT_TILE), 0)
        col = lax.broadcasted_iota(jnp.int32, (ATT_TILE, ATT_TILE), 1)
        for i in range(n_chain):
            t = i // 2
            lo = t * ATT_TILE
            parts = []
            if t:
                parts.append(jnp.exp2(s_ref[slot, i, :, 0:lo]).astype(BF16))
            s_diag = jnp.where(col <= row, s_ref[slot, i, :, lo:lo + ATT_TILE], NEG)
            parts.append(jnp.exp2(s_diag).astype(BF16))
            p = parts[0] if len(parts) == 1 else jnp.concatenate(parts, axis=1)
            rows = pl.ds(pl.multiple_of(g * span, span), lo + ATT_TILE)
            acc_ref[i] += _dot(p, vaug_ref[i % 2, rows, :])

    def emit(qi):
        for t in range(Q_SUB):
            rows = pl.ds(pl.multiple_of((qi * Q_SUB + t) * ATT_TILE, ATT_TILE), ATT_TILE)
            o_ref[rows, :] = _normalised(acc_ref, t).astype(BF16)

    for i, operand in enumerate(lhs_of_tile(0)):
        lhs_ref[i] = operand
    scores_diagonal(0, 0)

    def tile(qi, _):
        acc_ref[...] = jnp.zeros_like(acc_ref)
        n_full = qi

        @pl.when(n_full > 0)
        def _():
            def slot_of(g):
                return jnp.where(g == 0, PRE_SLOT, lax.rem(g, 2))

            def body(g, _):
                consume(g, slot_of(g))
                scores(g + 1, lax.rem(g + 1, 2))
                return 0

            lax.fori_loop(0, n_full - 1, body, 0)
            consume(n_full - 1, slot_of(n_full - 1))
            scores_diagonal(n_full, lax.rem(n_full, 2))

        consume_diagonal(n_full, lax.rem(n_full, 2))
        nxt = lhs_of_tile(jnp.minimum(qi + 1, n_q - 1))
        kb = kaug_ref[group_rows(0), :]
        for i in range(n_chain):
            lhs_ref[i] = nxt[i]
            s_ref[PRE_SLOT, i] = _nt(nxt[i], kb)
        emit(qi)
        return 0

    lax.fori_loop(0, n_q, tile, 0)


def _attend_running_max(lhs_of_tile, kaug_ref, vaug_ref, acc_ref, o_ref, n_q):
    span = KEY_GROUP * ATT_TILE
    n_chain = 2 * Q_SUB
    masks = _causal_masks()

    def tile(qi, _):
        lhs = lhs_of_tile(qi)

        def step(i, g, m, mask):
            rows = pl.ds(pl.multiple_of(g * span, span), span)
            s = _nt(lhs[i], kaug_ref[rows, :])
            if mask is not None:
                s = jnp.where(mask, s, NEG)
            m_new = jnp.maximum(m, jnp.max(s, axis=1, keepdims=True))
            p = jnp.exp2(s - m_new).astype(BF16)
            acc_ref[i] = jnp.exp2(m - m_new) * acc_ref[i] + _dot(p, vaug_ref[i % 2, rows, :])
            return m_new

        acc_ref[...] = jnp.zeros_like(acc_ref)
        m0 = jnp.full((ATT_TILE, 1), -jnp.inf, F32)
        ms = lax.fori_loop(
            0, qi, lambda g, ms: tuple(step(i, g, ms[i], None) for i in range(n_chain)),
            (m0,) * n_chain)
        for i in range(n_chain):
            step(i, qi, ms[i], masks[i // 2])
        for t in range(Q_SUB):
            rows = pl.ds(pl.multiple_of((qi * Q_SUB + t) * ATT_TILE, ATT_TILE), ATT_TILE)
            o_ref[rows, :] = _normalised(acc_ref, t).astype(BF16)
        return 0

    lax.fori_loop(0, n_q, tile, 0)


def _attend(lhs_of_tile, kaug_ref, vaug_ref, acc_ref, s_ref, lhs_ref, o_ref, n_q, bounded):
    if bounded:
        _attend_shifted(lhs_of_tile, kaug_ref, vaug_ref, acc_ref, s_ref, lhs_ref, o_ref, n_q)
    else:
        _attend_running_max(lhs_of_tile, kaug_ref, vaug_ref, acc_ref, o_ref, n_q)


def _attention_scratch(seq):
    n_chain = 2 * Q_SUB
    return [pltpu.VMEM((seq, 2 * LANES), BF16),
            pltpu.VMEM((2, seq, LANES), BF16),
            pltpu.VMEM((n_chain, ATT_TILE, LANES), F32),
            pltpu.VMEM((3, n_chain, ATT_TILE, KEY_GROUP * ATT_TILE), F32),
            pltpu.VMEM((n_chain, ATT_TILE, 2 * LANES), BF16)]


def _moba_kernel(q_ref, k_ref, v_ref, km_ref, shift_ref, o_ref, kaug_ref, vaug_ref,
                 acc_ref, s_ref, lhs_ref, *, bounded):
    seq = k_ref.shape[0]
    n_blocks = seq // MOBA_BLOCK
    lane = _lane_iota()

    kaug_ref[:, 0:LANES] = k_ref[...]

    def fill(j, _):
        rows = pl.ds(pl.multiple_of(j * MOBA_BLOCK, MOBA_BLOCK), MOBA_BLOCK)
        onehot = (lane == j) | (lane == MOBA_SHIFT_LANE)
        kaug_ref[rows, LANES:2 * LANES] = jnp.where(onehot, 1.0, 0.0).astype(BF16)
        return 0

    lax.fori_loop(0, n_blocks, fill, 0)
    _fill_values(v_ref, vaug_ref)

    def lhs_of_tile(tile):
        km = km_ref[0]
        km = jnp.concatenate([km, jnp.zeros((LANES - n_blocks, LANES), F32)], axis=0)
        km_hi = km.astype(BF16)
        km_lo = (km - km_hi.astype(F32)).astype(BF16)
        shift = shift_ref[:, MOBA_SHIFT_LANE:MOBA_SHIFT_LANE + 1] if bounded else 0.0
        blk = lax.broadcasted_iota(jnp.int32, (n_blocks, ATT_TILE), 0)
        blk_f = blk.astype(F32)
        pad_row = lax.broadcasted_iota(jnp.int32, (LANES - n_blocks, ATT_TILE), 0) + n_blocks
        pad_bias = jnp.where(pad_row == MOBA_SHIFT_LANE, -shift, NEG)
        lhs = []
        for t in range(Q_SUB):
            own = tile * Q_SUB + t
            q2 = q_ref[pl.ds(pl.multiple_of(own * ATT_TILE, ATT_TILE), ATT_TILE), :]
            for c in range(2):
                qc = q2 * _head_lanes(c, BF16)
                gate = (_nt(km_hi, qc) + _nt(km_lo, qc))[0:n_blocks]
                g = jnp.where(blk < own, gate, -jnp.inf)
                sel = blk == own
                for _ in range(MOBA_TOPK):
                    mx = jnp.max(g, axis=0, keepdims=True)
                    first = jnp.min(jnp.where(g == mx, blk_f, float(LANES)), axis=0,
                                    keepdims=True)
                    pick = (blk_f == first) & (mx > -jnp.inf)
                    sel = sel | pick
                    g = jnp.where(pick, -jnp.inf, g)
                bias = jnp.concatenate([jnp.where(sel, 0.0, NEG), pad_bias], axis=0)
                lhs.append(jnp.concatenate([qc, bias.T.astype(BF16)], axis=1))
        return lhs

    _attend(lhs_of_tile, kaug_ref, vaug_ref, acc_ref, s_ref, lhs_ref, o_ref,
            seq // (Q_SUB * ATT_TILE), bounded)


def _moba_call(proj, kmean, shift_row, batch, seq, bounded):
    n_pairs = N_HEADS // 2
    q0 = (TILE_QM * COL_TILE) // LANES
    k0 = (TILE_KM * COL_TILE) // LANES
    v0 = k0 + COL_TILE // LANES
    return pl.pallas_call(
        functools.partial(_moba_kernel, bounded=bounded),
        grid=(batch, n_pairs),
        in_specs=[
            pl.BlockSpec((seq, LANES), lambda b, p: (b, q0 + p)),
            pl.BlockSpec((seq, LANES), lambda b, p: (b, k0 + p)),
            pl.BlockSpec((seq, LANES), lambda b, p: (b, v0 + p)),
            pl.BlockSpec((1, seq // MOBA_BLOCK, LANES), lambda b, p: (b, 0, p)),
            pl.BlockSpec((1, LANES), lambda b, p: (0, 0)),
        ],
        out_specs=pl.BlockSpec((seq, LANES), lambda b, p: (b, p)),
        out_shape=jax.ShapeDtypeStruct((batch * seq, ATT_WIDTH), BF16),
        scratch_shapes=_attention_scratch(seq),
        compiler_params=pltpu.CompilerParams(
            dimension_semantics=("parallel", "parallel"), vmem_limit_bytes=VMEM_LIMIT),
        name="moba_bounded" if bounded else "moba_general",
    )(proj, proj, proj, kmean, shift_row)


def _fox_kernel(q_ref, k_ref, v_ref, faug_ref, o_ref, kaug_ref, vaug_ref,
                acc_ref, s_ref, lhs_ref, *, bounded):
    pair = pl.program_id(1)
    n_extra = 3 * N_HEADS
    seq = k_ref.shape[0]

    kaug_ref[:, 0:LANES] = k_ref[...]

    def fill(j, _):
        rows = pl.ds(pl.multiple_of(j * ATT_TILE, ATT_TILE), ATT_TILE)
        lane = _lane_iota()
        ones = (lane >= FOX_Q_LANE0) & (lane < FOX_Q_LANE0 + n_extra)
        f = faug_ref[rows, :].astype(F32)
        kaug_ref[rows, LANES:2 * LANES] = (
            f * (lane < n_extra).astype(F32) + ones.astype(F32)).astype(BF16)
        return 0

    lax.fori_loop(0, seq // ATT_TILE, fill, 0)
    _fill_values(v_ref, vaug_ref)

    def lhs_of_tile(tile):
        lhs = []
        for t in range(Q_SUB):
            rows = pl.ds(pl.multiple_of((tile * Q_SUB + t) * ATT_TILE, ATT_TILE), ATT_TILE)
            q2 = q_ref[rows, :]
            fq = faug_ref[rows, :]
            for c in range(2):
                qc = q2 * _head_lanes(c, BF16)
                k_first = 3 * (2 * pair + c)
                extra = _lane_range(k_first, 3, BF16)
                if bounded:
                    extra = extra + fq * _lane_range(FOX_Q_LANE0 + k_first, 3, BF16)
                lhs.append(jnp.concatenate([qc, extra], axis=1))
        return lhs

    _attend(lhs_of_tile, kaug_ref, vaug_ref, acc_ref, s_ref, lhs_ref, o_ref,
            seq // (Q_SUB * ATT_TILE), bounded)


def _fox_call(proj, faug, batch, seq, bounded):
    n_pairs = N_HEADS // 2
    q0 = (TILE_QF * COL_TILE) // LANES
    k0 = (TILE_KF * COL_TILE) // LANES
    v0 = k0 + COL_TILE // LANES
    return pl.pallas_call(
        functools.partial(_fox_kernel, bounded=bounded),
        grid=(batch, n_pairs),
        in_specs=[
            pl.BlockSpec((seq, LANES), lambda b, p: (b, q0 + p)),
            pl.BlockSpec((seq, LANES), lambda b, p: (b, k0 + p)),
            pl.BlockSpec((seq, LANES), lambda b, p: (b, v0 + p)),
            pl.BlockSpec((seq, LANES), lambda b, p: (b, 0)),
        ],
        out_specs=pl.BlockSpec((seq, LANES), lambda b, p: (b, p)),
        out_shape=jax.ShapeDtypeStruct((batch * seq, ATT_WIDTH), BF16),
        scratch_shapes=_attention_scratch(seq),
        compiler_params=pltpu.CompilerParams(
            dimension_semantics=("parallel", "parallel"), vmem_limit_bytes=VMEM_LIMIT),
        name="fox_bounded" if bounded else "fox_general",
    )(proj, proj, proj, faug)


def _merge_kernel(x_ref, gmix_ref, qc_ref, ym_ref, yf_ref, mk_ref, mv_ref,
                  wg_ref, wbm_ref, wbf_ref, wbc_ref, wo_ref, o_ref):
    x = x_ref[...]
    h = _rms_normed(x, gmix_ref[...])
    qc = qc_ref[...]
    mk = mk_ref[0]
    mv = mv_ref[0]
    ys = []
    for hd in range(MEM_HEADS):
        cols = slice(hd * MEM_HEAD_DIM, (hd + 1) * MEM_HEAD_DIM)
        s = _nt(qc[:, cols], mk[:, cols])
        p = jnp.exp2(s - jnp.max(s, axis=1, keepdims=True))
        y = _dot(p.astype(BF16), mv[:, cols]) / jnp.sum(p, axis=1, keepdims=True)
        ys.append(y)
    yc = jnp.concatenate(ys, axis=1).astype(BF16)
    merged = None
    for i, (y, wb_ref) in enumerate(((ym_ref[...], wbm_ref), (yf_ref[...], wbf_ref),
                                     (yc, wbc_ref))):
        gate = jax.nn.sigmoid(_nt(h, wg_ref[i * D_MODEL:(i + 1) * D_MODEL, :]))
        term = gate * _dot(y, wb_ref[...])
        merged = term if merged is None else merged + term
    o_ref[...] = x + _dot(merged.astype(BF16), wo_ref[...])


def _merge_call(x2, gmix, proj, y_m, y_f, mk, mv, wg, wbm, wbf, wbc, wo, seq, tm):
    n_tok = x2.shape[0]
    mlen = mk.shape[1]
    seq_tiles = seq // tm
    const = lambda i: (0, 0)
    return pl.pallas_call(
        _merge_kernel,
        grid=(n_tok // tm,),
        in_specs=[
            pl.BlockSpec((tm, D_MODEL), lambda i: (i, 0)),
            pl.BlockSpec((1, D_MODEL), const),
            pl.BlockSpec((tm, ATT_WIDTH), lambda i: (i, TILE_QC)),
            pl.BlockSpec((tm, ATT_WIDTH), lambda i: (i, 0)),
            pl.BlockSpec((tm, ATT_WIDTH), lambda i: (i, 0)),
            pl.BlockSpec((1, mlen, ATT_WIDTH), lambda i: (i // seq_tiles, 0, 0)),
            pl.BlockSpec((1, mlen, ATT_WIDTH), lambda i: (i // seq_tiles, 0, 0)),
            pl.BlockSpec((3 * D_MODEL, D_MODEL), const),
            pl.BlockSpec((ATT_WIDTH, D_MODEL), const),
            pl.BlockSpec((ATT_WIDTH, D_MODEL), const),
            pl.BlockSpec((ATT_WIDTH, D_MODEL), const),
            pl.BlockSpec((D_MODEL, D_MODEL), const),
        ],
        out_specs=pl.BlockSpec((tm, D_MODEL), lambda i: (i, 0)),
        out_shape=jax.ShapeDtypeStruct((n_tok, D_MODEL), F32),
        compiler_params=pltpu.CompilerParams(
            dimension_semantics=("parallel",), vmem_limit_bytes=VMEM_LIMIT),
        name="merge",
    )(x2, gmix, proj, y_m, y_f, mk, mv, wg, wbm, wbf, wbc, wo)


def _ffn_kernel(x_ref, g_ref, wg_ref, wu_ref, wd_ref, o_ref):
    x = x_ref[...]
    h = _rms_normed(x, g_ref[...])
    d_ff = wg_ref.shape[1]
    out = x
    for lo in range(0, d_ff, FFN_CHUNK):
        cols = slice(lo, min(lo + FFN_CHUNK, d_ff))
        g = _dot(h, wg_ref[:, cols])
        u = _dot(h, wu_ref[:, cols])
        a = (g * jax.nn.sigmoid(g) * u).astype(BF16)
        out = out + _dot(a, wd_ref[cols, :])
    o_ref[...] = out


def _ffn_call(x1, g_ffn, wg, wu, wd, tm):
    n_tok = x1.shape[0]
    d_ff = wg.shape[1]
    const = lambda i: (0, 0)
    resident = dict(pipeline_mode=pl.Buffered(1))
    return pl.pallas_call(
        _ffn_kernel,
        grid=(n_tok // tm,),
        in_specs=[
            pl.BlockSpec((tm, D_MODEL), lambda i: (i, 0)),
            pl.BlockSpec((1, D_MODEL), const),
            pl.BlockSpec((D_MODEL, d_ff), const, **resident),
            pl.BlockSpec((D_MODEL, d_ff), const, **resident),
            pl.BlockSpec((d_ff, D_MODEL), const, **resident),
        ],
        out_specs=pl.BlockSpec((tm, D_MODEL), lambda i: (i, 0)),
        out_shape=jax.ShapeDtypeStruct((n_tok, D_MODEL), F32),
        compiler_params=pltpu.CompilerParams(
            dimension_semantics=("parallel",), vmem_limit_bytes=VMEM_LIMIT),
        name="ffn",
    )(x1, g_ffn, wg, wu, wd)


@functools.lru_cache(maxsize=None)
def _rope_tables(seq):
    half = ROPE_DIMS // 2
    inv_freq = 1.0 / (ROPE_THETA ** (np.arange(half, dtype=np.float64) * 2.0 / ROPE_DIMS))
    ang = np.arange(seq, dtype=np.float64)[:, None] * inv_freq[None, :]
    cos, sin = np.cos(ang), np.sin(ang)
    d = np.arange(LANES) % HEAD_DIM
    rc = np.where(d[None, :] < ROPE_DIMS, cos[:, d % half], 1.0)
    rs1 = np.where(d[None, :] < half, -sin[:, d % half], 0.0)
    rs2 = np.where((d[None, :] >= half) & (d[None, :] < ROPE_DIMS), sin[:, d % half], 0.0)
    return tuple(np.asarray(t, np.float32) for t in (rc, rs1, rs2))


@functools.lru_cache(maxsize=None)
def _segment_mean_matrices():
    def blockdiag(width):
        return np.kron(np.eye(SEG_TILE // width), np.ones((width, width))) / width
    return np.stack([blockdiag(HEAD_DIM), blockdiag(MEM_HEAD_DIM)]).astype(np.float32)


def _score_bound(g_q, g_k):
    return (BOUND_SLACK * HEAD_DIM ** 0.5 * jnp.max(jnp.abs(g_q)) * jnp.max(jnp.abs(g_k)))


def _lane_row(value, first, count):
    lane = jnp.arange(LANES)
    return jnp.where((lane >= first) & (lane < first + count), value, 0.0).astype(F32)[None, :]


def kernel(x, mem, g_mix, w_in, b_forget, g_q_moba, g_k_moba, g_q_fox, g_k_fox, g_q_mem, g_k_mem,
           g_mem, w_mem_kv, w_br_moba, w_br_fox, w_br_mem, w_out, g_ffn, w_gate, w_up, w_down):
    batch, seq, _ = x.shape
    depth = g_mix.shape[0]
    n_tok = batch * seq
    assert seq % (KEY_GROUP * ATT_TILE) == 0 and seq // MOBA_BLOCK < MOBA_SHIFT_LANE
    n_extra = 3 * N_HEADS

    rc, rs1, rs2 = (jnp.asarray(t) for t in _rope_tables(seq))
    bd = jnp.asarray(_segment_mean_matrices()).astype(BF16)
    ones = jnp.ones((COL_TILE,), F32)

    x2 = x.reshape(n_tok, D_MODEL)
    for layer in range(depth):
        w = w_in[layer]
        gates0 = 7 * ATT_WIDTH + N_HEADS
        qc0 = 6 * ATT_WIDTH + N_HEADS
        w_t = jnp.swapaxes(w, 0, 1)
        w_qkv = jnp.concatenate([w_t[:6 * ATT_WIDTH], w_t[qc0:gates0]], axis=0).astype(BF16)
        w_gates = w_t[gates0:].astype(BF16)
        wf3 = jnp.repeat(w_t[6 * ATT_WIDTH:qc0], 3, axis=0)
        gap = jnp.zeros((FOX_Q_LANE0 - n_extra, D_MODEL), F32)
        tail = jnp.zeros((LANES - FOX_Q_LANE0 - n_extra, D_MODEL), F32)
        wf = jnp.concatenate([wf3, gap, wf3, tail], axis=0).astype(BF16)
        bf3 = jnp.repeat(b_forget[layer], 3)
        bf_rep = jnp.concatenate([bf3, gap[:, 0], bf3, tail[:, 0]])[None, :]

        att_scale = HEAD_DIM ** -0.5 * LOG2E
        gains = [ones] * N_COL_TILES
        gains[TILE_QM] = jnp.tile(g_q_moba[layer], N_HEADS) * att_scale
        gains[TILE_KM] = jnp.tile(g_k_moba[layer], N_HEADS)
        gains[TILE_QF] = jnp.tile(g_q_fox[layer], N_HEADS) * att_scale
        gains[TILE_KF] = jnp.tile(g_k_fox[layer], N_HEADS)
        gains[TILE_QC] = jnp.tile(g_q_mem[layer], MEM_HEADS) * (MEM_HEAD_DIM ** -0.5 * LOG2E)
        gains = jnp.stack(gains)[:, None, :]

        bound_m = _score_bound(g_q_moba[layer], g_k_moba[layer])
        bound_f = _score_bound(g_q_fox[layer], g_k_fox[layer])
        shift_m = _lane_row(bound_m * LOG2E, MOBA_SHIFT_LANE, 1)
        shift_f = _lane_row(bound_f * LOG2E, FOX_Q_LANE0, n_extra)

        proj, faug, kmean = _proj_call(x2, g_mix[layer][None, :], w_qkv, wf, bf_rep, shift_f, bd,
                                       gains, rc, rs1, rs2, seq, tm=PROJ_TILE)
        kmean = kmean.reshape(batch, seq // MOBA_BLOCK, COL_TILE)
        mk, mv = _memkv_call(mem, g_mem[layer][None, :], w_mem_kv[layer].astype(BF16),
                             g_k_mem[layer][None, :])
        def attention(bounded):
            return (_moba_call(proj, kmean, shift_m, batch, seq, bounded),
                    _fox_call(proj, faug, batch, seq, bounded))

        y_m, y_f = lax.cond(2.0 * jnp.maximum(bound_m, bound_f) <= MAX_SHIFT_NATS,
                            lambda: attention(True), lambda: attention(False))
        x1 = _merge_call(x2, g_mix[layer][None, :], proj, y_m, y_f, mk, mv, w_gates,
                         w_br_moba[layer].astype(BF16), w_br_fox[layer].astype(BF16),
                         w_br_mem[layer].astype(BF16), w_out[layer].astype(BF16), seq,
                         tm=TOKEN_TILE)
        x2 = _ffn_call(x1, g_ffn[layer][None, :], w_gate[layer].astype(BF16),
                       w_up[layer].astype(BF16), w_down[layer].astype(BF16), tm=TOKEN_TILE)
    return x2.reshape(batch, seq, D_MODEL)
```

```python
import functools
import math

import numpy as np
import jax
import jax.numpy as jnp
from jax import lax
from jax.experimental import pallas as pl
from jax.experimental.pallas import tpu as pltpu

F32 = jnp.float32
BF16 = jnp.bfloat16

D_MODEL = 1024
HEAD_DIM = 64
N_HEADS = 8
MEM_HEADS = 4
MEM_HEAD_DIM = 128
ATT_WIDTH = 512
MOBA_BLOCK = 256
MOBA_TOPK = 3
ROPE_THETA = 500000.0
ROPE_DIMS = 16
NORM_EPS = 1e-6
NEG = -1e30
LOG2E = math.log2(math.e)

LANES = 128
COL_TILE = 512
SEG_TILE = 256
N_COL_TILES = 7
PROJ_WIDTH = COL_TILE * N_COL_TILES
TILE_QM, TILE_KM, TILE_QF, TILE_KF, TILE_QC = 0, 1, 3, 4, 6
ATT_TILE = 256
KEY_GROUP = 4
Q_SUB = KEY_GROUP
PRE_SLOT = 2
FFN_CHUNK = 1024
PROJ_TILE = 512
TOKEN_TILE = 1024
VMEM_LIMIT = 56 * 1024 * 1024

FOX_Q_LANE0 = 32
MOBA_SHIFT_LANE = LANES - 1
MAX_SHIFT_NATS = 60.0
BOUND_SLACK = 1.02


def _nt(a, b):
    return lax.dot_general(a, b, (((1,), (1,)), ((), ())), preferred_element_type=F32)


def _dot(a, b):
    return jnp.dot(a, b, preferred_element_type=F32)


def _split3(v):
    p1 = v.astype(BF16)
    r1 = v - p1.astype(F32)
    p2 = r1.astype(BF16)
    p3 = (r1 - p2.astype(F32)).astype(BF16)
    return p1, p2, p3


def _rms_normed(x, gain_row):
    ms = jnp.mean(x * x, axis=-1, keepdims=True)
    return (x * lax.rsqrt(ms + NORM_EPS) * gain_row).astype(BF16)


def _forget_lanes(cum, shift_row):
    lane = lax.broadcasted_iota(jnp.int32, cum.shape, 1)
    key_side = lane < 3 * N_HEADS
    query_side = (lane >= FOX_Q_LANE0) & (lane < FOX_Q_LANE0 + 3 * N_HEADS)
    cum2 = cum * LOG2E
    val = jnp.where(key_side, -cum2, cum2 - shift_row)
    c1, c2, c3 = _split3(val)
    piece = jnp.where(key_side, lane, lane - FOX_Q_LANE0) % 3
    sel = jnp.where(piece == 0, c1, jnp.where(piece == 1, c2, c3))
    return jnp.where(key_side | query_side, sel, jnp.zeros_like(sel))


def _proj_kernel(x_ref, gmix_ref, w_ref, wf_ref, bf_ref, shift_ref, bd_ref, gain_ref,
                 rc_ref, rs1_ref, rs2_ref, p_ref, faug_ref, kmean_ref, carry_ref, *, seq_tiles):
    tm = x_ref.shape[0]
    h = _rms_normed(x_ref[...], gmix_ref[...])

    @pl.when(pl.program_id(0) % seq_tiles == 0)
    def _():
        carry_ref[...] = jnp.zeros_like(carry_ref)

    z = _nt(h, wf_ref[...]) + bf_ref[...]
    lf = jnp.minimum(z, 0.0) - jnp.log1p(jnp.exp(-jnp.abs(z)))
    row = lax.broadcasted_iota(jnp.int32, (tm, tm), 0)
    col = lax.broadcasted_iota(jnp.int32, (tm, tm), 1)
    tri = jnp.where(col <= row, 1.0, 0.0).astype(BF16)
    p1, p2, p3 = _split3(lf)
    cum = _dot(tri, p1) + _dot(tri, p2) + _dot(tri, p3) + carry_ref[...]
    carry_ref[...] = cum[tm - 1:tm, :]
    faug_ref[...] = _forget_lanes(cum, shift_ref[...])

    def head_normed(t, k):
        seg = 1 if k == TILE_QC else 0
        sq = (t * t).astype(BF16)
        ms = jnp.concatenate(
            [_dot(sq[:, lo:lo + SEG_TILE], bd_ref[seg]) for lo in range(0, COL_TILE, SEG_TILE)],
            axis=1)
        return t * lax.rsqrt(ms + NORM_EPS) * gain_ref[k]

    def rotated(y):
        rc, rs1, rs2 = rc_ref[...], rs1_ref[...], rs2_ref[...]
        parts = []
        for g in range(COL_TILE // LANES):
            yg = y[:, g * LANES:(g + 1) * LANES]
            up = pltpu.roll(yg, LANES - ROPE_DIMS // 2, 1)
            dn = pltpu.roll(yg, ROPE_DIMS // 2, 1)
            parts.append(yg * rc + up * rs1 + dn * rs2)
        return jnp.concatenate(parts, axis=1)

    for k in range(N_COL_TILES):
        cols = slice(k * COL_TILE, (k + 1) * COL_TILE)
        t = _nt(h, w_ref[cols, :])
        if k in (TILE_QM, TILE_KM):
            t = rotated(head_normed(t, k))
        elif k in (TILE_QF, TILE_KF, TILE_QC):
            t = head_normed(t, k)
        p_ref[:, cols] = t.astype(BF16)
        if k == TILE_KM:
            for r in range(tm // MOBA_BLOCK):
                blk = t[r * MOBA_BLOCK:(r + 1) * MOBA_BLOCK, :]
                kmean_ref[0, pl.ds(r, 1), :] = jnp.mean(blk, axis=0, keepdims=True)


def _proj_call(x2, gmix, w_qkv, wf, bf_row, shift_row, bd, gains, rc, rs1, rs2, seq, tm):
    n_tok = x2.shape[0]
    seq_tiles = seq // tm
    const2 = lambda i: (0, 0)
    const3 = lambda i: (0, 0, 0)
    return pl.pallas_call(
        functools.partial(_proj_kernel, seq_tiles=seq_tiles),
        grid=(n_tok // tm,),
        in_specs=[
            pl.BlockSpec((tm, D_MODEL), lambda i: (i, 0)),
            pl.BlockSpec((1, D_MODEL), const2),
            pl.BlockSpec((PROJ_WIDTH, D_MODEL), const2),
            pl.BlockSpec((LANES, D_MODEL), const2),
            pl.BlockSpec((1, LANES), const2),
            pl.BlockSpec((1, LANES), const2),
            pl.BlockSpec((2, SEG_TILE, SEG_TILE), const3),
            pl.BlockSpec((N_COL_TILES, 1, COL_TILE), const3),
            pl.BlockSpec((tm, LANES), lambda i: (i % seq_tiles, 0)),
            pl.BlockSpec((tm, LANES), lambda i: (i % seq_tiles, 0)),
            pl.BlockSpec((tm, LANES), lambda i: (i % seq_tiles, 0)),
        ],
        out_specs=[
            pl.BlockSpec((tm, PROJ_WIDTH), lambda i: (i, 0)),
            pl.BlockSpec((tm, LANES), lambda i: (i, 0)),
            pl.BlockSpec((1, tm // MOBA_BLOCK, COL_TILE), lambda i: (i, 0, 0)),
        ],
        out_shape=[
            jax.ShapeDtypeStruct((n_tok, PROJ_WIDTH), BF16),
            jax.ShapeDtypeStruct((n_tok, LANES), BF16),
            jax.ShapeDtypeStruct((n_tok // tm, tm // MOBA_BLOCK, COL_TILE), F32),
        ],
        scratch_shapes=[pltpu.VMEM((1, LANES), F32)],
        compiler_params=pltpu.CompilerParams(
            dimension_semantics=("arbitrary",), vmem_limit_bytes=VMEM_LIMIT),
        name="proj",
    )(x2, gmix, w_qkv, wf, bf_row, shift_row, bd, gains, rc, rs1, rs2)


def _memkv_kernel(mem_ref, gmem_ref, w_ref, gk_ref, mk_ref, mv_ref):
    hm = _rms_normed(mem_ref[0], gmem_ref[...])
    kv = _dot(hm, w_ref[...])
    ks = []
    for h in range(MEM_HEADS):
        kh = kv[:, h * MEM_HEAD_DIM:(h + 1) * MEM_HEAD_DIM]
        msk = jnp.mean(kh * kh, axis=-1, keepdims=True)
        ks.append(kh * lax.rsqrt(msk + NORM_EPS) * gk_ref[...])
    mk_ref[0] = jnp.concatenate(ks, axis=1).astype(BF16)
    mv_ref[0] = kv[:, ATT_WIDTH:].astype(BF16)


def _memkv_call(mem, gmem, w_kv, gk):
    batch, mlen, _ = mem.shape
    return pl.pallas_call(
        _memkv_kernel,
        grid=(batch,),
        in_specs=[
            pl.BlockSpec((1, mlen, D_MODEL), lambda b: (b, 0, 0)),
            pl.BlockSpec((1, D_MODEL), lambda b: (0, 0)),
            pl.BlockSpec((D_MODEL, 2 * ATT_WIDTH), lambda b: (0, 0)),
            pl.BlockSpec((1, MEM_HEAD_DIM), lambda b: (0, 0)),
        ],
        out_specs=[
            pl.BlockSpec((1, mlen, ATT_WIDTH), lambda b: (b, 0, 0)),
            pl.BlockSpec((1, mlen, ATT_WIDTH), lambda b: (b, 0, 0)),
        ],
        out_shape=[jax.ShapeDtypeStruct((batch, mlen, ATT_WIDTH), BF16)] * 2,
        compiler_params=pltpu.CompilerParams(dimension_semantics=("parallel",)),
        name="memkv",
    )(mem, gmem, w_kv, gk)


def _lane_iota():
    return lax.broadcasted_iota(jnp.int32, (ATT_TILE, LANES), 1)


def _lane_range(first, count, dtype):
    lane = _lane_iota()
    return jnp.where((lane >= first) & (lane < first + count), 1.0, 0.0).astype(dtype)


def _head_lanes(c, dtype):
    return _lane_range(c * HEAD_DIM, HEAD_DIM, dtype)


def _fill_values(v_ref, vaug_ref):
    n_blocks = v_ref.shape[0] // ATT_TILE

    def fill(j, _):
        rows = pl.ds(pl.multiple_of(j * ATT_TILE, ATT_TILE), ATT_TILE)
        v = v_ref[rows, :].astype(F32)
        for c in range(2):
            own = _head_lanes(c, F32)
            vaug_ref[c, rows, :] = (v * own + (1.0 - own)).astype(BF16)
        return 0

    lax.fori_loop(0, n_blocks, fill, 0)


def _normalised(acc_ref, t):
    res = []
    for c in range(2):
        acc = acc_ref[2 * t + c]
        res.append(acc / pltpu.roll(acc, HEAD_DIM, 1))
    return jnp.where(_lane_iota() < HEAD_DIM, res[0], res[1])


def _causal_masks():
    span = KEY_GROUP * ATT_TILE
    diff = (lax.broadcasted_iota(jnp.int32, (ATT_TILE, span), 1)
            - lax.broadcasted_iota(jnp.int32, (ATT_TILE, span), 0))
    return [diff <= t * ATT_TILE for t in range(Q_SUB)]


def _attend_shifted(lhs_of_tile, kaug_ref, vaug_ref, acc_ref, s_ref, lhs_ref, o_ref, n_q):
    span = KEY_GROUP * ATT_TILE
    n_chain = 2 * Q_SUB

    def group_rows(g):
        return pl.ds(pl.multiple_of(g * span, span), span)

    def scores(g, slot):
        kb = kaug_ref[group_rows(g), :]
        for i in range(n_chain):
            s_ref[slot, i] = _nt(lhs_ref[i], kb)

    def scores_diagonal(g, slot):
        for i in range(n_chain):
            rows = pl.ds(pl.multiple_of(g * span, span), (i // 2 + 1) * ATT_TILE)
            s_ref[slot, i, :, 0:(i // 2 + 1) * ATT_TILE] = _nt(lhs_ref[i], kaug_ref[rows, :])

    def consume(g, slot):
        rows = group_rows(g)
        for i in range(n_chain):
            p = jnp.exp2(s_ref[slot, i]).astype(BF16)
            acc_ref[i] += _dot(p, vaug_ref[i % 2, rows, :])

    def consume_diagonal(g, slot):
        row = lax.broadcasted_iota(jnp.int32, (ATT_TILE, ATT_TILE), 0)
        col = lax.broadcasted_iota(jnp.int32, (ATT_TILE, ATT_TILE), 1)
        for i in range(n_chain):
            t = i // 2
            lo = t * ATT_TILE
            parts = []
            if t:
                parts.append(jnp.exp2(s_ref[slot, i, :, 0:lo]).astype(BF16))
            s_diag = jnp.where(col <= row, s_ref[slot, i, :, lo:lo + ATT_TILE], NEG)
            parts.append(jnp.exp2(s_diag).astype(BF16))
            p = parts[0] if len(parts) == 1 else jnp.concatenate(parts, axis=1)
            rows = pl.ds(pl.multiple_of(g * span, span), lo + ATT_TILE)
            acc_ref[i] += _dot(p, vaug_ref[i % 2, rows, :])

    def emit(qi):
        for t in range(Q_SUB):
            rows = pl.ds(pl.multiple_of((qi * Q_SUB + t) * ATT_TILE, ATT_TILE), ATT_TILE)
            o_ref[rows, :] = _normalised(acc_ref, t).astype(BF16)

    for i, operand in enumerate(lhs_of_tile(0)):
        lhs_ref[i] = operand
    scores_diagonal(0, 0)

    def tile(qi, _):
        acc_ref[...] = jnp.zeros_like(acc_ref)
        n_full = qi

        @pl.when(n_full > 0)
        def _():
            def slot_of(g):
                return jnp.where(g == 0, PRE_SLOT, lax.rem(g, 2))

            def pair(p, _):
                g = 2 * p
                consume(g, slot_of(g))
                scores(g + 1, 1)
                consume(g + 1, 1)
                scores(g + 2, 0)
                return 0

            n_loop = n_full - 1
            lax.fori_loop(0, n_loop // 2, pair, 0)

            @pl.when(lax.rem(n_loop, 2) == 1)
            def _():
                consume(n_loop - 1, slot_of(n_loop - 1))
                scores(n_loop, 1)

            consume(n_full - 1, slot_of(n_full - 1))
            scores_diagonal(n_full, lax.rem(n_full, 2))

        consume_diagonal(n_full, lax.rem(n_full, 2))
        nxt = lhs_of_tile(jnp.minimum(qi + 1, n_q - 1))
        kb = kaug_ref[group_rows(0), :]
        for i in range(n_chain):
            lhs_ref[i] = nxt[i]
            s_ref[PRE_SLOT, i] = _nt(nxt[i], kb)
        emit(qi)
        return 0

    lax.fori_loop(0, n_q, tile, 0)


def _attend_running_max(lhs_of_tile, kaug_ref, vaug_ref, acc_ref, o_ref, n_q):
    span = KEY_GROUP * ATT_TILE
    n_chain = 2 * Q_SUB
    masks = _causal_masks()

    def tile(qi, _):
        lhs = lhs_of_tile(qi)

        def step(i, g, m, mask):
            rows = pl.ds(pl.multiple_of(g * span, span), span)
            s = _nt(lhs[i], kaug_ref[rows, :])
            if mask is not None:
                s = jnp.where(mask, s, NEG)
            m_new = jnp.maximum(m, jnp.max(s, axis=1, keepdims=True))
            p = jnp.exp2(s - m_new).astype(BF16)
            acc_ref[i] = jnp.exp2(m - m_new) * acc_ref[i] + _dot(p, vaug_ref[i % 2, rows, :])
            return m_new

        acc_ref[...] = jnp.zeros_like(acc_ref)
        m0 = jnp.full((ATT_TILE, 1), -jnp.inf, F32)
        ms = lax.fori_loop(
            0, qi, lambda g, ms: tuple(step(i, g, ms[i], None) for i in range(n_chain)),
            (m0,) * n_chain)
        for i in range(n_chain):
            step(i, qi, ms[i], masks[i // 2])
        for t in range(Q_SUB):
            rows = pl.ds(pl.multiple_of((qi * Q_SUB + t) * ATT_TILE, ATT_TILE), ATT_TILE)
            o_ref[rows, :] = _normalised(acc_ref, t).astype(BF16)
        return 0

    lax.fori_loop(0, n_q, tile, 0)


def _attend(lhs_of_tile, kaug_ref, vaug_ref, acc_ref, s_ref, lhs_ref, o_ref, n_q, bounded):
    if bounded:
        _attend_shifted(lhs_of_tile, kaug_ref, vaug_ref, acc_ref, s_ref, lhs_ref, o_ref, n_q)
    else:
        _attend_running_max(lhs_of_tile, kaug_ref, vaug_ref, acc_ref, o_ref, n_q)


def _attention_scratch(seq):
    n_chain = 2 * Q_SUB
    return [pltpu.VMEM((seq, 2 * LANES), BF16),
            pltpu.VMEM((2, seq, LANES), BF16),
            pltpu.VMEM((n_chain, ATT_TILE, LANES), F32),
            pltpu.VMEM((3, n_chain, ATT_TILE, KEY_GROUP * ATT_TILE), F32),
            pltpu.VMEM((n_chain, ATT_TILE, 2 * LANES), BF16)]


def _moba_kernel(q_ref, k_ref, v_ref, km_ref, shift_ref, o_ref, kaug_ref, vaug_ref,
                 acc_ref, s_ref, lhs_ref, *, bounded):
    seq = k_ref.shape[0]
    n_blocks = seq // MOBA_BLOCK
    lane = _lane_iota()

    kaug_ref[:, 0:LANES] = k_ref[...]

    def fill(j, _):
        rows = pl.ds(pl.multiple_of(j * MOBA_BLOCK, MOBA_BLOCK), MOBA_BLOCK)
        onehot = (lane == j) | (lane == MOBA_SHIFT_LANE)
        kaug_ref[rows, LANES:2 * LANES] = jnp.where(onehot, 1.0, 0.0).astype(BF16)
        return 0

    lax.fori_loop(0, n_blocks, fill, 0)
    _fill_values(v_ref, vaug_ref)

    def lhs_of_tile(tile):
        km = km_ref[0]
        km = jnp.concatenate([km, jnp.zeros((LANES - n_blocks, LANES), F32)], axis=0)
        km_hi = km.astype(BF16)
        km_lo = (km - km_hi.astype(F32)).astype(BF16)
        shift = shift_ref[:, MOBA_SHIFT_LANE:MOBA_SHIFT_LANE + 1] if bounded else 0.0
        blk = lax.broadcasted_iota(jnp.int32, (n_blocks, ATT_TILE), 0)
        blk_f = blk.astype(F32)
        pad_row = lax.broadcasted_iota(jnp.int32, (LANES - n_blocks, ATT_TILE), 0) + n_blocks
        pad_bias = jnp.where(pad_row == MOBA_SHIFT_LANE, -shift, NEG)
        lhs = []
        for t in range(Q_SUB):
            own = tile * Q_SUB + t
            q2 = q_ref[pl.ds(pl.multiple_of(own * ATT_TILE, ATT_TILE), ATT_TILE), :]
            for c in range(2):
                qc = q2 * _head_lanes(c, BF16)
                gate = (_nt(km_hi, qc) + _nt(km_lo, qc))[0:n_blocks]
                g = jnp.where(blk < own, gate, -jnp.inf)
                sel = blk == own
                for _ in range(MOBA_TOPK):
                    mx = jnp.max(g, axis=0, keepdims=True)
                    first = jnp.min(jnp.where(g == mx, blk_f, float(LANES)), axis=0,
                                    keepdims=True)
                    pick = (blk_f == first) & (mx > -jnp.inf)
                    sel = sel | pick
                    g = jnp.where(pick, -jnp.inf, g)
                bias = jnp.concatenate([jnp.where(sel, 0.0, NEG), pad_bias], axis=0)
                lhs.append(jnp.concatenate([qc, bias.T.astype(BF16)], axis=1))
        return lhs

    _attend(lhs_of_tile, kaug_ref, vaug_ref, acc_ref, s_ref, lhs_ref, o_ref,
            seq // (Q_SUB * ATT_TILE), bounded)


def _moba_call(proj, kmean, shift_row, batch, seq, bounded):
    n_pairs = N_HEADS // 2
    q0 = (TILE_QM * COL_TILE) // LANES
    k0 = (TILE_KM * COL_TILE) // LANES
    v0 = k0 + COL_TILE // LANES
    return pl.pallas_call(
        functools.partial(_moba_kernel, bounded=bounded),
        grid=(batch, n_pairs),
        in_specs=[
            pl.BlockSpec((seq, LANES), lambda b, p: (b, q0 + p)),
            pl.BlockSpec((seq, LANES), lambda b, p: (b, k0 + p)),
            pl.BlockSpec((seq, LANES), lambda b, p: (b, v0 + p)),
            pl.BlockSpec((1, seq // MOBA_BLOCK, LANES), lambda b, p: (b, 0, p)),
            pl.BlockSpec((1, LANES), lambda b, p: (0, 0)),
        ],
        out_specs=pl.BlockSpec((seq, LANES), lambda b, p: (b, p)),
        out_shape=jax.ShapeDtypeStruct((batch * seq, ATT_WIDTH), BF16),
        scratch_shapes=_attention_scratch(seq),
        compiler_params=pltpu.CompilerParams(
            dimension_semantics=("parallel", "parallel"), vmem_limit_bytes=VMEM_LIMIT),
        name="moba_bounded" if bounded else "moba_general",
    )(proj, proj, proj, kmean, shift_row)


def _fox_kernel(q_ref, k_ref, v_ref, faug_ref, o_ref, kaug_ref, vaug_ref,
                acc_ref, s_ref, lhs_ref, *, bounded):
    pair = pl.program_id(1)
    n_extra = 3 * N_HEADS
    seq = k_ref.shape[0]

    kaug_ref[:, 0:LANES] = k_ref[...]

    def fill(j, _):
        rows = pl.ds(pl.multiple_of(j * ATT_TILE, ATT_TILE), ATT_TILE)
        lane = _lane_iota()
        ones = (lane >= FOX_Q_LANE0) & (lane < FOX_Q_LANE0 + n_extra)
        f = faug_ref[rows, :].astype(F32)
        kaug_ref[rows, LANES:2 * LANES] = (
            f * (lane < n_extra).astype(F32) + ones.astype(F32)).astype(BF16)
        return 0

    lax.fori_loop(0, seq // ATT_TILE, fill, 0)
    _fill_values(v_ref, vaug_ref)

    def lhs_of_tile(tile):
        lhs = []
        for t in range(Q_SUB):
            rows = pl.ds(pl.multiple_of((tile * Q_SUB + t) * ATT_TILE, ATT_TILE), ATT_TILE)
            q2 = q_ref[rows, :]
            fq = faug_ref[rows, :]
            for c in range(2):
                qc = q2 * _head_lanes(c, BF16)
                k_first = 3 * (2 * pair + c)
                extra = _lane_range(k_first, 3, BF16)
                if bounded:
                    extra = extra + fq * _lane_range(FOX_Q_LANE0 + k_first, 3, BF16)
                lhs.append(jnp.concatenate([qc, extra], axis=1))
        return lhs

    _attend(lhs_of_tile, kaug_ref, vaug_ref, acc_ref, s_ref, lhs_ref, o_ref,
            seq // (Q_SUB * ATT_TILE), bounded)


def _fox_call(proj, faug, batch, seq, bounded):
    n_pairs = N_HEADS // 2
    q0 = (TILE_QF * COL_TILE) // LANES
    k0 = (TILE_KF * COL_TILE) // LANES
    v0 = k0 + COL_TILE // LANES
    return pl.pallas_call(
        functools.partial(_fox_kernel, bounded=bounded),
        grid=(batch, n_pairs),
        in_specs=[
            pl.BlockSpec((seq, LANES), lambda b, p: (b, q0 + p)),
            pl.BlockSpec((seq, LANES), lambda b, p: (b, k0 + p)),
            pl.BlockSpec((seq, LANES), lambda b, p: (b, v0 + p)),
            pl.BlockSpec((seq, LANES), lambda b, p: (b, 0)),
        ],
        out_specs=pl.BlockSpec((seq, LANES), lambda b, p: (b, p)),
        out_shape=jax.ShapeDtypeStruct((batch * seq, ATT_WIDTH), BF16),
        scratch_shapes=_attention_scratch(seq),
        compiler_params=pltpu.CompilerParams(
            dimension_semantics=("parallel", "parallel"), vmem_limit_bytes=VMEM_LIMIT),
        name="fox_bounded" if bounded else "fox_general",
    )(proj, proj, proj, faug)


def _merge_kernel(x_ref, gmix_ref, qc_ref, ym_ref, yf_ref, mk_ref, mv_ref,
                  wg_ref, wbm_ref, wbf_ref, wbc_ref, wo_ref, o_ref):
    x = x_ref[...]
    h = _rms_normed(x, gmix_ref[...])
    qc = qc_ref[...]
    mk = mk_ref[0]
    mv = mv_ref[0]
    ys = []
    for hd in range(MEM_HEADS):
        cols = slice(hd * MEM_HEAD_DIM, (hd + 1) * MEM_HEAD_DIM)
        s = _nt(qc[:, cols], mk[:, cols])
        p = jnp.exp2(s - jnp.max(s, axis=1, keepdims=True))
        y = _dot(p.astype(BF16), mv[:, cols]) / jnp.sum(p, axis=1, keepdims=True)
        ys.append(y)
    yc = jnp.concatenate(ys, axis=1).astype(BF16)
    merged = None
    for i, (y, wb_ref) in enumerate(((ym_ref[...], wbm_ref), (yf_ref[...], wbf_ref),
                                     (yc, wbc_ref))):
        gate = jax.nn.sigmoid(_nt(h, wg_ref[i * D_MODEL:(i + 1) * D_MODEL, :]))
        term = gate * _dot(y, wb_ref[...])
        merged = term if merged is None else merged + term
    o_ref[...] = x + _dot(merged.astype(BF16), wo_ref[...])


def _merge_call(x2, gmix, proj, y_m, y_f, mk, mv, wg, wbm, wbf, wbc, wo, seq, tm):
    n_tok = x2.shape[0]
    mlen = mk.shape[1]
    seq_tiles = seq // tm
    const = lambda i: (0, 0)
    return pl.pallas_call(
        _merge_kernel,
        grid=(n_tok // tm,),
        in_specs=[
            pl.BlockSpec((tm, D_MODEL), lambda i: (i, 0)),
            pl.BlockSpec((1, D_MODEL), const),
            pl.BlockSpec((tm, ATT_WIDTH), lambda i: (i, TILE_QC)),
            pl.BlockSpec((tm, ATT_WIDTH), lambda i: (i, 0)),
            pl.BlockSpec((tm, ATT_WIDTH), lambda i: (i, 0)),
            pl.BlockSpec((1, mlen, ATT_WIDTH), lambda i: (i // seq_tiles, 0, 0)),
            pl.BlockSpec((1, mlen, ATT_WIDTH), lambda i: (i // seq_tiles, 0, 0)),
            pl.BlockSpec((3 * D_MODEL, D_MODEL), const),
            pl.BlockSpec((ATT_WIDTH, D_MODEL), const),
            pl.BlockSpec((ATT_WIDTH, D_MODEL), const),
            pl.BlockSpec((ATT_WIDTH, D_MODEL), const),
            pl.BlockSpec((D_MODEL, D_MODEL), const),
        ],
        out_specs=pl.BlockSpec((tm, D_MODEL), lambda i: (i, 0)),
        out_shape=jax.ShapeDtypeStruct((n_tok, D_MODEL), F32),
        compiler_params=pltpu.CompilerParams(
            dimension_semantics=("parallel",), vmem_limit_bytes=VMEM_LIMIT),
        name="merge",
    )(x2, gmix, proj, y_m, y_f, mk, mv, wg, wbm, wbf, wbc, wo)


def _ffn_kernel(x_ref, g_ref, wg_ref, wu_ref, wd_ref, o_ref):
    x = x_ref[...]
    h = _rms_normed(x, g_ref[...])
    d_ff = wg_ref.shape[1]
    out = x
    for lo in range(0, d_ff, FFN_CHUNK):
        cols = slice(lo, min(lo + FFN_CHUNK, d_ff))
        g = _dot(h, wg_ref[:, cols])
        u = _dot(h, wu_ref[:, cols])
        a = (g * jax.nn.sigmoid(g) * u).astype(BF16)
        out = out + _dot(a, wd_ref[cols, :])
    o_ref[...] = out


def _ffn_call(x1, g_ffn, wg, wu, wd, tm):
    n_tok = x1.shape[0]
    d_ff = wg.shape[1]
    const = lambda i: (0, 0)
    resident = dict(pipeline_mode=pl.Buffered(1))
    return pl.pallas_call(
        _ffn_kernel,
        grid=(n_tok // tm,),
        in_specs=[
            pl.BlockSpec((tm, D_MODEL), lambda i: (i, 0)),
            pl.BlockSpec((1, D_MODEL), const),
            pl.BlockSpec((D_MODEL, d_ff), const, **resident),
            pl.BlockSpec((D_MODEL, d_ff), const, **resident),
            pl.BlockSpec((d_ff, D_MODEL), const, **resident),
        ],
        out_specs=pl.BlockSpec((tm, D_MODEL), lambda i: (i, 0)),
        out_shape=jax.ShapeDtypeStruct((n_tok, D_MODEL), F32),
        compiler_params=pltpu.CompilerParams(
            dimension_semantics=("parallel",), vmem_limit_bytes=VMEM_LIMIT),
        name="ffn",
    )(x1, g_ffn, wg, wu, wd)


@functools.lru_cache(maxsize=None)
def _rope_tables(seq):
    half = ROPE_DIMS // 2
    inv_freq = 1.0 / (ROPE_THETA ** (np.arange(half, dtype=np.float64) * 2.0 / ROPE_DIMS))
    ang = np.arange(seq, dtype=np.float64)[:, None] * inv_freq[None, :]
    cos, sin = np.cos(ang), np.sin(ang)
    d = np.arange(LANES) % HEAD_DIM
    rc = np.where(d[None, :] < ROPE_DIMS, cos[:, d % half], 1.0)
    rs1 = np.where(d[None, :] < half, -sin[:, d % half], 0.0)
    rs2 = np.where((d[None, :] >= half) & (d[None, :] < ROPE_DIMS), sin[:, d % half], 0.0)
    return tuple(np.asarray(t, np.float32) for t in (rc, rs1, rs2))


@functools.lru_cache(maxsize=None)
def _segment_mean_matrices():
    def blockdiag(width):
        return np.kron(np.eye(SEG_TILE // width), np.ones((width, width))) / width
    return np.stack([blockdiag(HEAD_DIM), blockdiag(MEM_HEAD_DIM)]).astype(np.float32)


def _score_bound(g_q, g_k):
    return (BOUND_SLACK * HEAD_DIM ** 0.5 * jnp.max(jnp.abs(g_q)) * jnp.max(jnp.abs(g_k)))


def _lane_row(value, first, count):
    lane = jnp.arange(LANES)
    return jnp.where((lane >= first) & (lane < first + count), value, 0.0).astype(F32)[None, :]


def kernel(x, mem, g_mix, w_in, b_forget, g_q_moba, g_k_moba, g_q_fox, g_k_fox, g_q_mem, g_k_mem,
           g_mem, w_mem_kv, w_br_moba, w_br_fox, w_br_mem, w_out, g_ffn, w_gate, w_up, w_down):
    batch, seq, _ = x.shape
    depth = g_mix.shape[0]
    n_tok = batch * seq
    assert seq % (KEY_GROUP * ATT_TILE) == 0 and seq // MOBA_BLOCK < MOBA_SHIFT_LANE
    n_extra = 3 * N_HEADS

    rc, rs1, rs2 = (jnp.asarray(t) for t in _rope_tables(seq))
    bd = jnp.asarray(_segment_mean_matrices()).astype(BF16)
    ones = jnp.ones((COL_TILE,), F32)

    x2 = x.reshape(n_tok, D_MODEL)
    for layer in range(depth):
        w = w_in[layer]
        gates0 = 7 * ATT_WIDTH + N_HEADS
        qc0 = 6 * ATT_WIDTH + N_HEADS
        w_t = jnp.swapaxes(w, 0, 1)
        w_qkv = jnp.concatenate([w_t[:6 * ATT_WIDTH], w_t[qc0:gates0]], axis=0).astype(BF16)
        w_gates = w_t[gates0:].astype(BF16)
        wf3 = jnp.repeat(w_t[6 * ATT_WIDTH:qc0], 3, axis=0)
        gap = jnp.zeros((FOX_Q_LANE0 - n_extra, D_MODEL), F32)
        tail = jnp.zeros((LANES - FOX_Q_LANE0 - n_extra, D_MODEL), F32)
        wf = jnp.concatenate([wf3, gap, wf3, tail], axis=0).astype(BF16)
        bf3 = jnp.repeat(b_forget[layer], 3)
        bf_rep = jnp.concatenate([bf3, gap[:, 0], bf3, tail[:, 0]])[None, :]

        att_scale = HEAD_DIM ** -0.5 * LOG2E
        gains = [ones] * N_COL_TILES
        gains[TILE_QM] = jnp.tile(g_q_moba[layer], N_HEADS) * att_scale
        gains[TILE_KM] = jnp.tile(g_k_moba[layer], N_HEADS)
        gains[TILE_QF] = jnp.tile(g_q_fox[layer], N_HEADS) * att_scale
        gains[TILE_KF] = jnp.tile(g_k_fox[layer], N_HEADS)
        gains[TILE_QC] = jnp.tile(g_q_mem[layer], MEM_HEADS) * (MEM_HEAD_DIM ** -0.5 * LOG2E)
        gains = jnp.stack(gains)[:, None, :]

        bound_m = _score_bound(g_q_moba[layer], g_k_moba[layer])
        bound_f = _score_bound(g_q_fox[layer], g_k_fox[layer])
        shift_m = _lane_row(bound_m * LOG2E, MOBA_SHIFT_LANE, 1)
        shift_f = _lane_row(bound_f * LOG2E, FOX_Q_LANE0, n_extra)

        proj, faug, kmean = _proj_call(x2, g_mix[layer][None, :], w_qkv, wf, bf_rep, shift_f, bd,
                                       gains, rc, rs1, rs2, seq, tm=PROJ_TILE)
        kmean = kmean.reshape(batch, seq // MOBA_BLOCK, COL_TILE)
        mk, mv = _memkv_call(mem, g_mem[layer][None, :], w_mem_kv[layer].astype(BF16),
                             g_k_mem[layer][None, :])
        def attention(bounded):
            return (_moba_call(proj, kmean, shift_m, batch, seq, bounded),
                    _fox_call(proj, faug, batch, seq, bounded))

        y_m, y_f = lax.cond(2.0 * jnp.maximum(bound_m, bound_f) <= MAX_SHIFT_NATS,
                            lambda: attention(True), lambda: attention(False))
        x1 = _merge_call(x2, g_mix[layer][None, :], proj, y_m, y_f, mk, mv, w_gates,
                         w_br_moba[layer].astype(BF16), w_br_fox[layer].astype(BF16),
                         w_br_mem[layer].astype(BF16), w_out[layer].astype(BF16), seq,
                         tm=TOKEN_TILE)
        x2 = _ffn_call(x1, g_ffn[layer][None, :], w_gate[layer].astype(BF16),
                       w_up[layer].astype(BF16), w_down[layer].astype(BF16), tm=TOKEN_TILE)
    return x2.reshape(batch, seq, D_MODEL)
```

```python
import functools
import math

import numpy as np
import jax
import jax.numpy as jnp
from jax import lax
from jax.experimental import pallas as pl
from jax.experimental.pallas import tpu as pltpu

F32 = jnp.float32
BF16 = jnp.bfloat16

D_MODEL = 1024
HEAD_DIM = 64
N_HEADS = 8
MEM_HEADS = 4
MEM_HEAD_DIM = 128
ATT_WIDTH = 512
MOBA_BLOCK = 256
MOBA_TOPK = 3
ROPE_THETA = 500000.0
ROPE_DIMS = 16
NORM_EPS = 1e-6
NEG = -1e30
LOG2E = math.log2(math.e)

LANES = 128
COL_TILE = 512
SEG_TILE = 256
N_COL_TILES = 7
PROJ_WIDTH = COL_TILE * N_COL_TILES
TILE_QM, TILE_KM, TILE_QF, TILE_KF, TILE_QC = 0, 1, 3, 4, 6
ATT_TILE = 256
KEY_GROUP = 4
Q_SUB = KEY_GROUP
PRE_SLOT = 2
FFN_CHUNK = 1024
PROJ_TILE = 512
TOKEN_TILE = 1024
VMEM_LIMIT = 56 * 1024 * 1024

FOX_Q_LANE0 = 32
MOBA_SHIFT_LANE = LANES - 1
MAX_SHIFT_NATS = 60.0
BOUND_SLACK = 1.02


def _nt(a, b):
    return lax.dot_general(a, b, (((1,), (1,)), ((), ())), preferred_element_type=F32)


def _dot(a, b):
    return jnp.dot(a, b, preferred_element_type=F32)


def _split3(v):
    p1 = v.astype(BF16)
    r1 = v - p1.astype(F32)
    p2 = r1.astype(BF16)
    p3 = (r1 - p2.astype(F32)).astype(BF16)
    return p1, p2, p3


def _rms_normed(x, gain_row):
    ms = jnp.mean(x * x, axis=-1, keepdims=True)
    return (x * lax.rsqrt(ms + NORM_EPS) * gain_row).astype(BF16)


def _forget_lanes(cum, shift_row):
    lane = lax.broadcasted_iota(jnp.int32, cum.shape, 1)
    key_side = lane < 3 * N_HEADS
    query_side = (lane >= FOX_Q_LANE0) & (lane < FOX_Q_LANE0 + 3 * N_HEADS)
    cum2 = cum * LOG2E
    val = jnp.where(key_side, -cum2, cum2 - shift_row)
    c1, c2, c3 = _split3(val)
    piece = jnp.where(key_side, lane, lane - FOX_Q_LANE0) % 3
    sel = jnp.where(piece == 0, c1, jnp.where(piece == 1, c2, c3))
    return jnp.where(key_side | query_side, sel, jnp.zeros_like(sel))


def _proj_kernel(x_ref, gmix_ref, w_ref, wc_ref, wf_ref, bf_ref, shift_ref, bd_ref, gain_ref,
                 rc_ref, rs1_ref, rs2_ref, p_ref, faug_ref, kmean_ref, carry_ref, *, seq_tiles):
    tm = x_ref.shape[0]
    h = _rms_normed(x_ref[...], gmix_ref[...])

    @pl.when(pl.program_id(0) % seq_tiles == 0)
    def _():
        carry_ref[...] = jnp.zeros_like(carry_ref)

    z = _nt(h, wf_ref[...]) + bf_ref[...]
    lf = jnp.minimum(z, 0.0) - jnp.log1p(jnp.exp(-jnp.abs(z)))
    row = lax.broadcasted_iota(jnp.int32, (SEG_TILE, SEG_TILE), 0)
    col = lax.broadcasted_iota(jnp.int32, (SEG_TILE, SEG_TILE), 1)
    tri = jnp.where(col <= row, 1.0, 0.0).astype(BF16)
    pieces = _split3(lf)
    carry = carry_ref[...]
    parts = []
    for r in range(0, tm, SEG_TILE):
        part = carry
        for piece in pieces:
            part = part + _dot(tri, piece[r:r + SEG_TILE, :])
        carry = part[SEG_TILE - 1:SEG_TILE, :]
        parts.append(part)
    carry_ref[...] = carry
    faug_ref[...] = _forget_lanes(jnp.concatenate(parts, axis=0), shift_ref[...])

    def head_normed(t, k):
        seg = 1 if k == TILE_QC else 0
        sq = (t * t).astype(BF16)
        ms = jnp.concatenate(
            [_dot(sq[:, lo:lo + SEG_TILE], bd_ref[seg]) for lo in range(0, COL_TILE, SEG_TILE)],
            axis=1)
        return t * lax.rsqrt(ms + NORM_EPS) * gain_ref[k]

    def rotated(y):
        rc, rs1, rs2 = rc_ref[...], rs1_ref[...], rs2_ref[...]
        parts = []
        for g in range(COL_TILE // LANES):
            yg = y[:, g * LANES:(g + 1) * LANES]
            up = pltpu.roll(yg, LANES - ROPE_DIMS // 2, 1)
            dn = pltpu.roll(yg, ROPE_DIMS // 2, 1)
            parts.append(yg * rc + up * rs1 + dn * rs2)
        return jnp.concatenate(parts, axis=1)

    for k in range(N_COL_TILES):
        cols = slice(k * COL_TILE, (k + 1) * COL_TILE)
        t = _nt(h, wc_ref[...] if k == TILE_QC else w_ref[cols, :])
        if k in (TILE_QM, TILE_KM):
            t = rotated(head_normed(t, k))
        elif k in (TILE_QF, TILE_KF, TILE_QC):
            t = head_normed(t, k)
        p_ref[:, cols] = t.astype(BF16)
        if k == TILE_KM:
            for r in range(tm // MOBA_BLOCK):
                blk = t[r * MOBA_BLOCK:(r + 1) * MOBA_BLOCK, :]
                kmean_ref[0, pl.ds(r, 1), :] = jnp.mean(blk, axis=0, keepdims=True)


def _proj_call(x2, gmix, w_qkv, w_qc, wf, bf_row, shift_row, bd, gains, rc, rs1, rs2, seq, tm):
    n_tok = x2.shape[0]
    seq_tiles = seq // tm
    const2 = lambda i: (0, 0)
    const3 = lambda i: (0, 0, 0)
    return pl.pallas_call(
        functools.partial(_proj_kernel, seq_tiles=seq_tiles),
        grid=(n_tok // tm,),
        in_specs=[
            pl.BlockSpec((tm, D_MODEL), lambda i: (i, 0)),
            pl.BlockSpec((1, D_MODEL), const2),
            pl.BlockSpec((PROJ_WIDTH - COL_TILE, D_MODEL), const2),
            pl.BlockSpec((COL_TILE, D_MODEL), const2),
            pl.BlockSpec((LANES, D_MODEL), const2),
            pl.BlockSpec((1, LANES), const2),
            pl.BlockSpec((1, LANES), const2),
            pl.BlockSpec((2, SEG_TILE, SEG_TILE), const3),
            pl.BlockSpec((N_COL_TILES, 1, COL_TILE), const3),
            pl.BlockSpec((tm, LANES), lambda i: (i % seq_tiles, 0)),
            pl.BlockSpec((tm, LANES), lambda i: (i % seq_tiles, 0)),
            pl.BlockSpec((tm, LANES), lambda i: (i % seq_tiles, 0)),
        ],
        out_specs=[
            pl.BlockSpec((tm, PROJ_WIDTH), lambda i: (i, 0)),
            pl.BlockSpec((tm, LANES), lambda i: (i, 0)),
            pl.BlockSpec((1, tm // MOBA_BLOCK, COL_TILE), lambda i: (i, 0, 0)),
        ],
        out_shape=[
            jax.ShapeDtypeStruct((n_tok, PROJ_WIDTH), BF16),
            jax.ShapeDtypeStruct((n_tok, LANES), BF16),
            jax.ShapeDtypeStruct((n_tok // tm, tm // MOBA_BLOCK, COL_TILE), F32),
        ],
        scratch_shapes=[pltpu.VMEM((1, LANES), F32)],
        compiler_params=pltpu.CompilerParams(
            dimension_semantics=("arbitrary",), vmem_limit_bytes=VMEM_LIMIT),
        name="proj",
    )(x2, gmix, w_qkv, w_qc, wf, bf_row, shift_row, bd, gains, rc, rs1, rs2)


def _memkv_kernel(mem_ref, gmem_ref, w_ref, gk_ref, mk_ref, mv_ref):
    hm = _rms_normed(mem_ref[0], gmem_ref[...])
    kv = _dot(hm, w_ref[...])
    ks = []
    for h in range(MEM_HEADS):
        kh = kv[:, h * MEM_HEAD_DIM:(h + 1) * MEM_HEAD_DIM]
        msk = jnp.mean(kh * kh, axis=-1, keepdims=True)
        ks.append(kh * lax.rsqrt(msk + NORM_EPS) * gk_ref[...])
    mk_ref[0] = jnp.concatenate(ks, axis=1).astype(BF16)
    mv_ref[0] = kv[:, ATT_WIDTH:].astype(BF16)


def _memkv_call(mem, gmem, w_kv, gk):
    batch, mlen, _ = mem.shape
    return pl.pallas_call(
        _memkv_kernel,
        grid=(batch,),
        in_specs=[
            pl.BlockSpec((1, mlen, D_MODEL), lambda b: (b, 0, 0)),
            pl.BlockSpec((1, D_MODEL), lambda b: (0, 0)),
            pl.BlockSpec((D_MODEL, 2 * ATT_WIDTH), lambda b: (0, 0)),
            pl.BlockSpec((1, MEM_HEAD_DIM), lambda b: (0, 0)),
        ],
        out_specs=[
            pl.BlockSpec((1, mlen, ATT_WIDTH), lambda b: (b, 0, 0)),
            pl.BlockSpec((1, mlen, ATT_WIDTH), lambda b: (b, 0, 0)),
        ],
        out_shape=[jax.ShapeDtypeStruct((batch, mlen, ATT_WIDTH), BF16)] * 2,
        compiler_params=pltpu.CompilerParams(dimension_semantics=("parallel",)),
        name="memkv",
    )(mem, gmem, w_kv, gk)


def _lane_iota():
    return lax.broadcasted_iota(jnp.int32, (ATT_TILE, LANES), 1)


def _lane_range(first, count, dtype):
    lane = _lane_iota()
    return jnp.where((lane >= first) & (lane < first + count), 1.0, 0.0).astype(dtype)


def _head_lanes(c, dtype):
    return _lane_range(c * HEAD_DIM, HEAD_DIM, dtype)


def _fill_values(v_ref, vaug_ref):
    n_blocks = v_ref.shape[0] // ATT_TILE

    def fill(j, _):
        rows = pl.ds(pl.multiple_of(j * ATT_TILE, ATT_TILE), ATT_TILE)
        v = v_ref[rows, :].astype(F32)
        for c in range(2):
            own = _head_lanes(c, F32)
            vaug_ref[c, rows, :] = (v * own + (1.0 - own)).astype(BF16)
        return 0

    lax.fori_loop(0, n_blocks, fill, 0)


def _normalised(acc_ref, t):
    res = []
    for c in range(2):
        acc = acc_ref[2 * t + c]
        res.append(acc / pltpu.roll(acc, HEAD_DIM, 1))
    return jnp.where(_lane_iota() < HEAD_DIM, res[0], res[1])


def _causal_masks():
    span = KEY_GROUP * ATT_TILE
    diff = (lax.broadcasted_iota(jnp.int32, (ATT_TILE, span), 1)
            - lax.broadcasted_iota(jnp.int32, (ATT_TILE, span), 0))
    return [diff <= t * ATT_TILE for t in range(Q_SUB)]


def _attend_shifted(lhs_of_tile, kaug_ref, vaug_ref, acc_ref, s_ref, lhs_ref, o_ref, n_q):
    span = KEY_GROUP * ATT_TILE
    n_chain = 2 * Q_SUB

    def group_rows(g):
        return pl.ds(pl.multiple_of(g * span, span), span)

    def scores(g, slot):
        kb = kaug_ref[group_rows(g), :]
        for i in range(n_chain):
            s_ref[slot, i] = _nt(lhs_ref[i], kb)

    def scores_diagonal(g, slot):
        for i in range(n_chain):
            rows = pl.ds(pl.multiple_of(g * span, span), (i // 2 + 1) * ATT_TILE)
            s_ref[slot, i, :, 0:(i // 2 + 1) * ATT_TILE] = _nt(lhs_ref[i], kaug_ref[rows, :])

    def consume(g, slot):
        rows = group_rows(g)
        for i in range(n_chain):
            p = jnp.exp2(s_ref[slot, i]).astype(BF16)
            acc_ref[i] += _dot(p, vaug_ref[i % 2, rows, :])

    def consume_diagonal(g, slot):
        row = lax.broadcasted_iota(jnp.int32, (ATT_TILE, ATT_TILE), 0)
        col = lax.broadcasted_iota(jnp.int32, (ATT_TILE, ATT_TILE), 1)
        for i in range(n_chain):
            t = i // 2
            lo = t * ATT_TILE
            parts = []
            if t:
                parts.append(jnp.exp2(s_ref[slot, i, :, 0:lo]).astype(BF16))
            s_diag = jnp.where(col <= row, s_ref[slot, i, :, lo:lo + ATT_TILE], NEG)
            parts.append(jnp.exp2(s_diag).astype(BF16))
            p = parts[0] if len(parts) == 1 else jnp.concatenate(parts, axis=1)
            rows = pl.ds(pl.multiple_of(g * span, span), lo + ATT_TILE)
            acc_ref[i] += _dot(p, vaug_ref[i % 2, rows, :])

    def emit(qi):
        for t in range(Q_SUB):
            rows = pl.ds(pl.multiple_of((qi * Q_SUB + t) * ATT_TILE, ATT_TILE), ATT_TILE)
            o_ref[rows, :] = _normalised(acc_ref, t).astype(BF16)

    for i, operand in enumerate(lhs_of_tile(0)):
        lhs_ref[i] = operand
    scores_diagonal(0, 0)

    def tile(qi, _):
        acc_ref[...] = jnp.zeros_like(acc_ref)
        n_full = qi

        @pl.when(n_full > 0)
        def _():
            def slot_of(g):
                return jnp.where(g == 0, PRE_SLOT, lax.rem(g, 2))

            def pair(p, _):
                g = 2 * p
                consume(g, slot_of(g))
                scores(g + 1, 1)
                consume(g + 1, 1)
                scores(g + 2, 0)
                return 0

            n_loop = n_full - 1
            lax.fori_loop(0, n_loop // 2, pair, 0)

            @pl.when(lax.rem(n_loop, 2) == 1)
            def _():
                consume(n_loop - 1, slot_of(n_loop - 1))
                scores(n_loop, 1)

            consume(n_full - 1, slot_of(n_full - 1))
            scores_diagonal(n_full, lax.rem(n_full, 2))

        consume_diagonal(n_full, lax.rem(n_full, 2))
        nxt = lhs_of_tile(jnp.minimum(qi + 1, n_q - 1))
        kb = kaug_ref[group_rows(0), :]
        for i in range(n_chain):
            lhs_ref[i] = nxt[i]
            s_ref[PRE_SLOT, i] = _nt(nxt[i], kb)
        emit(qi)
        return 0

    lax.fori_loop(0, n_q, tile, 0)


def _attend_running_max(lhs_of_tile, kaug_ref, vaug_ref, acc_ref, o_ref, n_q):
    span = KEY_GROUP * ATT_TILE
    n_chain = 2 * Q_SUB
    masks = _causal_masks()

    def tile(qi, _):
        lhs = lhs_of_tile(qi)

        def step(i, g, m, mask):
            rows = pl.ds(pl.multiple_of(g * span, span), span)
            s = _nt(lhs[i], kaug_ref[rows, :])
            if mask is not None:
                s = jnp.where(mask, s, NEG)
            m_new = jnp.maximum(m, jnp.max(s, axis=1, keepdims=True))
            p = jnp.exp2(s - m_new).astype(BF16)
            acc_ref[i] = jnp.exp2(m - m_new) * acc_ref[i] + _dot(p, vaug_ref[i % 2, rows, :])
            return m_new

        acc_ref[...] = jnp.zeros_like(acc_ref)
        m0 = jnp.full((ATT_TILE, 1), -jnp.inf, F32)
        ms = lax.fori_loop(
            0, qi, lambda g, ms: tuple(step(i, g, ms[i], None) for i in range(n_chain)),
            (m0,) * n_chain)
        for i in range(n_chain):
            step(i, qi, ms[i], masks[i // 2])
        for t in range(Q_SUB):
            rows = pl.ds(pl.multiple_of((qi * Q_SUB + t) * ATT_TILE, ATT_TILE), ATT_TILE)
            o_ref[rows, :] = _normalised(acc_ref, t).astype(BF16)
        return 0

    lax.fori_loop(0, n_q, tile, 0)


def _attend(lhs_of_tile, kaug_ref, vaug_ref, acc_ref, s_ref, lhs_ref, o_ref, n_q, bounded):
    if bounded:
        _attend_shifted(lhs_of_tile, kaug_ref, vaug_ref, acc_ref, s_ref, lhs_ref, o_ref, n_q)
    else:
        _attend_running_max(lhs_of_tile, kaug_ref, vaug_ref, acc_ref, o_ref, n_q)


def _attention_scratch(seq):
    n_chain = 2 * Q_SUB
    return [pltpu.VMEM((seq, 2 * LANES), BF16),
            pltpu.VMEM((2, seq, LANES), BF16),
            pltpu.VMEM((n_chain, ATT_TILE, LANES), F32),
            pltpu.VMEM((3, n_chain, ATT_TILE, KEY_GROUP * ATT_TILE), F32),
            pltpu.VMEM((n_chain, ATT_TILE, 2 * LANES), BF16)]


def _moba_kernel(q_ref, k_ref, v_ref, km_ref, shift_ref, o_ref, kaug_ref, vaug_ref,
                 acc_ref, s_ref, lhs_ref, *, bounded):
    seq = k_ref.shape[0]
    n_blocks = seq // MOBA_BLOCK
    lane = _lane_iota()

    kaug_ref[:, 0:LANES] = k_ref[...]

    def fill(j, _):
        rows = pl.ds(pl.multiple_of(j * MOBA_BLOCK, MOBA_BLOCK), MOBA_BLOCK)
        onehot = (lane == j) | (lane == MOBA_SHIFT_LANE)
        kaug_ref[rows, LANES:2 * LANES] = jnp.where(onehot, 1.0, 0.0).astype(BF16)
        return 0

    lax.fori_loop(0, n_blocks, fill, 0)
    _fill_values(v_ref, vaug_ref)

    def lhs_of_tile(tile):
        km = km_ref[0]
        km = jnp.concatenate([km, jnp.zeros((LANES - n_blocks, LANES), F32)], axis=0)
        km_hi = km.astype(BF16)
        km_lo = (km - km_hi.astype(F32)).astype(BF16)
        shift = shift_ref[:, MOBA_SHIFT_LANE:MOBA_SHIFT_LANE + 1] if bounded else 0.0
        blk = lax.broadcasted_iota(jnp.int32, (n_blocks, ATT_TILE), 0)
        blk_f = blk.astype(F32)
        pad_row = lax.broadcasted_iota(jnp.int32, (LANES - n_blocks, ATT_TILE), 0) + n_blocks
        pad_bias = jnp.where(pad_row == MOBA_SHIFT_LANE, -shift, NEG)
        lhs = []
        for t in range(Q_SUB):
            own = tile * Q_SUB + t
            q2 = q_ref[pl.ds(pl.multiple_of(own * ATT_TILE, ATT_TILE), ATT_TILE), :]
            for c in range(2):
                qc = q2 * _head_lanes(c, BF16)
                gate = (_nt(km_hi, qc) + _nt(km_lo, qc))[0:n_blocks]
                g = jnp.where(blk < own, gate, -jnp.inf)
                sel = blk == own
                for _ in range(MOBA_TOPK):
                    mx = jnp.max(g, axis=0, keepdims=True)
                    first = jnp.min(jnp.where(g == mx, blk_f, float(LANES)), axis=0,
                                    keepdims=True)
                    pick = (blk_f == first) & (mx > -jnp.inf)
                    sel = sel | pick
                    g = jnp.where(pick, -jnp.inf, g)
                bias = jnp.concatenate([jnp.where(sel, 0.0, NEG), pad_bias], axis=0)
                lhs.append(jnp.concatenate([qc, bias.T.astype(BF16)], axis=1))
        return lhs

    _attend(lhs_of_tile, kaug_ref, vaug_ref, acc_ref, s_ref, lhs_ref, o_ref,
            seq // (Q_SUB * ATT_TILE), bounded)


def _moba_call(proj, kmean, shift_row, batch, seq, bounded):
    n_pairs = N_HEADS // 2
    q0 = (TILE_QM * COL_TILE) // LANES
    k0 = (TILE_KM * COL_TILE) // LANES
    v0 = k0 + COL_TILE // LANES
    return pl.pallas_call(
        functools.partial(_moba_kernel, bounded=bounded),
        grid=(batch, n_pairs),
        in_specs=[
            pl.BlockSpec((seq, LANES), lambda b, p: (b, q0 + p)),
            pl.BlockSpec((seq, LANES), lambda b, p: (b, k0 + p)),
            pl.BlockSpec((seq, LANES), lambda b, p: (b, v0 + p)),
            pl.BlockSpec((1, seq // MOBA_BLOCK, LANES), lambda b, p: (b, 0, p)),
            pl.BlockSpec((1, LANES), lambda b, p: (0, 0)),
        ],
        out_specs=pl.BlockSpec((seq, LANES), lambda b, p: (b, p)),
        out_shape=jax.ShapeDtypeStruct((batch * seq, ATT_WIDTH), BF16),
        scratch_shapes=_attention_scratch(seq),
        compiler_params=pltpu.CompilerParams(
            dimension_semantics=("parallel", "parallel"), vmem_limit_bytes=VMEM_LIMIT),
        name="moba_bounded" if bounded else "moba_general",
    )(proj, proj, proj, kmean, shift_row)


def _fox_kernel(q_ref, k_ref, v_ref, faug_ref, o_ref, kaug_ref, vaug_ref,
                acc_ref, s_ref, lhs_ref, *, bounded):
    pair = pl.program_id(1)
    n_extra = 3 * N_HEADS
    seq = k_ref.shape[0]

    kaug_ref[:, 0:LANES] = k_ref[...]

    def fill(j, _):
        rows = pl.ds(pl.multiple_of(j * ATT_TILE, ATT_TILE), ATT_TILE)
        lane = _lane_iota()
        ones = (lane >= FOX_Q_LANE0) & (lane < FOX_Q_LANE0 + n_extra)
        f = faug_ref[rows, :].astype(F32)
        kaug_ref[rows, LANES:2 * LANES] = (
            f * (lane < n_extra).astype(F32) + ones.astype(F32)).astype(BF16)
        return 0

    lax.fori_loop(0, seq // ATT_TILE, fill, 0)
    _fill_values(v_ref, vaug_ref)

    def lhs_of_tile(tile):
        lhs = []
        for t in range(Q_SUB):
            rows = pl.ds(pl.multiple_of((tile * Q_SUB + t) * ATT_TILE, ATT_TILE), ATT_TILE)
            q2 = q_ref[rows, :]
            fq = faug_ref[rows, :]
            for c in range(2):
                qc = q2 * _head_lanes(c, BF16)
                k_first = 3 * (2 * pair + c)
                extra = _lane_range(k_first, 3, BF16)
                if bounded:
                    extra = extra + fq * _lane_range(FOX_Q_LANE0 + k_first, 3, BF16)
                lhs.append(jnp.concatenate([qc, extra], axis=1))
        return lhs

    _attend(lhs_of_tile, kaug_ref, vaug_ref, acc_ref, s_ref, lhs_ref, o_ref,
            seq // (Q_SUB * ATT_TILE), bounded)


def _fox_call(proj, faug, batch, seq, bounded):
    n_pairs = N_HEADS // 2
    q0 = (TILE_QF * COL_TILE) // LANES
    k0 = (TILE_KF * COL_TILE) // LANES
    v0 = k0 + COL_TILE // LANES
    return pl.pallas_call(
        functools.partial(_fox_kernel, bounded=bounded),
        grid=(batch, n_pairs),
        in_specs=[
            pl.BlockSpec((seq, LANES), lambda b, p: (b, q0 + p)),
            pl.BlockSpec((seq, LANES), lambda b, p: (b, k0 + p)),
            pl.BlockSpec((seq, LANES), lambda b, p: (b, v0 + p)),
            pl.BlockSpec((seq, LANES), lambda b, p: (b, 0)),
        ],
        out_specs=pl.BlockSpec((seq, LANES), lambda b, p: (b, p)),
        out_shape=jax.ShapeDtypeStruct((batch * seq, ATT_WIDTH), BF16),
        scratch_shapes=_attention_scratch(seq),
        compiler_params=pltpu.CompilerParams(
            dimension_semantics=("parallel", "parallel"), vmem_limit_bytes=VMEM_LIMIT),
        name="fox_bounded" if bounded else "fox_general",
    )(proj, proj, proj, faug)


def _merge_kernel(x_ref, gmix_ref, qc_ref, ym_ref, yf_ref, mk_ref, mv_ref,
                  wg_ref, wbm_ref, wbf_ref, wbc_ref, wo_ref, o_ref):
    x = x_ref[...]
    h = _rms_normed(x, gmix_ref[...])
    qc = qc_ref[...]
    mk = mk_ref[0]
    mv = mv_ref[0]
    ys = []
    for hd in range(MEM_HEADS):
        cols = slice(hd * MEM_HEAD_DIM, (hd + 1) * MEM_HEAD_DIM)
        s = _nt(qc[:, cols], mk[:, cols])
        p = jnp.exp2(s - jnp.max(s, axis=1, keepdims=True))
        y = _dot(p.astype(BF16), mv[:, cols]) / jnp.sum(p, axis=1, keepdims=True)
        ys.append(y)
    yc = jnp.concatenate(ys, axis=1).astype(BF16)
    merged = None
    for i, (y, wb_ref) in enumerate(((ym_ref[...], wbm_ref), (yf_ref[...], wbf_ref),
                                     (yc, wbc_ref))):
        gate = jax.nn.sigmoid(_nt(h, wg_ref[i * D_MODEL:(i + 1) * D_MODEL, :]))
        term = gate * _dot(y, wb_ref[...])
        merged = term if merged is None else merged + term
    o_ref[...] = x + _dot(merged.astype(BF16), wo_ref[...])


def _merge_call(x2, gmix, proj, y_m, y_f, mk, mv, wg, wbm, wbf, wbc, wo, seq, tm):
    n_tok = x2.shape[0]
    mlen = mk.shape[1]
    seq_tiles = seq // tm
    const = lambda i: (0, 0)
    return pl.pallas_call(
        _merge_kernel,
        grid=(n_tok // tm,),
        in_specs=[
            pl.BlockSpec((tm, D_MODEL), lambda i: (i, 0)),
            pl.BlockSpec((1, D_MODEL), const),
            pl.BlockSpec((tm, ATT_WIDTH), lambda i: (i, TILE_QC)),
            pl.BlockSpec((tm, ATT_WIDTH), lambda i: (i, 0)),
            pl.BlockSpec((tm, ATT_WIDTH), lambda i: (i, 0)),
            pl.BlockSpec((1, mlen, ATT_WIDTH), lambda i: (i // seq_tiles, 0, 0)),
            pl.BlockSpec((1, mlen, ATT_WIDTH), lambda i: (i // seq_tiles, 0, 0)),
            pl.BlockSpec((3 * D_MODEL, D_MODEL), const),
            pl.BlockSpec((ATT_WIDTH, D_MODEL), const),
            pl.BlockSpec((ATT_WIDTH, D_MODEL), const),
            pl.BlockSpec((ATT_WIDTH, D_MODEL), const),
            pl.BlockSpec((D_MODEL, D_MODEL), const),
        ],
        out_specs=pl.BlockSpec((tm, D_MODEL), lambda i: (i, 0)),
        out_shape=jax.ShapeDtypeStruct((n_tok, D_MODEL), F32),
        compiler_params=pltpu.CompilerParams(
            dimension_semantics=("parallel",), vmem_limit_bytes=VMEM_LIMIT),
        name="merge",
    )(x2, gmix, proj, y_m, y_f, mk, mv, wg, wbm, wbf, wbc, wo)


def _ffn_kernel(x_ref, g_ref, wg_ref, wu_ref, wd_ref, o_ref):
    x = x_ref[...]
    h = _rms_normed(x, g_ref[...])
    d_ff = wg_ref.shape[1]
    out = x
    for lo in range(0, d_ff, FFN_CHUNK):
        cols = slice(lo, min(lo + FFN_CHUNK, d_ff))
        g = _dot(h, wg_ref[:, cols])
        u = _dot(h, wu_ref[:, cols])
        a = (g * jax.nn.sigmoid(g) * u).astype(BF16)
        out = out + _dot(a, wd_ref[cols, :])
    o_ref[...] = out


def _ffn_call(x1, g_ffn, wg, wu, wd, tm):
    n_tok = x1.shape[0]
    d_ff = wg.shape[1]
    const = lambda i: (0, 0)
    resident = dict(pipeline_mode=pl.Buffered(1))
    return pl.pallas_call(
        _ffn_kernel,
        grid=(n_tok // tm,),
        in_specs=[
            pl.BlockSpec((tm, D_MODEL), lambda i: (i, 0)),
            pl.BlockSpec((1, D_MODEL), const),
            pl.BlockSpec((D_MODEL, d_ff), const, **resident),
            pl.BlockSpec((D_MODEL, d_ff), const, **resident),
            pl.BlockSpec((d_ff, D_MODEL), const, **resident),
        ],
        out_specs=pl.BlockSpec((tm, D_MODEL), lambda i: (i, 0)),
        out_shape=jax.ShapeDtypeStruct((n_tok, D_MODEL), F32),
        compiler_params=pltpu.CompilerParams(
            dimension_semantics=("parallel",), vmem_limit_bytes=VMEM_LIMIT),
        name="ffn",
    )(x1, g_ffn, wg, wu, wd)


@functools.lru_cache(maxsize=None)
def _rope_tables(seq):
    half = ROPE_DIMS // 2
    inv_freq = 1.0 / (ROPE_THETA ** (np.arange(half, dtype=np.float64) * 2.0 / ROPE_DIMS))
    ang = np.arange(seq, dtype=np.float64)[:, None] * inv_freq[None, :]
    cos, sin = np.cos(ang), np.sin(ang)
    d = np.arange(LANES) % HEAD_DIM
    rc = np.where(d[None, :] < ROPE_DIMS, cos[:, d % half], 1.0)
    rs1 = np.where(d[None, :] < half, -sin[:, d % half], 0.0)
    rs2 = np.where((d[None, :] >= half) & (d[None, :] < ROPE_DIMS), sin[:, d % half], 0.0)
    return tuple(np.asarray(t, np.float32) for t in (rc, rs1, rs2))


@functools.lru_cache(maxsize=None)
def _segment_mean_matrices():
    def blockdiag(width):
        return np.kron(np.eye(SEG_TILE // width), np.ones((width, width))) / width
    return np.stack([blockdiag(HEAD_DIM), blockdiag(MEM_HEAD_DIM)]).astype(np.float32)


def _score_bound(g_q, g_k):
    return (BOUND_SLACK * HEAD_DIM ** 0.5 * jnp.max(jnp.abs(g_q)) * jnp.max(jnp.abs(g_k)))


def _lane_row(value, first, count):
    lane = jnp.arange(LANES)
    return jnp.where((lane >= first) & (lane < first + count), value, 0.0).astype(F32)[None, :]


def kernel(x, mem, g_mix, w_in, b_forget, g_q_moba, g_k_moba, g_q_fox, g_k_fox, g_q_mem, g_k_mem,
           g_mem, w_mem_kv, w_br_moba, w_br_fox, w_br_mem, w_out, g_ffn, w_gate, w_up, w_down):
    batch, seq, _ = x.shape
    depth = g_mix.shape[0]
    n_tok = batch * seq
    assert seq % (KEY_GROUP * ATT_TILE) == 0 and seq // MOBA_BLOCK < MOBA_SHIFT_LANE
    n_extra = 3 * N_HEADS

    rc, rs1, rs2 = (jnp.asarray(t) for t in _rope_tables(seq))
    bd = jnp.asarray(_segment_mean_matrices()).astype(BF16)
    ones = jnp.ones((COL_TILE,), F32)

    x2 = x.reshape(n_tok, D_MODEL)
    for layer in range(depth):
        w = w_in[layer]
        gates0 = 7 * ATT_WIDTH + N_HEADS
        qc0 = 6 * ATT_WIDTH + N_HEADS
        w_t = jnp.swapaxes(w, 0, 1)
        w_qkv = w_t[:6 * ATT_WIDTH].astype(BF16)
        w_qc = w_t[qc0:gates0].astype(BF16)
        w_gates = w_t[gates0:].astype(BF16)
        wf3 = jnp.repeat(w_t[6 * ATT_WIDTH:qc0], 3, axis=0)
        gap = jnp.zeros((FOX_Q_LANE0 - n_extra, D_MODEL), F32)
        tail = jnp.zeros((LANES - FOX_Q_LANE0 - n_extra, D_MODEL), F32)
        wf = jnp.concatenate([wf3, gap, wf3, tail], axis=0).astype(BF16)
        bf3 = jnp.repeat(b_forget[layer], 3)
        bf_rep = jnp.concatenate([bf3, gap[:, 0], bf3, tail[:, 0]])[None, :]

        att_scale = HEAD_DIM ** -0.5 * LOG2E
        gains = [ones] * N_COL_TILES
        gains[TILE_QM] = jnp.tile(g_q_moba[layer], N_HEADS) * att_scale
        gains[TILE_KM] = jnp.tile(g_k_moba[layer], N_HEADS)
        gains[TILE_QF] = jnp.tile(g_q_fox[layer], N_HEADS) * att_scale
        gains[TILE_KF] = jnp.tile(g_k_fox[layer], N_HEADS)
        gains[TILE_QC] = jnp.tile(g_q_mem[layer], MEM_HEADS) * (MEM_HEAD_DIM ** -0.5 * LOG2E)
        gains = jnp.stack(gains)[:, None, :]

        bound_m = _score_bound(g_q_moba[layer], g_k_moba[layer])
        bound_f = _score_bound(g_q_fox[layer], g_k_fox[layer])
        shift_m = _lane_row(bound_m * LOG2E, MOBA_SHIFT_LANE, 1)
        shift_f = _lane_row(bound_f * LOG2E, FOX_Q_LANE0, n_extra)

        proj, faug, kmean = _proj_call(x2, g_mix[layer][None, :], w_qkv, w_qc, wf, bf_rep, shift_f,
                                       bd, gains, rc, rs1, rs2, seq, tm=PROJ_TILE)
        kmean = kmean.reshape(batch, seq // MOBA_BLOCK, COL_TILE)
        mk, mv = _memkv_call(mem, g_mem[layer][None, :], w_mem_kv[layer].astype(BF16),
                             g_k_mem[layer][None, :])
        def attention(bounded):
            return (_moba_call(proj, kmean, shift_m, batch, seq, bounded),
                    _fox_call(proj, faug, batch, seq, bounded))

        y_m, y_f = lax.cond(2.0 * jnp.maximum(bound_m, bound_f) <= MAX_SHIFT_NATS,
                            lambda: attention(True), lambda: attention(False))
        x1 = _merge_call(x2, g_mix[layer][None, :], proj, y_m, y_f, mk, mv, w_gates,
                         w_br_moba[layer].astype(BF16), w_br_fox[layer].astype(BF16),
                         w_br_mem[layer].astype(BF16), w_out[layer].astype(BF16), seq,
                         tm=TOKEN_TILE)
        x2 = _ffn_call(x1, g_ffn[layer][None, :], w_gate[layer].astype(BF16),
                       w_up[layer].astype(BF16), w_down[layer].astype(BF16), tm=TOKEN_TILE)
    return x2.reshape(batch, seq, D_MODEL)
```

```python
import functools
import math

import numpy as np
import jax
import jax.numpy as jnp
from jax import lax
from jax.experimental import pallas as pl
from jax.experimental.pallas import tpu as pltpu

F32 = jnp.float32
BF16 = jnp.bfloat16

D_MODEL = 1024
HEAD_DIM = 64
N_HEADS = 8
MEM_HEADS = 4
MEM_HEAD_DIM = 128
ATT_WIDTH = 512
MOBA_BLOCK = 256
MOBA_TOPK = 3
ROPE_THETA = 500000.0
ROPE_DIMS = 16
NORM_EPS = 1e-6
NEG = -1e30
LOG2E = math.log2(math.e)

LANES = 128
COL_TILE = 512
SEG_TILE = 256
N_COL_TILES = 7
PROJ_WIDTH = COL_TILE * N_COL_TILES
TILE_QM, TILE_KM, TILE_QF, TILE_KF, TILE_QC = 0, 1, 3, 4, 6
ATT_TILE = 256
KEY_GROUP = 4
Q_SUB = KEY_GROUP
DIAG_SLOT = 2
FFN_CHUNK = 1024
PROJ_TILE = 512
TOKEN_TILE = 1024
VMEM_LIMIT = 56 * 1024 * 1024

FOX_Q_LANE0 = 32
MOBA_SHIFT_LANE = LANES - 1
MAX_SHIFT_NATS = 60.0
BOUND_SLACK = 1.02


def _nt(a, b):
    return lax.dot_general(a, b, (((1,), (1,)), ((), ())), preferred_element_type=F32)


def _dot(a, b):
    return jnp.dot(a, b, preferred_element_type=F32)


def _split3(v):
    p1 = v.astype(BF16)
    r1 = v - p1.astype(F32)
    p2 = r1.astype(BF16)
    p3 = (r1 - p2.astype(F32)).astype(BF16)
    return p1, p2, p3


def _rms_normed(x, gain_row):
    ms = jnp.mean(x * x, axis=-1, keepdims=True)
    return (x * lax.rsqrt(ms + NORM_EPS) * gain_row).astype(BF16)


def _forget_lanes(cum, shift_row):
    lane = lax.broadcasted_iota(jnp.int32, cum.shape, 1)
    key_side = lane < 3 * N_HEADS
    query_side = (lane >= FOX_Q_LANE0) & (lane < FOX_Q_LANE0 + 3 * N_HEADS)
    cum2 = cum * LOG2E
    val = jnp.where(key_side, -cum2, cum2 - shift_row)
    c1, c2, c3 = _split3(val)
    piece = jnp.where(key_side, lane, lane - FOX_Q_LANE0) % 3
    sel = jnp.where(piece == 0, c1, jnp.where(piece == 1, c2, c3))
    return jnp.where(key_side | query_side, sel, jnp.zeros_like(sel))


def _proj_kernel(x_ref, gmix_ref, w_ref, wc_ref, wf_ref, bf_ref, shift_ref, bd_ref, gain_ref,
                 rc_ref, rs1_ref, rs2_ref, p_ref, faug_ref, kmean_ref, carry_ref, *, seq_tiles):
    tm = x_ref.shape[0]

    @pl.when(pl.program_id(0) % seq_tiles == 0)
    def _():
        carry_ref[...] = jnp.zeros_like(carry_ref)

    h = _rms_normed(x_ref[...], gmix_ref[...])

    z = _nt(h, wf_ref[...]) + bf_ref[...]
    lf = jnp.minimum(z, 0.0) - jnp.log1p(jnp.exp(-jnp.abs(z)))
    row = lax.broadcasted_iota(jnp.int32, (SEG_TILE, SEG_TILE), 0)
    col = lax.broadcasted_iota(jnp.int32, (SEG_TILE, SEG_TILE), 1)
    tri = jnp.where(col <= row, 1.0, 0.0).astype(BF16)
    pieces = _split3(lf)
    carry = carry_ref[...]
    parts = []
    for r in range(0, tm, SEG_TILE):
        part = carry
        for piece in pieces:
            part = part + _dot(tri, piece[r:r + SEG_TILE, :])
        carry = part[SEG_TILE - 1:SEG_TILE, :]
        parts.append(part)
    carry_ref[...] = carry
    faug_ref[...] = _forget_lanes(jnp.concatenate(parts, axis=0), shift_ref[...])

    def head_normed(t, k):
        seg = 1 if k == TILE_QC else 0
        sq = (t * t).astype(BF16)
        ms = jnp.concatenate(
            [_dot(sq[:, lo:lo + SEG_TILE], bd_ref[seg]) for lo in range(0, COL_TILE, SEG_TILE)],
            axis=1)
        return t * lax.rsqrt(ms + NORM_EPS) * gain_ref[k]

    def rotated(y):
        rc, rs1, rs2 = rc_ref[...], rs1_ref[...], rs2_ref[...]
        parts = []
        for g in range(COL_TILE // LANES):
            yg = y[:, g * LANES:(g + 1) * LANES]
            up = pltpu.roll(yg, LANES - ROPE_DIMS // 2, 1)
            dn = pltpu.roll(yg, ROPE_DIMS // 2, 1)
            parts.append(yg * rc + up * rs1 + dn * rs2)
        return jnp.concatenate(parts, axis=1)

    for k in range(N_COL_TILES):
        cols = slice(k * COL_TILE, (k + 1) * COL_TILE)
        t = _nt(h, wc_ref[...] if k == TILE_QC else w_ref[cols, :])
        if k in (TILE_QM, TILE_KM):
            t = rotated(head_normed(t, k))
        elif k in (TILE_QF, TILE_KF, TILE_QC):
            t = head_normed(t, k)
        p_ref[:, cols] = t.astype(BF16)
        if k == TILE_KM:
            for r in range(tm // MOBA_BLOCK):
                blk = t[r * MOBA_BLOCK:(r + 1) * MOBA_BLOCK, :]
                kmean_ref[0, pl.ds(r, 1), :] = jnp.mean(blk, axis=0, keepdims=True)


def _proj_call(x2, gmix, w_qkv, w_qc, wf, bf_row, shift_row, bd, gains, rc, rs1, rs2, seq, tm):
    n_tok = x2.shape[0]
    seq_tiles = seq // tm
    const2 = lambda i: (0, 0)
    const3 = lambda i: (0, 0, 0)
    return pl.pallas_call(
        functools.partial(_proj_kernel, seq_tiles=seq_tiles),
        grid=(n_tok // tm,),
        in_specs=[
            pl.BlockSpec((tm, D_MODEL), lambda i: (i, 0)),
            pl.BlockSpec((1, D_MODEL), const2),
            pl.BlockSpec((PROJ_WIDTH - COL_TILE, D_MODEL), const2),
            pl.BlockSpec((COL_TILE, D_MODEL), const2),
            pl.BlockSpec((LANES, D_MODEL), const2),
            pl.BlockSpec((1, LANES), const2),
            pl.BlockSpec((1, LANES), const2),
            pl.BlockSpec((2, SEG_TILE, SEG_TILE), const3),
            pl.BlockSpec((N_COL_TILES, 1, COL_TILE), const3),
            pl.BlockSpec((tm, LANES), lambda i: (i % seq_tiles, 0)),
            pl.BlockSpec((tm, LANES), lambda i: (i % seq_tiles, 0)),
            pl.BlockSpec((tm, LANES), lambda i: (i % seq_tiles, 0)),
        ],
        out_specs=[
            pl.BlockSpec((tm, PROJ_WIDTH), lambda i: (i, 0)),
            pl.BlockSpec((tm, LANES), lambda i: (i, 0)),
            pl.BlockSpec((1, tm // MOBA_BLOCK, COL_TILE), lambda i: (i, 0, 0)),
        ],
        out_shape=[
            jax.ShapeDtypeStruct((n_tok, PROJ_WIDTH), BF16),
            jax.ShapeDtypeStruct((n_tok, LANES), BF16),
            jax.ShapeDtypeStruct((n_tok // tm, tm // MOBA_BLOCK, COL_TILE), F32),
        ],
        scratch_shapes=[pltpu.VMEM((1, LANES), F32)],
        compiler_params=pltpu.CompilerParams(
            dimension_semantics=("arbitrary",), vmem_limit_bytes=VMEM_LIMIT),
        name="proj",
    )(x2, gmix, w_qkv, w_qc, wf, bf_row, shift_row, bd, gains, rc, rs1, rs2)


def _memkv_kernel(mem_ref, gmem_ref, w_ref, gk_ref, mk_ref, mv_ref):
    hm = _rms_normed(mem_ref[0], gmem_ref[...])
    kv = _dot(hm, w_ref[...])
    ks = []
    for h in range(MEM_HEADS):
        kh = kv[:, h * MEM_HEAD_DIM:(h + 1) * MEM_HEAD_DIM]
        msk = jnp.mean(kh * kh, axis=-1, keepdims=True)
        ks.append(kh * lax.rsqrt(msk + NORM_EPS) * gk_ref[...])
    mk_ref[0] = jnp.concatenate(ks, axis=1).astype(BF16)
    mv_ref[0] = kv[:, ATT_WIDTH:].astype(BF16)


def _memkv_call(mem, gmem, w_kv, gk):
    batch, mlen, _ = mem.shape
    return pl.pallas_call(
        _memkv_kernel,
        grid=(batch,),
        in_specs=[
            pl.BlockSpec((1, mlen, D_MODEL), lambda b: (b, 0, 0)),
            pl.BlockSpec((1, D_MODEL), lambda b: (0, 0)),
            pl.BlockSpec((D_MODEL, 2 * ATT_WIDTH), lambda b: (0, 0)),
            pl.BlockSpec((1, MEM_HEAD_DIM), lambda b: (0, 0)),
        ],
        out_specs=[
            pl.BlockSpec((1, mlen, ATT_WIDTH), lambda b: (b, 0, 0)),
            pl.BlockSpec((1, mlen, ATT_WIDTH), lambda b: (b, 0, 0)),
        ],
        out_shape=[jax.ShapeDtypeStruct((batch, mlen, ATT_WIDTH), BF16)] * 2,
        compiler_params=pltpu.CompilerParams(dimension_semantics=("parallel",)),
        name="memkv",
    )(mem, gmem, w_kv, gk)


def _lane_iota():
    return lax.broadcasted_iota(jnp.int32, (ATT_TILE, LANES), 1)


def _lane_range(first, count, dtype):
    lane = _lane_iota()
    return jnp.where((lane >= first) & (lane < first + count), 1.0, 0.0).astype(dtype)


def _head_lanes(c, dtype):
    return _lane_range(c * HEAD_DIM, HEAD_DIM, dtype)


def _fill_values(v_ref, vaug_ref):
    n_blocks = v_ref.shape[0] // ATT_TILE

    def fill(j, _):
        rows = pl.ds(pl.multiple_of(j * ATT_TILE, ATT_TILE), ATT_TILE)
        v = v_ref[rows, :].astype(F32)
        for c in range(2):
            own = _head_lanes(c, F32)
            vaug_ref[c, rows, :] = (v * own + (1.0 - own)).astype(BF16)
        return 0

    lax.fori_loop(0, n_blocks, fill, 0)


def _normalised(acc_ref, t):
    res = []
    for c in range(2):
        acc = acc_ref[2 * t + c]
        res.append(acc / pltpu.roll(acc, HEAD_DIM, 1))
    return jnp.where(_lane_iota() < HEAD_DIM, res[0], res[1])


def _causal_masks():
    span = KEY_GROUP * ATT_TILE
    diff = (lax.broadcasted_iota(jnp.int32, (ATT_TILE, span), 1)
            - lax.broadcasted_iota(jnp.int32, (ATT_TILE, span), 0))
    return [diff <= t * ATT_TILE for t in range(Q_SUB)]


def _attend_shifted(lhs_of_tile, kaug_ref, vaug_ref, acc_ref, s_ref, lhs_ref, o_ref, n_q):
    span = KEY_GROUP * ATT_TILE
    n_chain = 2 * Q_SUB

    def group_rows(g):
        return pl.ds(pl.multiple_of(g * span, span), span)

    def scores(g, slot):
        kb = kaug_ref[group_rows(g), :]
        for i in range(n_chain):
            s_ref[slot, i] = _nt(lhs_ref[i], kb)

    def scores_diagonal(g, slot):
        for i in range(n_chain):
            rows = pl.ds(pl.multiple_of(g * span, span), (i // 2 + 1) * ATT_TILE)
            s_ref[slot, i, :, 0:(i // 2 + 1) * ATT_TILE] = _nt(lhs_ref[i], kaug_ref[rows, :])

    def consume(g, slot):
        rows = group_rows(g)
        for i in range(n_chain):
            p = jnp.exp2(s_ref[slot, i]).astype(BF16)
            acc_ref[i] += _dot(p, vaug_ref[i % 2, rows, :])

    def consume_diagonal(g, slot):
        row = lax.broadcasted_iota(jnp.int32, (ATT_TILE, ATT_TILE), 0)
        col = lax.broadcasted_iota(jnp.int32, (ATT_TILE, ATT_TILE), 1)
        for i in range(n_chain):
            t = i // 2
            lo = t * ATT_TILE
            parts = []
            if t:
                parts.append(jnp.exp2(s_ref[slot, i, :, 0:lo]).astype(BF16))
            s_diag = jnp.where(col <= row, s_ref[slot, i, :, lo:lo + ATT_TILE], NEG)
            parts.append(jnp.exp2(s_diag).astype(BF16))
            p = parts[0] if len(parts) == 1 else jnp.concatenate(parts, axis=1)
            rows = pl.ds(pl.multiple_of(g * span, span), lo + ATT_TILE)
            acc_ref[i] += _dot(p, vaug_ref[i % 2, rows, :])

    def emit(qi):
        for t in range(Q_SUB):
            rows = pl.ds(pl.multiple_of((qi * Q_SUB + t) * ATT_TILE, ATT_TILE), ATT_TILE)
            o_ref[rows, :] = _normalised(acc_ref, t).astype(BF16)

    for i, operand in enumerate(lhs_of_tile(0)):
        lhs_ref[i] = operand
    scores_diagonal(0, DIAG_SLOT)

    def tile(qi, _):
        acc_ref[...] = jnp.zeros_like(acc_ref)
        n_full = qi

        @pl.when(n_full > 0)
        def _():
            def pair(p, _):
                g = 2 * p
                consume(g, 0)
                scores(g + 1, 1)
                consume(g + 1, 1)
                scores(g + 2, 0)
                return 0

            n_loop = n_full - 1
            lax.fori_loop(0, n_loop // 2, pair, 0)

            @pl.when(lax.rem(n_loop, 2) == 1)
            def _():
                consume(n_loop - 1, 0)
                scores(n_loop, 1)

            consume(n_full - 1, lax.rem(n_full - 1, 2))
            scores_diagonal(n_full, DIAG_SLOT)

        consume_diagonal(n_full, DIAG_SLOT)
        nxt = lhs_of_tile(jnp.minimum(qi + 1, n_q - 1))
        kb = kaug_ref[group_rows(0), :]
        for i in range(n_chain):
            lhs_ref[i] = nxt[i]
            s_ref[0, i] = _nt(nxt[i], kb)
        emit(qi)
        return 0

    lax.fori_loop(0, n_q, tile, 0)


def _attend_running_max(lhs_of_tile, kaug_ref, vaug_ref, acc_ref, o_ref, n_q):
    span = KEY_GROUP * ATT_TILE
    n_chain = 2 * Q_SUB
    masks = _causal_masks()

    def tile(qi, _):
        lhs = lhs_of_tile(qi)

        def step(i, g, m, mask):
            rows = pl.ds(pl.multiple_of(g * span, span), span)
            s = _nt(lhs[i], kaug_ref[rows, :])
            if mask is not None:
                s = jnp.where(mask, s, NEG)
            m_new = jnp.maximum(m, jnp.max(s, axis=1, keepdims=True))
            p = jnp.exp2(s - m_new).astype(BF16)
            acc_ref[i] = jnp.exp2(m - m_new) * acc_ref[i] + _dot(p, vaug_ref[i % 2, rows, :])
            return m_new

        acc_ref[...] = jnp.zeros_like(acc_ref)
        m0 = jnp.full((ATT_TILE, 1), -jnp.inf, F32)
        ms = lax.fori_loop(
            0, qi, lambda g, ms: tuple(step(i, g, ms[i], None) for i in range(n_chain)),
            (m0,) * n_chain)
        for i in range(n_chain):
            step(i, qi, ms[i], masks[i // 2])
        for t in range(Q_SUB):
            rows = pl.ds(pl.multiple_of((qi * Q_SUB + t) * ATT_TILE, ATT_TILE), ATT_TILE)
            o_ref[rows, :] = _normalised(acc_ref, t).astype(BF16)
        return 0

    lax.fori_loop(0, n_q, tile, 0)


def _attend(lhs_of_tile, kaug_ref, vaug_ref, acc_ref, s_ref, lhs_ref, o_ref, n_q, bounded):
    if bounded:
        _attend_shifted(lhs_of_tile, kaug_ref, vaug_ref, acc_ref, s_ref, lhs_ref, o_ref, n_q)
    else:
        _attend_running_max(lhs_of_tile, kaug_ref, vaug_ref, acc_ref, o_ref, n_q)


def _attention_scratch(seq):
    n_chain = 2 * Q_SUB
    return [pltpu.VMEM((seq, 2 * LANES), BF16),
            pltpu.VMEM((2, seq, LANES), BF16),
            pltpu.VMEM((n_chain, ATT_TILE, LANES), F32),
            pltpu.VMEM((3, n_chain, ATT_TILE, KEY_GROUP * ATT_TILE), F32),
            pltpu.VMEM((n_chain, ATT_TILE, 2 * LANES), BF16)]


def _moba_kernel(q_ref, k_ref, v_ref, km_ref, shift_ref, o_ref, kaug_ref, vaug_ref,
                 acc_ref, s_ref, lhs_ref, *, bounded):
    seq = k_ref.shape[0]
    n_blocks = seq // MOBA_BLOCK
    lane = _lane_iota()

    kaug_ref[:, 0:LANES] = k_ref[...]

    def fill(j, _):
        rows = pl.ds(pl.multiple_of(j * MOBA_BLOCK, MOBA_BLOCK), MOBA_BLOCK)
        onehot = (lane == j) | (lane == MOBA_SHIFT_LANE)
        kaug_ref[rows, LANES:2 * LANES] = jnp.where(onehot, 1.0, 0.0).astype(BF16)
        return 0

    lax.fori_loop(0, n_blocks, fill, 0)
    _fill_values(v_ref, vaug_ref)

    def lhs_of_tile(tile):
        km = km_ref[0]
        km = jnp.concatenate([km, jnp.zeros((LANES - n_blocks, LANES), F32)], axis=0)
        km_hi = km.astype(BF16)
        km_lo = (km - km_hi.astype(F32)).astype(BF16)
        shift = shift_ref[:, MOBA_SHIFT_LANE:MOBA_SHIFT_LANE + 1] if bounded else 0.0
        blk = lax.broadcasted_iota(jnp.int32, (n_blocks, ATT_TILE), 0)
        blk_f = blk.astype(F32)
        pad_row = lax.broadcasted_iota(jnp.int32, (LANES - n_blocks, ATT_TILE), 0) + n_blocks
        pad_bias = jnp.where(pad_row == MOBA_SHIFT_LANE, -shift, NEG)
        lhs = []
        for t in range(Q_SUB):
            own = tile * Q_SUB + t
            q2 = q_ref[pl.ds(pl.multiple_of(own * ATT_TILE, ATT_TILE), ATT_TILE), :]
            for c in range(2):
                qc = q2 * _head_lanes(c, BF16)
                gate = (_nt(km_hi, qc) + _nt(km_lo, qc))[0:n_blocks]
                g = jnp.where(blk < own, gate, -jnp.inf)
                sel = blk == own
                for _ in range(MOBA_TOPK):
                    mx = jnp.max(g, axis=0, keepdims=True)
                    first = jnp.min(jnp.where(g == mx, blk_f, float(LANES)), axis=0,
                                    keepdims=True)
                    pick = (blk_f == first) & (mx > -jnp.inf)
                    sel = sel | pick
                    g = jnp.where(pick, -jnp.inf, g)
                bias = jnp.concatenate([jnp.where(sel, 0.0, NEG), pad_bias], axis=0)
                lhs.append(jnp.concatenate([qc, bias.T.astype(BF16)], axis=1))
        return lhs

    _attend(lhs_of_tile, kaug_ref, vaug_ref, acc_ref, s_ref, lhs_ref, o_ref,
            seq // (Q_SUB * ATT_TILE), bounded)


def _moba_call(proj, kmean, shift_row, batch, seq, bounded):
    n_pairs = N_HEADS // 2
    q0 = (TILE_QM * COL_TILE) // LANES
    k0 = (TILE_KM * COL_TILE) // LANES
    v0 = k0 + COL_TILE // LANES
    return pl.pallas_call(
        functools.partial(_moba_kernel, bounded=bounded),
        grid=(batch, n_pairs),
        in_specs=[
            pl.BlockSpec((seq, LANES), lambda b, p: (b, q0 + p)),
            pl.BlockSpec((seq, LANES), lambda b, p: (b, k0 + p)),
            pl.BlockSpec((seq, LANES), lambda b, p: (b, v0 + p)),
            pl.BlockSpec((1, seq // MOBA_BLOCK, LANES), lambda b, p: (b, 0, p)),
            pl.BlockSpec((1, LANES), lambda b, p: (0, 0)),
        ],
        out_specs=pl.BlockSpec((seq, LANES), lambda b, p: (b, p)),
        out_shape=jax.ShapeDtypeStruct((batch * seq, ATT_WIDTH), BF16),
        scratch_shapes=_attention_scratch(seq),
        compiler_params=pltpu.CompilerParams(
            dimension_semantics=("parallel", "parallel"), vmem_limit_bytes=VMEM_LIMIT),
        name="moba_bounded" if bounded else "moba_general",
    )(proj, proj, proj, kmean, shift_row)


def _fox_kernel(q_ref, k_ref, v_ref, faug_ref, o_ref, kaug_ref, vaug_ref,
                acc_ref, s_ref, lhs_ref, *, bounded):
    pair = pl.program_id(1)
    n_extra = 3 * N_HEADS
    seq = k_ref.shape[0]

    kaug_ref[:, 0:LANES] = k_ref[...]

    def fill(j, _):
        rows = pl.ds(pl.multiple_of(j * ATT_TILE, ATT_TILE), ATT_TILE)
        lane = _lane_iota()
        ones = (lane >= FOX_Q_LANE0) & (lane < FOX_Q_LANE0 + n_extra)
        f = faug_ref[rows, :].astype(F32)
        kaug_ref[rows, LANES:2 * LANES] = (
            f * (lane < n_extra).astype(F32) + ones.astype(F32)).astype(BF16)
        return 0

    lax.fori_loop(0, seq // ATT_TILE, fill, 0)
    _fill_values(v_ref, vaug_ref)

    def lhs_of_tile(tile):
        lhs = []
        for t in range(Q_SUB):
            rows = pl.ds(pl.multiple_of((tile * Q_SUB + t) * ATT_TILE, ATT_TILE), ATT_TILE)
            q2 = q_ref[rows, :]
            fq = faug_ref[rows, :]
            for c in range(2):
                qc = q2 * _head_lanes(c, BF16)
                k_first = 3 * (2 * pair + c)
                extra = _lane_range(k_first, 3, BF16)
                if bounded:
                    extra = extra + fq * _lane_range(FOX_Q_LANE0 + k_first, 3, BF16)
                lhs.append(jnp.concatenate([qc, extra], axis=1))
        return lhs

    _attend(lhs_of_tile, kaug_ref, vaug_ref, acc_ref, s_ref, lhs_ref, o_ref,
            seq // (Q_SUB * ATT_TILE), bounded)


def _fox_call(proj, faug, batch, seq, bounded):
    n_pairs = N_HEADS // 2
    q0 = (TILE_QF * COL_TILE) // LANES
    k0 = (TILE_KF * COL_TILE) // LANES
    v0 = k0 + COL_TILE // LANES
    return pl.pallas_call(
        functools.partial(_fox_kernel, bounded=bounded),
        grid=(batch, n_pairs),
        in_specs=[
            pl.BlockSpec((seq, LANES), lambda b, p: (b, q0 + p)),
            pl.BlockSpec((seq, LANES), lambda b, p: (b, k0 + p)),
            pl.BlockSpec((seq, LANES), lambda b, p: (b, v0 + p)),
            pl.BlockSpec((seq, LANES), lambda b, p: (b, 0)),
        ],
        out_specs=pl.BlockSpec((seq, LANES), lambda b, p: (b, p)),
        out_shape=jax.ShapeDtypeStruct((batch * seq, ATT_WIDTH), BF16),
        scratch_shapes=_attention_scratch(seq),
        compiler_params=pltpu.CompilerParams(
            dimension_semantics=("parallel", "parallel"), vmem_limit_bytes=VMEM_LIMIT),
        name="fox_bounded" if bounded else "fox_general",
    )(proj, proj, proj, faug)


def _merge_kernel(x_ref, gmix_ref, qc_ref, ym_ref, yf_ref, mk_ref, mv_ref,
                  wg_ref, wbm_ref, wbf_ref, wbc_ref, wo_ref, o_ref):
    x = x_ref[...]
    h = _rms_normed(x, gmix_ref[...])
    qc = qc_ref[...]
    mk = mk_ref[0]
    mv = mv_ref[0]
    ys = []
    for hd in range(MEM_HEADS):
        cols = slice(hd * MEM_HEAD_DIM, (hd + 1) * MEM_HEAD_DIM)
        s = _nt(qc[:, cols], mk[:, cols])
        p = jnp.exp2(s - jnp.max(s, axis=1, keepdims=True))
        y = _dot(p.astype(BF16), mv[:, cols]) / jnp.sum(p, axis=1, keepdims=True)
        ys.append(y)
    yc = jnp.concatenate(ys, axis=1).astype(BF16)
    merged = None
    for i, (y, wb_ref) in enumerate(((ym_ref[...], wbm_ref), (yf_ref[...], wbf_ref),
                                     (yc, wbc_ref))):
        gate = jax.nn.sigmoid(_nt(h, wg_ref[i * D_MODEL:(i + 1) * D_MODEL, :]))
        term = gate * _dot(y, wb_ref[...])
        merged = term if merged is None else merged + term
    o_ref[...] = x + _dot(merged.astype(BF16), wo_ref[...])


def _merge_call(x2, gmix, proj, y_m, y_f, mk, mv, wg, wbm, wbf, wbc, wo, seq, tm):
    n_tok = x2.shape[0]
    mlen = mk.shape[1]
    seq_tiles = seq // tm
    const = lambda i: (0, 0)
    return pl.pallas_call(
        _merge_kernel,
        grid=(n_tok // tm,),
        in_specs=[
            pl.BlockSpec((tm, D_MODEL), lambda i: (i, 0)),
            pl.BlockSpec((1, D_MODEL), const),
            pl.BlockSpec((tm, ATT_WIDTH), lambda i: (i, TILE_QC)),
            pl.BlockSpec((tm, ATT_WIDTH), lambda i: (i, 0)),
            pl.BlockSpec((tm, ATT_WIDTH), lambda i: (i, 0)),
            pl.BlockSpec((1, mlen, ATT_WIDTH), lambda i: (i // seq_tiles, 0, 0)),
            pl.BlockSpec((1, mlen, ATT_WIDTH), lambda i: (i // seq_tiles, 0, 0)),
            pl.BlockSpec((3 * D_MODEL, D_MODEL), const),
            pl.BlockSpec((ATT_WIDTH, D_MODEL), const),
            pl.BlockSpec((ATT_WIDTH, D_MODEL), const),
            pl.BlockSpec((ATT_WIDTH, D_MODEL), const),
            pl.BlockSpec((D_MODEL, D_MODEL), const),
        ],
        out_specs=pl.BlockSpec((tm, D_MODEL), lambda i: (i, 0)),
        out_shape=jax.ShapeDtypeStruct((n_tok, D_MODEL), F32),
        compiler_params=pltpu.CompilerParams(
            dimension_semantics=("parallel",), vmem_limit_bytes=VMEM_LIMIT),
        name="merge",
    )(x2, gmix, proj, y_m, y_f, mk, mv, wg, wbm, wbf, wbc, wo)


def _ffn_kernel(x_ref, g_ref, wg_ref, wu_ref, wd_ref, o_ref):
    x = x_ref[...]
    h = _rms_normed(x, g_ref[...])
    d_ff = wg_ref.shape[1]
    out = x
    for lo in range(0, d_ff, FFN_CHUNK):
        cols = slice(lo, min(lo + FFN_CHUNK, d_ff))
        g = _dot(h, wg_ref[:, cols])
        u = _dot(h, wu_ref[:, cols])
        a = (g * jax.nn.sigmoid(g) * u).astype(BF16)
        out = out + _dot(a, wd_ref[cols, :])
    o_ref[...] = out


def _ffn_call(x1, g_ffn, wg, wu, wd, tm):
    n_tok = x1.shape[0]
    d_ff = wg.shape[1]
    const = lambda i: (0, 0)
    resident = dict(pipeline_mode=pl.Buffered(1))
    return pl.pallas_call(
        _ffn_kernel,
        grid=(n_tok // tm,),
        in_specs=[
            pl.BlockSpec((tm, D_MODEL), lambda i: (i, 0)),
            pl.BlockSpec((1, D_MODEL), const),
            pl.BlockSpec((D_MODEL, d_ff), const, **resident),
            pl.BlockSpec((D_MODEL, d_ff), const, **resident),
            pl.BlockSpec((d_ff, D_MODEL), const, **resident),
        ],
        out_specs=pl.BlockSpec((tm, D_MODEL), lambda i: (i, 0)),
        out_shape=jax.ShapeDtypeStruct((n_tok, D_MODEL), F32),
        compiler_params=pltpu.CompilerParams(
            dimension_semantics=("parallel",), vmem_limit_bytes=VMEM_LIMIT),
        name="ffn",
    )(x1, g_ffn, wg, wu, wd)


@functools.lru_cache(maxsize=None)
def _rope_tables(seq):
    half = ROPE_DIMS // 2
    inv_freq = 1.0 / (ROPE_THETA ** (np.arange(half, dtype=np.float64) * 2.0 / ROPE_DIMS))
    ang = np.arange(seq, dtype=np.float64)[:, None] * inv_freq[None, :]
    cos, sin = np.cos(ang), np.sin(ang)
    d = np.arange(LANES) % HEAD_DIM
    rc = np.where(d[None, :] < ROPE_DIMS, cos[:, d % half], 1.0)
    rs1 = np.where(d[None, :] < half, -sin[:, d % half], 0.0)
    rs2 = np.where((d[None, :] >= half) & (d[None, :] < ROPE_DIMS), sin[:, d % half], 0.0)
    return tuple(np.asarray(t, np.float32) for t in (rc, rs1, rs2))


@functools.lru_cache(maxsize=None)
def _segment_mean_matrices():
    def blockdiag(width):
        return np.kron(np.eye(SEG_TILE // width), np.ones((width, width))) / width
    return np.stack([blockdiag(HEAD_DIM), blockdiag(MEM_HEAD_DIM)]).astype(np.float32)


def _score_bound(g_q, g_k):
    return (BOUND_SLACK * HEAD_DIM ** 0.5 * jnp.max(jnp.abs(g_q)) * jnp.max(jnp.abs(g_k)))


def _lane_row(value, first, count):
    lane = jnp.arange(LANES)
    return jnp.where((lane >= first) & (lane < first + count), value, 0.0).astype(F32)[None, :]


def kernel(x, mem, g_mix, w_in, b_forget, g_q_moba, g_k_moba, g_q_fox, g_k_fox, g_q_mem, g_k_mem,
           g_mem, w_mem_kv, w_br_moba, w_br_fox, w_br_mem, w_out, g_ffn, w_gate, w_up, w_down):
    batch, seq, _ = x.shape
    depth = g_mix.shape[0]
    n_tok = batch * seq
    assert seq % (KEY_GROUP * ATT_TILE) == 0 and seq // MOBA_BLOCK < MOBA_SHIFT_LANE
    n_extra = 3 * N_HEADS

    rc, rs1, rs2 = (jnp.asarray(t) for t in _rope_tables(seq))
    bd = jnp.asarray(_segment_mean_matrices()).astype(BF16)
    ones = jnp.ones((COL_TILE,), F32)

    x2 = x.reshape(n_tok, D_MODEL)
    for layer in range(depth):
        w = w_in[layer]
        gates0 = 7 * ATT_WIDTH + N_HEADS
        qc0 = 6 * ATT_WIDTH + N_HEADS
        w_t = jnp.swapaxes(w, 0, 1)
        w_qkv = w_t[:6 * ATT_WIDTH].astype(BF16)
        w_qc = w_t[qc0:gates0].astype(BF16)
        w_gates = w_t[gates0:].astype(BF16)
        wf3 = jnp.repeat(w_t[6 * ATT_WIDTH:qc0], 3, axis=0)
        gap = jnp.zeros((FOX_Q_LANE0 - n_extra, D_MODEL), F32)
        tail = jnp.zeros((LANES - FOX_Q_LANE0 - n_extra, D_MODEL), F32)
        wf = jnp.concatenate([wf3, gap, wf3, tail], axis=0).astype(BF16)
        bf3 = jnp.repeat(b_forget[layer], 3)
        bf_rep = jnp.concatenate([bf3, gap[:, 0], bf3, tail[:, 0]])[None, :]

        att_scale = HEAD_DIM ** -0.5 * LOG2E
        gains = [ones] * N_COL_TILES
        gains[TILE_QM] = jnp.tile(g_q_moba[layer], N_HEADS) * att_scale
        gains[TILE_KM] = jnp.tile(g_k_moba[layer], N_HEADS)
        gains[TILE_QF] = jnp.tile(g_q_fox[layer], N_HEADS) * att_scale
        gains[TILE_KF] = jnp.tile(g_k_fox[layer], N_HEADS)
        gains[TILE_QC] = jnp.tile(g_q_mem[layer], MEM_HEADS) * (MEM_HEAD_DIM ** -0.5 * LOG2E)
        gains = jnp.stack(gains)[:, None, :]

        bound_m = _score_bound(g_q_moba[layer], g_k_moba[layer])
        bound_f = _score_bound(g_q_fox[layer], g_k_fox[layer])
        shift_m = _lane_row(bound_m * LOG2E, MOBA_SHIFT_LANE, 1)
        shift_f = _lane_row(bound_f * LOG2E, FOX_Q_LANE0, n_extra)

        proj, faug, kmean = _proj_call(x2, g_mix[layer][None, :], w_qkv, w_qc, wf, bf_rep, shift_f,
                                       bd, gains, rc, rs1, rs2, seq, tm=PROJ_TILE)
        kmean = kmean.reshape(batch, seq // MOBA_BLOCK, COL_TILE)
        mk, mv = _memkv_call(mem, g_mem[layer][None, :], w_mem_kv[layer].astype(BF16),
                             g_k_mem[layer][None, :])
        def attention(bounded):
            return (_moba_call(proj, kmean, shift_m, batch, seq, bounded),
                    _fox_call(proj, faug, batch, seq, bounded))

        y_m, y_f = lax.cond(2.0 * jnp.maximum(bound_m, bound_f) <= MAX_SHIFT_NATS,
                            lambda: attention(True), lambda: attention(False))
        x1 = _merge_call(x2, g_mix[layer][None, :], proj, y_m, y_f, mk, mv, w_gates,
                         w_br_moba[layer].astype(BF16), w_br_fox[layer].astype(BF16),
                         w_br_mem[layer].astype(BF16), w_out[layer].astype(BF16), seq,
                         tm=TOKEN_TILE)
        x2 = _ffn_call(x1, g_ffn[layer][None, :], w_gate[layer].astype(BF16),
                       w_up[layer].astype(BF16), w_down[layer].astype(BF16), tm=TOKEN_TILE)
    return x2.reshape(batch, seq, D_MODEL)
```

```python
import functools
import math

import numpy as np
import jax
import jax.numpy as jnp
from jax import lax
from jax.experimental import pallas as pl
from jax.experimental.pallas import tpu as pltpu

F32 = jnp.float32
BF16 = jnp.bfloat16

D_MODEL = 1024
HEAD_DIM = 64
N_HEADS = 8
MEM_HEADS = 4
MEM_HEAD_DIM = 128
ATT_WIDTH = 512
MOBA_BLOCK = 256
MOBA_TOPK = 3
ROPE_THETA = 500000.0
ROPE_DIMS = 16
NORM_EPS = 1e-6
NEG = -1e30
LOG2E = math.log2(math.e)

LANES = 128
COL_TILE = 512
SEG_TILE = 256
N_COL_TILES = 7
PROJ_WIDTH = COL_TILE * N_COL_TILES
TILE_QM, TILE_KM, TILE_QF, TILE_KF, TILE_QC = 0, 1, 3, 4, 6
ATT_TILE = 256
KEY_GROUP = 4
Q_SUB = KEY_GROUP
DIAG_SLOT = 2
FFN_CHUNK = 1024
TOKEN_TILE = 1024
VMEM_LIMIT = 56 * 1024 * 1024

FOX_Q_LANE0 = 32
MOBA_SHIFT_LANE = LANES - 1
MAX_SHIFT_NATS = 60.0
BOUND_SLACK = 1.02


def _nt(a, b):
    return lax.dot_general(a, b, (((1,), (1,)), ((), ())), preferred_element_type=F32)


def _dot(a, b):
    return jnp.dot(a, b, preferred_element_type=F32)


def _split3(v):
    p1 = v.astype(BF16)
    r1 = v - p1.astype(F32)
    p2 = r1.astype(BF16)
    p3 = (r1 - p2.astype(F32)).astype(BF16)
    return p1, p2, p3


def _rms_normed(x, gain_row):
    ms = jnp.mean(x * x, axis=-1, keepdims=True)
    return (x * lax.rsqrt(ms + NORM_EPS) * gain_row).astype(BF16)


def _forget_lanes(cum, shift_row):
    lane = lax.broadcasted_iota(jnp.int32, cum.shape, 1)
    key_side = lane < 3 * N_HEADS
    query_side = (lane >= FOX_Q_LANE0) & (lane < FOX_Q_LANE0 + 3 * N_HEADS)
    cum2 = cum * LOG2E
    val = jnp.where(key_side, -cum2, cum2 - shift_row)
    c1, c2, c3 = _split3(val)
    piece = jnp.where(key_side, lane, lane - FOX_Q_LANE0) % 3
    sel = jnp.where(piece == 0, c1, jnp.where(piece == 1, c2, c3))
    return jnp.where(key_side | query_side, sel, jnp.zeros_like(sel))


def _proj_kernel(x_ref, gmix_ref, w_ref, wc_ref, wf_ref, bf_ref, shift_ref, bd_ref, gain_ref,
                 rc_ref, rs1_ref, rs2_ref, p_ref, faug_ref, kmean_ref, carry_ref, *, seq_tiles):
    tm = x_ref.shape[0]

    @pl.when(pl.program_id(0) % seq_tiles == 0)
    def _():
        carry_ref[...] = jnp.zeros_like(carry_ref)

    h = _rms_normed(x_ref[...], gmix_ref[...])

    z = _nt(h, wf_ref[...]) + bf_ref[...]
    lf = jnp.minimum(z, 0.0) - jnp.log1p(jnp.exp(-jnp.abs(z)))
    row = lax.broadcasted_iota(jnp.int32, (SEG_TILE, SEG_TILE), 0)
    col = lax.broadcasted_iota(jnp.int32, (SEG_TILE, SEG_TILE), 1)
    tri = jnp.where(col <= row, 1.0, 0.0).astype(BF16)
    pieces = _split3(lf)
    carry = carry_ref[...]
    parts = []
    for r in range(0, tm, SEG_TILE):
        part = carry
        for piece in pieces:
            part = part + _dot(tri, piece[r:r + SEG_TILE, :])
        carry = part[SEG_TILE - 1:SEG_TILE, :]
        parts.append(part)
    carry_ref[...] = carry
    faug_ref[...] = _forget_lanes(jnp.concatenate(parts, axis=0), shift_ref[...])

    def head_normed(t, k):
        seg = 1 if k == TILE_QC else 0
        sq = (t * t).astype(BF16)
        ms = jnp.concatenate(
            [_dot(sq[:, lo:lo + SEG_TILE], bd_ref[seg]) for lo in range(0, COL_TILE, SEG_TILE)],
            axis=1)
        return t * lax.rsqrt(ms + NORM_EPS) * gain_ref[k]

    def rotated(y):
        rc, rs1, rs2 = rc_ref[...], rs1_ref[...], rs2_ref[...]
        parts = []
        for g in range(COL_TILE // LANES):
            yg = y[:, g * LANES:(g + 1) * LANES]
            up = pltpu.roll(yg, LANES - ROPE_DIMS // 2, 1)
            dn = pltpu.roll(yg, ROPE_DIMS // 2, 1)
            parts.append(yg * rc + up * rs1 + dn * rs2)
        return jnp.concatenate(parts, axis=1)

    for k in range(N_COL_TILES):
        cols = slice(k * COL_TILE, (k + 1) * COL_TILE)
        t = _nt(h, wc_ref[...] if k == TILE_QC else w_ref[cols, :])
        if k in (TILE_QM, TILE_KM):
            t = rotated(head_normed(t, k))
        elif k in (TILE_QF, TILE_KF, TILE_QC):
            t = head_normed(t, k)
        p_ref[:, cols] = t.astype(BF16)
        if k == TILE_KM:
            for r in range(tm // MOBA_BLOCK):
                blk = t[r * MOBA_BLOCK:(r + 1) * MOBA_BLOCK, :]
                kmean_ref[0, pl.ds(r, 1), :] = jnp.mean(blk, axis=0, keepdims=True)


def _proj_call(x2, gmix, w_qkv, w_qc, wf, bf_row, shift_row, bd, gains, rc, rs1, rs2, seq, tm):
    n_tok = x2.shape[0]
    seq_tiles = seq // tm
    const2 = lambda i: (0, 0)
    const3 = lambda i: (0, 0, 0)
    return pl.pallas_call(
        functools.partial(_proj_kernel, seq_tiles=seq_tiles),
        grid=(n_tok // tm,),
        in_specs=[
            pl.BlockSpec((tm, D_MODEL), lambda i: (i, 0)),
            pl.BlockSpec((1, D_MODEL), const2),
            pl.BlockSpec((PROJ_WIDTH - COL_TILE, D_MODEL), const2),
            pl.BlockSpec((COL_TILE, D_MODEL), const2),
            pl.BlockSpec((LANES, D_MODEL), const2),
            pl.BlockSpec((1, LANES), const2),
            pl.BlockSpec((1, LANES), const2),
            pl.BlockSpec((2, SEG_TILE, SEG_TILE), const3),
            pl.BlockSpec((N_COL_TILES, 1, COL_TILE), const3),
            pl.BlockSpec((tm, LANES), lambda i: (i % seq_tiles, 0)),
            pl.BlockSpec((tm, LANES), lambda i: (i % seq_tiles, 0)),
            pl.BlockSpec((tm, LANES), lambda i: (i % seq_tiles, 0)),
        ],
        out_specs=[
            pl.BlockSpec((tm, PROJ_WIDTH), lambda i: (i, 0)),
            pl.BlockSpec((tm, LANES), lambda i: (i, 0)),
            pl.BlockSpec((1, tm // MOBA_BLOCK, COL_TILE), lambda i: (i, 0, 0)),
        ],
        out_shape=[
            jax.ShapeDtypeStruct((n_tok, PROJ_WIDTH), BF16),
            jax.ShapeDtypeStruct((n_tok, LANES), BF16),
            jax.ShapeDtypeStruct((n_tok // tm, tm // MOBA_BLOCK, COL_TILE), F32),
        ],
        scratch_shapes=[pltpu.VMEM((1, LANES), F32)],
        compiler_params=pltpu.CompilerParams(
            dimension_semantics=("arbitrary",), vmem_limit_bytes=VMEM_LIMIT),
        name="proj",
    )(x2, gmix, w_qkv, w_qc, wf, bf_row, shift_row, bd, gains, rc, rs1, rs2)


def _memkv_kernel(mem_ref, gmem_ref, w_ref, gk_ref, mk_ref, mv_ref):
    hm = _rms_normed(mem_ref[0], gmem_ref[...])
    kv = _dot(hm, w_ref[...])
    ks = []
    for h in range(MEM_HEADS):
        kh = kv[:, h * MEM_HEAD_DIM:(h + 1) * MEM_HEAD_DIM]
        msk = jnp.mean(kh * kh, axis=-1, keepdims=True)
        ks.append(kh * lax.rsqrt(msk + NORM_EPS) * gk_ref[...])
    mk_ref[0] = jnp.concatenate(ks, axis=1).astype(BF16)
    mv_ref[0] = kv[:, ATT_WIDTH:].astype(BF16)


def _memkv_call(mem, gmem, w_kv, gk):
    batch, mlen, _ = mem.shape
    return pl.pallas_call(
        _memkv_kernel,
        grid=(batch,),
        in_specs=[
            pl.BlockSpec((1, mlen, D_MODEL), lambda b: (b, 0, 0)),
            pl.BlockSpec((1, D_MODEL), lambda b: (0, 0)),
            pl.BlockSpec((D_MODEL, 2 * ATT_WIDTH), lambda b: (0, 0)),
            pl.BlockSpec((1, MEM_HEAD_DIM), lambda b: (0, 0)),
        ],
        out_specs=[
            pl.BlockSpec((1, mlen, ATT_WIDTH), lambda b: (b, 0, 0)),
            pl.BlockSpec((1, mlen, ATT_WIDTH), lambda b: (b, 0, 0)),
        ],
        out_shape=[jax.ShapeDtypeStruct((batch, mlen, ATT_WIDTH), BF16)] * 2,
        compiler_params=pltpu.CompilerParams(dimension_semantics=("parallel",)),
        name="memkv",
    )(mem, gmem, w_kv, gk)


def _lane_iota():
    return lax.broadcasted_iota(jnp.int32, (ATT_TILE, LANES), 1)


def _lane_range(first, count, dtype):
    lane = _lane_iota()
    return jnp.where((lane >= first) & (lane < first + count), 1.0, 0.0).astype(dtype)


def _head_lanes(c, dtype):
    return _lane_range(c * HEAD_DIM, HEAD_DIM, dtype)


def _fill_values(v_ref, vaug_ref):
    n_blocks = v_ref.shape[0] // ATT_TILE

    def fill(j, _):
        rows = pl.ds(pl.multiple_of(j * ATT_TILE, ATT_TILE), ATT_TILE)
        v = v_ref[rows, :].astype(F32)
        for c in range(2):
            own = _head_lanes(c, F32)
            vaug_ref[c, rows, :] = (v * own + (1.0 - own)).astype(BF16)
        return 0

    lax.fori_loop(0, n_blocks, fill, 0)


def _normalised(acc_ref, t):
    res = []
    for c in range(2):
        acc = acc_ref[2 * t + c]
        res.append(acc / pltpu.roll(acc, HEAD_DIM, 1))
    return jnp.where(_lane_iota() < HEAD_DIM, res[0], res[1])


def _causal_masks():
    span = KEY_GROUP * ATT_TILE
    diff = (lax.broadcasted_iota(jnp.int32, (ATT_TILE, span), 1)
            - lax.broadcasted_iota(jnp.int32, (ATT_TILE, span), 0))
    return [diff <= t * ATT_TILE for t in range(Q_SUB)]


def _attend_shifted(lhs_of_tile, kaug_ref, vaug_ref, acc_ref, s_ref, lhs_ref, o_ref, n_q):
    span = KEY_GROUP * ATT_TILE
    n_chain = 2 * Q_SUB

    def group_rows(g):
        return pl.ds(pl.multiple_of(g * span, span), span)

    def scores(g, slot):
        kb = kaug_ref[group_rows(g), :]
        for i in range(n_chain):
            s_ref[slot, i] = _nt(lhs_ref[i], kb)

    def scores_diagonal(g, slot):
        for i in range(n_chain):
            rows = pl.ds(pl.multiple_of(g * span, span), (i // 2 + 1) * ATT_TILE)
            s_ref[slot, i, :, 0:(i // 2 + 1) * ATT_TILE] = _nt(lhs_ref[i], kaug_ref[rows, :])

    def consume(g, slot):
        rows = group_rows(g)
        for i in range(n_chain):
            p = jnp.exp2(s_ref[slot, i]).astype(BF16)
            acc_ref[i] += _dot(p, vaug_ref[i % 2, rows, :])

    def consume_diagonal(g, slot):
        row = lax.broadcasted_iota(jnp.int32, (ATT_TILE, ATT_TILE), 0)
        col = lax.broadcasted_iota(jnp.int32, (ATT_TILE, ATT_TILE), 1)
        for i in range(n_chain):
            t = i // 2
            lo = t * ATT_TILE
            parts = []
            if t:
                parts.append(jnp.exp2(s_ref[slot, i, :, 0:lo]).astype(BF16))
            s_diag = jnp.where(col <= row, s_ref[slot, i, :, lo:lo + ATT_TILE], NEG)
            parts.append(jnp.exp2(s_diag).astype(BF16))
            p = parts[0] if len(parts) == 1 else jnp.concatenate(parts, axis=1)
            rows = pl.ds(pl.multiple_of(g * span, span), lo + ATT_TILE)
            acc_ref[i] += _dot(p, vaug_ref[i % 2, rows, :])

    def emit(qi):
        for t in range(Q_SUB):
            rows = pl.ds(pl.multiple_of((qi * Q_SUB + t) * ATT_TILE, ATT_TILE), ATT_TILE)
            o_ref[rows, :] = _normalised(acc_ref, t).astype(BF16)

    for i, operand in enumerate(lhs_of_tile(0)):
        lhs_ref[i] = operand
    scores_diagonal(0, DIAG_SLOT)

    def tile(qi, _):
        acc_ref[...] = jnp.zeros_like(acc_ref)
        n_full = qi

        @pl.when(n_full > 0)
        def _():
            def pair(p, _):
                g = 2 * p
                consume(g, 0)
                scores(g + 1, 1)
                consume(g + 1, 1)
                scores(g + 2, 0)
                return 0

            n_loop = n_full - 1
            lax.fori_loop(0, n_loop // 2, pair, 0)

            @pl.when(lax.rem(n_loop, 2) == 1)
            def _():
                consume(n_loop - 1, 0)
                scores(n_loop, 1)

            consume(n_full - 1, lax.rem(n_full - 1, 2))
            scores_diagonal(n_full, DIAG_SLOT)

        consume_diagonal(n_full, DIAG_SLOT)
        nxt = lhs_of_tile(jnp.minimum(qi + 1, n_q - 1))
        kb = kaug_ref[group_rows(0), :]
        for i in range(n_chain):
            lhs_ref[i] = nxt[i]
            s_ref[0, i] = _nt(nxt[i], kb)
        emit(qi)
        return 0

    lax.fori_loop(0, n_q, tile, 0)


def _attend_running_max(lhs_of_tile, kaug_ref, vaug_ref, acc_ref, o_ref, n_q):
    span = KEY_GROUP * ATT_TILE
    n_chain = 2 * Q_SUB
    masks = _causal_masks()

    def tile(qi, _):
        lhs = lhs_of_tile(qi)

        def step(i, g, m, mask):
            rows = pl.ds(pl.multiple_of(g * span, span), span)
            s = _nt(lhs[i], kaug_ref[rows, :])
            if mask is not None:
                s = jnp.where(mask, s, NEG)
            m_new = jnp.maximum(m, jnp.max(s, axis=1, keepdims=True))
            p = jnp.exp2(s - m_new).astype(BF16)
            acc_ref[i] = jnp.exp2(m - m_new) * acc_ref[i] + _dot(p, vaug_ref[i % 2, rows, :])
            return m_new

        acc_ref[...] = jnp.zeros_like(acc_ref)
        m0 = jnp.full((ATT_TILE, 1), -jnp.inf, F32)
        ms = lax.fori_loop(
            0, qi, lambda g, ms: tuple(step(i, g, ms[i], None) for i in range(n_chain)),
            (m0,) * n_chain)
        for i in range(n_chain):
            step(i, qi, ms[i], masks[i // 2])
        for t in range(Q_SUB):
            rows = pl.ds(pl.multiple_of((qi * Q_SUB + t) * ATT_TILE, ATT_TILE), ATT_TILE)
            o_ref[rows, :] = _normalised(acc_ref, t).astype(BF16)
        return 0

    lax.fori_loop(0, n_q, tile, 0)


def _attend(lhs_of_tile, kaug_ref, vaug_ref, acc_ref, s_ref, lhs_ref, o_ref, n_q, bounded):
    if bounded:
        _attend_shifted(lhs_of_tile, kaug_ref, vaug_ref, acc_ref, s_ref, lhs_ref, o_ref, n_q)
    else:
        _attend_running_max(lhs_of_tile, kaug_ref, vaug_ref, acc_ref, o_ref, n_q)


def _attention_scratch(seq):
    n_chain = 2 * Q_SUB
    return [pltpu.VMEM((seq, 2 * LANES), BF16),
            pltpu.VMEM((2, seq, LANES), BF16),
            pltpu.VMEM((n_chain, ATT_TILE, LANES), F32),
            pltpu.VMEM((3, n_chain, ATT_TILE, KEY_GROUP * ATT_TILE), F32),
            pltpu.VMEM((n_chain, ATT_TILE, 2 * LANES), BF16)]


def _moba_kernel(q_ref, k_ref, v_ref, km_ref, shift_ref, o_ref, kaug_ref, vaug_ref,
                 acc_ref, s_ref, lhs_ref, *, bounded):
    seq = k_ref.shape[0]
    n_blocks = seq // MOBA_BLOCK
    lane = _lane_iota()

    kaug_ref[:, 0:LANES] = k_ref[...]

    def fill(j, _):
        rows = pl.ds(pl.multiple_of(j * MOBA_BLOCK, MOBA_BLOCK), MOBA_BLOCK)
        onehot = (lane == j) | (lane == MOBA_SHIFT_LANE)
        kaug_ref[rows, LANES:2 * LANES] = jnp.where(onehot, 1.0, 0.0).astype(BF16)
        return 0

    lax.fori_loop(0, n_blocks, fill, 0)
    _fill_values(v_ref, vaug_ref)

    def lhs_of_tile(tile):
        km = km_ref[0]
        km = jnp.concatenate([km, jnp.zeros((LANES - n_blocks, LANES), F32)], axis=0)
        km_hi = km.astype(BF16)
        km_lo = (km - km_hi.astype(F32)).astype(BF16)
        shift = shift_ref[:, MOBA_SHIFT_LANE:MOBA_SHIFT_LANE + 1] if bounded else 0.0
        blk = lax.broadcasted_iota(jnp.int32, (n_blocks, ATT_TILE), 0)
        blk_f = blk.astype(F32)
        pad_row = lax.broadcasted_iota(jnp.int32, (LANES - n_blocks, ATT_TILE), 0) + n_blocks
        pad_bias = jnp.where(pad_row == MOBA_SHIFT_LANE, -shift, NEG)
        lhs = []
        for t in range(Q_SUB):
            own = tile * Q_SUB + t
            q2 = q_ref[pl.ds(pl.multiple_of(own * ATT_TILE, ATT_TILE), ATT_TILE), :]
            for c in range(2):
                qc = q2 * _head_lanes(c, BF16)
                gate = (_nt(km_hi, qc) + _nt(km_lo, qc))[0:n_blocks]
                g = jnp.where(blk < own, gate, -jnp.inf)
                sel = blk == own
                for _ in range(MOBA_TOPK):
                    mx = jnp.max(g, axis=0, keepdims=True)
                    first = jnp.min(jnp.where(g == mx, blk_f, float(LANES)), axis=0,
                                    keepdims=True)
                    pick = (blk_f == first) & (mx > -jnp.inf)
                    sel = sel | pick
                    g = jnp.where(pick, -jnp.inf, g)
                bias = jnp.concatenate([jnp.where(sel, 0.0, NEG), pad_bias], axis=0)
                lhs.append(jnp.concatenate([qc, bias.T.astype(BF16)], axis=1))
        return lhs

    _attend(lhs_of_tile, kaug_ref, vaug_ref, acc_ref, s_ref, lhs_ref, o_ref,
            seq // (Q_SUB * ATT_TILE), bounded)


def _moba_call(proj, kmean, shift_row, batch, seq, bounded):
    n_pairs = N_HEADS // 2
    q0 = (TILE_QM * COL_TILE) // LANES
    k0 = (TILE_KM * COL_TILE) // LANES
    v0 = k0 + COL_TILE // LANES
    return pl.pallas_call(
        functools.partial(_moba_kernel, bounded=bounded),
        grid=(batch, n_pairs),
        in_specs=[
            pl.BlockSpec((seq, LANES), lambda b, p: (b, q0 + p)),
            pl.BlockSpec((seq, LANES), lambda b, p: (b, k0 + p)),
            pl.BlockSpec((seq, LANES), lambda b, p: (b, v0 + p)),
            pl.BlockSpec((1, seq // MOBA_BLOCK, LANES), lambda b, p: (b, 0, p)),
            pl.BlockSpec((1, LANES), lambda b, p: (0, 0)),
        ],
        out_specs=pl.BlockSpec((seq, LANES), lambda b, p: (b, p)),
        out_shape=jax.ShapeDtypeStruct((batch * seq, ATT_WIDTH), BF16),
        scratch_shapes=_attention_scratch(seq),
        compiler_params=pltpu.CompilerParams(
            dimension_semantics=("parallel", "parallel"), vmem_limit_bytes=VMEM_LIMIT),
        name="moba_bounded" if bounded else "moba_general",
    )(proj, proj, proj, kmean, shift_row)


def _fox_kernel(q_ref, k_ref, v_ref, faug_ref, o_ref, kaug_ref, vaug_ref,
                acc_ref, s_ref, lhs_ref, *, bounded):
    pair = pl.program_id(1)
    n_extra = 3 * N_HEADS
    seq = k_ref.shape[0]

    kaug_ref[:, 0:LANES] = k_ref[...]

    def fill(j, _):
        rows = pl.ds(pl.multiple_of(j * ATT_TILE, ATT_TILE), ATT_TILE)
        lane = _lane_iota()
        ones = (lane >= FOX_Q_LANE0) & (lane < FOX_Q_LANE0 + n_extra)
        f = faug_ref[rows, :].astype(F32)
        kaug_ref[rows, LANES:2 * LANES] = (
            f * (lane < n_extra).astype(F32) + ones.astype(F32)).astype(BF16)
        return 0

    lax.fori_loop(0, seq // ATT_TILE, fill, 0)
    _fill_values(v_ref, vaug_ref)

    def lhs_of_tile(tile):
        lhs = []
        for t in range(Q_SUB):
            rows = pl.ds(pl.multiple_of((tile * Q_SUB + t) * ATT_TILE, ATT_TILE), ATT_TILE)
            q2 = q_ref[rows, :]
            fq = faug_ref[rows, :]
            for c in range(2):
                qc = q2 * _head_lanes(c, BF16)
                k_first = 3 * (2 * pair + c)
                extra = _lane_range(k_first, 3, BF16)
                if bounded:
                    extra = extra + fq * _lane_range(FOX_Q_LANE0 + k_first, 3, BF16)
                lhs.append(jnp.concatenate([qc, extra], axis=1))
        return lhs

    _attend(lhs_of_tile, kaug_ref, vaug_ref, acc_ref, s_ref, lhs_ref, o_ref,
            seq // (Q_SUB * ATT_TILE), bounded)


def _fox_call(proj, faug, batch, seq, bounded):
    n_pairs = N_HEADS // 2
    q0 = (TILE_QF * COL_TILE) // LANES
    k0 = (TILE_KF * COL_TILE) // LANES
    v0 = k0 + COL_TILE // LANES
    return pl.pallas_call(
        functools.partial(_fox_kernel, bounded=bounded),
        grid=(batch, n_pairs),
        in_specs=[
            pl.BlockSpec((seq, LANES), lambda b, p: (b, q0 + p)),
            pl.BlockSpec((seq, LANES), lambda b, p: (b, k0 + p)),
            pl.BlockSpec((seq, LANES), lambda b, p: (b, v0 + p)),
            pl.BlockSpec((seq, LANES), lambda b, p: (b, 0)),
        ],
        out_specs=pl.BlockSpec((seq, LANES), lambda b, p: (b, p)),
        out_shape=jax.ShapeDtypeStruct((batch * seq, ATT_WIDTH), BF16),
        scratch_shapes=_attention_scratch(seq),
        compiler_params=pltpu.CompilerParams(
            dimension_semantics=("parallel", "parallel"), vmem_limit_bytes=VMEM_LIMIT),
        name="fox_bounded" if bounded else "fox_general",
    )(proj, proj, proj, faug)


def _merge_kernel(x_ref, gmix_ref, qc_ref, ym_ref, yf_ref, mk_ref, mv_ref,
                  wg_ref, wbm_ref, wbf_ref, wbc_ref, wo_ref, o_ref):
    x = x_ref[...]
    h = _rms_normed(x, gmix_ref[...])
    qc = qc_ref[...]
    mk = mk_ref[0]
    mv = mv_ref[0]
    ys = []
    for hd in range(MEM_HEADS):
        cols = slice(hd * MEM_HEAD_DIM, (hd + 1) * MEM_HEAD_DIM)
        s = _nt(qc[:, cols], mk[:, cols])
        p = jnp.exp2(s - jnp.max(s, axis=1, keepdims=True))
        y = _dot(p.astype(BF16), mv[:, cols]) / jnp.sum(p, axis=1, keepdims=True)
        ys.append(y)
    yc = jnp.concatenate(ys, axis=1).astype(BF16)
    merged = None
    for i, (y, wb_ref) in enumerate(((ym_ref[...], wbm_ref), (yf_ref[...], wbf_ref),
                                     (yc, wbc_ref))):
        gate = jax.nn.sigmoid(_nt(h, wg_ref[i * D_MODEL:(i + 1) * D_MODEL, :]))
        term = gate * _dot(y, wb_ref[...])
        merged = term if merged is None else merged + term
    o_ref[...] = x + _dot(merged.astype(BF16), wo_ref[...])


def _merge_call(x2, gmix, proj, y_m, y_f, mk, mv, wg, wbm, wbf, wbc, wo, seq, tm):
    n_tok = x2.shape[0]
    mlen = mk.shape[1]
    seq_tiles = seq // tm
    const = lambda i: (0, 0)
    return pl.pallas_call(
        _merge_kernel,
        grid=(n_tok // tm,),
        in_specs=[
            pl.BlockSpec((tm, D_MODEL), lambda i: (i, 0)),
            pl.BlockSpec((1, D_MODEL), const),
            pl.BlockSpec((tm, ATT_WIDTH), lambda i: (i, TILE_QC)),
            pl.BlockSpec((tm, ATT_WIDTH), lambda i: (i, 0)),
            pl.BlockSpec((tm, ATT_WIDTH), lambda i: (i, 0)),
            pl.BlockSpec((1, mlen, ATT_WIDTH), lambda i: (i // seq_tiles, 0, 0)),
            pl.BlockSpec((1, mlen, ATT_WIDTH), lambda i: (i // seq_tiles, 0, 0)),
            pl.BlockSpec((3 * D_MODEL, D_MODEL), const),
            pl.BlockSpec((ATT_WIDTH, D_MODEL), const),
            pl.BlockSpec((ATT_WIDTH, D_MODEL), const),
            pl.BlockSpec((ATT_WIDTH, D_MODEL), const),
            pl.BlockSpec((D_MODEL, D_MODEL), const),
        ],
        out_specs=pl.BlockSpec((tm, D_MODEL), lambda i: (i, 0)),
        out_shape=jax.ShapeDtypeStruct((n_tok, D_MODEL), F32),
        compiler_params=pltpu.CompilerParams(
            dimension_semantics=("parallel",), vmem_limit_bytes=VMEM_LIMIT),
        name="merge",
    )(x2, gmix, proj, y_m, y_f, mk, mv, wg, wbm, wbf, wbc, wo)


def _ffn_kernel(x_ref, g_ref, wg_ref, wu_ref, wd_ref, o_ref):
    x = x_ref[...]
    h = _rms_normed(x, g_ref[...])
    d_ff = wg_ref.shape[1]
    out = x
    for lo in range(0, d_ff, FFN_CHUNK):
        cols = slice(lo, min(lo + FFN_CHUNK, d_ff))
        g = _dot(h, wg_ref[:, cols])
        u = _dot(h, wu_ref[:, cols])
        a = (g * jax.nn.sigmoid(g) * u).astype(BF16)
        out = out + _dot(a, wd_ref[cols, :])
    o_ref[...] = out


def _ffn_call(x1, g_ffn, wg, wu, wd, tm):
    n_tok = x1.shape[0]
    d_ff = wg.shape[1]
    const = lambda i: (0, 0)
    resident = dict(pipeline_mode=pl.Buffered(1))
    return pl.pallas_call(
        _ffn_kernel,
        grid=(n_tok // tm,),
        in_specs=[
            pl.BlockSpec((tm, D_MODEL), lambda i: (i, 0)),
            pl.BlockSpec((1, D_MODEL), const),
            pl.BlockSpec((D_MODEL, d_ff), const, **resident),
            pl.BlockSpec((D_MODEL, d_ff), const, **resident),
            pl.BlockSpec((d_ff, D_MODEL), const, **resident),
        ],
        out_specs=pl.BlockSpec((tm, D_MODEL), lambda i: (i, 0)),
        out_shape=jax.ShapeDtypeStruct((n_tok, D_MODEL), F32),
        compiler_params=pltpu.CompilerParams(
            dimension_semantics=("parallel",), vmem_limit_bytes=VMEM_LIMIT),
        name="ffn",
    )(x1, g_ffn, wg, wu, wd)


@functools.lru_cache(maxsize=None)
def _rope_tables(seq):
    half = ROPE_DIMS // 2
    inv_freq = 1.0 / (ROPE_THETA ** (np.arange(half, dtype=np.float64) * 2.0 / ROPE_DIMS))
    ang = np.arange(seq, dtype=np.float64)[:, None] * inv_freq[None, :]
    cos, sin = np.cos(ang), np.sin(ang)
    d = np.arange(LANES) % HEAD_DIM
    rc = np.where(d[None, :] < ROPE_DIMS, cos[:, d % half], 1.0)
    rs1 = np.where(d[None, :] < half, -sin[:, d % half], 0.0)
    rs2 = np.where((d[None, :] >= half) & (d[None, :] < ROPE_DIMS), sin[:, d % half], 0.0)
    return tuple(np.asarray(t, np.float32) for t in (rc, rs1, rs2))


@functools.lru_cache(maxsize=None)
def _segment_mean_matrices():
    def blockdiag(width):
        return np.kron(np.eye(SEG_TILE // width), np.ones((width, width))) / width
    return np.stack([blockdiag(HEAD_DIM), blockdiag(MEM_HEAD_DIM)]).astype(np.float32)


def _score_bound(g_q, g_k):
    return (BOUND_SLACK * HEAD_DIM ** 0.5 * jnp.max(jnp.abs(g_q)) * jnp.max(jnp.abs(g_k)))


def _lane_row(value, first, count):
    lane = jnp.arange(LANES)
    return jnp.where((lane >= first) & (lane < first + count), value, 0.0).astype(F32)[None, :]


def kernel(x, mem, g_mix, w_in, b_forget, g_q_moba, g_k_moba, g_q_fox, g_k_fox, g_q_mem, g_k_mem,
           g_mem, w_mem_kv, w_br_moba, w_br_fox, w_br_mem, w_out, g_ffn, w_gate, w_up, w_down):
    batch, seq, _ = x.shape
    depth = g_mix.shape[0]
    n_tok = batch * seq
    assert seq % (KEY_GROUP * ATT_TILE) == 0 and seq // MOBA_BLOCK < MOBA_SHIFT_LANE
    n_extra = 3 * N_HEADS

    rc, rs1, rs2 = (jnp.asarray(t) for t in _rope_tables(seq))
    bd = jnp.asarray(_segment_mean_matrices()).astype(BF16)
    ones = jnp.ones((COL_TILE,), F32)

    x2 = x.reshape(n_tok, D_MODEL)
    for layer in range(depth):
        w = w_in[layer]
        gates0 = 7 * ATT_WIDTH + N_HEADS
        qc0 = 6 * ATT_WIDTH + N_HEADS
        w_t = jnp.swapaxes(w, 0, 1)
        w_qkv = w_t[:6 * ATT_WIDTH].astype(BF16)
        w_qc = w_t[qc0:gates0].astype(BF16)
        w_gates = w_t[gates0:].astype(BF16)
        wf3 = jnp.repeat(w_t[6 * ATT_WIDTH:qc0], 3, axis=0)
        gap = jnp.zeros((FOX_Q_LANE0 - n_extra, D_MODEL), F32)
        tail = jnp.zeros((LANES - FOX_Q_LANE0 - n_extra, D_MODEL), F32)
        wf = jnp.concatenate([wf3, gap, wf3, tail], axis=0).astype(BF16)
        bf3 = jnp.repeat(b_forget[layer], 3)
        bf_rep = jnp.concatenate([bf3, gap[:, 0], bf3, tail[:, 0]])[None, :]

        att_scale = HEAD_DIM ** -0.5 * LOG2E
        gains = [ones] * N_COL_TILES
        gains[TILE_QM] = jnp.tile(g_q_moba[layer], N_HEADS) * att_scale
        gains[TILE_KM] = jnp.tile(g_k_moba[layer], N_HEADS)
        gains[TILE_QF] = jnp.tile(g_q_fox[layer], N_HEADS) * att_scale
        gains[TILE_KF] = jnp.tile(g_k_fox[layer], N_HEADS)
        gains[TILE_QC] = jnp.tile(g_q_mem[layer], MEM_HEADS) * (MEM_HEAD_DIM ** -0.5 * LOG2E)
        gains = jnp.stack(gains)[:, None, :]

        bound_m = _score_bound(g_q_moba[layer], g_k_moba[layer])
        bound_f = _score_bound(g_q_fox[layer], g_k_fox[layer])
        shift_m = _lane_row(bound_m * LOG2E, MOBA_SHIFT_LANE, 1)
        shift_f = _lane_row(bound_f * LOG2E, FOX_Q_LANE0, n_extra)

        proj, faug, kmean = _proj_call(x2, g_mix[layer][None, :], w_qkv, w_qc, wf, bf_rep, shift_f,
                                       bd, gains, rc, rs1, rs2, seq, tm=TOKEN_TILE)
        kmean = kmean.reshape(batch, seq // MOBA_BLOCK, COL_TILE)
        mk, mv = _memkv_call(mem, g_mem[layer][None, :], w_mem_kv[layer].astype(BF16),
                             g_k_mem[layer][None, :])
        def attention(bounded):
            return (_moba_call(proj, kmean, shift_m, batch, seq, bounded),
                    _fox_call(proj, faug, batch, seq, bounded))

        y_m, y_f = lax.cond(2.0 * jnp.maximum(bound_m, bound_f) <= MAX_SHIFT_NATS,
                            lambda: attention(True), lambda: attention(False))
        x1 = _merge_call(x2, g_mix[layer][None, :], proj, y_m, y_f, mk, mv, w_gates,
                         w_br_moba[layer].astype(BF16), w_br_fox[layer].astype(BF16),
                         w_br_mem[layer].astype(BF16), w_out[layer].astype(BF16), seq,
                         tm=TOKEN_TILE)
        x2 = _ffn_call(x1, g_ffn[layer][None, :], w_gate[layer].astype(BF16),
                       w_up[layer].astype(BF16), w_down[layer].astype(BF16), tm=TOKEN_TILE)
    return x2.reshape(batch, seq, D_MODEL)
```

```python
import functools
import math

import numpy as np
import jax
import jax.numpy as jnp
from jax import lax
from jax.experimental import pallas as pl
from jax.experimental.pallas import tpu as pltpu

F32 = jnp.float32
BF16 = jnp.bfloat16

D_MODEL = 1024
HEAD_DIM = 64
N_HEADS = 8
MEM_HEADS = 4
MEM_HEAD_DIM = 128
ATT_WIDTH = 512
MOBA_BLOCK = 256
MOBA_TOPK = 3
ROPE_THETA = 500000.0
ROPE_DIMS = 16
NORM_EPS = 1e-6
NEG = -1e30
LOG2E = math.log2(math.e)

LANES = 128
COL_TILE = 512
SEG_TILE = 256
N_COL_TILES = 7
PROJ_WIDTH = COL_TILE * N_COL_TILES
TILE_QM, TILE_KM, TILE_QF, TILE_KF, TILE_QC = 0, 1, 3, 4, 6
ATT_TILE = 256
KEY_GROUP = 4
Q_SUB = KEY_GROUP
DIAG_SLOT = 2
FFN_CHUNK = 1024
TOKEN_TILE = 1024
TAIL_TILE = 512
VMEM_LIMIT = 56 * 1024 * 1024

FOX_Q_LANE0 = 32
MOBA_SHIFT_LANE = LANES - 1
MAX_SHIFT_NATS = 60.0
BOUND_SLACK = 1.02


def _nt(a, b):
    return lax.dot_general(a, b, (((1,), (1,)), ((), ())), preferred_element_type=F32)


def _dot(a, b):
    return jnp.dot(a, b, preferred_element_type=F32)


def _split3(v):
    p1 = v.astype(BF16)
    r1 = v - p1.astype(F32)
    p2 = r1.astype(BF16)
    p3 = (r1 - p2.astype(F32)).astype(BF16)
    return p1, p2, p3


def _rms_normed(x, gain_row):
    ms = jnp.mean(x * x, axis=-1, keepdims=True)
    return (x * lax.rsqrt(ms + NORM_EPS) * gain_row).astype(BF16)


def _forget_lanes(cum, shift_row):
    lane = lax.broadcasted_iota(jnp.int32, cum.shape, 1)
    key_side = lane < 3 * N_HEADS
    query_side = (lane >= FOX_Q_LANE0) & (lane < FOX_Q_LANE0 + 3 * N_HEADS)
    cum2 = cum * LOG2E
    val = jnp.where(key_side, -cum2, cum2 - shift_row)
    c1, c2, c3 = _split3(val)
    piece = jnp.where(key_side, lane, lane - FOX_Q_LANE0) % 3
    sel = jnp.where(piece == 0, c1, jnp.where(piece == 1, c2, c3))
    return jnp.where(key_side | query_side, sel, jnp.zeros_like(sel))


def _proj_kernel(x_ref, gmix_ref, w_ref, wc_ref, wf_ref, bf_ref, shift_ref, bd_ref, gain_ref,
                 rc_ref, rs1_ref, rs2_ref, p_ref, faug_ref, kmean_ref, carry_ref, *, seq_tiles):
    tm = x_ref.shape[0]

    @pl.when(pl.program_id(0) % seq_tiles == 0)
    def _():
        carry_ref[...] = jnp.zeros_like(carry_ref)

    h = _rms_normed(x_ref[...], gmix_ref[...])

    z = _nt(h, wf_ref[...]) + bf_ref[...]
    lf = jnp.minimum(z, 0.0) - jnp.log1p(jnp.exp(-jnp.abs(z)))
    row = lax.broadcasted_iota(jnp.int32, (SEG_TILE, SEG_TILE), 0)
    col = lax.broadcasted_iota(jnp.int32, (SEG_TILE, SEG_TILE), 1)
    tri = jnp.where(col <= row, 1.0, 0.0).astype(BF16)
    pieces = _split3(lf)
    carry = carry_ref[...]
    parts = []
    for r in range(0, tm, SEG_TILE):
        part = carry
        for piece in pieces:
            part = part + _dot(tri, piece[r:r + SEG_TILE, :])
        carry = part[SEG_TILE - 1:SEG_TILE, :]
        parts.append(part)
    carry_ref[...] = carry
    faug_ref[...] = _forget_lanes(jnp.concatenate(parts, axis=0), shift_ref[...])

    def head_normed(t, k):
        seg = 1 if k == TILE_QC else 0
        sq = (t * t).astype(BF16)
        ms = jnp.concatenate(
            [_dot(sq[:, lo:lo + SEG_TILE], bd_ref[seg]) for lo in range(0, COL_TILE, SEG_TILE)],
            axis=1)
        return t * lax.rsqrt(ms + NORM_EPS) * gain_ref[k]

    def rotated(y):
        rc, rs1, rs2 = rc_ref[...], rs1_ref[...], rs2_ref[...]
        parts = []
        for g in range(COL_TILE // LANES):
            yg = y[:, g * LANES:(g + 1) * LANES]
            up = pltpu.roll(yg, LANES - ROPE_DIMS // 2, 1)
            dn = pltpu.roll(yg, ROPE_DIMS // 2, 1)
            parts.append(yg * rc + up * rs1 + dn * rs2)
        return jnp.concatenate(parts, axis=1)

    for k in range(N_COL_TILES):
        cols = slice(k * COL_TILE, (k + 1) * COL_TILE)
        t = _nt(h, wc_ref[...] if k == TILE_QC else w_ref[cols, :])
        if k in (TILE_QM, TILE_KM):
            t = rotated(head_normed(t, k))
        elif k in (TILE_QF, TILE_KF, TILE_QC):
            t = head_normed(t, k)
        p_ref[:, cols] = t.astype(BF16)
        if k == TILE_KM:
            for r in range(tm // MOBA_BLOCK):
                blk = t[r * MOBA_BLOCK:(r + 1) * MOBA_BLOCK, :]
                kmean_ref[0, pl.ds(r, 1), :] = jnp.mean(blk, axis=0, keepdims=True)


def _proj_call(x2, gmix, w_qkv, w_qc, wf, bf_row, shift_row, bd, gains, rc, rs1, rs2, seq, tm):
    n_tok = x2.shape[0]
    seq_tiles = seq // tm
    const2 = lambda i: (0, 0)
    const3 = lambda i: (0, 0, 0)
    return pl.pallas_call(
        functools.partial(_proj_kernel, seq_tiles=seq_tiles),
        grid=(n_tok // tm,),
        in_specs=[
            pl.BlockSpec((tm, D_MODEL), lambda i: (i, 0)),
            pl.BlockSpec((1, D_MODEL), const2),
            pl.BlockSpec((PROJ_WIDTH - COL_TILE, D_MODEL), const2),
            pl.BlockSpec((COL_TILE, D_MODEL), const2),
            pl.BlockSpec((LANES, D_MODEL), const2),
            pl.BlockSpec((1, LANES), const2),
            pl.BlockSpec((1, LANES), const2),
            pl.BlockSpec((2, SEG_TILE, SEG_TILE), const3),
            pl.BlockSpec((N_COL_TILES, 1, COL_TILE), const3),
            pl.BlockSpec((tm, LANES), lambda i: (i % seq_tiles, 0)),
            pl.BlockSpec((tm, LANES), lambda i: (i % seq_tiles, 0)),
            pl.BlockSpec((tm, LANES), lambda i: (i % seq_tiles, 0)),
        ],
        out_specs=[
            pl.BlockSpec((tm, PROJ_WIDTH), lambda i: (i, 0)),
            pl.BlockSpec((tm, LANES), lambda i: (i, 0)),
            pl.BlockSpec((1, tm // MOBA_BLOCK, COL_TILE), lambda i: (i, 0, 0)),
        ],
        out_shape=[
            jax.ShapeDtypeStruct((n_tok, PROJ_WIDTH), BF16),
            jax.ShapeDtypeStruct((n_tok, LANES), BF16),
            jax.ShapeDtypeStruct((n_tok // tm, tm // MOBA_BLOCK, COL_TILE), F32),
        ],
        scratch_shapes=[pltpu.VMEM((1, LANES), F32)],
        compiler_params=pltpu.CompilerParams(
            dimension_semantics=("arbitrary",), vmem_limit_bytes=VMEM_LIMIT),
        name="proj",
    )(x2, gmix, w_qkv, w_qc, wf, bf_row, shift_row, bd, gains, rc, rs1, rs2)


def _memkv_kernel(mem_ref, gmem_ref, w_ref, gk_ref, mk_ref, mv_ref):
    hm = _rms_normed(mem_ref[0], gmem_ref[...])
    kv = _dot(hm, w_ref[...])
    ks = []
    for h in range(MEM_HEADS):
        kh = kv[:, h * MEM_HEAD_DIM:(h + 1) * MEM_HEAD_DIM]
        msk = jnp.mean(kh * kh, axis=-1, keepdims=True)
        ks.append(kh * lax.rsqrt(msk + NORM_EPS) * gk_ref[...])
    mk_ref[0] = jnp.concatenate(ks, axis=1).astype(BF16)
    mv_ref[0] = kv[:, ATT_WIDTH:].astype(BF16)


def _memkv_call(mem, gmem, w_kv, gk):
    batch, mlen, _ = mem.shape
    return pl.pallas_call(
        _memkv_kernel,
        grid=(batch,),
        in_specs=[
            pl.BlockSpec((1, mlen, D_MODEL), lambda b: (b, 0, 0)),
            pl.BlockSpec((1, D_MODEL), lambda b: (0, 0)),
            pl.BlockSpec((D_MODEL, 2 * ATT_WIDTH), lambda b: (0, 0)),
            pl.BlockSpec((1, MEM_HEAD_DIM), lambda b: (0, 0)),
        ],
        out_specs=[
            pl.BlockSpec((1, mlen, ATT_WIDTH), lambda b: (b, 0, 0)),
            pl.BlockSpec((1, mlen, ATT_WIDTH), lambda b: (b, 0, 0)),
        ],
        out_shape=[jax.ShapeDtypeStruct((batch, mlen, ATT_WIDTH), BF16)] * 2,
        compiler_params=pltpu.CompilerParams(dimension_semantics=("parallel",)),
        name="memkv",
    )(mem, gmem, w_kv, gk)


def _lane_iota():
    return lax.broadcasted_iota(jnp.int32, (ATT_TILE, LANES), 1)


def _lane_range(first, count, dtype):
    lane = _lane_iota()
    return jnp.where((lane >= first) & (lane < first + count), 1.0, 0.0).astype(dtype)


def _head_lanes(c, dtype):
    return _lane_range(c * HEAD_DIM, HEAD_DIM, dtype)


def _fill_values(v_ref, vaug_ref):
    n_blocks = v_ref.shape[0] // ATT_TILE

    def fill(j, _):
        rows = pl.ds(pl.multiple_of(j * ATT_TILE, ATT_TILE), ATT_TILE)
        v = v_ref[rows, :].astype(F32)
        for c in range(2):
            own = _head_lanes(c, F32)
            vaug_ref[c, rows, :] = (v * own + (1.0 - own)).astype(BF16)
        return 0

    lax.fori_loop(0, n_blocks, fill, 0)


def _normalised(acc_ref, t):
    res = []
    for c in range(2):
        acc = acc_ref[2 * t + c]
        res.append(acc / pltpu.roll(acc, HEAD_DIM, 1))
    return jnp.where(_lane_iota() < HEAD_DIM, res[0], res[1])


def _causal_masks():
    span = KEY_GROUP * ATT_TILE
    diff = (lax.broadcasted_iota(jnp.int32, (ATT_TILE, span), 1)
            - lax.broadcasted_iota(jnp.int32, (ATT_TILE, span), 0))
    return [diff <= t * ATT_TILE for t in range(Q_SUB)]


def _attend_shifted(lhs_of_tile, kaug_ref, vaug_ref, acc_ref, s_ref, lhs_ref, o_ref, n_q):
    span = KEY_GROUP * ATT_TILE
    n_chain = 2 * Q_SUB

    def group_rows(g):
        return pl.ds(pl.multiple_of(g * span, span), span)

    def scores(g, slot):
        kb = kaug_ref[group_rows(g), :]
        for i in range(n_chain):
            s_ref[slot, i] = _nt(lhs_ref[i], kb)

    def scores_diagonal(g, slot):
        for i in range(n_chain):
            rows = pl.ds(pl.multiple_of(g * span, span), (i // 2 + 1) * ATT_TILE)
            s_ref[slot, i, :, 0:(i // 2 + 1) * ATT_TILE] = _nt(lhs_ref[i], kaug_ref[rows, :])

    def consume(g, slot):
        rows = group_rows(g)
        for i in range(n_chain):
            p = jnp.exp2(s_ref[slot, i]).astype(BF16)
            acc_ref[i] += _dot(p, vaug_ref[i % 2, rows, :])

    def consume_diagonal(g, slot):
        row = lax.broadcasted_iota(jnp.int32, (ATT_TILE, ATT_TILE), 0)
        col = lax.broadcasted_iota(jnp.int32, (ATT_TILE, ATT_TILE), 1)
        for i in range(n_chain):
            t = i // 2
            lo = t * ATT_TILE
            parts = []
            if t:
                parts.append(jnp.exp2(s_ref[slot, i, :, 0:lo]).astype(BF16))
            s_diag = jnp.where(col <= row, s_ref[slot, i, :, lo:lo + ATT_TILE], NEG)
            parts.append(jnp.exp2(s_diag).astype(BF16))
            p = parts[0] if len(parts) == 1 else jnp.concatenate(parts, axis=1)
            rows = pl.ds(pl.multiple_of(g * span, span), lo + ATT_TILE)
            acc_ref[i] += _dot(p, vaug_ref[i % 2, rows, :])

    def emit(qi):
        for t in range(Q_SUB):
            rows = pl.ds(pl.multiple_of((qi * Q_SUB + t) * ATT_TILE, ATT_TILE), ATT_TILE)
            o_ref[rows, :] = _normalised(acc_ref, t).astype(BF16)

    for i, operand in enumerate(lhs_of_tile(0)):
        lhs_ref[i] = operand
    scores_diagonal(0, DIAG_SLOT)

    def tile(qi, _):
        acc_ref[...] = jnp.zeros_like(acc_ref)
        n_full = qi

        @pl.when(n_full > 0)
        def _():
            def pair(p, _):
                g = 2 * p
                consume(g, 0)
                scores(g + 1, 1)
                consume(g + 1, 1)
                scores(g + 2, 0)
                return 0

            n_loop = n_full - 1
            lax.fori_loop(0, n_loop // 2, pair, 0)

            @pl.when(lax.rem(n_loop, 2) == 1)
            def _():
                consume(n_loop - 1, 0)
                scores(n_loop, 1)

            consume(n_full - 1, lax.rem(n_full - 1, 2))
            scores_diagonal(n_full, DIAG_SLOT)

        consume_diagonal(n_full, DIAG_SLOT)
        nxt = lhs_of_tile(jnp.minimum(qi + 1, n_q - 1))
        kb = kaug_ref[group_rows(0), :]
        for i in range(n_chain):
            lhs_ref[i] = nxt[i]
            s_ref[0, i] = _nt(nxt[i], kb)
        emit(qi)
        return 0

    lax.fori_loop(0, n_q, tile, 0)


def _attend_running_max(lhs_of_tile, kaug_ref, vaug_ref, acc_ref, o_ref, n_q):
    span = KEY_GROUP * ATT_TILE
    n_chain = 2 * Q_SUB
    masks = _causal_masks()

    def tile(qi, _):
        lhs = lhs_of_tile(qi)

        def step(i, g, m, mask):
            rows = pl.ds(pl.multiple_of(g * span, span), span)
            s = _nt(lhs[i], kaug_ref[rows, :])
            if mask is not None:
                s = jnp.where(mask, s, NEG)
            m_new = jnp.maximum(m, jnp.max(s, axis=1, keepdims=True))
            p = jnp.exp2(s - m_new).astype(BF16)
            acc_ref[i] = jnp.exp2(m - m_new) * acc_ref[i] + _dot(p, vaug_ref[i % 2, rows, :])
            return m_new

        acc_ref[...] = jnp.zeros_like(acc_ref)
        m0 = jnp.full((ATT_TILE, 1), -jnp.inf, F32)
        ms = lax.fori_loop(
            0, qi, lambda g, ms: tuple(step(i, g, ms[i], None) for i in range(n_chain)),
            (m0,) * n_chain)
        for i in range(n_chain):
            step(i, qi, ms[i], masks[i // 2])
        for t in range(Q_SUB):
            rows = pl.ds(pl.multiple_of((qi * Q_SUB + t) * ATT_TILE, ATT_TILE), ATT_TILE)
            o_ref[rows, :] = _normalised(acc_ref, t).astype(BF16)
        return 0

    lax.fori_loop(0, n_q, tile, 0)


def _attend(lhs_of_tile, kaug_ref, vaug_ref, acc_ref, s_ref, lhs_ref, o_ref, n_q, bounded):
    if bounded:
        _attend_shifted(lhs_of_tile, kaug_ref, vaug_ref, acc_ref, s_ref, lhs_ref, o_ref, n_q)
    else:
        _attend_running_max(lhs_of_tile, kaug_ref, vaug_ref, acc_ref, o_ref, n_q)


def _attention_scratch(seq):
    n_chain = 2 * Q_SUB
    return [pltpu.VMEM((seq, 2 * LANES), BF16),
            pltpu.VMEM((2, seq, LANES), BF16),
            pltpu.VMEM((n_chain, ATT_TILE, LANES), F32),
            pltpu.VMEM((3, n_chain, ATT_TILE, KEY_GROUP * ATT_TILE), F32),
            pltpu.VMEM((n_chain, ATT_TILE, 2 * LANES), BF16)]


def _moba_kernel(q_ref, k_ref, v_ref, km_ref, shift_ref, o_ref, kaug_ref, vaug_ref,
                 acc_ref, s_ref, lhs_ref, *, bounded):
    seq = k_ref.shape[0]
    n_blocks = seq // MOBA_BLOCK
    lane = _lane_iota()

    kaug_ref[:, 0:LANES] = k_ref[...]

    def fill(j, _):
        rows = pl.ds(pl.multiple_of(j * MOBA_BLOCK, MOBA_BLOCK), MOBA_BLOCK)
        onehot = (lane == j) | (lane == MOBA_SHIFT_LANE)
        kaug_ref[rows, LANES:2 * LANES] = jnp.where(onehot, 1.0, 0.0).astype(BF16)
        return 0

    lax.fori_loop(0, n_blocks, fill, 0)
    _fill_values(v_ref, vaug_ref)

    def lhs_of_tile(tile):
        km = km_ref[0]
        km = jnp.concatenate([km, jnp.zeros((LANES - n_blocks, LANES), F32)], axis=0)
        km_hi = km.astype(BF16)
        km_lo = (km - km_hi.astype(F32)).astype(BF16)
        shift = shift_ref[:, MOBA_SHIFT_LANE:MOBA_SHIFT_LANE + 1] if bounded else 0.0
        blk = lax.broadcasted_iota(jnp.int32, (n_blocks, ATT_TILE), 0)
        blk_f = blk.astype(F32)
        pad_row = lax.broadcasted_iota(jnp.int32, (LANES - n_blocks, ATT_TILE), 0) + n_blocks
        pad_bias = jnp.where(pad_row == MOBA_SHIFT_LANE, -shift, NEG)
        lhs = []
        for t in range(Q_SUB):
            own = tile * Q_SUB + t
            q2 = q_ref[pl.ds(pl.multiple_of(own * ATT_TILE, ATT_TILE), ATT_TILE), :]
            for c in range(2):
                qc = q2 * _head_lanes(c, BF16)
                gate = (_nt(km_hi, qc) + _nt(km_lo, qc))[0:n_blocks]
                g = jnp.where(blk < own, gate, -jnp.inf)
                sel = blk == own
                for _ in range(MOBA_TOPK):
                    mx = jnp.max(g, axis=0, keepdims=True)
                    first = jnp.min(jnp.where(g == mx, blk_f, float(LANES)), axis=0,
                                    keepdims=True)
                    pick = (blk_f == first) & (mx > -jnp.inf)
                    sel = sel | pick
                    g = jnp.where(pick, -jnp.inf, g)
                bias = jnp.concatenate([jnp.where(sel, 0.0, NEG), pad_bias], axis=0)
                lhs.append(jnp.concatenate([qc, bias.T.astype(BF16)], axis=1))
        return lhs

    _attend(lhs_of_tile, kaug_ref, vaug_ref, acc_ref, s_ref, lhs_ref, o_ref,
            seq // (Q_SUB * ATT_TILE), bounded)


def _moba_call(proj, kmean, shift_row, batch, seq, bounded):
    n_pairs = N_HEADS // 2
    q0 = (TILE_QM * COL_TILE) // LANES
    k0 = (TILE_KM * COL_TILE) // LANES
    v0 = k0 + COL_TILE // LANES
    return pl.pallas_call(
        functools.partial(_moba_kernel, bounded=bounded),
        grid=(batch, n_pairs),
        in_specs=[
            pl.BlockSpec((seq, LANES), lambda b, p: (b, q0 + p)),
            pl.BlockSpec((seq, LANES), lambda b, p: (b, k0 + p)),
            pl.BlockSpec((seq, LANES), lambda b, p: (b, v0 + p)),
            pl.BlockSpec((1, seq // MOBA_BLOCK, LANES), lambda b, p: (b, 0, p)),
            pl.BlockSpec((1, LANES), lambda b, p: (0, 0)),
        ],
        out_specs=pl.BlockSpec((seq, LANES), lambda b, p: (b, p)),
        out_shape=jax.ShapeDtypeStruct((batch * seq, ATT_WIDTH), BF16),
        scratch_shapes=_attention_scratch(seq),
        compiler_params=pltpu.CompilerParams(
            dimension_semantics=("parallel", "parallel"), vmem_limit_bytes=VMEM_LIMIT),
        name="moba_bounded" if bounded else "moba_general",
    )(proj, proj, proj, kmean, shift_row)


def _fox_kernel(q_ref, k_ref, v_ref, faug_ref, o_ref, kaug_ref, vaug_ref,
                acc_ref, s_ref, lhs_ref, *, bounded):
    pair = pl.program_id(1)
    n_extra = 3 * N_HEADS
    seq = k_ref.shape[0]

    kaug_ref[:, 0:LANES] = k_ref[...]

    def fill(j, _):
        rows = pl.ds(pl.multiple_of(j * ATT_TILE, ATT_TILE), ATT_TILE)
        lane = _lane_iota()
        ones = (lane >= FOX_Q_LANE0) & (lane < FOX_Q_LANE0 + n_extra)
        f = faug_ref[rows, :].astype(F32)
        kaug_ref[rows, LANES:2 * LANES] = (
            f * (lane < n_extra).astype(F32) + ones.astype(F32)).astype(BF16)
        return 0

    lax.fori_loop(0, seq // ATT_TILE, fill, 0)
    _fill_values(v_ref, vaug_ref)

    def lhs_of_tile(tile):
        lhs = []
        for t in range(Q_SUB):
            rows = pl.ds(pl.multiple_of((tile * Q_SUB + t) * ATT_TILE, ATT_TILE), ATT_TILE)
            q2 = q_ref[rows, :]
            fq = faug_ref[rows, :]
            for c in range(2):
                qc = q2 * _head_lanes(c, BF16)
                k_first = 3 * (2 * pair + c)
                extra = _lane_range(k_first, 3, BF16)
                if bounded:
                    extra = extra + fq * _lane_range(FOX_Q_LANE0 + k_first, 3, BF16)
                lhs.append(jnp.concatenate([qc, extra], axis=1))
        return lhs

    _attend(lhs_of_tile, kaug_ref, vaug_ref, acc_ref, s_ref, lhs_ref, o_ref,
            seq // (Q_SUB * ATT_TILE), bounded)


def _fox_call(proj, faug, batch, seq, bounded):
    n_pairs = N_HEADS // 2
    q0 = (TILE_QF * COL_TILE) // LANES
    k0 = (TILE_KF * COL_TILE) // LANES
    v0 = k0 + COL_TILE // LANES
    return pl.pallas_call(
        functools.partial(_fox_kernel, bounded=bounded),
        grid=(batch, n_pairs),
        in_specs=[
            pl.BlockSpec((seq, LANES), lambda b, p: (b, q0 + p)),
            pl.BlockSpec((seq, LANES), lambda b, p: (b, k0 + p)),
            pl.BlockSpec((seq, LANES), lambda b, p: (b, v0 + p)),
            pl.BlockSpec((seq, LANES), lambda b, p: (b, 0)),
        ],
        out_specs=pl.BlockSpec((seq, LANES), lambda b, p: (b, p)),
        out_shape=jax.ShapeDtypeStruct((batch * seq, ATT_WIDTH), BF16),
        scratch_shapes=_attention_scratch(seq),
        compiler_params=pltpu.CompilerParams(
            dimension_semantics=("parallel", "parallel"), vmem_limit_bytes=VMEM_LIMIT),
        name="fox_bounded" if bounded else "fox_general",
    )(proj, proj, proj, faug)


def _merge_kernel(x_ref, gmix_ref, qc_ref, ym_ref, yf_ref, mk_ref, mv_ref,
                  wg_ref, wbm_ref, wbf_ref, wbc_ref, wo_ref, o_ref):
    x = x_ref[...]
    h = _rms_normed(x, gmix_ref[...])
    qc = qc_ref[...]
    mk = mk_ref[0]
    mv = mv_ref[0]
    ys = []
    for hd in range(MEM_HEADS):
        cols = slice(hd * MEM_HEAD_DIM, (hd + 1) * MEM_HEAD_DIM)
        s = _nt(qc[:, cols], mk[:, cols])
        p = jnp.exp2(s - jnp.max(s, axis=1, keepdims=True))
        y = _dot(p.astype(BF16), mv[:, cols]) / jnp.sum(p, axis=1, keepdims=True)
        ys.append(y)
    yc = jnp.concatenate(ys, axis=1).astype(BF16)
    merged = None
    for i, (y, wb_ref) in enumerate(((ym_ref[...], wbm_ref), (yf_ref[...], wbf_ref),
                                     (yc, wbc_ref))):
        gate = jax.nn.sigmoid(_nt(h, wg_ref[i * D_MODEL:(i + 1) * D_MODEL, :]))
        term = gate * _dot(y, wb_ref[...])
        merged = term if merged is None else merged + term
    o_ref[...] = x + _dot(merged.astype(BF16), wo_ref[...])


def _merge_call(x2, gmix, proj, y_m, y_f, mk, mv, wg, wbm, wbf, wbc, wo, seq, tm):
    n_tok = x2.shape[0]
    mlen = mk.shape[1]
    seq_tiles = seq // tm
    const = lambda i: (0, 0)
    return pl.pallas_call(
        _merge_kernel,
        grid=(n_tok // tm,),
        in_specs=[
            pl.BlockSpec((tm, D_MODEL), lambda i: (i, 0)),
            pl.BlockSpec((1, D_MODEL), const),
            pl.BlockSpec((tm, ATT_WIDTH), lambda i: (i, TILE_QC)),
            pl.BlockSpec((tm, ATT_WIDTH), lambda i: (i, 0)),
            pl.BlockSpec((tm, ATT_WIDTH), lambda i: (i, 0)),
            pl.BlockSpec((1, mlen, ATT_WIDTH), lambda i: (i // seq_tiles, 0, 0)),
            pl.BlockSpec((1, mlen, ATT_WIDTH), lambda i: (i // seq_tiles, 0, 0)),
            pl.BlockSpec((3 * D_MODEL, D_MODEL), const),
            pl.BlockSpec((ATT_WIDTH, D_MODEL), const),
            pl.BlockSpec((ATT_WIDTH, D_MODEL), const),
            pl.BlockSpec((ATT_WIDTH, D_MODEL), const),
            pl.BlockSpec((D_MODEL, D_MODEL), const),
        ],
        out_specs=pl.BlockSpec((tm, D_MODEL), lambda i: (i, 0)),
        out_shape=jax.ShapeDtypeStruct((n_tok, D_MODEL), F32),
        compiler_params=pltpu.CompilerParams(
            dimension_semantics=("parallel",), vmem_limit_bytes=VMEM_LIMIT),
        name="merge",
    )(x2, gmix, proj, y_m, y_f, mk, mv, wg, wbm, wbf, wbc, wo)


def _ffn_kernel(x_ref, g_ref, wg_ref, wu_ref, wd_ref, o_ref):
    x = x_ref[...]
    h = _rms_normed(x, g_ref[...])
    d_ff = wg_ref.shape[1]
    out = x
    for lo in range(0, d_ff, FFN_CHUNK):
        cols = slice(lo, min(lo + FFN_CHUNK, d_ff))
        g = _dot(h, wg_ref[:, cols])
        u = _dot(h, wu_ref[:, cols])
        a = (g * jax.nn.sigmoid(g) * u).astype(BF16)
        out = out + _dot(a, wd_ref[cols, :])
    o_ref[...] = out


def _ffn_call(x1, g_ffn, wg, wu, wd, tm):
    n_tok = x1.shape[0]
    d_ff = wg.shape[1]
    const = lambda i: (0, 0)
    resident = dict(pipeline_mode=pl.Buffered(1))
    return pl.pallas_call(
        _ffn_kernel,
        grid=(n_tok // tm,),
        in_specs=[
            pl.BlockSpec((tm, D_MODEL), lambda i: (i, 0)),
            pl.BlockSpec((1, D_MODEL), const),
            pl.BlockSpec((D_MODEL, d_ff), const, **resident),
            pl.BlockSpec((D_MODEL, d_ff), const, **resident),
            pl.BlockSpec((d_ff, D_MODEL), const, **resident),
        ],
        out_specs=pl.BlockSpec((tm, D_MODEL), lambda i: (i, 0)),
        out_shape=jax.ShapeDtypeStruct((n_tok, D_MODEL), F32),
        compiler_params=pltpu.CompilerParams(
            dimension_semantics=("parallel",), vmem_limit_bytes=VMEM_LIMIT),
        name="ffn",
    )(x1, g_ffn, wg, wu, wd)


def _tail_kernel(x_ref, gmix_ref, qc_ref, ym_ref, yf_ref, mk_ref, mv_ref, wg_ref, wbm_ref,
                 wbf_ref, wbc_ref, wo_ref, gffn_ref, w1_ref, w2_ref, w3_ref, o_ref, x1_ref):
    _merge_kernel(x_ref, gmix_ref, qc_ref, ym_ref, yf_ref, mk_ref, mv_ref,
                  wg_ref, wbm_ref, wbf_ref, wbc_ref, wo_ref, x1_ref)
    _ffn_kernel(x1_ref, gffn_ref, w1_ref, w2_ref, w3_ref, o_ref)


def _tail_call(x2, gmix, proj, y_m, y_f, mk, mv, wg, wbm, wbf, wbc, wo, g_ffn, w1, w2, w3,
               seq, tm):
    n_tok = x2.shape[0]
    mlen = mk.shape[1]
    d_ff = w1.shape[1]
    seq_tiles = seq // tm
    const = lambda i: (0, 0)
    resident = dict(pipeline_mode=pl.Buffered(1))
    return pl.pallas_call(
        _tail_kernel,
        grid=(n_tok // tm,),
        in_specs=[
            pl.BlockSpec((tm, D_MODEL), lambda i: (i, 0)),
            pl.BlockSpec((1, D_MODEL), const),
            pl.BlockSpec((tm, ATT_WIDTH), lambda i: (i, TILE_QC)),
            pl.BlockSpec((tm, ATT_WIDTH), lambda i: (i, 0)),
            pl.BlockSpec((tm, ATT_WIDTH), lambda i: (i, 0)),
            pl.BlockSpec((1, mlen, ATT_WIDTH), lambda i: (i // seq_tiles, 0, 0)),
            pl.BlockSpec((1, mlen, ATT_WIDTH), lambda i: (i // seq_tiles, 0, 0)),
            pl.BlockSpec((3 * D_MODEL, D_MODEL), const, **resident),
            pl.BlockSpec((ATT_WIDTH, D_MODEL), const, **resident),
            pl.BlockSpec((ATT_WIDTH, D_MODEL), const, **resident),
            pl.BlockSpec((ATT_WIDTH, D_MODEL), const, **resident),
            pl.BlockSpec((D_MODEL, D_MODEL), const, **resident),
            pl.BlockSpec((1, D_MODEL), const),
            pl.BlockSpec((D_MODEL, d_ff), const, **resident),
            pl.BlockSpec((D_MODEL, d_ff), const, **resident),
            pl.BlockSpec((d_ff, D_MODEL), const, **resident),
        ],
        out_specs=pl.BlockSpec((tm, D_MODEL), lambda i: (i, 0)),
        out_shape=jax.ShapeDtypeStruct((n_tok, D_MODEL), F32),
        scratch_shapes=[pltpu.VMEM((tm, D_MODEL), F32)],
        compiler_params=pltpu.CompilerParams(
            dimension_semantics=("parallel",), vmem_limit_bytes=VMEM_LIMIT),
        name="tail",
    )(x2, gmix, proj, y_m, y_f, mk, mv, wg, wbm, wbf, wbc, wo, g_ffn, w1, w2, w3)


@functools.lru_cache(maxsize=None)
def _rope_tables(seq):
    half = ROPE_DIMS // 2
    inv_freq = 1.0 / (ROPE_THETA ** (np.arange(half, dtype=np.float64) * 2.0 / ROPE_DIMS))
    ang = np.arange(seq, dtype=np.float64)[:, None] * inv_freq[None, :]
    cos, sin = np.cos(ang), np.sin(ang)
    d = np.arange(LANES) % HEAD_DIM
    rc = np.where(d[None, :] < ROPE_DIMS, cos[:, d % half], 1.0)
    rs1 = np.where(d[None, :] < half, -sin[:, d % half], 0.0)
    rs2 = np.where((d[None, :] >= half) & (d[None, :] < ROPE_DIMS), sin[:, d % half], 0.0)
    return tuple(np.asarray(t, np.float32) for t in (rc, rs1, rs2))


@functools.lru_cache(maxsize=None)
def _segment_mean_matrices():
    def blockdiag(width):
        return np.kron(np.eye(SEG_TILE // width), np.ones((width, width))) / width
    return np.stack([blockdiag(HEAD_DIM), blockdiag(MEM_HEAD_DIM)]).astype(np.float32)


def _score_bound(g_q, g_k):
    return (BOUND_SLACK * HEAD_DIM ** 0.5 * jnp.max(jnp.abs(g_q)) * jnp.max(jnp.abs(g_k)))


def _lane_row(value, first, count):
    lane = jnp.arange(LANES)
    return jnp.where((lane >= first) & (lane < first + count), value, 0.0).astype(F32)[None, :]


def kernel(x, mem, g_mix, w_in, b_forget, g_q_moba, g_k_moba, g_q_fox, g_k_fox, g_q_mem, g_k_mem,
           g_mem, w_mem_kv, w_br_moba, w_br_fox, w_br_mem, w_out, g_ffn, w_gate, w_up, w_down):
    batch, seq, _ = x.shape
    depth = g_mix.shape[0]
    n_tok = batch * seq
    assert seq % (KEY_GROUP * ATT_TILE) == 0 and seq // MOBA_BLOCK < MOBA_SHIFT_LANE
    n_extra = 3 * N_HEADS

    rc, rs1, rs2 = (jnp.asarray(t) for t in _rope_tables(seq))
    bd = jnp.asarray(_segment_mean_matrices()).astype(BF16)
    ones = jnp.ones((COL_TILE,), F32)

    x2 = x.reshape(n_tok, D_MODEL)
    for layer in range(depth):
        w = w_in[layer]
        gates0 = 7 * ATT_WIDTH + N_HEADS
        qc0 = 6 * ATT_WIDTH + N_HEADS
        w_t = jnp.swapaxes(w, 0, 1)
        w_qkv = w_t[:6 * ATT_WIDTH].astype(BF16)
        w_qc = w_t[qc0:gates0].astype(BF16)
        w_gates = w_t[gates0:].astype(BF16)
        wf3 = jnp.repeat(w_t[6 * ATT_WIDTH:qc0], 3, axis=0)
        gap = jnp.zeros((FOX_Q_LANE0 - n_extra, D_MODEL), F32)
        tail = jnp.zeros((LANES - FOX_Q_LANE0 - n_extra, D_MODEL), F32)
        wf = jnp.concatenate([wf3, gap, wf3, tail], axis=0).astype(BF16)
        bf3 = jnp.repeat(b_forget[layer], 3)
        bf_rep = jnp.concatenate([bf3, gap[:, 0], bf3, tail[:, 0]])[None, :]

        att_scale = HEAD_DIM ** -0.5 * LOG2E
        gains = [ones] * N_COL_TILES
        gains[TILE_QM] = jnp.tile(g_q_moba[layer], N_HEADS) * att_scale
        gains[TILE_KM] = jnp.tile(g_k_moba[layer], N_HEADS)
        gains[TILE_QF] = jnp.tile(g_q_fox[layer], N_HEADS) * att_scale
        gains[TILE_KF] = jnp.tile(g_k_fox[layer], N_HEADS)
        gains[TILE_QC] = jnp.tile(g_q_mem[layer], MEM_HEADS) * (MEM_HEAD_DIM ** -0.5 * LOG2E)
        gains = jnp.stack(gains)[:, None, :]

        bound_m = _score_bound(g_q_moba[layer], g_k_moba[layer])
        bound_f = _score_bound(g_q_fox[layer], g_k_fox[layer])
        shift_m = _lane_row(bound_m * LOG2E, MOBA_SHIFT_LANE, 1)
        shift_f = _lane_row(bound_f * LOG2E, FOX_Q_LANE0, n_extra)

        proj, faug, kmean = _proj_call(x2, g_mix[layer][None, :], w_qkv, w_qc, wf, bf_rep, shift_f,
                                       bd, gains, rc, rs1, rs2, seq, tm=TOKEN_TILE)
        kmean = kmean.reshape(batch, seq // MOBA_BLOCK, COL_TILE)
        mk, mv = _memkv_call(mem, g_mem[layer][None, :], w_mem_kv[layer].astype(BF16),
                             g_k_mem[layer][None, :])
        def attention(bounded):
            return (_moba_call(proj, kmean, shift_m, batch, seq, bounded),
                    _fox_call(proj, faug, batch, seq, bounded))

        y_m, y_f = lax.cond(2.0 * jnp.maximum(bound_m, bound_f) <= MAX_SHIFT_NATS,
                            lambda: attention(True), lambda: attention(False))
        x2 = _tail_call(x2, g_mix[layer][None, :], proj, y_m, y_f, mk, mv, w_gates,
                        w_br_moba[layer].astype(BF16), w_br_fox[layer].astype(BF16),
                        w_br_mem[layer].astype(BF16), w_out[layer].astype(BF16),
                        g_ffn[layer][None, :], w_gate[layer].astype(BF16),
                        w_up[layer].astype(BF16), w_down[layer].astype(BF16), seq, tm=TAIL_TILE)
    return x2.reshape(batch, seq, D_MODEL)
```

```python
import functools
import math

import numpy as np
import jax
import jax.numpy as jnp
from jax import lax
from jax.experimental import pallas as pl
from jax.experimental.pallas import tpu as pltpu

F32 = jnp.float32
BF16 = jnp.bfloat16

D_MODEL = 1024
HEAD_DIM = 64
N_HEADS = 8
MEM_HEADS = 4
MEM_HEAD_DIM = 128
ATT_WIDTH = 512
MOBA_BLOCK = 256
MOBA_TOPK = 3
ROPE_THETA = 500000.0
ROPE_DIMS = 16
NORM_EPS = 1e-6
NEG = -1e30
LOG2E = math.log2(math.e)

LANES = 128
COL_TILE = 512
SEG_TILE = 256
N_COL_TILES = 7
PROJ_WIDTH = COL_TILE * N_COL_TILES
TILE_QM, TILE_KM, TILE_QF, TILE_KF, TILE_QC = 0, 1, 3, 4, 6
ATT_TILE = 256
KEY_GROUP = 4
Q_SUB = KEY_GROUP
DIAG_SLOT = 2
FFN_CHUNK = 1024
TOKEN_TILE = 1024
VMEM_LIMIT = 56 * 1024 * 1024

FOX_Q_LANE0 = 32
MOBA_SHIFT_LANE = LANES - 1
MAX_SHIFT_NATS = 60.0
BOUND_SLACK = 1.02


def _nt(a, b):
    return lax.dot_general(a, b, (((1,), (1,)), ((), ())), preferred_element_type=F32)


def _dot(a, b):
    return jnp.dot(a, b, preferred_element_type=F32)


def _split3(v):
    p1 = v.astype(BF16)
    r1 = v - p1.astype(F32)
    p2 = r1.astype(BF16)
    p3 = (r1 - p2.astype(F32)).astype(BF16)
    return p1, p2, p3


def _rms_normed(x, gain_row):
    ms = jnp.mean(x * x, axis=-1, keepdims=True)
    return (x * lax.rsqrt(ms + NORM_EPS) * gain_row).astype(BF16)


def _forget_lanes(cum, shift_row):
    lane = lax.broadcasted_iota(jnp.int32, cum.shape, 1)
    key_side = lane < 3 * N_HEADS
    query_side = (lane >= FOX_Q_LANE0) & (lane < FOX_Q_LANE0 + 3 * N_HEADS)
    cum2 = cum * LOG2E
    val = jnp.where(key_side, -cum2, cum2 - shift_row)
    c1, c2, c3 = _split3(val)
    piece = jnp.where(key_side, lane, lane - FOX_Q_LANE0) % 3
    sel = jnp.where(piece == 0, c1, jnp.where(piece == 1, c2, c3))
    return jnp.where(key_side | query_side, sel, jnp.zeros_like(sel))


def _proj_kernel(x_ref, gmix_ref, w_ref, wc_ref, wf_ref, bf_ref, shift_ref, bd_ref, gain_ref,
                 rc_ref, rs1_ref, rs2_ref, p_ref, faug_ref, kmean_ref, carry_ref, *, seq_tiles):
    tm = x_ref.shape[0]

    @pl.when(pl.program_id(0) % seq_tiles == 0)
    def _():
        carry_ref[...] = jnp.zeros_like(carry_ref)

    h = _rms_normed(x_ref[...], gmix_ref[...])

    z = _nt(h, wf_ref[...]) + bf_ref[...]
    lf = jnp.minimum(z, 0.0) - jnp.log1p(jnp.exp(-jnp.abs(z)))
    row = lax.broadcasted_iota(jnp.int32, (SEG_TILE, SEG_TILE), 0)
    col = lax.broadcasted_iota(jnp.int32, (SEG_TILE, SEG_TILE), 1)
    tri = jnp.where(col <= row, 1.0, 0.0).astype(BF16)
    pieces = _split3(lf)
    carry = carry_ref[...]
    parts = []
    for r in range(0, tm, SEG_TILE):
        part = carry
        for piece in pieces:
            part = part + _dot(tri, piece[r:r + SEG_TILE, :])
        carry = part[SEG_TILE - 1:SEG_TILE, :]
        parts.append(part)
    carry_ref[...] = carry
    faug_ref[...] = _forget_lanes(jnp.concatenate(parts, axis=0), shift_ref[...])

    def head_normed(t, k):
        seg = 1 if k == TILE_QC else 0
        sq = (t * t).astype(BF16)
        ms = jnp.concatenate(
            [_dot(sq[:, lo:lo + SEG_TILE], bd_ref[seg]) for lo in range(0, COL_TILE, SEG_TILE)],
            axis=1)
        return t * lax.rsqrt(ms + NORM_EPS) * gain_ref[k]

    def rotated(y):
        rc, rs1, rs2 = rc_ref[...], rs1_ref[...], rs2_ref[...]
        parts = []
        for g in range(COL_TILE // LANES):
            yg = y[:, g * LANES:(g + 1) * LANES]
            up = pltpu.roll(yg, LANES - ROPE_DIMS // 2, 1)
            dn = pltpu.roll(yg, ROPE_DIMS // 2, 1)
            parts.append(yg * rc + up * rs1 + dn * rs2)
        return jnp.concatenate(parts, axis=1)

    for k in range(N_COL_TILES):
        cols = slice(k * COL_TILE, (k + 1) * COL_TILE)
        t = _nt(h, wc_ref[...] if k == TILE_QC else w_ref[cols, :])
        if k in (TILE_QM, TILE_KM):
            t = rotated(head_normed(t, k))
        elif k in (TILE_QF, TILE_KF, TILE_QC):
            t = head_normed(t, k)
        p_ref[:, cols] = t.astype(BF16)
        if k == TILE_KM:
            for r in range(tm // MOBA_BLOCK):
                blk = t[r * MOBA_BLOCK:(r + 1) * MOBA_BLOCK, :]
                kmean_ref[0, pl.ds(r, 1), :] = jnp.mean(blk, axis=0, keepdims=True)


def _proj_call(x2, gmix, w_qkv, w_qc, wf, bf_row, shift_row, bd, gains, rc, rs1, rs2, seq, tm):
    n_tok = x2.shape[0]
    seq_tiles = seq // tm
    const2 = lambda i: (0, 0)
    const3 = lambda i: (0, 0, 0)
    return pl.pallas_call(
        functools.partial(_proj_kernel, seq_tiles=seq_tiles),
        grid=(n_tok // tm,),
        in_specs=[
            pl.BlockSpec((tm, D_MODEL), lambda i: (i, 0)),
            pl.BlockSpec((1, D_MODEL), const2),
            pl.BlockSpec((PROJ_WIDTH - COL_TILE, D_MODEL), const2),
            pl.BlockSpec((COL_TILE, D_MODEL), const2),
            pl.BlockSpec((LANES, D_MODEL), const2),
            pl.BlockSpec((1, LANES), const2),
            pl.BlockSpec((1, LANES), const2),
            pl.BlockSpec((2, SEG_TILE, SEG_TILE), const3),
            pl.BlockSpec((N_COL_TILES, 1, COL_TILE), const3),
            pl.BlockSpec((tm, LANES), lambda i: (i % seq_tiles, 0)),
            pl.BlockSpec((tm, LANES), lambda i: (i % seq_tiles, 0)),
            pl.BlockSpec((tm, LANES), lambda i: (i % seq_tiles, 0)),
        ],
        out_specs=[
            pl.BlockSpec((tm, PROJ_WIDTH), lambda i: (i, 0)),
            pl.BlockSpec((tm, LANES), lambda i: (i, 0)),
            pl.BlockSpec((1, tm // MOBA_BLOCK, COL_TILE), lambda i: (i, 0, 0)),
        ],
        out_shape=[
            jax.ShapeDtypeStruct((n_tok, PROJ_WIDTH), BF16),
            jax.ShapeDtypeStruct((n_tok, LANES), BF16),
            jax.ShapeDtypeStruct((n_tok // tm, tm // MOBA_BLOCK, COL_TILE), F32),
        ],
        scratch_shapes=[pltpu.VMEM((1, LANES), F32)],
        compiler_params=pltpu.CompilerParams(
            dimension_semantics=("arbitrary",), vmem_limit_bytes=VMEM_LIMIT),
        name="proj",
    )(x2, gmix, w_qkv, w_qc, wf, bf_row, shift_row, bd, gains, rc, rs1, rs2)


def _memkv_kernel(mem_ref, gmem_ref, w_ref, gk_ref, mk_ref, mv_ref):
    hm = _rms_normed(mem_ref[0], gmem_ref[...])
    kv = _dot(hm, w_ref[...])
    ks = []
    for h in range(MEM_HEADS):
        kh = kv[:, h * MEM_HEAD_DIM:(h + 1) * MEM_HEAD_DIM]
        msk = jnp.mean(kh * kh, axis=-1, keepdims=True)
        ks.append(kh * lax.rsqrt(msk + NORM_EPS) * gk_ref[...])
    mk_ref[0] = jnp.concatenate(ks, axis=1).astype(BF16)
    mv_ref[0] = kv[:, ATT_WIDTH:].astype(BF16)


def _memkv_call(mem, gmem, w_kv, gk):
    batch, mlen, _ = mem.shape
    return pl.pallas_call(
        _memkv_kernel,
        grid=(batch,),
        in_specs=[
            pl.BlockSpec((1, mlen, D_MODEL), lambda b: (b, 0, 0)),
            pl.BlockSpec((1, D_MODEL), lambda b: (0, 0)),
            pl.BlockSpec((D_MODEL, 2 * ATT_WIDTH), lambda b: (0, 0)),
            pl.BlockSpec((1, MEM_HEAD_DIM), lambda b: (0, 0)),
        ],
        out_specs=[
            pl.BlockSpec((1, mlen, ATT_WIDTH), lambda b: (b, 0, 0)),
            pl.BlockSpec((1, mlen, ATT_WIDTH), lambda b: (b, 0, 0)),
        ],
        out_shape=[jax.ShapeDtypeStruct((batch, mlen, ATT_WIDTH), BF16)] * 2,
        compiler_params=pltpu.CompilerParams(dimension_semantics=("parallel",)),
        name="memkv",
    )(mem, gmem, w_kv, gk)


def _lane_iota():
    return lax.broadcasted_iota(jnp.int32, (ATT_TILE, LANES), 1)


def _lane_range(first, count, dtype):
    lane = _lane_iota()
    return jnp.where((lane >= first) & (lane < first + count), 1.0, 0.0).astype(dtype)


def _head_lanes(c, dtype):
    return _lane_range(c * HEAD_DIM, HEAD_DIM, dtype)


def _fill_values(v_ref, vaug_ref):
    n_blocks = v_ref.shape[0] // ATT_TILE

    def fill(j, _):
        rows = pl.ds(pl.multiple_of(j * ATT_TILE, ATT_TILE), ATT_TILE)
        v = v_ref[rows, :].astype(F32)
        for c in range(2):
            own = _head_lanes(c, F32)
            vaug_ref[c, rows, :] = (v * own + (1.0 - own)).astype(BF16)
        return 0

    for j in range(n_blocks):
        fill(j, 0)


def _normalised(acc_ref, t):
    res = []
    for c in range(2):
        acc = acc_ref[2 * t + c]
        res.append(acc / pltpu.roll(acc, HEAD_DIM, 1))
    return jnp.where(_lane_iota() < HEAD_DIM, res[0], res[1])


def _causal_masks():
    span = KEY_GROUP * ATT_TILE
    diff = (lax.broadcasted_iota(jnp.int32, (ATT_TILE, span), 1)
            - lax.broadcasted_iota(jnp.int32, (ATT_TILE, span), 0))
    return [diff <= t * ATT_TILE for t in range(Q_SUB)]


def _attend_shifted(lhs_of_tile, kaug_ref, vaug_ref, acc_ref, s_ref, lhs_ref, o_ref, n_q):
    span = KEY_GROUP * ATT_TILE
    n_chain = 2 * Q_SUB

    def group_rows(g):
        return pl.ds(pl.multiple_of(g * span, span), span)

    def scores(g, slot):
        kb = kaug_ref[group_rows(g), :]
        for i in range(n_chain):
            s_ref[slot, i] = _nt(lhs_ref[i], kb)

    def scores_diagonal(g, slot):
        for i in range(n_chain):
            rows = pl.ds(pl.multiple_of(g * span, span), (i // 2 + 1) * ATT_TILE)
            s_ref[slot, i, :, 0:(i // 2 + 1) * ATT_TILE] = _nt(lhs_ref[i], kaug_ref[rows, :])

    def consume(g, slot):
        rows = group_rows(g)
        for i in range(n_chain):
            p = jnp.exp2(s_ref[slot, i]).astype(BF16)
            acc_ref[i] += _dot(p, vaug_ref[i % 2, rows, :])

    def consume_diagonal(g, slot):
        row = lax.broadcasted_iota(jnp.int32, (ATT_TILE, ATT_TILE), 0)
        col = lax.broadcasted_iota(jnp.int32, (ATT_TILE, ATT_TILE), 1)
        for i in range(n_chain):
            t = i // 2
            lo = t * ATT_TILE
            parts = []
            if t:
                parts.append(jnp.exp2(s_ref[slot, i, :, 0:lo]).astype(BF16))
            s_diag = jnp.where(col <= row, s_ref[slot, i, :, lo:lo + ATT_TILE], NEG)
            parts.append(jnp.exp2(s_diag).astype(BF16))
            p = parts[0] if len(parts) == 1 else jnp.concatenate(parts, axis=1)
            rows = pl.ds(pl.multiple_of(g * span, span), lo + ATT_TILE)
            acc_ref[i] += _dot(p, vaug_ref[i % 2, rows, :])

    def emit(qi):
        for t in range(Q_SUB):
            rows = pl.ds(pl.multiple_of((qi * Q_SUB + t) * ATT_TILE, ATT_TILE), ATT_TILE)
            o_ref[rows, :] = _normalised(acc_ref, t).astype(BF16)

    for i, operand in enumerate(lhs_of_tile(0)):
        lhs_ref[i] = operand
    scores_diagonal(0, DIAG_SLOT)

    def tile(qi, _):
        acc_ref[...] = jnp.zeros_like(acc_ref)
        n_full = qi

        @pl.when(n_full > 0)
        def _():
            def pair(p, _):
                g = 2 * p
                consume(g, 0)
                scores(g + 1, 1)
                consume(g + 1, 1)
                scores(g + 2, 0)
                return 0

            n_loop = n_full - 1
            lax.fori_loop(0, n_loop // 2, pair, 0)

            @pl.when(lax.rem(n_loop, 2) == 1)
            def _():
                consume(n_loop - 1, 0)
                scores(n_loop, 1)

            consume(n_full - 1, lax.rem(n_full - 1, 2))
            scores_diagonal(n_full, DIAG_SLOT)

        consume_diagonal(n_full, DIAG_SLOT)
        nxt = lhs_of_tile(jnp.minimum(qi + 1, n_q - 1))
        kb = kaug_ref[group_rows(0), :]
        for i in range(n_chain):
            lhs_ref[i] = nxt[i]
            s_ref[0, i] = _nt(nxt[i], kb)
        emit(qi)
        return 0

    lax.fori_loop(0, n_q, tile, 0)


def _attend_running_max(lhs_of_tile, kaug_ref, vaug_ref, acc_ref, o_ref, n_q):
    span = KEY_GROUP * ATT_TILE
    n_chain = 2 * Q_SUB
    masks = _causal_masks()

    def tile(qi, _):
        lhs = lhs_of_tile(qi)

        def step(i, g, m, mask):
            rows = pl.ds(pl.multiple_of(g * span, span), span)
            s = _nt(lhs[i], kaug_ref[rows, :])
            if mask is not None:
                s = jnp.where(mask, s, NEG)
            m_new = jnp.maximum(m, jnp.max(s, axis=1, keepdims=True))
            p = jnp.exp2(s - m_new).astype(BF16)
            acc_ref[i] = jnp.exp2(m - m_new) * acc_ref[i] + _dot(p, vaug_ref[i % 2, rows, :])
            return m_new

        acc_ref[...] = jnp.zeros_like(acc_ref)
        m0 = jnp.full((ATT_TILE, 1), -jnp.inf, F32)
        ms = lax.fori_loop(
            0, qi, lambda g, ms: tuple(step(i, g, ms[i], None) for i in range(n_chain)),
            (m0,) * n_chain)
        for i in range(n_chain):
            step(i, qi, ms[i], masks[i // 2])
        for t in range(Q_SUB):
            rows = pl.ds(pl.multiple_of((qi * Q_SUB + t) * ATT_TILE, ATT_TILE), ATT_TILE)
            o_ref[rows, :] = _normalised(acc_ref, t).astype(BF16)
        return 0

    lax.fori_loop(0, n_q, tile, 0)


def _attend(lhs_of_tile, kaug_ref, vaug_ref, acc_ref, s_ref, lhs_ref, o_ref, n_q, bounded):
    if bounded:
        _attend_shifted(lhs_of_tile, kaug_ref, vaug_ref, acc_ref, s_ref, lhs_ref, o_ref, n_q)
    else:
        _attend_running_max(lhs_of_tile, kaug_ref, vaug_ref, acc_ref, o_ref, n_q)


def _attention_scratch(seq):
    n_chain = 2 * Q_SUB
    return [pltpu.VMEM((seq, 2 * LANES), BF16),
            pltpu.VMEM((2, seq, LANES), BF16),
            pltpu.VMEM((n_chain, ATT_TILE, LANES), F32),
            pltpu.VMEM((3, n_chain, ATT_TILE, KEY_GROUP * ATT_TILE), F32),
            pltpu.VMEM((n_chain, ATT_TILE, 2 * LANES), BF16)]


def _moba_kernel(q_ref, k_ref, v_ref, km_ref, shift_ref, o_ref, kaug_ref, vaug_ref,
                 acc_ref, s_ref, lhs_ref, *, bounded):
    seq = k_ref.shape[0]
    n_blocks = seq // MOBA_BLOCK
    lane = _lane_iota()

    kaug_ref[:, 0:LANES] = k_ref[...]

    def fill(j, _):
        rows = pl.ds(pl.multiple_of(j * MOBA_BLOCK, MOBA_BLOCK), MOBA_BLOCK)
        onehot = (lane == j) | (lane == MOBA_SHIFT_LANE)
        kaug_ref[rows, LANES:2 * LANES] = jnp.where(onehot, 1.0, 0.0).astype(BF16)
        return 0

    for j in range(n_blocks):
        fill(j, 0)
    _fill_values(v_ref, vaug_ref)

    def lhs_of_tile(tile):
        km = km_ref[0]
        km = jnp.concatenate([km, jnp.zeros((LANES - n_blocks, LANES), F32)], axis=0)
        km_hi = km.astype(BF16)
        km_lo = (km - km_hi.astype(F32)).astype(BF16)
        shift = shift_ref[:, MOBA_SHIFT_LANE:MOBA_SHIFT_LANE + 1] if bounded else 0.0
        blk = lax.broadcasted_iota(jnp.int32, (n_blocks, ATT_TILE), 0)
        blk_f = blk.astype(F32)
        pad_row = lax.broadcasted_iota(jnp.int32, (LANES - n_blocks, ATT_TILE), 0) + n_blocks
        pad_bias = jnp.where(pad_row == MOBA_SHIFT_LANE, -shift, NEG)
        lhs = []
        for t in range(Q_SUB):
            own = tile * Q_SUB + t
            q2 = q_ref[pl.ds(pl.multiple_of(own * ATT_TILE, ATT_TILE), ATT_TILE), :]
            for c in range(2):
                qc = q2 * _head_lanes(c, BF16)
                gate = (_nt(km_hi, qc) + _nt(km_lo, qc))[0:n_blocks]
                g = jnp.where(blk < own, gate, -jnp.inf)
                sel = blk == own
                for _ in range(MOBA_TOPK):
                    mx = jnp.max(g, axis=0, keepdims=True)
                    first = jnp.min(jnp.where(g == mx, blk_f, float(LANES)), axis=0,
                                    keepdims=True)
                    pick = (blk_f == first) & (mx > -jnp.inf)
                    sel = sel | pick
                    g = jnp.where(pick, -jnp.inf, g)
                bias = jnp.concatenate([jnp.where(sel, 0.0, NEG), pad_bias], axis=0)
                lhs.append(jnp.concatenate([qc, bias.T.astype(BF16)], axis=1))
        return lhs

    _attend(lhs_of_tile, kaug_ref, vaug_ref, acc_ref, s_ref, lhs_ref, o_ref,
            seq // (Q_SUB * ATT_TILE), bounded)


def _moba_call(proj, kmean, shift_row, batch, seq, bounded):
    n_pairs = N_HEADS // 2
    q0 = (TILE_QM * COL_TILE) // LANES
    k0 = (TILE_KM * COL_TILE) // LANES
    v0 = k0 + COL_TILE // LANES
    return pl.pallas_call(
        functools.partial(_moba_kernel, bounded=bounded),
        grid=(batch, n_pairs),
        in_specs=[
            pl.BlockSpec((seq, LANES), lambda b, p: (b, q0 + p)),
            pl.BlockSpec((seq, LANES), lambda b, p: (b, k0 + p)),
            pl.BlockSpec((seq, LANES), lambda b, p: (b, v0 + p)),
            pl.BlockSpec((1, seq // MOBA_BLOCK, LANES), lambda b, p: (b, 0, p)),
            pl.BlockSpec((1, LANES), lambda b, p: (0, 0)),
        ],
        out_specs=pl.BlockSpec((seq, LANES), lambda b, p: (b, p)),
        out_shape=jax.ShapeDtypeStruct((batch * seq, ATT_WIDTH), BF16),
        scratch_shapes=_attention_scratch(seq),
        compiler_params=pltpu.CompilerParams(
            dimension_semantics=("parallel", "parallel"), vmem_limit_bytes=VMEM_LIMIT),
        name="moba_bounded" if bounded else "moba_general",
    )(proj, proj, proj, kmean, shift_row)


def _fox_kernel(q_ref, k_ref, v_ref, faug_ref, o_ref, kaug_ref, vaug_ref,
                acc_ref, s_ref, lhs_ref, *, bounded):
    pair = pl.program_id(1)
    n_extra = 3 * N_HEADS
    seq = k_ref.shape[0]

    kaug_ref[:, 0:LANES] = k_ref[...]

    def fill(j, _):
        rows = pl.ds(pl.multiple_of(j * ATT_TILE, ATT_TILE), ATT_TILE)
        lane = _lane_iota()
        ones = (lane >= FOX_Q_LANE0) & (lane < FOX_Q_LANE0 + n_extra)
        f = faug_ref[rows, :].astype(F32)
        kaug_ref[rows, LANES:2 * LANES] = (
            f * (lane < n_extra).astype(F32) + ones.astype(F32)).astype(BF16)
        return 0

    for j in range(seq // ATT_TILE):
        fill(j, 0)
    _fill_values(v_ref, vaug_ref)

    def lhs_of_tile(tile):
        lhs = []
        for t in range(Q_SUB):
            rows = pl.ds(pl.multiple_of((tile * Q_SUB + t) * ATT_TILE, ATT_TILE), ATT_TILE)
            q2 = q_ref[rows, :]
            fq = faug_ref[rows, :]
            for c in range(2):
                qc = q2 * _head_lanes(c, BF16)
                k_first = 3 * (2 * pair + c)
                extra = _lane_range(k_first, 3, BF16)
                if bounded:
                    extra = extra + fq * _lane_range(FOX_Q_LANE0 + k_first, 3, BF16)
                lhs.append(jnp.concatenate([qc, extra], axis=1))
        return lhs

    _attend(lhs_of_tile, kaug_ref, vaug_ref, acc_ref, s_ref, lhs_ref, o_ref,
            seq // (Q_SUB * ATT_TILE), bounded)


def _fox_call(proj, faug, batch, seq, bounded):
    n_pairs = N_HEADS // 2
    q0 = (TILE_QF * COL_TILE) // LANES
    k0 = (TILE_KF * COL_TILE) // LANES
    v0 = k0 + COL_TILE // LANES
    return pl.pallas_call(
        functools.partial(_fox_kernel, bounded=bounded),
        grid=(batch, n_pairs),
        in_specs=[
            pl.BlockSpec((seq, LANES), lambda b, p: (b, q0 + p)),
            pl.BlockSpec((seq, LANES), lambda b, p: (b, k0 + p)),
            pl.BlockSpec((seq, LANES), lambda b, p: (b, v0 + p)),
            pl.BlockSpec((seq, LANES), lambda b, p: (b, 0)),
        ],
        out_specs=pl.BlockSpec((seq, LANES), lambda b, p: (b, p)),
        out_shape=jax.ShapeDtypeStruct((batch * seq, ATT_WIDTH), BF16),
        scratch_shapes=_attention_scratch(seq),
        compiler_params=pltpu.CompilerParams(
            dimension_semantics=("parallel", "parallel"), vmem_limit_bytes=VMEM_LIMIT),
        name="fox_bounded" if bounded else "fox_general",
    )(proj, proj, proj, faug)


def _merge_kernel(x_ref, gmix_ref, qc_ref, ym_ref, yf_ref, mk_ref, mv_ref,
                  wg_ref, wbm_ref, wbf_ref, wbc_ref, wo_ref, o_ref):
    x = x_ref[...]
    h = _rms_normed(x, gmix_ref[...])
    qc = qc_ref[...]
    mk = mk_ref[0]
    mv = mv_ref[0]
    ys = []
    for hd in range(MEM_HEADS):
        cols = slice(hd * MEM_HEAD_DIM, (hd + 1) * MEM_HEAD_DIM)
        s = _nt(qc[:, cols], mk[:, cols])
        p = jnp.exp2(s - jnp.max(s, axis=1, keepdims=True))
        y = _dot(p.astype(BF16), mv[:, cols]) / jnp.sum(p, axis=1, keepdims=True)
        ys.append(y)
    yc = jnp.concatenate(ys, axis=1).astype(BF16)
    merged = None
    for i, (y, wb_ref) in enumerate(((ym_ref[...], wbm_ref), (yf_ref[...], wbf_ref),
                                     (yc, wbc_ref))):
        gate = jax.nn.sigmoid(_nt(h, wg_ref[i * D_MODEL:(i + 1) * D_MODEL, :]))
        term = gate * _dot(y, wb_ref[...])
        merged = term if merged is None else merged + term
    o_ref[...] = x + _dot(merged.astype(BF16), wo_ref[...])


def _merge_call(x2, gmix, proj, y_m, y_f, mk, mv, wg, wbm, wbf, wbc, wo, seq, tm):
    n_tok = x2.shape[0]
    mlen = mk.shape[1]
    seq_tiles = seq // tm
    const = lambda i: (0, 0)
    return pl.pallas_call(
        _merge_kernel,
        grid=(n_tok // tm,),
        in_specs=[
            pl.BlockSpec((tm, D_MODEL), lambda i: (i, 0)),
            pl.BlockSpec((1, D_MODEL), const),
            pl.BlockSpec((tm, ATT_WIDTH), lambda i: (i, TILE_QC)),
            pl.BlockSpec((tm, ATT_WIDTH), lambda i: (i, 0)),
            pl.BlockSpec((tm, ATT_WIDTH), lambda i: (i, 0)),
            pl.BlockSpec((1, mlen, ATT_WIDTH), lambda i: (i // seq_tiles, 0, 0)),
            pl.BlockSpec((1, mlen, ATT_WIDTH), lambda i: (i // seq_tiles, 0, 0)),
            pl.BlockSpec((3 * D_MODEL, D_MODEL), const),
            pl.BlockSpec((ATT_WIDTH, D_MODEL), const),
            pl.BlockSpec((ATT_WIDTH, D_MODEL), const),
            pl.BlockSpec((ATT_WIDTH, D_MODEL), const),
            pl.BlockSpec((D_MODEL, D_MODEL), const),
        ],
        out_specs=pl.BlockSpec((tm, D_MODEL), lambda i: (i, 0)),
        out_shape=jax.ShapeDtypeStruct((n_tok, D_MODEL), F32),
        compiler_params=pltpu.CompilerParams(
            dimension_semantics=("parallel",), vmem_limit_bytes=VMEM_LIMIT),
        name="merge",
    )(x2, gmix, proj, y_m, y_f, mk, mv, wg, wbm, wbf, wbc, wo)


def _ffn_kernel(x_ref, g_ref, wg_ref, wu_ref, wd_ref, o_ref):
    x = x_ref[...]
    h = _rms_normed(x, g_ref[...])
    d_ff = wg_ref.shape[1]
    out = x
    for lo in range(0, d_ff, FFN_CHUNK):
        cols = slice(lo, min(lo + FFN_CHUNK, d_ff))
        g = _dot(h, wg_ref[:, cols])
        u = _dot(h, wu_ref[:, cols])
        a = (g * jax.nn.sigmoid(g) * u).astype(BF16)
        out = out + _dot(a, wd_ref[cols, :])
    o_ref[...] = out


def _ffn_call(x1, g_ffn, wg, wu, wd, tm):
    n_tok = x1.shape[0]
    d_ff = wg.shape[1]
    const = lambda i: (0, 0)
    resident = dict(pipeline_mode=pl.Buffered(1))
    return pl.pallas_call(
        _ffn_kernel,
        grid=(n_tok // tm,),
        in_specs=[
            pl.BlockSpec((tm, D_MODEL), lambda i: (i, 0)),
            pl.BlockSpec((1, D_MODEL), const),
            pl.BlockSpec((D_MODEL, d_ff), const, **resident),
            pl.BlockSpec((D_MODEL, d_ff), const, **resident),
            pl.BlockSpec((d_ff, D_MODEL), const, **resident),
        ],
        out_specs=pl.BlockSpec((tm, D_MODEL), lambda i: (i, 0)),
        out_shape=jax.ShapeDtypeStruct((n_tok, D_MODEL), F32),
        compiler_params=pltpu.CompilerParams(
            dimension_semantics=("parallel",), vmem_limit_bytes=VMEM_LIMIT),
        name="ffn",
    )(x1, g_ffn, wg, wu, wd)


@functools.lru_cache(maxsize=None)
def _rope_tables(seq):
    half = ROPE_DIMS // 2
    inv_freq = 1.0 / (ROPE_THETA ** (np.arange(half, dtype=np.float64) * 2.0 / ROPE_DIMS))
    ang = np.arange(seq, dtype=np.float64)[:, None] * inv_freq[None, :]
    cos, sin = np.cos(ang), np.sin(ang)
    d = np.arange(LANES) % HEAD_DIM
    rc = np.where(d[None, :] < ROPE_DIMS, cos[:, d % half], 1.0)
    rs1 = np.where(d[None, :] < half, -sin[:, d % half], 0.0)
    rs2 = np.where((d[None, :] >= half) & (d[None, :] < ROPE_DIMS), sin[:, d % half], 0.0)
    return tuple(np.asarray(t, np.float32) for t in (rc, rs1, rs2))


@functools.lru_cache(maxsize=None)
def _segment_mean_matrices():
    def blockdiag(width):
        return np.kron(np.eye(SEG_TILE // width), np.ones((width, width))) / width
    return np.stack([blockdiag(HEAD_DIM), blockdiag(MEM_HEAD_DIM)]).astype(np.float32)


def _score_bound(g_q, g_k):
    return (BOUND_SLACK * HEAD_DIM ** 0.5 * jnp.max(jnp.abs(g_q)) * jnp.max(jnp.abs(g_k)))


def _lane_row(value, first, count):
    lane = jnp.arange(LANES)
    return jnp.where((lane >= first) & (lane < first + count), value, 0.0).astype(F32)[None, :]


def kernel(x, mem, g_mix, w_in, b_forget, g_q_moba, g_k_moba, g_q_fox, g_k_fox, g_q_mem, g_k_mem,
           g_mem, w_mem_kv, w_br_moba, w_br_fox, w_br_mem, w_out, g_ffn, w_gate, w_up, w_down):
    batch, seq, _ = x.shape
    depth = g_mix.shape[0]
    n_tok = batch * seq
    assert seq % (KEY_GROUP * ATT_TILE) == 0 and seq // MOBA_BLOCK < MOBA_SHIFT_LANE
    n_extra = 3 * N_HEADS

    rc, rs1, rs2 = (jnp.asarray(t) for t in _rope_tables(seq))
    bd = jnp.asarray(_segment_mean_matrices()).astype(BF16)
    ones = jnp.ones((COL_TILE,), F32)

    x2 = x.reshape(n_tok, D_MODEL)
    for layer in range(depth):
        w = w_in[layer]
        gates0 = 7 * ATT_WIDTH + N_HEADS
        qc0 = 6 * ATT_WIDTH + N_HEADS
        w_t = jnp.swapaxes(w, 0, 1)
        w_qkv = w_t[:6 * ATT_WIDTH].astype(BF16)
        w_qc = w_t[qc0:gates0].astype(BF16)
        w_gates = w_t[gates0:].astype(BF16)
        wf3 = jnp.repeat(w_t[6 * ATT_WIDTH:qc0], 3, axis=0)
        gap = jnp.zeros((FOX_Q_LANE0 - n_extra, D_MODEL), F32)
        tail = jnp.zeros((LANES - FOX_Q_LANE0 - n_extra, D_MODEL), F32)
        wf = jnp.concatenate([wf3, gap, wf3, tail], axis=0).astype(BF16)
        bf3 = jnp.repeat(b_forget[layer], 3)
        bf_rep = jnp.concatenate([bf3, gap[:, 0], bf3, tail[:, 0]])[None, :]

        att_scale = HEAD_DIM ** -0.5 * LOG2E
        gains = [ones] * N_COL_TILES
        gains[TILE_QM] = jnp.tile(g_q_moba[layer], N_HEADS) * att_scale
        gains[TILE_KM] = jnp.tile(g_k_moba[layer], N_HEADS)
        gains[TILE_QF] = jnp.tile(g_q_fox[layer], N_HEADS) * att_scale
        gains[TILE_KF] = jnp.tile(g_k_fox[layer], N_HEADS)
        gains[TILE_QC] = jnp.tile(g_q_mem[layer], MEM_HEADS) * (MEM_HEAD_DIM ** -0.5 * LOG2E)
        gains = jnp.stack(gains)[:, None, :]

        bound_m = _score_bound(g_q_moba[layer], g_k_moba[layer])
        bound_f = _score_bound(g_q_fox[layer], g_k_fox[layer])
        shift_m = _lane_row(bound_m * LOG2E, MOBA_SHIFT_LANE, 1)
        shift_f = _lane_row(bound_f * LOG2E, FOX_Q_LANE0, n_extra)

        proj, faug, kmean = _proj_call(x2, g_mix[layer][None, :], w_qkv, w_qc, wf, bf_rep, shift_f,
                                       bd, gains, rc, rs1, rs2, seq, tm=TOKEN_TILE)
        kmean = kmean.reshape(batch, seq // MOBA_BLOCK, COL_TILE)
        mk, mv = _memkv_call(mem, g_mem[layer][None, :], w_mem_kv[layer].astype(BF16),
                             g_k_mem[layer][None, :])
        def attention(bounded):
            return (_moba_call(proj, kmean, shift_m, batch, seq, bounded),
                    _fox_call(proj, faug, batch, seq, bounded))

        y_m, y_f = lax.cond(2.0 * jnp.maximum(bound_m, bound_f) <= MAX_SHIFT_NATS,
                            lambda: attention(True), lambda: attention(False))
        x1 = _merge_call(x2, g_mix[layer][None, :], proj, y_m, y_f, mk, mv, w_gates,
                         w_br_moba[layer].astype(BF16), w_br_fox[layer].astype(BF16),
                         w_br_mem[layer].astype(BF16), w_out[layer].astype(BF16), seq,
                         tm=TOKEN_TILE)
        x2 = _ffn_call(x1, g_ffn[layer][None, :], w_gate[layer].astype(BF16),
                       w_up[layer].astype(BF16), w_down[layer].astype(BF16), tm=TOKEN_TILE)
    return x2.reshape(batch, seq, D_MODEL)
```
